```python
import math
import jax, jax.numpy as jnp
from jax import lax
import numpy as np

D_MODEL = 1024
BATCH = 8
SEQ = 2048
DEPTH = 1
DEC_BATCH = 128
DEC_SEQ = 8
PAST_LEN = 2048
PAGE_SIZE = 128

N_META = 16
GDN_HEADS = 8
GDN_DK = 128
GDN_DV = 128
CONV_W = 4
GDN_CHUNK = 64
ATT_HEADS = 8
ATT_KV_HEADS = 2
ATT_DH = 128
IDX_HEADS = 16
IDX_DIM = 64
TOPK_MAX = 256
Q_BLOCK = 128
PEER_HEADS = 8
PEER_NKEYS = 128
PEER_EXPERTS = PEER_NKEYS * PEER_NKEYS
PEER_QDIM = 256
PEER_TOPK = 16
PEER_TOK_BLOCK = 256
DN_ALPHA = (2 * DEPTH) ** 0.25
DN_BETA = (8 * DEPTH) ** -0.25
LN_EPS = 1e-5
RMS_EPS = 1e-6

GDN_QK = GDN_HEADS * GDN_DK
GDN_V = GDN_HEADS * GDN_DV
CONV_DIM = 2 * GDN_QK + GDN_V
ATT_Q = ATT_HEADS * ATT_DH
ATT_KV = ATT_KV_HEADS * ATT_DH
IDX_Q = IDX_HEADS * IDX_DIM
IN_SIZES = (CONV_DIM, GDN_V, GDN_HEADS, GDN_HEADS, ATT_Q, ATT_KV, ATT_KV, IDX_Q, IDX_DIM, IDX_HEADS, D_MODEL, D_MODEL)
IN_DIM = sum(IN_SIZES)

kernel_name = 'hybrid_gdn_dsa_peer_step'


def layer_norm(x, g, b):
    xf = x.astype(jnp.float32)
    mu = xf.mean(-1, keepdims=True)
    var = jnp.square(xf - mu).mean(-1, keepdims=True)
    return ((xf - mu) * lax.rsqrt(var + LN_EPS) * g + b).astype(x.dtype)


def l2norm(x):
    xf = x.astype(jnp.float32)
    return xf * lax.rsqrt(jnp.sum(xf * xf, -1, keepdims=True) + RMS_EPS)


def in_projection(x, w_in):
    offsets = np.cumsum(IN_SIZES)[:-1].tolist()
    return jnp.split(x @ w_in, offsets, axis=-1)


def short_conv(u, prev, conv_w):
    T = u.shape[1]
    up = jnp.concatenate([prev.astype(u.dtype), u], axis=1)
    out = up[:, 0:T] * conv_w[0]
    for i in range(1, CONV_W):
        out = out + up[:, i:i + T] * conv_w[i]
    return jax.nn.silu(out), up[:, -(CONV_W - 1):]


def gdn_features(qkv_pre, conv_prev, b_raw, a_raw, conv_w, a_log, dt_bias):
    qkv, conv_new = short_conv(qkv_pre, conv_prev, conv_w)
    B, T, _ = qkv.shape
    q, k, v = jnp.split(qkv, [GDN_QK, 2 * GDN_QK], axis=-1)
    q = l2norm(q.reshape(B, T, GDN_HEADS, GDN_DK)) * (GDN_DK ** -0.5)
    k = l2norm(k.reshape(B, T, GDN_HEADS, GDN_DK))
    v = v.reshape(B, T, GDN_HEADS, GDN_DV)
    beta = jax.nn.sigmoid(b_raw.astype(jnp.float32))
    log_a = -jnp.exp(a_log.astype(jnp.float32)) * jax.nn.softplus(a_raw.astype(jnp.float32) + dt_bias.astype(jnp.float32))
    return q, k, v, beta, log_a, conv_new


def gdn_chunk(S, q, k, v, beta, log_a):
    f32 = jnp.float32
    q, k, v = (t.astype(f32).transpose(0, 2, 1, 3) for t in (q, k, v))
    beta = beta.astype(f32).transpose(0, 2, 1)
    g = jnp.cumsum(log_a.astype(f32).transpose(0, 2, 1), axis=-1)
    C = q.shape[2]
    incl = jnp.tril(jnp.ones((C, C), dtype=bool))
    strict = jnp.tril(jnp.ones((C, C), dtype=bool), -1)
    dec_incl = jnp.exp(jnp.where(incl, g[..., :, None] - g[..., None, :], -jnp.inf))
    dec_strict = jnp.where(strict, dec_incl, 0.0)
    lower = beta[..., None] * dec_strict * jnp.einsum('bhtd,bhjd->bhtj', k, k)
    eg = jnp.exp(g)[..., None]
    rhs = beta[..., None] * (v - eg * jnp.einsum('bhtd,bhde->bhte', k, S))
    u = lax.linalg.triangular_solve(lower, rhs, left_side=True, lower=True, unit_diagonal=True)
    qk = jnp.einsum('bhtd,bhjd->bhtj', q, k)
    o = eg * jnp.einsum('bhtd,bhde->bhte', q, S) + jnp.einsum('bhtj,bhje->bhte', dec_incl * qk, u)
    g_last = g[..., -1:]
    S_new = jnp.exp(g_last)[..., None] * S + jnp.einsum('bhjd,bhje->bhde', k * jnp.exp(g_last - g)[..., None], u)
    return S_new, o.transpose(0, 2, 1, 3)


def gdn_prompt(q, k, v, beta, log_a):
    B = q.shape[0]
    S0 = jnp.zeros((B, GDN_HEADS, GDN_DK, GDN_DV), jnp.float32)
    S, o_meta = gdn_chunk(S0, q[:, :N_META], k[:, :N_META], v[:, :N_META], beta[:, :N_META], log_a[:, :N_META])

    def chunks(t):
        t = t[:, N_META:]
        return t.reshape(B, -1, GDN_CHUNK, *t.shape[2:]).swapaxes(0, 1)

    S, o_rest = lax.scan(lambda s, c: gdn_chunk(s, *c), S, tuple(chunks(t) for t in (q, k, v, beta, log_a)))
    o_rest = o_rest.swapaxes(0, 1).reshape(B, -1, GDN_HEADS, GDN_DV)
    return jnp.concatenate([o_meta, o_rest], axis=1), S


def gdn_output(o, z, norm_g):
    B, T = o.shape[:2]
    of = o * lax.rsqrt(jnp.mean(o * o, -1, keepdims=True) + RMS_EPS) * norm_g
    zz = z.reshape(B, T, GDN_HEADS, GDN_DV).astype(jnp.float32)
    return (of * jax.nn.silu(zz)).astype(z.dtype).reshape(B, T, GDN_V)


def indexer_scores(qi, ki, wi, qpos, kpos):
    s = jax.nn.relu(jnp.einsum('bqhd,bkd->bqhk', qi, ki).astype(jnp.float32)) * (IDX_DIM ** -0.5)
    I = jnp.einsum('bqhk,bqh->bqk', s, wi.astype(jnp.float32) * (IDX_HEADS ** -0.5))
    return jnp.where(kpos[None, :] <= qpos[:, None], I, -jnp.inf)


def attend_selected(q, k_sel, v_sel, valid):
    B, Q = q.shape[:2]
    qg = q.reshape(B, Q, ATT_KV_HEADS, ATT_HEADS // ATT_KV_HEADS, ATT_DH)
    s = jnp.einsum('bqhgd,bqkhd->bqhgk', qg, k_sel).astype(jnp.float32) * (ATT_DH ** -0.5)
    s = jnp.where(valid[:, :, None, None, :], s, -jnp.inf)
    p = jax.nn.softmax(s, axis=-1).astype(v_sel.dtype)
    o = jnp.einsum('bqhgk,bqkhd->bqhgd', p, v_sel)
    return o.reshape(B, Q, ATT_Q)


def gather_rows(arr, idx):
    return jax.vmap(lambda a, i: a[i])(arr, idx)


def sparse_attn_prompt(q, k, v, qi, ki, wi):
    B, L = q.shape[:2]
    n_sel = min(TOPK_MAX, L // 4)
    n_blk = -(-L // Q_BLOCK)
    pad = n_blk * Q_BLOCK - L

    def blocks(t):
        t = jnp.pad(t, [(0, 0), (0, pad)] + [(0, 0)] * (t.ndim - 2))
        return t.reshape(B, n_blk, Q_BLOCK, *t.shape[2:]).swapaxes(0, 1)

    kpos = jnp.arange(L)
    qpos_all = jnp.arange(n_blk * Q_BLOCK).reshape(n_blk, Q_BLOCK)

    def one_block(args):
        qb, qib, wib, qpos = args
        I = indexer_scores(qib, ki, wib, qpos, kpos)
        _, idx = lax.top_k(I, n_sel)
        valid = idx <= qpos[:, None]
        flat = idx.reshape(B, -1)
        k_sel = gather_rows(k, flat).reshape(B, Q_BLOCK, n_sel, ATT_KV_HEADS, ATT_DH)
        v_sel = gather_rows(v, flat).reshape(B, Q_BLOCK, n_sel, ATT_KV_HEADS, ATT_DH)
        return attend_selected(qb, k_sel, v_sel, valid)

    o = lax.map(one_block, (blocks(q), blocks(qi), blocks(wi), qpos_all))
    return o.swapaxes(0, 1).reshape(B, n_blk * Q_BLOCK, ATT_Q)[:, :L]


def sparse_attn_sample(q, k_new, v_new, qi, ki_new, wi, cache_k, cache_v, cache_ik, page_table):
    DB, T = q.shape[:2]
    past = page_table.shape[1] * PAGE_SIZE
    L = past + T
    n_sel = min(TOPK_MAX, L // 4)
    ki_past = cache_ik[page_table].reshape(DB, past, IDX_DIM).astype(ki_new.dtype)
    ki_all = jnp.concatenate([ki_past, ki_new], axis=1)
    qpos = past + jnp.arange(T)
    I = indexer_scores(qi, ki_all, wi, qpos, jnp.arange(L))
    _, idx = lax.top_k(I, n_sel)
    valid = idx <= qpos[:, None]
    flat = idx.reshape(DB, -1)
    in_past = flat < past
    page = gather_rows(page_table, jnp.minimum(flat, past - 1) // PAGE_SIZE)
    phys = page * PAGE_SIZE + flat % PAGE_SIZE
    new_pos = jnp.clip(flat - past, 0, T - 1)

    def gather(pool, new):
        from_pool = pool.reshape(-1, ATT_KV_HEADS, ATT_DH)[phys].astype(new.dtype)
        from_new = gather_rows(new, new_pos)
        return jnp.where(in_past[:, :, None, None], from_pool, from_new).reshape(DB, T, n_sel, ATT_KV_HEADS, ATT_DH)

    return attend_selected(q, gather(cache_k, k_new), gather(cache_v, v_new), valid)


def peer_ffn(h, peer_wq, peer_subkeys, peer_u, peer_v):
    N = h.shape[0]
    n_blk = -(-N // PEER_TOK_BLOCK)
    hb = jnp.pad(h, ((0, n_blk * PEER_TOK_BLOCK - N), (0, 0))).reshape(n_blk, PEER_TOK_BLOCK, D_MODEL)

    def one_block(xb):
        T = xb.shape[0]
        qry = (xb @ peer_wq).reshape(T, PEER_HEADS, 2, PEER_QDIM // 2)
        s = jnp.einsum('thcd,hcnd->thcn', qry, peer_subkeys).astype(jnp.float32)
        sv, si = lax.top_k(s, PEER_TOPK)
        cand_s = (sv[:, :, 0, :, None] + sv[:, :, 1, None, :]).reshape(T, PEER_HEADS, PEER_TOPK * PEER_TOPK)
        cand_i = (si[:, :, 0, :, None] * PEER_NKEYS + si[:, :, 1, None, :]).reshape(T, PEER_HEADS, PEER_TOPK * PEER_TOPK)
        top_s, pos = lax.top_k(cand_s, PEER_TOPK)
        expert = jnp.take_along_axis(cand_i, pos, axis=-1)
        gate = jax.nn.softmax(top_s, axis=-1)
        act = jax.nn.gelu(jnp.einsum('thkd,td->thk', peer_u[expert], xb).astype(jnp.float32), approximate=False)
        return jnp.einsum('thk,thkd->td', (gate * act).astype(xb.dtype), peer_v[expert])

    return lax.map(one_block, hb).reshape(-1, D_MODEL)[:N]


def finish_layer(x, o_gdn, o_att, ga, gb, w_bg, w_ba, w_out, ln1_g, ln1_b, peer_wq, peer_subkeys, peer_u, peer_v, ln2_g, ln2_b):
    merged = jax.nn.sigmoid(ga) * (o_gdn @ w_bg) + jax.nn.sigmoid(gb) * (o_att @ w_ba)
    h = layer_norm(DN_ALPHA * x + merged @ w_out, ln1_g, ln1_b)
    f = peer_ffn(h.reshape(-1, D_MODEL), peer_wq, peer_subkeys, peer_u, peer_v).reshape(h.shape)
    return layer_norm(DN_ALPHA * h + f, ln2_g, ln2_b)


def setup_inputs(seed: int = 0) -> dict:
    key = jax.random.key(seed)
    ks = iter(jax.random.split(key, 40))
    f32 = jnp.float32

    def nrm(shape, scale):
        return jax.random.normal(next(ks), shape, f32) * scale

    n_pages = PAST_LEN // PAGE_SIZE
    n_used = DEC_BATCH * n_pages
    n_pool = n_used + n_used // 4
    x_prompt = nrm((BATCH, SEQ, D_MODEL), 1.0)
    x_sample = nrm((DEC_BATCH, DEC_SEQ, D_MODEL), 1.0)
    cache_k = nrm((DEPTH, n_pool, PAGE_SIZE, ATT_KV_HEADS, ATT_DH), 1.0)
    cache_v = nrm((DEPTH, n_pool, PAGE_SIZE, ATT_KV_HEADS, ATT_DH), 1.0)
    cache_idx_k = nrm((DEPTH, n_pool, PAGE_SIZE, IDX_DIM), 1.0)
    state_conv = nrm((DEPTH, DEC_BATCH, CONV_W - 1, CONV_DIM), 1.0)
    state_delta = nrm((DEPTH, DEC_BATCH, GDN_HEADS, GDN_DK, GDN_DV), 0.05)
    page_table = jax.random.permutation(next(ks), n_pool)[:n_used].reshape(DEC_BATCH, n_pages).astype(jnp.int32)
    meta_tokens = nrm((N_META, D_MODEL), 1.0)
    w_in = nrm((DEPTH, D_MODEL, IN_DIM), D_MODEL ** -0.5)
    conv_w = nrm((DEPTH, CONV_W, CONV_DIM), CONV_W ** -0.5)
    a_log = jnp.log(jax.random.uniform(next(ks), (DEPTH, GDN_HEADS), f32, 1.0, 16.0))
    dt = jnp.exp(jax.random.uniform(next(ks), (DEPTH, GDN_HEADS), f32, math.log(1e-3), math.log(1e-1)))
    dt_bias = dt + jnp.log(-jnp.expm1(-dt))
    gdn_norm_g = 1.0 + nrm((DEPTH, GDN_DV), 0.02)
    w_branch_gdn = nrm((DEPTH, GDN_V, D_MODEL), GDN_V ** -0.5)
    w_branch_attn = nrm((DEPTH, ATT_Q, D_MODEL), ATT_Q ** -0.5)
    w_out = nrm((DEPTH, D_MODEL, D_MODEL), DN_BETA * D_MODEL ** -0.5)
    ln1_g = 1.0 + nrm((DEPTH, D_MODEL), 0.02)
    ln1_b = nrm((DEPTH, D_MODEL), 0.02)
    peer_wq = nrm((DEPTH, D_MODEL, PEER_HEADS * PEER_QDIM), D_MODEL ** -0.5)
    peer_subkeys = nrm((DEPTH, PEER_HEADS, 2, PEER_NKEYS, PEER_QDIM // 2), (PEER_QDIM // 2) ** -0.5)
    peer_u = nrm((DEPTH, PEER_EXPERTS, D_MODEL), D_MODEL ** -0.5)
    peer_v = nrm((DEPTH, PEER_EXPERTS, D_MODEL), DN_BETA * PEER_HEADS ** -0.5)
    ln2_g = 1.0 + nrm((DEPTH, D_MODEL), 0.02)
    ln2_b = nrm((DEPTH, D_MODEL), 0.02)
    return {'x_prompt': x_prompt, 'x_sample': x_sample, 'cache_k': cache_k, 'cache_v': cache_v,
            'cache_idx_k': cache_idx_k, 'state_conv': state_conv, 'state_delta': state_delta,
            'page_table': page_table, 'meta_tokens': meta_tokens, 'w_in': w_in, 'conv_w': conv_w,
            'a_log': a_log, 'dt_bias': dt_bias, 'gdn_norm_g': gdn_norm_g, 'w_branch_gdn': w_branch_gdn,
            'w_branch_attn': w_branch_attn, 'w_out': w_out, 'ln1_g': ln1_g, 'ln1_b': ln1_b,
            'peer_wq': peer_wq, 'peer_subkeys': peer_subkeys, 'peer_u': peer_u, 'peer_v': peer_v,
            'ln2_g': ln2_g, 'ln2_b': ln2_b}


def reference(x_prompt, x_sample, cache_k, cache_v, cache_idx_k, state_conv, state_delta, page_table,
              meta_tokens, w_in, conv_w, a_log, dt_bias, gdn_norm_g, w_branch_gdn, w_branch_attn, w_out,
              ln1_g, ln1_b, peer_wq, peer_subkeys, peer_u, peer_v, ln2_g, ln2_b):
    B = x_prompt.shape[0]
    DB, T = x_sample.shape[:2]
    hp = jnp.concatenate([jnp.broadcast_to(meta_tokens[None].astype(x_prompt.dtype), (B, N_META, D_MODEL)), x_prompt], axis=1)
    hs = x_sample
    Lp = hp.shape[1]
    p_conv_l, p_delta_l, p_k_l, p_v_l, p_ik_l = [], [], [], [], []
    s_conv_l, s_delta_l, s_k_l, s_v_l, s_ik_l = [], [], [], [], []
    for l in range(DEPTH):
        qkv_pre, z, b_raw, a_raw, aq, ak, av, iq, ik, iw, ga, gb = in_projection(hp, w_in[l])
        zero_prev = jnp.zeros((B, CONV_W - 1, CONV_DIM), hp.dtype)
        q, k, v, beta, log_a, p_conv = gdn_features(qkv_pre, zero_prev, b_raw, a_raw, conv_w[l], a_log[l], dt_bias[l])
        o_gdn, p_S = gdn_prompt(q, k, v, beta, log_a)
        o_gdn = gdn_output(o_gdn, z, gdn_norm_g[l])
        pk = ak.reshape(B, Lp, ATT_KV_HEADS, ATT_DH)
        pv = av.reshape(B, Lp, ATT_KV_HEADS, ATT_DH)
        o_att = sparse_attn_prompt(aq.reshape(B, Lp, ATT_HEADS, ATT_DH), pk, pv,
                                   iq.reshape(B, Lp, IDX_HEADS, IDX_DIM), ik, iw)
        hp = finish_layer(hp, o_gdn, o_att, ga, gb, w_branch_gdn[l], w_branch_attn[l], w_out[l], ln1_g[l], ln1_b[l],
                          peer_wq[l], peer_subkeys[l], peer_u[l], peer_v[l], ln2_g[l], ln2_b[l])
        p_conv_l.append(p_conv)
        p_delta_l.append(p_S.astype(state_delta.dtype))
        p_k_l.append(pk)
        p_v_l.append(pv)
        p_ik_l.append(ik)
        qkv_pre, z, b_raw, a_raw, aq, ak, av, iq, ik, iw, ga, gb = in_projection(hs, w_in[l])
        q, k, v, beta, log_a, s_conv = gdn_features(qkv_pre, state_conv[l], b_raw, a_raw, conv_w[l], a_log[l], dt_bias[l])
        s_S, o_gdn = gdn_chunk(state_delta[l].astype(jnp.float32), q, k, v, beta, log_a)
        o_gdn = gdn_output(o_gdn, z, gdn_norm_g[l])
        sk = ak.reshape(DB, T, ATT_KV_HEADS, ATT_DH)
        sv = av.reshape(DB, T, ATT_KV_HEADS, ATT_DH)
        o_att = sparse_attn_sample(aq.reshape(DB, T, ATT_HEADS, ATT_DH), sk, sv,
                                   iq.reshape(DB, T, IDX_HEADS, IDX_DIM), ik, iw,
                                   cache_k[l], cache_v[l], cache_idx_k[l], page_table)
        hs = finish_layer(hs, o_gdn, o_att, ga, gb, w_branch_gdn[l], w_branch_attn[l], w_out[l], ln1_g[l], ln1_b[l],
                          peer_wq[l], peer_subkeys[l], peer_u[l], peer_v[l], ln2_g[l], ln2_b[l])
        s_conv_l.append(s_conv)
        s_delta_l.append(s_S.astype(state_delta.dtype))
        s_k_l.append(sk)
        s_v_l.append(sv)
        s_ik_l.append(ik)
    y_prompt = hp[:, N_META:]
    y_sample = hs
    p_conv = jnp.stack(p_conv_l)
    p_delta = jnp.stack(p_delta_l)
    p_k = jnp.stack(p_k_l)
    p_v = jnp.stack(p_v_l)
    p_idx_k = jnp.stack(p_ik_l)
    s_conv = jnp.stack(s_conv_l)
    s_delta = jnp.stack(s_delta_l)
    s_k = jnp.stack(s_k_l)
    s_v = jnp.stack(s_v_l)
    s_idx_k = jnp.stack(s_ik_l)
    return (y_prompt, y_sample, p_conv, p_delta, p_k, p_v, p_idx_k, s_conv, s_delta, s_k, s_v, s_idx_k)
```

```python
import functools

import jax
import jax.numpy as jnp
from jax import lax
from jax.experimental import pallas as pl
from jax.experimental.pallas import tpu as pltpu

F32 = jnp.float32
BF16 = jnp.bfloat16
I32 = jnp.int32

GDN_HEADS = 8
GDN_D = 128
CONV_W = 4
GDN_CHUNK = 64
ATT_HEADS = 8
ATT_KV_HEADS = 2
ATT_DH = 128
IDX_HEADS = 16
IDX_DIM = 64
TOPK_MAX = 256
PEER_HEADS = 8
PEER_NKEYS = 128
PEER_TOPK = 16
LN_EPS = 1e-5
RMS_EPS = 1e-6

LANE = 128
SUBLANE = 8
VMEM_LIMIT = 56 * 1024 * 1024

D = 1024
COL_QKV = 0
COL_Z = 3072
COL_AQ = 4096
COL_IQ = 5120
COL_GA = 6144
COL_GB = 7168
COL_AK = 8192
COL_AV = 8448
COL_SM = 8704
N_COLS = 8832
SM_IK = 0
SM_B = 64
SM_A = 72
SM_IW = 80

NEG_INF = float("-inf")
INT_MIN = -(2 ** 31)


def _nt(a, b):
    return lax.dot_general(a, b, (((1,), (1,)), ((), ())), preferred_element_type=F32)


def _tn(a, b):
    return lax.dot_general(a, b, (((0,), (0,)), ((), ())), preferred_element_type=F32)


def _mm(a, b):
    return jnp.dot(a, b, preferred_element_type=F32)


def _mm_f32(a, b):
    return jnp.dot(a, b, preferred_element_type=F32, precision=lax.Precision.HIGHEST)


def _cparams(sem):
    return pltpu.CompilerParams(dimension_semantics=sem, vmem_limit_bytes=VMEM_LIMIT)


def _proj_kernel(x_ref, w_ref, o_ref):
    o_ref[...] = _mm(x_ref[...].astype(BF16), w_ref[...])


def _project(x, w_bf, tm):
    n = x.shape[0]
    tn = N_COLS // 3
    return pl.pallas_call(
        _proj_kernel,
        grid=(3, n // tm),
        in_specs=[pl.BlockSpec((tm, D), lambda j, i: (i, 0)),
                  pl.BlockSpec((D, tn), lambda j, i: (0, j))],
        out_specs=pl.BlockSpec((tm, tn), lambda j, i: (i, j)),
        out_shape=jax.ShapeDtypeStruct((n, N_COLS), F32),
        compiler_params=_cparams(("arbitrary", "arbitrary")),
        name="in_projection",
    )(x, w_bf)


def _sibling_mask(ri, ci, lvl):
    return ((ri >> (lvl + 1)) == (ci >> (lvl + 1))) & (((ri >> lvl) & 1) == 1) & (((ci >> lvl) & 1) == 0)


def _gdn_kernel(qkv_ref, z_ref, sm_ref, prev_ref, s0_ref, cw_ref, hp_ref, ng_ref,
                o_ref, sfin_ref, s_scr, ext_scr, *, c_in, n_valid):
    C = ext_scr.shape[0] - SUBLANE
    c = pl.program_id(1)

    @pl.when(c == 0)
    def _():
        s_scr[...] = s0_ref[...]
        ext_scr[0:SUBLANE, :] = prev_ref[...]

    u = qkv_ref[...]
    sm = sm_ref[...]
    z = z_ref[...]
    if c_in < C:
        u = jnp.concatenate([u, jnp.zeros((C - c_in, u.shape[1]), F32)], axis=0)
        sm = jnp.concatenate([sm, jnp.zeros((C - c_in, LANE), F32)], axis=0)
        z = jnp.concatenate([z, jnp.zeros((C - c_in, z.shape[1]), F32)], axis=0)
    ext_scr[SUBLANE:SUBLANE + C, :] = u
    cw = cw_ref[...]
    conv = u * cw[CONV_W - 1:CONV_W, :]
    for i in range(CONV_W - 1):
        off = SUBLANE - (CONV_W - 1) + i
        conv = conv + ext_scr[off:off + C, :] * cw[i:i + 1, :]
    tail = ext_scr[c_in:c_in + SUBLANE, :]
    ext_scr[0:SUBLANE, :] = tail
    qkv = conv * jax.nn.sigmoid(conv)

    row = lax.broadcasted_iota(I32, (C, 1), 0)
    valid = row < n_valid
    hp = hp_ref[...]
    xs = sm + hp[1:2, :]
    softplus = jnp.maximum(xs, 0.0) + jnp.log1p(jnp.exp(-jnp.abs(xs)))
    la = -jnp.exp(hp[0:1, :]) * softplus
    beta = jax.nn.sigmoid(sm)
    if n_valid < C:
        qkv = jnp.where(valid, qkv, 0.0)
        la = jnp.where(valid, la, 0.0)
        beta = jnp.where(valid, beta, 0.0)

    ri = lax.broadcasted_iota(I32, (C, C), 0)
    ci = lax.broadcasted_iota(I32, (C, C), 1)
    incl = ri >= ci
    strict = ri > ci
    g = _mm_f32(incl.astype(F32), la)
    gt = g.T
    eye = (ri == ci).astype(F32)
    ng = ng_ref[...]
    nh = GDN_HEADS
    hd = GDN_D

    for h in range(nh):
        col = SM_A + h
        gcol = g[:, col:col + 1]
        grow = gt[col:col + 1, :]
        bcol = beta[:, SM_B + h:SM_B + h + 1]
        q = qkv[:, h * hd:(h + 1) * hd]
        k = qkv[:, nh * hd + h * hd:nh * hd + (h + 1) * hd]
        v = qkv[:, 2 * nh * hd + h * hd:2 * nh * hd + (h + 1) * hd]
        q = q * lax.rsqrt(jnp.sum(q * q, axis=-1, keepdims=True) + RMS_EPS) * (hd ** -0.5)
        k = k * lax.rsqrt(jnp.sum(k * k, axis=-1, keepdims=True) + RMS_EPS)
        qb = q.astype(BF16)
        kb = k.astype(BF16)
        kk = _nt(kb, kb)
        qk = _nt(qb, kb)
        dec = jnp.exp(jnp.where(incl, gcol - grow, NEG_INF))
        low = bcol * jnp.where(strict, dec, 0.0) * kk
        inv = eye - jnp.where(_sibling_mask(ri, ci, 0), low, 0.0)
        for lvl in range(1, C.bit_length() - 1):
            off = jnp.where(_sibling_mask(ri, ci, lvl), low, 0.0)
            inv = inv - _mm_f32(inv, _mm_f32(off, inv))
        eg = jnp.exp(gcol)
        rhs = jnp.concatenate([bcol * v, (bcol * eg) * k], axis=1)
        uw = _mm_f32(inv, rhs)
        S = s_scr[h]
        Sb = S.astype(BF16)
        un = uw[:, :hd] - _mm(uw[:, hd:].astype(BF16), Sb)
        ub = un.astype(BF16)
        o = eg * _mm(qb, Sb) + _mm((dec * qk).astype(BF16), ub)
        glast = g[C - 1:C, col:col + 1]
        kd = k * jnp.exp(glast - gcol)
        s_scr[h] = jnp.exp(glast) * S + _tn(kd.astype(BF16), ub)
        on = o * lax.rsqrt(jnp.mean(o * o, axis=-1, keepdims=True) + RMS_EPS) * ng
        zz = z[:, h * hd:(h + 1) * hd]
        res = on * (zz * jax.nn.sigmoid(zz))
        o_ref[:, h * hd:(h + 1) * hd] = res[:c_in]

    @pl.when(c == pl.num_programs(1) - 1)
    def _():
        sfin_ref[...] = s_scr[...]


def _gdn(parts, prev, s0, cw, hp, ng, *, batch, n_chunks, c_in, n_valid, shared_init):
    n = parts.shape[0]
    C = max(c_in, GDN_CHUNK)
    init = (lambda b, c: (0, 0, 0)) if shared_init else (lambda b, c: (b, 0, 0))
    init4 = (lambda b, c: (0, 0, 0, 0)) if shared_init else (lambda b, c: (b, 0, 0, 0))
    qkv_w = 3 * GDN_HEADS * GDN_D
    v_w = GDN_HEADS * GDN_D
    return pl.pallas_call(
        functools.partial(_gdn_kernel, c_in=c_in, n_valid=n_valid),
        grid=(batch, n_chunks),
        in_specs=[
            pl.BlockSpec((c_in, qkv_w), lambda b, c: (b * n_chunks + c, COL_QKV // qkv_w)),
            pl.BlockSpec((c_in, v_w), lambda b, c: (b * n_chunks + c, COL_Z // v_w)),
            pl.BlockSpec((c_in, LANE), lambda b, c: (b * n_chunks + c, COL_SM // LANE)),
            pl.BlockSpec((None, SUBLANE, qkv_w), init),
            pl.BlockSpec((None, GDN_HEADS, GDN_D, GDN_D), init4),
            pl.BlockSpec((CONV_W, qkv_w), lambda b, c: (0, 0)),
            pl.BlockSpec((2, LANE), lambda b, c: (0, 0)),
            pl.BlockSpec((1, GDN_D), lambda b, c: (0, 0)),
        ],
        out_specs=[
            pl.BlockSpec((c_in, v_w), lambda b, c: (b * n_chunks + c, 0)),
            pl.BlockSpec((None, GDN_HEADS, GDN_D, GDN_D), lambda b, c: (b, 0, 0, 0)),
        ],
        out_shape=[jax.ShapeDtypeStruct((n, v_w), F32),
                   jax.ShapeDtypeStruct((batch, GDN_HEADS, GDN_D, GDN_D), F32)],
        scratch_shapes=[pltpu.VMEM((GDN_HEADS, GDN_D, GDN_D), F32),
                        pltpu.VMEM((SUBLANE + C, qkv_w), F32)],
        compiler_params=_cparams(("arbitrary", "arbitrary")),
        name="gated_deltanet",
    )(parts, parts, parts, prev, s0, cw, hp, ng)


def _sort_key(x):
    b = pltpu.bitcast(x + 0.0, I32)
    return b ^ ((b >> 31) & 0x7FFFFFFF)


KEY_NEG_INF = -(2 ** 31) + 0x7FFFFF


def _count_ge(key_tiles, cand):
    tot = None
    for kt in key_tiles:
        c = jnp.sum((kt >= cand).astype(F32), axis=1, keepdims=True)
        tot = c if tot is None else tot + c
    return tot


def _attn_prompt_kernel(q_ref, iq_ref, smq_ref, k_ref, v_ref, smk_ref, mk_ref, mv_ref, msm_ref,
                        o_ref, kbf, vbf, kia, kib, key_scr, bias_scr, s_scr, cut_scr, *, n_sel, n_meta, kt):
    tq = q_ref.shape[0]
    n_real = k_ref.shape[0]
    qb_i = pl.program_id(1)
    nt = (qb_i * tq + tq + kt - 1) // kt
    half = IDX_DIM

    @pl.when(qb_i == 0)
    def _():
        kbf[0:n_real, :] = k_ref[...].astype(BF16)
        vbf[0:n_real, :] = v_ref[...].astype(BF16)
        kbf[n_real:, :] = jnp.zeros((LANE, kbf.shape[1]), BF16)
        vbf[n_real:, :] = jnp.zeros((LANE, vbf.shape[1]), BF16)
        kbf[n_real:n_real + n_meta, :] = mk_ref[...].astype(BF16)
        vbf[n_real:n_real + n_meta, :] = mv_ref[...].astype(BF16)
        lane = lax.broadcasted_iota(I32, (1, LANE), 1)
        sk = smk_ref[...]
        kia[0:n_real, :] = jnp.where(lane < half, sk, 0.0).astype(BF16)
        kib[0:n_real, :] = jnp.where(lane >= half, pltpu.roll(sk, half, axis=1), 0.0).astype(BF16)
        kia[n_real:, :] = jnp.zeros((LANE, LANE), BF16)
        kib[n_real:, :] = jnp.zeros((LANE, LANE), BF16)
        ms = msm_ref[...]
        kia[n_real:n_real + n_meta, :] = jnp.where(lane < half, ms, 0.0).astype(BF16)
        kib[n_real:n_real + n_meta, :] = jnp.where(lane >= half, pltpu.roll(ms, half, axis=1), 0.0).astype(BF16)

    iqb = iq_ref[...].astype(BF16)
    w = smq_ref[...][:, SM_IW:SM_IW + IDX_HEADS] * ((IDX_DIM ** -0.5) * (IDX_HEADS ** -0.5))
    qpos = qb_i * tq + lax.broadcasted_iota(I32, (tq, 1), 0)

    def index_scores(start, width):
        acc = jnp.zeros((tq, width), F32)
        ka = kia[pl.ds(start, width), :]
        kb = kib[pl.ds(start, width), :]
        for p in range(IDX_HEADS // 2):
            qp = iqb[:, p * LANE:(p + 1) * LANE]
            acc = acc + jnp.maximum(_nt(qp, ka), 0.0) * w[:, 2 * p:2 * p + 1]
            acc = acc + jnp.maximum(_nt(qp, kb), 0.0) * w[:, 2 * p + 1:2 * p + 2]
        return acc

    def real_tile(t, carry):
        start = pl.multiple_of(t * kt, kt)
        acc = index_scores(start, kt)
        kpos = start + lax.broadcasted_iota(I32, (1, kt), 1)
        key_scr[:, pl.ds(start, kt)] = _sort_key(jnp.where(kpos <= qpos, acc, NEG_INF))
        return carry

    lax.fori_loop(0, nt, real_tile, 0)
    mlane = lax.broadcasted_iota(I32, (1, LANE), 1)
    macc = index_scores(n_real, LANE)
    key_scr[:, n_real:n_real + LANE] = _sort_key(jnp.where(mlane < n_meta, macc, NEG_INF))

    def count_ge(cand):
        candb = jnp.broadcast_to(cand, (tq, kt))

        def body(t, acc):
            start = pl.multiple_of(t * kt, kt)
            return acc + (key_scr[:, pl.ds(start, kt)] >= candb).astype(F32)

        acc = lax.fori_loop(0, nt, body, jnp.zeros((tq, kt), F32))
        macc_ = (key_scr[:, n_real:n_real + LANE] >= candb[:, :LANE]).astype(F32)
        return jnp.sum(acc, axis=1, keepdims=True) + jnp.sum(macc_, axis=1, keepdims=True)

    def bit_body(it, thr):
        cand = thr + lax.shift_left(jnp.int32(1), 31 - it)
        return jnp.where(count_ge(cand) >= n_sel, cand, thr)

    thr = lax.fori_loop(0, 32, bit_body, jnp.full((tq, 1), INT_MIN, I32))
    n_ge = count_ge(thr)
    tie = (n_ge > n_sel) & (thr > KEY_NEG_INF)
    n_pos = n_real + n_meta

    def pos_of(start, width, is_meta):
        lane = lax.broadcasted_iota(I32, (1, width), 1)
        return lane if is_meta else n_meta + start + lane

    def count_eq_le(cut):
        def body(t, acc):
            start = pl.multiple_of(t * kt, kt)
            kk = key_scr[:, pl.ds(start, kt)]
            return acc + ((kk == thr) & (pos_of(start, kt, False) <= cut)).astype(F32)

        acc = lax.fori_loop(0, nt, body, jnp.zeros((tq, kt), F32))
        mk = key_scr[:, n_real:n_real + LANE]
        macc_ = ((mk == thr) & (pos_of(0, LANE, True) <= cut)).astype(F32)
        return jnp.sum(acc, axis=1, keepdims=True) + jnp.sum(macc_, axis=1, keepdims=True)

    cut_scr[...] = jnp.full(cut_scr.shape, n_pos, I32)

    @pl.when(jnp.max(tie.astype(F32)) > 0.0)
    def _tie_cut():
        n_gt = count_ge(thr + 1)
        need = n_sel - n_gt
        nbits = max(1, (n_pos - 1).bit_length())

        def body(it, cut):
            cand = cut - lax.shift_left(jnp.int32(1), nbits - 1 - it)
            ok = (cand >= 0) & (count_eq_le(cand) >= need)
            return jnp.where(ok, cand, cut)

        cut = lax.fori_loop(0, nbits, body, jnp.full((tq, 1), (1 << nbits) - 1, I32))
        cut_scr[...] = jnp.broadcast_to(jnp.where(tie, cut, n_pos), cut_scr.shape)

    cut = cut_scr[:, 0:1]

    def bias_tile(start, width, is_meta):
        kk = key_scr[:, pl.ds(start, width)]
        sel = (kk > thr) | ((kk == thr) & (pos_of(start, width, is_meta) <= cut))
        sel = sel & (kk > KEY_NEG_INF)
        bias_scr[:, pl.ds(start, width)] = jnp.where(sel, 0.0, NEG_INF)

    def bias_body(t, carry):
        bias_tile(pl.multiple_of(t * kt, kt), kt, False)
        return carry

    lax.fori_loop(0, nt, bias_body, 0)
    bias_tile(n_real, LANE, True)

    scale = ATT_DH ** -0.5
    group = ATT_HEADS // ATT_KV_HEADS
    for h in range(ATT_HEADS):
        g = h // group
        qh = q_ref[:, h * ATT_DH:(h + 1) * ATT_DH].astype(BF16)

        def scores(start, width):
            s = _nt(qh, kbf[pl.ds(start, width), g * ATT_DH:(g + 1) * ATT_DH]) * scale
            s = s + bias_scr[:, pl.ds(start, width)]
            s_scr[:, pl.ds(start, width)] = s
            return jnp.max(s, axis=1, keepdims=True)

        def pass1(t, m):
            return jnp.maximum(m, scores(pl.multiple_of(t * kt, kt), kt))

        m = lax.fori_loop(0, nt, pass1, scores(n_real, LANE))

        def probs(start, width):
            p = jnp.exp(s_scr[:, pl.ds(start, width)] - m)
            pv = _mm(p.astype(BF16), vbf[pl.ds(start, width), g * ATT_DH:(g + 1) * ATT_DH])
            return jnp.sum(p, axis=1, keepdims=True), pv

        def pass2(t, carry):
            l, acc = carry
            dl, dacc = probs(pl.multiple_of(t * kt, kt), kt)
            return l + dl, acc + dacc

        l, acc = lax.fori_loop(0, nt, pass2, probs(n_real, LANE))
        o_ref[:, h * ATT_DH:(h + 1) * ATT_DH] = acc / l


def _attn_prompt(parts, mparts, *, batch, seq, n_meta, tq, kt):
    nqb = seq // tq
    n_sel = min(TOPK_MAX, (seq + n_meta) // 4)
    kvw = ATT_KV_HEADS * ATT_DH
    nk = seq + LANE
    return pl.pallas_call(
        functools.partial(_attn_prompt_kernel, n_sel=n_sel, n_meta=n_meta, kt=kt),
        grid=(batch, nqb),
        in_specs=[
            pl.BlockSpec((tq, D), lambda b, i: (b * nqb + i, COL_AQ // D)),
            pl.BlockSpec((tq, D), lambda b, i: (b * nqb + i, COL_IQ // D)),
            pl.BlockSpec((tq, LANE), lambda b, i: (b * nqb + i, COL_SM // LANE)),
            pl.BlockSpec((seq, kvw), lambda b, i: (b, COL_AK // kvw)),
            pl.BlockSpec((seq, kvw), lambda b, i: (b, COL_AV // kvw)),
            pl.BlockSpec((seq, LANE), lambda b, i: (b, COL_SM // LANE)),
            pl.BlockSpec((n_meta, kvw), lambda b, i: (0, COL_AK // kvw)),
            pl.BlockSpec((n_meta, kvw), lambda b, i: (0, COL_AV // kvw)),
            pl.BlockSpec((n_meta, LANE), lambda b, i: (0, COL_SM // LANE)),
        ],
        out_specs=pl.BlockSpec((tq, D), lambda b, i: (b * nqb + i, 0)),
        out_shape=jax.ShapeDtypeStruct((batch * seq, D), F32),
        scratch_shapes=[pltpu.VMEM((nk, kvw), BF16), pltpu.VMEM((nk, kvw), BF16),
                        pltpu.VMEM((nk, LANE), BF16), pltpu.VMEM((nk, LANE), BF16),
                        pltpu.VMEM((tq, nk), I32), pltpu.VMEM((tq, nk), F32), pltpu.VMEM((tq, nk), F32),
                        pltpu.VMEM((tq, LANE), I32)],
        compiler_params=_cparams(("arbitrary", "arbitrary")),
        name="sparse_attention_prompt",
    )(parts, parts, parts, parts, parts, parts, mparts, mparts, mparts)


def _attn_sample_kernel(pt_ref, *refs, n_pages, n_sel):
    kp = refs[0:n_pages]
    vp = refs[n_pages:2 * n_pages]
    ip = refs[2 * n_pages:3 * n_pages]
    q_ref, iq_ref, kn_ref, vn_ref, sm_ref, o_ref, kbf, vbf, kibf = refs[3 * n_pages:]
    del pt_ref
    t_new = q_ref.shape[0]
    page = kp[0].shape[0]
    past = n_pages * page
    for j in range(n_pages):
        kbf[j * page:(j + 1) * page, :] = kp[j][...].astype(BF16)
        vbf[j * page:(j + 1) * page, :] = vp[j][...].astype(BF16)
        kibf[j * page:(j + 1) * page, :] = ip[j][...].astype(BF16)
    sm = sm_ref[...]
    zpad = jnp.zeros((LANE - t_new, kbf.shape[1]), BF16)
    kbf[past:, :] = jnp.concatenate([kn_ref[...].astype(BF16), zpad], axis=0)
    vbf[past:, :] = jnp.concatenate([vn_ref[...].astype(BF16), zpad], axis=0)
    kibf[past:, :] = jnp.concatenate(
        [sm[:, SM_IK:SM_IK + IDX_DIM].astype(BF16), jnp.zeros((LANE - t_new, IDX_DIM), BF16)], axis=0)

    iq = iq_ref[...]
    qi = jnp.concatenate([iq[:, h * IDX_DIM:(h + 1) * IDX_DIM] for h in range(IDX_HEADS)], axis=0).astype(BF16)
    wcol = jnp.concatenate([sm[:, SM_IW + h:SM_IW + h + 1] for h in range(IDX_HEADS)], axis=0)
    wcol = wcol * ((IDX_DIM ** -0.5) * (IDX_HEADS ** -0.5))

    def idx_scores(start, width):
        s = jnp.maximum(_nt(qi, kibf[start:start + width, :]), 0.0) * wcol
        acc = s[0:t_new]
        for h in range(1, IDX_HEADS):
            acc = acc + s[h * t_new:(h + 1) * t_new]
        return acc

    i_past = idx_scores(0, past)
    i_new = idx_scores(past, LANE)
    trow = lax.broadcasted_iota(I32, (t_new, LANE), 0)
    tlane = lax.broadcasted_iota(I32, (t_new, LANE), 1)
    key_past = _sort_key(i_past)
    key_new = _sort_key(jnp.where(tlane <= trow, i_new, NEG_INF))
    tiles = [key_past, key_new]

    def bit_body(it, thr):
        cand = thr + lax.shift_left(jnp.int32(1), 31 - it)
        return jnp.where(_count_ge(tiles, cand) >= n_sel, cand, thr)

    thr = lax.fori_loop(0, 32, bit_body, jnp.full((t_new, 1), INT_MIN, I32))
    n_ge = _count_ge(tiles, thr)
    n_gt = _count_ge(tiles, thr + 1)
    tie = (n_ge > n_sel) & (thr > KEY_NEG_INF)
    need = n_sel - n_gt
    pos_past = lax.broadcasted_iota(I32, (1, past), 1)
    pos_new = past + lax.broadcasted_iota(I32, (1, LANE), 1)
    n_pos = past + t_new
    nbits = max(1, (n_pos - 1).bit_length())

    def cut_body(it, cut):
        cand = cut - lax.shift_left(jnp.int32(1), nbits - 1 - it)
        cnt = (jnp.sum(((key_past == thr) & (pos_past <= cand)).astype(F32), axis=1, keepdims=True)
               + jnp.sum(((key_new == thr) & (pos_new <= cand)).astype(F32), axis=1, keepdims=True))
        ok = (cand >= 0) & (cnt >= need)
        return jnp.where(ok, cand, cut)

    cut = lax.fori_loop(0, nbits, cut_body, jnp.full((t_new, 1), (1 << nbits) - 1, I32))
    cut = jnp.where(tie, cut, n_pos)

    def bias_of(kk, pos):
        sel = ((kk > thr) | ((kk == thr) & (pos <= cut))) & (kk > KEY_NEG_INF)
        return jnp.where(sel, 0.0, NEG_INF)

    group = ATT_HEADS // ATT_KV_HEADS
    b_past = jnp.concatenate([bias_of(key_past, pos_past)] * group, axis=0)
    b_new = jnp.concatenate([bias_of(key_new, pos_new)] * group, axis=0)
    scale = ATT_DH ** -0.5
    q = q_ref[...]
    for g in range(ATT_KV_HEADS):
        qg = jnp.concatenate(
            [q[:, (g * group + j) * ATT_DH:(g * group + j + 1) * ATT_DH] for j in range(group)], axis=0).astype(BF16)
        kg = kbf[:, g * ATT_DH:(g + 1) * ATT_DH]
        vg = vbf[:, g * ATT_DH:(g + 1) * ATT_DH]
        s_p = _nt(qg, kg[0:past]) * scale + b_past
        s_n = _nt(qg, kg[past:]) * scale + b_new
        m = jnp.maximum(jnp.max(s_p, axis=1, keepdims=True), jnp.max(s_n, axis=1, keepdims=True))
        p_p = jnp.exp(s_p - m)
        p_n = jnp.exp(s_n - m)
        l = jnp.sum(p_p, axis=1, keepdims=True) + jnp.sum(p_n, axis=1, keepdims=True)
        acc = _mm(p_p.astype(BF16), vg[0:past]) + _mm(p_n.astype(BF16), vg[past:])
        res = acc / l
        for j in range(group):
            hh = g * group + j
            o_ref[:, hh * ATT_DH:(hh + 1) * ATT_DH] = res[j * t_new:(j + 1) * t_new]


def _attn_sample(parts, cache_k, cache_v, cache_ik, page_table, *, t_new):
    db, n_pages = page_table.shape
    page = cache_k.shape[1]
    kvw = ATT_KV_HEADS * ATT_DH
    ck = cache_k.reshape(cache_k.shape[0], page, kvw)
    cv = cache_v.reshape(cache_v.shape[0], page, kvw)
    past = n_pages * page
    n_sel = min(TOPK_MAX, (past + t_new) // 4)

    def page_spec(width, j):
        return pl.BlockSpec((None, page, width), lambda b, pt, j=j: (pt[b, j], 0, 0))

    in_specs = ([page_spec(kvw, j) for j in range(n_pages)] + [page_spec(kvw, j) for j in range(n_pages)]
                + [page_spec(IDX_DIM, j) for j in range(n_pages)]
                + [pl.BlockSpec((t_new, D), lambda b, pt: (b, COL_AQ // D)),
                   pl.BlockSpec((t_new, D), lambda b, pt: (b, COL_IQ // D)),
                   pl.BlockSpec((t_new, kvw), lambda b, pt: (b, COL_AK // kvw)),
                   pl.BlockSpec((t_new, kvw), lambda b, pt: (b, COL_AV // kvw)),
                   pl.BlockSpec((t_new, LANE), lambda b, pt: (b, COL_SM // LANE))])
    grid_spec = pltpu.PrefetchScalarGridSpec(
        num_scalar_prefetch=1, grid=(db,), in_specs=in_specs,
        out_specs=pl.BlockSpec((t_new, D), lambda b, pt: (b, 0)),
        scratch_shapes=[pltpu.VMEM((past + LANE, kvw), BF16), pltpu.VMEM((past + LANE, kvw), BF16),
                        pltpu.VMEM((past + LANE, IDX_DIM), BF16)])
    return pl.pallas_call(
        functools.partial(_attn_sample_kernel, n_pages=n_pages, n_sel=n_sel),
        grid_spec=grid_spec,
        out_shape=jax.ShapeDtypeStruct((db * t_new, D), F32),
        compiler_params=_cparams(("arbitrary",)),
        name="sparse_attention_sample",
    )(page_table, *([ck] * n_pages), *([cv] * n_pages), *([cache_ik] * n_pages), parts, parts, parts, parts, parts)


def _layer_norm(x, g, b):
    mu = jnp.mean(x, axis=-1, keepdims=True)
    xc = x - mu
    var = jnp.mean(xc * xc, axis=-1, keepdims=True)
    return xc * lax.rsqrt(var + LN_EPS) * g + b


def _finish_kernel(x_ref, og_ref, oa_ref, ga_ref, gb_ref, wbg_ref, wba_ref, wo_ref, g_ref, b_ref, h_ref, *, alpha):
    a = _mm(og_ref[...].astype(BF16), wbg_ref[...])
    b = _mm(oa_ref[...].astype(BF16), wba_ref[...])
    merged = jax.nn.sigmoid(ga_ref[...]) * a + jax.nn.sigmoid(gb_ref[...]) * b
    y = alpha * x_ref[...] + _mm(merged.astype(BF16), wo_ref[...])
    h_ref[...] = _layer_norm(y, g_ref[...], b_ref[...])


def _finish(x, o_gdn, o_att, parts, wbg, wba, wo, g, b, *, alpha, tm):
    n = x.shape[0]
    tm = min(tm, n)
    row = lambda i: (i, 0)
    full = lambda i: (0, 0)
    return pl.pallas_call(
        functools.partial(_finish_kernel, alpha=alpha),
        grid=(n // tm,),
        in_specs=[pl.BlockSpec((tm, D), row), pl.BlockSpec((tm, D), row), pl.BlockSpec((tm, D), row),
                  pl.BlockSpec((tm, D), lambda i: (i, COL_GA // D)), pl.BlockSpec((tm, D), lambda i: (i, COL_GB // D)),
                  pl.BlockSpec((D, D), full), pl.BlockSpec((D, D), full), pl.BlockSpec((D, D), full),
                  pl.BlockSpec((1, D), full), pl.BlockSpec((1, D), full)],
        out_specs=pl.BlockSpec((tm, D), row),
        out_shape=jax.ShapeDtypeStruct((n, D), F32),
        compiler_params=_cparams(("arbitrary",)),
        name="merge_layernorm",
    )(x, o_gdn, o_att, parts, parts, wbg, wba, wo, g, b)


_CAND = [(r0, r1) for r0 in range(PEER_TOPK) for r1 in range(PEER_TOPK) if (r0 + 1) * (r1 + 1) <= PEER_TOPK]
_CAND_OFF = [next(i for i, c in enumerate(_CAND) if c[0] == r0) for r0 in range(PEER_TOPK)]
_CAND_LEN = [sum(1 for c in _CAND if c[0] == r0) for r0 in range(PEER_TOPK)]
_CAND_ROWS = -(-len(_CAND) // SUBLANE) * SUBLANE


def _top_rows(s, n_top, vals_ref):
    rows, cols = s.shape
    iota = lax.broadcasted_iota(I32, (rows, cols), 0).astype(F32)

    def body(r, carry):
        s, rank = carry
        m = jnp.max(s, axis=0, keepdims=True)
        first = jnp.min(jnp.where(s == m, iota, float(rows)), axis=0, keepdims=True)
        hit = iota == first
        vals_ref[pl.ds(r, 1), :] = m
        return jnp.where(hit, NEG_INF, s), jnp.where(hit, r.astype(F32), rank)

    _, rank = lax.fori_loop(0, n_top, body, (s, jnp.full((rows, cols), float(n_top), F32)))
    return rank


def _peer_kernel(h_ref, wq_ref, sk_ref, u_ref, vt_ref, g_ref, b_ref, y_ref,
                 hb_scr, q_scr, rank1_scr, bt_scr, nt_scr, at_scr, va_scr, vb_scr, vc_scr,
                 s_scr, p_scr, acc_scr, *, alpha):
    tt = h_ref.shape[0]
    eb = u_ref.shape[0]
    e = pl.program_id(1)
    nk = PEER_NKEYS
    kk = PEER_TOPK

    @pl.when(e == 0)
    def _prep():
        hb = h_ref[...].astype(BF16)
        hb_scr[...] = hb
        q_scr[...] = _mm(hb, wq_ref[...]).astype(BF16)
        acc_scr[...] = jnp.zeros(acc_scr.shape, F32)

        def head(hd, carry):
            q0 = q_scr[:, pl.ds(pl.multiple_of(hd * 2 * nk, nk), nk)]
            q1 = q_scr[:, pl.ds(pl.multiple_of(hd * 2 * nk + nk, nk), nk)]
            s0 = _nt(sk_ref[hd, 0], q0)
            s1 = _nt(sk_ref[hd, 1], q1)
            rank0 = _top_rows(s0, kk, va_scr)
            rank1 = _top_rows(s1, kk, vb_scr)
            a = va_scr[...]
            b = vb_scr[...]
            pad = [jnp.full((_CAND_ROWS - len(_CAND), tt), NEG_INF, F32)]
            cand = jnp.concatenate([a[r0:r0 + 1] + b[r1:r1 + 1] for r0, r1 in _CAND] + pad, axis=0)
            crank = _top_rows(cand, kk, vc_scr)
            top = vc_scr[...]
            zsum = jnp.sum(jnp.exp(top - top[0:1]), axis=0, keepdims=True)
            chosen = (crank < float(kk)).astype(F32)
            crow = lax.broadcasted_iota(I32, (_CAND_ROWS, 1), 0)
            nsel = jnp.zeros((nk, tt), F32)
            for r0 in range(kk):
                in_group = (crow >= _CAND_OFF[r0]) & (crow < _CAND_OFF[r0] + _CAND_LEN[r0])
                cnt = jnp.sum(jnp.where(in_group, chosen, 0.0), axis=0, keepdims=True)
                nsel = nsel + jnp.where(rank0 == float(r0), cnt, 0.0)
            rank1_scr[hd] = rank1
            nt_scr[hd] = nsel
            at_scr[hd] = jnp.exp(s0 - a[0:1])
            bt_scr[hd] = jnp.exp(s1 - b[0:1]) / zsum
            return carry

        lax.fori_loop(0, PEER_HEADS, head, 0)

    s_scr[...] = _nt(u_ref[...], hb_scr[...])
    groups = eb // nk
    tchunks = tt // LANE

    def tile(idx, carry):
        ii = idx // tchunks
        tc = idx % tchunks
        i = e * groups + ii
        r0 = pl.multiple_of(ii * nk, nk)
        c0 = pl.multiple_of(tc * LANE, LANE)
        gate = jnp.zeros((nk, LANE), F32)
        base = pl.multiple_of((i // SUBLANE) * SUBLANE, SUBLANE)
        pick = lax.broadcasted_iota(I32, (SUBLANE, LANE), 0) == i % SUBLANE
        for hd in range(PEER_HEADS):
            nblk = nt_scr[hd, pl.ds(base, SUBLANE), pl.ds(c0, LANE)]
            ablk = at_scr[hd, pl.ds(base, SUBLANE), pl.ds(c0, LANE)]
            nrow = jnp.sum(jnp.where(pick, nblk, 0.0), axis=0, keepdims=True)
            arow = jnp.sum(jnp.where(pick, ablk, 0.0), axis=0, keepdims=True)
            gate = gate + jnp.where(rank1_scr[hd, :, pl.ds(c0, LANE)] < nrow, bt_scr[hd, :, pl.ds(c0, LANE)], 0.0) * arow
        s = s_scr[pl.ds(r0, nk), pl.ds(c0, LANE)]
        act = 0.5 * s * (1.0 + lax.erf(s * (2.0 ** -0.5)))
        p_scr[pl.ds(r0, nk), pl.ds(c0, LANE)] = (gate * act).astype(BF16)
        return carry

    lax.fori_loop(0, groups * tchunks, tile, 0)
    acc_scr[...] += _mm(vt_ref[...], p_scr[...])

    @pl.when(e == pl.num_programs(1) - 1)
    def _():
        y = alpha * h_ref[...] + acc_scr[...].T
        y_ref[...] = _layer_norm(y, g_ref[...], b_ref[...])


def _peer(h, wq, sk, u, vt, g, b, *, alpha, tt, eb):
    n = h.shape[0]
    n_exp = u.shape[0]
    nk = PEER_NKEYS
    return pl.pallas_call(
        functools.partial(_peer_kernel, alpha=alpha),
        grid=(n // tt, n_exp // eb),
        in_specs=[pl.BlockSpec((tt, D), lambda t, e: (t, 0)),
                  pl.BlockSpec(wq.shape, lambda t, e: (0, 0)),
                  pl.BlockSpec(sk.shape, lambda t, e: (0, 0, 0, 0)),
                  pl.BlockSpec((eb, D), lambda t, e: (e, 0)),
                  pl.BlockSpec((D, eb), lambda t, e: (0, e)),
                  pl.BlockSpec((1, D), lambda t, e: (0, 0)),
                  pl.BlockSpec((1, D), lambda t, e: (0, 0))],
        out_specs=pl.BlockSpec((tt, D), lambda t, e: (t, 0)),
        out_shape=jax.ShapeDtypeStruct((n, D), F32),
        scratch_shapes=[pltpu.VMEM((tt, D), BF16), pltpu.VMEM((tt, wq.shape[1]), BF16),
                        pltpu.VMEM((PEER_HEADS, nk, tt), F32), pltpu.VMEM((PEER_HEADS, nk, tt), F32),
                        pltpu.VMEM((PEER_HEADS, nk, tt), F32), pltpu.VMEM((PEER_HEADS, nk, tt), F32),
                        pltpu.VMEM((PEER_TOPK, tt), F32), pltpu.VMEM((PEER_TOPK, tt), F32),
                        pltpu.VMEM((PEER_TOPK, tt), F32),
                        pltpu.VMEM((eb, tt), F32), pltpu.VMEM((eb, tt), BF16), pltpu.VMEM((D, tt), F32)],
        compiler_params=_cparams(("arbitrary", "arbitrary")),
        name="peer_ffn",
    )(h, wq, sk, u, vt, g, b)


def _permute_w_in(w):
    gq = 3 * GDN_HEADS * GDN_D
    gv = GDN_HEADS * GDN_D
    aq = ATT_HEADS * ATT_DH
    akv = ATT_KV_HEADS * ATT_DH
    iqw = IDX_HEADS * IDX_DIM
    sizes = (gq, gv, GDN_HEADS, GDN_HEADS, aq, akv, akv, iqw, IDX_DIM, IDX_HEADS, D, D)
    offs = [0]
    for s in sizes:
        offs.append(offs[-1] + s)
    seg = [w[:, offs[i]:offs[i + 1]] for i in range(len(sizes))]
    qkv, z, b, a, q, k, v, iq, ik, iw, ga, gb = seg
    pad = jnp.zeros((w.shape[0], LANE - IDX_DIM - 2 * GDN_HEADS - IDX_HEADS), w.dtype)
    return jnp.concatenate([qkv, z, q, iq, ga, gb, k, v, ik, b, a, iw, pad], axis=1).astype(BF16)


def kernel(x_prompt, x_sample, cache_k, cache_v, cache_idx_k, state_conv, state_delta, page_table, meta_tokens, w_in, conv_w, a_log, dt_bias, gdn_norm_g, w_branch_gdn, w_branch_attn, w_out, ln1_g, ln1_b, peer_wq, peer_subkeys, peer_u, peer_v, ln2_g, ln2_b):
    depth = w_in.shape[0]
    assert depth == 1, "single-layer step"
    batch, seq, d = x_prompt.shape
    db, t_new, _ = x_sample.shape
    n_meta = meta_tokens.shape[0]
    assert d == D and seq % GDN_CHUNK == 0 and n_meta % SUBLANE == 0 and n_meta <= GDN_CHUNK
    alpha = (2 * depth) ** 0.25
    qkv_w = 3 * GDN_HEADS * GDN_D
    kvw = ATT_KV_HEADS * ATT_DH

    w_r = _permute_w_in(w_in[0])
    xp = x_prompt.reshape(batch * seq, D)
    xs = x_sample.reshape(db * t_new, D)
    tm = 512 if (batch * seq) % 512 == 0 else 256
    parts_p = _project(xp, w_r, tm)
    parts_s = _project(xs, w_r, min(tm, db * t_new))
    parts_m = _project(meta_tokens.astype(F32), w_r, n_meta)

    hp = jnp.zeros((2, LANE), F32).at[0, SM_A:SM_A + GDN_HEADS].set(a_log[0]).at[1, SM_A:SM_A + GDN_HEADS].set(dt_bias[0])
    cw = conv_w[0]
    ng = gdn_norm_g[0].reshape(1, GDN_D)
    zero_prev = jnp.zeros((1, SUBLANE, qkv_w), F32)
    zero_state = jnp.zeros((1, GDN_HEADS, GDN_D, GDN_D), F32)
    _, s_meta = _gdn(parts_m, zero_prev, zero_state, cw, hp, ng, batch=1, n_chunks=1, c_in=n_meta, n_valid=n_meta,
                     shared_init=True)
    meta_prev = parts_m[n_meta - SUBLANE:, :qkv_w].reshape(1, SUBLANE, qkv_w)
    og_p, p_delta = _gdn(parts_p, meta_prev, s_meta, cw, hp, ng, batch=batch, n_chunks=seq // GDN_CHUNK,
                         c_in=GDN_CHUNK, n_valid=GDN_CHUNK, shared_init=True)
    samp_prev = jnp.pad(state_conv[0], ((0, 0), (SUBLANE - (CONV_W - 1), 0), (0, 0)))
    og_s, s_delta = _gdn(parts_s, samp_prev, state_delta[0].astype(F32), cw, hp, ng, batch=db, n_chunks=1,
                         c_in=t_new, n_valid=t_new, shared_init=False)

    tq = 128
    oa_p = _attn_prompt(parts_p, parts_m, batch=batch, seq=seq, n_meta=n_meta, tq=tq, kt=256)
    oa_s = _attn_sample(parts_s, cache_k[0], cache_v[0], cache_idx_k[0], page_table, t_new=t_new)

    wbg = w_branch_gdn[0].astype(BF16)
    wba = w_branch_attn[0].astype(BF16)
    wo = w_out[0].astype(BF16)
    g1 = ln1_g[0].reshape(1, D)
    b1 = ln1_b[0].reshape(1, D)
    h_p = _finish(xp, og_p, oa_p, parts_p, wbg, wba, wo, g1, b1, alpha=alpha, tm=256)
    h_s = _finish(xs, og_s, oa_s, parts_s, wbg, wba, wo, g1, b1, alpha=alpha, tm=256)

    wq = peer_wq[0].astype(BF16)
    sk = peer_subkeys[0].astype(BF16)
    u = peer_u[0].astype(BF16)
    vt = peer_v[0].astype(BF16).T
    g2 = ln2_g[0].reshape(1, D)
    b2 = ln2_b[0].reshape(1, D)
    tt = 512 if (batch * seq) % 512 == 0 else 256
    y_p = _peer(h_p, wq, sk, u, vt, g2, b2, alpha=alpha, tt=tt, eb=512)
    y_s = _peer(h_s, wq, sk, u, vt, g2, b2, alpha=alpha, tt=min(tt, db * t_new), eb=512)

    y_prompt = y_p.reshape(batch, seq, D)
    y_sample = y_s.reshape(db, t_new, D)
    pp = parts_p.reshape(batch, seq, N_COLS)
    ps = parts_s.reshape(db, t_new, N_COLS)
    p_conv = pp[:, seq - (CONV_W - 1):, :qkv_w][None]
    s_conv = jnp.concatenate([state_conv[0].astype(F32), ps[:, :, :qkv_w]], axis=1)[:, -(CONV_W - 1):][None]

    def with_meta(col, width):
        m = jnp.broadcast_to(parts_m[None, :, col:col + width], (batch, n_meta, width))
        return jnp.concatenate([m, pp[:, :, col:col + width]], axis=1)

    p_k = with_meta(COL_AK, kvw).reshape(1, batch, seq + n_meta, ATT_KV_HEADS, ATT_DH)
    p_v = with_meta(COL_AV, kvw).reshape(1, batch, seq + n_meta, ATT_KV_HEADS, ATT_DH)
    p_idx_k = with_meta(COL_SM + SM_IK, IDX_DIM)[None]
    s_k = ps[:, :, COL_AK:COL_AK + kvw].reshape(1, db, t_new, ATT_KV_HEADS, ATT_DH)
    s_v = ps[:, :, COL_AV:COL_AV + kvw].reshape(1, db, t_new, ATT_KV_HEADS, ATT_DH)
    s_idx_k = ps[:, :, COL_SM + SM_IK:COL_SM + SM_IK + IDX_DIM][None]
    return (y_prompt, y_sample, p_conv, p_delta[None], p_k, p_v, p_idx_k, s_conv, s_delta[None].astype(state_delta.dtype),
            s_k, s_v, s_idx_k)
```

```python
import functools

import jax
import jax.numpy as jnp
from jax import lax
from jax.experimental import pallas as pl
from jax.experimental.pallas import tpu as pltpu

F32 = jnp.float32
BF16 = jnp.bfloat16
I32 = jnp.int32

GDN_HEADS = 8
GDN_D = 128
CONV_W = 4
GDN_CHUNK = 64
ATT_HEADS = 8
ATT_KV_HEADS = 2
ATT_DH = 128
IDX_HEADS = 16
IDX_DIM = 64
TOPK_MAX = 256
PEER_HEADS = 8
PEER_NKEYS = 128
PEER_TOPK = 16
LN_EPS = 1e-5
RMS_EPS = 1e-6

LANE = 128
SUBLANE = 8
VMEM_LIMIT = 56 * 1024 * 1024

D = 1024
COL_QKV = 0
COL_Z = 3072
COL_AQ = 4096
COL_IQ = 5120
COL_GA = 6144
COL_GB = 7168
COL_AK = 8192
COL_AV = 8448
COL_SM = 8704
N_COLS = 8832
SM_IK = 0
SM_B = 64
SM_A = 72
SM_IW = 80

NEG_INF = float("-inf")
INT_MIN = -(2 ** 31)


def _nt(a, b):
    return lax.dot_general(a, b, (((1,), (1,)), ((), ())), preferred_element_type=F32)


def _tn(a, b):
    return lax.dot_general(a, b, (((0,), (0,)), ((), ())), preferred_element_type=F32)


def _mm(a, b):
    return jnp.dot(a, b, preferred_element_type=F32)


def _mm_f32(a, b):
    return jnp.dot(a, b, preferred_element_type=F32, precision=lax.Precision.HIGHEST)


def _cparams(sem):
    return pltpu.CompilerParams(dimension_semantics=sem, vmem_limit_bytes=VMEM_LIMIT)


def _proj_kernel(x_ref, w_ref, o_ref):
    o_ref[...] = _mm(x_ref[...].astype(BF16), w_ref[...])


def _project(x, w_bf, tm):
    n = x.shape[0]
    tn = N_COLS // 3
    return pl.pallas_call(
        _proj_kernel,
        grid=(3, n // tm),
        in_specs=[pl.BlockSpec((tm, D), lambda j, i: (i, 0)),
                  pl.BlockSpec((D, tn), lambda j, i: (0, j))],
        out_specs=pl.BlockSpec((tm, tn), lambda j, i: (i, j)),
        out_shape=jax.ShapeDtypeStruct((n, N_COLS), F32),
        compiler_params=_cparams(("arbitrary", "arbitrary")),
        name="in_projection",
    )(x, w_bf)


def _sibling_mask(ri, ci, lvl):
    return ((ri >> (lvl + 1)) == (ci >> (lvl + 1))) & (((ri >> lvl) & 1) == 1) & (((ci >> lvl) & 1) == 0)


_BATCH0 = ((0,), (0,))


def _bmm(a, b):
    return lax.dot_general(a, b, (((2,), (1,)), _BATCH0), preferred_element_type=F32)


def _bnt(a, b):
    return lax.dot_general(a, b, (((2,), (2,)), _BATCH0), preferred_element_type=F32)


def _btn(a, b):
    return lax.dot_general(a, b, (((1,), (1,)), _BATCH0), preferred_element_type=F32)


def _split(x):
    hi = x.astype(BF16)
    return hi, (x - hi.astype(F32)).astype(BF16)


def _bmm3(a, b):
    ah, al = a
    bh, bl = b
    return _bmm(ah, bh) + (_bmm(ah, bl) + _bmm(al, bh))


def _gdn_kernel(qkv_ref, z_ref, sm_ref, prev_ref, s0_ref, cw_ref, hp_ref, ng_ref,
                o_ref, sfin_ref, s_scr, ext_scr, *, c_in, n_valid, n_seq):
    nh = GDN_HEADS
    hd = GDN_D
    stride = ext_scr.shape[0] // n_seq
    L = stride - SUBLANE
    C = L * n_seq
    ls = L.bit_length() - 1
    c = pl.program_id(1)

    @pl.when(c == 0)
    def _():
        s_scr[...] = s0_ref[...]
        for b in range(n_seq):
            ext_scr[b * stride:b * stride + SUBLANE, :] = prev_ref[b]

    u = qkv_ref[...]
    sm = sm_ref[...]
    z = z_ref[...]
    if c_in < C:
        u = jnp.concatenate([u, jnp.zeros((C - c_in, u.shape[1]), F32)], axis=0)
        sm = jnp.concatenate([sm, jnp.zeros((C - c_in, LANE), F32)], axis=0)
        z = jnp.concatenate([z, jnp.zeros((C - c_in, z.shape[1]), F32)], axis=0)
    cw = cw_ref[...]
    pieces = []
    for b in range(n_seq):
        base = b * stride + SUBLANE
        ub_ = u[b * L:(b + 1) * L]
        ext_scr[base:base + L, :] = ub_
        acc = ub_ * cw[CONV_W - 1:CONV_W, :]
        for i in range(CONV_W - 1):
            off = base - (CONV_W - 1) + i
            acc = acc + ext_scr[off:off + L, :] * cw[i:i + 1, :]
        pieces.append(acc)
    conv = pieces[0] if n_seq == 1 else jnp.concatenate(pieces, axis=0)
    if n_seq == 1:
        tail = ext_scr[c_in:c_in + SUBLANE, :]
        ext_scr[0:SUBLANE, :] = tail
    qkv = conv * jax.nn.sigmoid(conv)

    row = lax.broadcasted_iota(I32, (C, 1), 0)
    hp = hp_ref[...]
    xs = sm + hp[1:2, :]
    softplus = jnp.maximum(xs, 0.0) + jnp.log1p(jnp.exp(-jnp.abs(xs)))
    la = -jnp.exp(hp[0:1, :]) * softplus
    beta = jax.nn.sigmoid(sm)
    if n_valid < C:
        valid = row < n_valid
        qkv = jnp.where(valid, qkv, 0.0)
        la = jnp.where(valid, la, 0.0)
        beta = jnp.where(valid, beta, 0.0)

    ri = lax.broadcasted_iota(I32, (C, C), 0)
    ci = lax.broadcasted_iota(I32, (C, C), 1)
    same = (ri >> ls) == (ci >> ls)
    incl = same & (ri >= ci)
    strict = same & (ri > ci)
    g = _mm_f32(incl.astype(F32), la)
    g_end = _mm_f32((ci == (ri | (L - 1))).astype(F32), g)
    gt = g.T
    eye = (ri == ci).astype(F32)

    q_l, k_l, kd_l, dec_l, lows_l, rhs_l, eg_l, z_l = [], [], [], [], [], [], [], []
    for h in range(nh):
        col = SM_A + h
        gcol = g[:, col:col + 1]
        grow = gt[col:col + 1, :]
        bcol = beta[:, SM_B + h:SM_B + h + 1]
        q = qkv[:, h * hd:(h + 1) * hd]
        k = qkv[:, nh * hd + h * hd:nh * hd + (h + 1) * hd]
        v = qkv[:, 2 * nh * hd + h * hd:2 * nh * hd + (h + 1) * hd]
        q = q * lax.rsqrt(jnp.sum(q * q, axis=-1, keepdims=True) + RMS_EPS) * (hd ** -0.5)
        k = k * lax.rsqrt(jnp.sum(k * k, axis=-1, keepdims=True) + RMS_EPS)
        dec = jnp.exp(jnp.where(incl, gcol - grow, NEG_INF))
        eg = jnp.exp(gcol)
        q_l.append(q.astype(BF16))
        k_l.append(k.astype(BF16))
        kd_l.append((k * jnp.exp(g_end[:, col:col + 1] - gcol)).astype(BF16))
        dec_l.append(dec)
        lows_l.append(bcol * jnp.where(strict, dec, 0.0))
        rhs_l.append(jnp.concatenate([bcol * v, (bcol * eg) * k], axis=1))
        eg_l.append(eg)
        z_l.append(z[:, h * hd:(h + 1) * hd])
    qb = jnp.stack(q_l)
    kb = jnp.stack(k_l)
    kd = jnp.stack(kd_l)
    dec = jnp.stack(dec_l)
    eg = jnp.stack(eg_l)

    low = jnp.stack(lows_l) * _bnt(kb, kb)
    qk = _bnt(qb, kb)
    inv = eye[None] - jnp.where(_sibling_mask(ri, ci, 0)[None], low, 0.0)
    for lvl in range(1, ls):
        off = jnp.where(_sibling_mask(ri, ci, lvl)[None], low, 0.0)
        inv_s = _split(inv)
        inv = inv - _bmm3(inv_s, _split(_bmm3(_split(off), inv_s)))
    uw = _bmm3(_split(inv), _split(jnp.stack(rhs_l)))
    w_b = uw[:, :, hd:].astype(BF16)

    ws = None
    qs = None
    s_old = []
    for b in range(n_seq):
        S = s_scr[b]
        s_old.append(S)
        Sb = S.astype(BF16)
        if n_seq == 1:
            wm, qm = w_b, qb
        else:
            mine = ((row >> ls) == b)[None]
            wm = jnp.where(mine, w_b, jnp.zeros_like(w_b))
            qm = jnp.where(mine, qb, jnp.zeros_like(qb))
        ws = _bmm(wm, Sb) if ws is None else ws + _bmm(wm, Sb)
        qs = _bmm(qm, Sb) if qs is None else qs + _bmm(qm, Sb)
    ub = (uw[:, :, :hd] - ws).astype(BF16)
    o = eg * qs + _bmm((dec * qk).astype(BF16), ub)
    for b in range(n_seq):
        last = b * L + L - 1
        glast = jnp.stack([g[last:last + 1, SM_A + h:SM_A + h + 1] for h in range(nh)])
        kdm = kd if n_seq == 1 else jnp.where(((row >> ls) == b)[None], kd, jnp.zeros_like(kd))
        s_scr[b] = jnp.exp(glast) * s_old[b] + _btn(kdm, ub)

    on = o * lax.rsqrt(jnp.mean(o * o, axis=-1, keepdims=True) + RMS_EPS) * ng_ref[...]
    zz = jnp.stack(z_l)
    res = on * (zz * jax.nn.sigmoid(zz))
    for h in range(nh):
        o_ref[:, h * hd:(h + 1) * hd] = res[h, :c_in]

    @pl.when(c == pl.num_programs(1) - 1)
    def _():
        sfin_ref[...] = s_scr[...]


def _gdn(parts, prev, s0, cw, hp, ng, *, batch, n_chunks, c_in, n_valid, n_seq, shared_init):
    n = parts.shape[0]
    C = max(c_in, GDN_CHUNK)
    assert n_seq == 1 or (n_chunks == 1 and c_in == C)
    L = C // n_seq
    init = (lambda b, c: (0, 0, 0)) if shared_init else (lambda b, c: (b, 0, 0))
    init4 = (lambda b, c: (0, 0, 0, 0)) if shared_init else (lambda b, c: (b, 0, 0, 0))
    qkv_w = 3 * GDN_HEADS * GDN_D
    v_w = GDN_HEADS * GDN_D
    return pl.pallas_call(
        functools.partial(_gdn_kernel, c_in=c_in, n_valid=n_valid, n_seq=n_seq),
        grid=(batch, n_chunks),
        in_specs=[
            pl.BlockSpec((c_in, qkv_w), lambda b, c: (b * n_chunks + c, COL_QKV // qkv_w)),
            pl.BlockSpec((c_in, v_w), lambda b, c: (b * n_chunks + c, COL_Z // v_w)),
            pl.BlockSpec((c_in, LANE), lambda b, c: (b * n_chunks + c, COL_SM // LANE)),
            pl.BlockSpec((n_seq, SUBLANE, qkv_w), init),
            pl.BlockSpec((n_seq, GDN_HEADS, GDN_D, GDN_D), init4),
            pl.BlockSpec((CONV_W, qkv_w), lambda b, c: (0, 0)),
            pl.BlockSpec((2, LANE), lambda b, c: (0, 0)),
            pl.BlockSpec((1, GDN_D), lambda b, c: (0, 0)),
        ],
        out_specs=[
            pl.BlockSpec((c_in, v_w), lambda b, c: (b * n_chunks + c, 0)),
            pl.BlockSpec((n_seq, GDN_HEADS, GDN_D, GDN_D), lambda b, c: (b, 0, 0, 0)),
        ],
        out_shape=[jax.ShapeDtypeStruct((n, v_w), F32),
                   jax.ShapeDtypeStruct((batch * n_seq, GDN_HEADS, GDN_D, GDN_D), F32)],
        scratch_shapes=[pltpu.VMEM((n_seq, GDN_HEADS, GDN_D, GDN_D), F32),
                        pltpu.VMEM((n_seq * (SUBLANE + L), qkv_w), F32)],
        compiler_params=_cparams(("arbitrary", "arbitrary")),
        name="gated_deltanet",
    )(parts, parts, parts, prev, s0, cw, hp, ng)


def _sort_key(x):
    b = pltpu.bitcast(x + 0.0, I32)
    return b ^ ((b >> 31) & 0x7FFFFFFF)


KEY_NEG_INF = -(2 ** 31) + 0x7FFFFF


def _count_ge(key_tiles, cand):
    tot = None
    for kt in key_tiles:
        c = jnp.sum((kt >= cand).astype(F32), axis=1, keepdims=True)
        tot = c if tot is None else tot + c
    return tot


def _attn_prompt_kernel(q_ref, iq_ref, smq_ref, k_ref, v_ref, smk_ref, mk_ref, mv_ref, msm_ref,
                        o_ref, kbf, vbf, kia, kib, key_scr, bias_scr, s_scr, cut_scr, *, n_sel, n_meta, kt):
    tq = q_ref.shape[0]
    n_real = k_ref.shape[0]
    qb_i = pl.program_id(1)
    nt = (qb_i * tq + tq + kt - 1) // kt
    half = IDX_DIM

    @pl.when(qb_i == 0)
    def _():
        kbf[0:n_real, :] = k_ref[...].astype(BF16)
        vbf[0:n_real, :] = v_ref[...].astype(BF16)
        kbf[n_real:, :] = jnp.zeros((LANE, kbf.shape[1]), BF16)
        vbf[n_real:, :] = jnp.zeros((LANE, vbf.shape[1]), BF16)
        kbf[n_real:n_real + n_meta, :] = mk_ref[...].astype(BF16)
        vbf[n_real:n_real + n_meta, :] = mv_ref[...].astype(BF16)
        lane = lax.broadcasted_iota(I32, (1, LANE), 1)
        sk = smk_ref[...]
        kia[0:n_real, :] = jnp.where(lane < half, sk, 0.0).astype(BF16)
        kib[0:n_real, :] = jnp.where(lane >= half, pltpu.roll(sk, half, axis=1), 0.0).astype(BF16)
        kia[n_real:, :] = jnp.zeros((LANE, LANE), BF16)
        kib[n_real:, :] = jnp.zeros((LANE, LANE), BF16)
        ms = msm_ref[...]
        kia[n_real:n_real + n_meta, :] = jnp.where(lane < half, ms, 0.0).astype(BF16)
        kib[n_real:n_real + n_meta, :] = jnp.where(lane >= half, pltpu.roll(ms, half, axis=1), 0.0).astype(BF16)

    iqb = iq_ref[...].astype(BF16)
    w = smq_ref[...][:, SM_IW:SM_IW + IDX_HEADS] * ((IDX_DIM ** -0.5) * (IDX_HEADS ** -0.5))
    qpos = qb_i * tq + lax.broadcasted_iota(I32, (tq, 1), 0)

    def index_scores(start, width):
        acc = jnp.zeros((tq, width), F32)
        ka = kia[pl.ds(start, width), :]
        kb = kib[pl.ds(start, width), :]
        for p in range(IDX_HEADS // 2):
            qp = iqb[:, p * LANE:(p + 1) * LANE]
            acc = acc + jnp.maximum(_nt(qp, ka), 0.0) * w[:, 2 * p:2 * p + 1]
            acc = acc + jnp.maximum(_nt(qp, kb), 0.0) * w[:, 2 * p + 1:2 * p + 2]
        return acc

    def real_tile(t, carry):
        start = pl.multiple_of(t * kt, kt)
        acc = index_scores(start, kt)
        kpos = start + lax.broadcasted_iota(I32, (1, kt), 1)
        key_scr[:, pl.ds(start, kt)] = _sort_key(jnp.where(kpos <= qpos, acc, NEG_INF))
        return carry

    lax.fori_loop(0, nt, real_tile, 0)
    mlane = lax.broadcasted_iota(I32, (1, LANE), 1)
    macc = index_scores(n_real, LANE)
    key_scr[:, n_real:n_real + LANE] = _sort_key(jnp.where(mlane < n_meta, macc, NEG_INF))

    def count_ge(cand):
        candb = jnp.broadcast_to(cand, (tq, kt))

        def body(t, acc):
            start = pl.multiple_of(t * kt, kt)
            return acc + (key_scr[:, pl.ds(start, kt)] >= candb).astype(F32)

        acc = lax.fori_loop(0, nt, body, jnp.zeros((tq, kt), F32))
        macc_ = (key_scr[:, n_real:n_real + LANE] >= candb[:, :LANE]).astype(F32)
        return jnp.sum(acc, axis=1, keepdims=True) + jnp.sum(macc_, axis=1, keepdims=True)

    def bit_body(it, thr):
        cand = thr + lax.shift_left(jnp.int32(1), 31 - it)
        return jnp.where(count_ge(cand) >= n_sel, cand, thr)

    thr = lax.fori_loop(0, 32, bit_body, jnp.full((tq, 1), INT_MIN, I32))
    n_ge = count_ge(thr)
    tie = (n_ge > n_sel) & (thr > KEY_NEG_INF)
    n_pos = n_real + n_meta

    def pos_of(start, width, is_meta):
        lane = lax.broadcasted_iota(I32, (1, width), 1)
        return lane if is_meta else n_meta + start + lane

    def count_eq_le(cut):
        def body(t, acc):
            start = pl.multiple_of(t * kt, kt)
            kk = key_scr[:, pl.ds(start, kt)]
            return acc + ((kk == thr) & (pos_of(start, kt, False) <= cut)).astype(F32)

        acc = lax.fori_loop(0, nt, body, jnp.zeros((tq, kt), F32))
        mk = key_scr[:, n_real:n_real + LANE]
        macc_ = ((mk == thr) & (pos_of(0, LANE, True) <= cut)).astype(F32)
        return jnp.sum(acc, axis=1, keepdims=True) + jnp.sum(macc_, axis=1, keepdims=True)

    cut_scr[...] = jnp.full(cut_scr.shape, n_pos, I32)

    @pl.when(jnp.max(tie.astype(F32)) > 0.0)
    def _tie_cut():
        n_gt = count_ge(thr + 1)
        need = n_sel - n_gt
        nbits = max(1, (n_pos - 1).bit_length())

        def body(it, cut):
            cand = cut - lax.shift_left(jnp.int32(1), nbits - 1 - it)
            ok = (cand >= 0) & (count_eq_le(cand) >= need)
            return jnp.where(ok, cand, cut)

        cut = lax.fori_loop(0, nbits, body, jnp.full((tq, 1), (1 << nbits) - 1, I32))
        cut_scr[...] = jnp.broadcast_to(jnp.where(tie, cut, n_pos), cut_scr.shape)

    cut = cut_scr[:, 0:1]

    def bias_tile(start, width, is_meta):
        kk = key_scr[:, pl.ds(start, width)]
        sel = (kk > thr) | ((kk == thr) & (pos_of(start, width, is_meta) <= cut))
        sel = sel & (kk > KEY_NEG_INF)
        bias_scr[:, pl.ds(start, width)] = jnp.where(sel, 0.0, NEG_INF)

    def bias_body(t, carry):
        bias_tile(pl.multiple_of(t * kt, kt), kt, False)
        return carry

    lax.fori_loop(0, nt, bias_body, 0)
    bias_tile(n_real, LANE, True)

    scale = ATT_DH ** -0.5
    group = ATT_HEADS // ATT_KV_HEADS
    for h in range(ATT_HEADS):
        g = h // group
        qh = q_ref[:, h * ATT_DH:(h + 1) * ATT_DH].astype(BF16)

        def scores(start, width):
            s = _nt(qh, kbf[pl.ds(start, width), g * ATT_DH:(g + 1) * ATT_DH]) * scale
            s = s + bias_scr[:, pl.ds(start, width)]
            s_scr[:, pl.ds(start, width)] = s
            return jnp.max(s, axis=1, keepdims=True)

        def pass1(t, m):
            return jnp.maximum(m, scores(pl.multiple_of(t * kt, kt), kt))

        m = lax.fori_loop(0, nt, pass1, scores(n_real, LANE))

        def probs(start, width):
            p = jnp.exp(s_scr[:, pl.ds(start, width)] - m)
            pv = _mm(p.astype(BF16), vbf[pl.ds(start, width), g * ATT_DH:(g + 1) * ATT_DH])
            return jnp.sum(p, axis=1, keepdims=True), pv

        def pass2(t, carry):
            l, acc = carry
            dl, dacc = probs(pl.multiple_of(t * kt, kt), kt)
            return l + dl, acc + dacc

        l, acc = lax.fori_loop(0, nt, pass2, probs(n_real, LANE))
        o_ref[:, h * ATT_DH:(h + 1) * ATT_DH] = acc / l


def _attn_prompt(parts, mparts, *, batch, seq, n_meta, tq, kt):
    nqb = seq // tq
    n_sel = min(TOPK_MAX, (seq + n_meta) // 4)
    kvw = ATT_KV_HEADS * ATT_DH
    nk = seq + LANE
    return pl.pallas_call(
        functools.partial(_attn_prompt_kernel, n_sel=n_sel, n_meta=n_meta, kt=kt),
        grid=(batch, nqb),
        in_specs=[
            pl.BlockSpec((tq, D), lambda b, i: (b * nqb + i, COL_AQ // D)),
            pl.BlockSpec((tq, D), lambda b, i: (b * nqb + i, COL_IQ // D)),
            pl.BlockSpec((tq, LANE), lambda b, i: (b * nqb + i, COL_SM // LANE)),
            pl.BlockSpec((seq, kvw), lambda b, i: (b, COL_AK // kvw)),
            pl.BlockSpec((seq, kvw), lambda b, i: (b, COL_AV // kvw)),
            pl.BlockSpec((seq, LANE), lambda b, i: (b, COL_SM // LANE)),
            pl.BlockSpec((n_meta, kvw), lambda b, i: (0, COL_AK // kvw)),
            pl.BlockSpec((n_meta, kvw), lambda b, i: (0, COL_AV // kvw)),
            pl.BlockSpec((n_meta, LANE), lambda b, i: (0, COL_SM // LANE)),
        ],
        out_specs=pl.BlockSpec((tq, D), lambda b, i: (b * nqb + i, 0)),
        out_shape=jax.ShapeDtypeStruct((batch * seq, D), F32),
        scratch_shapes=[pltpu.VMEM((nk, kvw), BF16), pltpu.VMEM((nk, kvw), BF16),
                        pltpu.VMEM((nk, LANE), BF16), pltpu.VMEM((nk, LANE), BF16),
                        pltpu.VMEM((tq, nk), I32), pltpu.VMEM((tq, nk), F32), pltpu.VMEM((tq, nk), F32),
                        pltpu.VMEM((tq, LANE), I32)],
        compiler_params=_cparams(("arbitrary", "arbitrary")),
        name="sparse_attention_prompt",
    )(parts, parts, parts, parts, parts, parts, mparts, mparts, mparts)


def _attn_sample_kernel(pt_ref, *refs, n_pages, n_sel):
    kp = refs[0:n_pages]
    vp = refs[n_pages:2 * n_pages]
    ip = refs[2 * n_pages:3 * n_pages]
    q_ref, iq_ref, kn_ref, vn_ref, sm_ref, o_ref, kbf, vbf, kibf = refs[3 * n_pages:]
    del pt_ref
    t_new = q_ref.shape[0]
    page = kp[0].shape[0]
    past = n_pages * page
    for j in range(n_pages):
        kbf[j * page:(j + 1) * page, :] = kp[j][...].astype(BF16)
        vbf[j * page:(j + 1) * page, :] = vp[j][...].astype(BF16)
        kibf[j * page:(j + 1) * page, :] = ip[j][...].astype(BF16)
    sm = sm_ref[...]
    zpad = jnp.zeros((LANE - t_new, kbf.shape[1]), BF16)
    kbf[past:, :] = jnp.concatenate([kn_ref[...].astype(BF16), zpad], axis=0)
    vbf[past:, :] = jnp.concatenate([vn_ref[...].astype(BF16), zpad], axis=0)
    kibf[past:, :] = jnp.concatenate(
        [sm[:, SM_IK:SM_IK + IDX_DIM].astype(BF16), jnp.zeros((LANE - t_new, IDX_DIM), BF16)], axis=0)

    iq = iq_ref[...]
    qi = jnp.concatenate([iq[:, h * IDX_DIM:(h + 1) * IDX_DIM] for h in range(IDX_HEADS)], axis=0).astype(BF16)
    wcol = jnp.concatenate([sm[:, SM_IW + h:SM_IW + h + 1] for h in range(IDX_HEADS)], axis=0)
    wcol = wcol * ((IDX_DIM ** -0.5) * (IDX_HEADS ** -0.5))

    def idx_scores(start, width):
        s = jnp.maximum(_nt(qi, kibf[start:start + width, :]), 0.0) * wcol
        acc = s[0:t_new]
        for h in range(1, IDX_HEADS):
            acc = acc + s[h * t_new:(h + 1) * t_new]
        return acc

    i_past = idx_scores(0, past)
    i_new = idx_scores(past, LANE)
    trow = lax.broadcasted_iota(I32, (t_new, LANE), 0)
    tlane = lax.broadcasted_iota(I32, (t_new, LANE), 1)
    key_past = _sort_key(i_past)
    key_new = _sort_key(jnp.where(tlane <= trow, i_new, NEG_INF))
    tiles = [key_past, key_new]

    def bit_body(it, thr):
        cand = thr + lax.shift_left(jnp.int32(1), 31 - it)
        return jnp.where(_count_ge(tiles, cand) >= n_sel, cand, thr)

    thr = lax.fori_loop(0, 32, bit_body, jnp.full((t_new, 1), INT_MIN, I32))
    n_ge = _count_ge(tiles, thr)
    n_gt = _count_ge(tiles, thr + 1)
    tie = (n_ge > n_sel) & (thr > KEY_NEG_INF)
    need = n_sel - n_gt
    pos_past = lax.broadcasted_iota(I32, (1, past), 1)
    pos_new = past + lax.broadcasted_iota(I32, (1, LANE), 1)
    n_pos = past + t_new
    nbits = max(1, (n_pos - 1).bit_length())

    def cut_body(it, cut):
        cand = cut - lax.shift_left(jnp.int32(1), nbits - 1 - it)
        cnt = (jnp.sum(((key_past == thr) & (pos_past <= cand)).astype(F32), axis=1, keepdims=True)
               + jnp.sum(((key_new == thr) & (pos_new <= cand)).astype(F32), axis=1, keepdims=True))
        ok = (cand >= 0) & (cnt >= need)
        return jnp.where(ok, cand, cut)

    cut = lax.fori_loop(0, nbits, cut_body, jnp.full((t_new, 1), (1 << nbits) - 1, I32))
    cut = jnp.where(tie, cut, n_pos)

    def bias_of(kk, pos):
        sel = ((kk > thr) | ((kk == thr) & (pos <= cut))) & (kk > KEY_NEG_INF)
        return jnp.where(sel, 0.0, NEG_INF)

    group = ATT_HEADS // ATT_KV_HEADS
    b_past = jnp.concatenate([bias_of(key_past, pos_past)] * group, axis=0)
    b_new = jnp.concatenate([bias_of(key_new, pos_new)] * group, axis=0)
    scale = ATT_DH ** -0.5
    q = q_ref[...]
    for g in range(ATT_KV_HEADS):
        qg = jnp.concatenate(
            [q[:, (g * group + j) * ATT_DH:(g * group + j + 1) * ATT_DH] for j in range(group)], axis=0).astype(BF16)
        kg = kbf[:, g * ATT_DH:(g + 1) * ATT_DH]
        vg = vbf[:, g * ATT_DH:(g + 1) * ATT_DH]
        s_p = _nt(qg, kg[0:past]) * scale + b_past
        s_n = _nt(qg, kg[past:]) * scale + b_new
        m = jnp.maximum(jnp.max(s_p, axis=1, keepdims=True), jnp.max(s_n, axis=1, keepdims=True))
        p_p = jnp.exp(s_p - m)
        p_n = jnp.exp(s_n - m)
        l = jnp.sum(p_p, axis=1, keepdims=True) + jnp.sum(p_n, axis=1, keepdims=True)
        acc = _mm(p_p.astype(BF16), vg[0:past]) + _mm(p_n.astype(BF16), vg[past:])
        res = acc / l
        for j in range(group):
            hh = g * group + j
            o_ref[:, hh * ATT_DH:(hh + 1) * ATT_DH] = res[j * t_new:(j + 1) * t_new]


def _attn_sample(parts, cache_k, cache_v, cache_ik, page_table, *, t_new):
    db, n_pages = page_table.shape
    page = cache_k.shape[1]
    kvw = ATT_KV_HEADS * ATT_DH
    ck = cache_k.reshape(cache_k.shape[0], page, kvw)
    cv = cache_v.reshape(cache_v.shape[0], page, kvw)
    past = n_pages * page
    n_sel = min(TOPK_MAX, (past + t_new) // 4)

    def page_spec(width, j):
        return pl.BlockSpec((None, page, width), lambda b, pt, j=j: (pt[b, j], 0, 0))

    in_specs = ([page_spec(kvw, j) for j in range(n_pages)] + [page_spec(kvw, j) for j in range(n_pages)]
                + [page_spec(IDX_DIM, j) for j in range(n_pages)]
                + [pl.BlockSpec((t_new, D), lambda b, pt: (b, COL_AQ // D)),
                   pl.BlockSpec((t_new, D), lambda b, pt: (b, COL_IQ // D)),
                   pl.BlockSpec((t_new, kvw), lambda b, pt: (b, COL_AK // kvw)),
                   pl.BlockSpec((t_new, kvw), lambda b, pt: (b, COL_AV // kvw)),
                   pl.BlockSpec((t_new, LANE), lambda b, pt: (b, COL_SM // LANE))])
    grid_spec = pltpu.PrefetchScalarGridSpec(
        num_scalar_prefetch=1, grid=(db,), in_specs=in_specs,
        out_specs=pl.BlockSpec((t_new, D), lambda b, pt: (b, 0)),
        scratch_shapes=[pltpu.VMEM((past + LANE, kvw), BF16), pltpu.VMEM((past + LANE, kvw), BF16),
                        pltpu.VMEM((past + LANE, IDX_DIM), BF16)])
    return pl.pallas_call(
        functools.partial(_attn_sample_kernel, n_pages=n_pages, n_sel=n_sel),
        grid_spec=grid_spec,
        out_shape=jax.ShapeDtypeStruct((db * t_new, D), F32),
        compiler_params=_cparams(("arbitrary",)),
        name="sparse_attention_sample",
    )(page_table, *([ck] * n_pages), *([cv] * n_pages), *([cache_ik] * n_pages), parts, parts, parts, parts, parts)


def _layer_norm(x, g, b):
    mu = jnp.mean(x, axis=-1, keepdims=True)
    xc = x - mu
    var = jnp.mean(xc * xc, axis=-1, keepdims=True)
    return xc * lax.rsqrt(var + LN_EPS) * g + b


def _finish_kernel(x_ref, og_ref, oa_ref, ga_ref, gb_ref, wbg_ref, wba_ref, wo_ref, g_ref, b_ref, h_ref, *, alpha):
    a = _mm(og_ref[...].astype(BF16), wbg_ref[...])
    b = _mm(oa_ref[...].astype(BF16), wba_ref[...])
    merged = jax.nn.sigmoid(ga_ref[...]) * a + jax.nn.sigmoid(gb_ref[...]) * b
    y = alpha * x_ref[...] + _mm(merged.astype(BF16), wo_ref[...])
    h_ref[...] = _layer_norm(y, g_ref[...], b_ref[...])


def _finish(x, o_gdn, o_att, parts, wbg, wba, wo, g, b, *, alpha, tm):
    n = x.shape[0]
    tm = min(tm, n)
    row = lambda i: (i, 0)
    full = lambda i: (0, 0)
    return pl.pallas_call(
        functools.partial(_finish_kernel, alpha=alpha),
        grid=(n // tm,),
        in_specs=[pl.BlockSpec((tm, D), row), pl.BlockSpec((tm, D), row), pl.BlockSpec((tm, D), row),
                  pl.BlockSpec((tm, D), lambda i: (i, COL_GA // D)), pl.BlockSpec((tm, D), lambda i: (i, COL_GB // D)),
                  pl.BlockSpec((D, D), full), pl.BlockSpec((D, D), full), pl.BlockSpec((D, D), full),
                  pl.BlockSpec((1, D), full), pl.BlockSpec((1, D), full)],
        out_specs=pl.BlockSpec((tm, D), row),
        out_shape=jax.ShapeDtypeStruct((n, D), F32),
        compiler_params=_cparams(("arbitrary",)),
        name="merge_layernorm",
    )(x, o_gdn, o_att, parts, parts, wbg, wba, wo, g, b)


_CAND = [(r0, r1) for r0 in range(PEER_TOPK) for r1 in range(PEER_TOPK) if (r0 + 1) * (r1 + 1) <= PEER_TOPK]
_CAND_OFF = [next(i for i, c in enumerate(_CAND) if c[0] == r0) for r0 in range(PEER_TOPK)]
_CAND_LEN = [sum(1 for c in _CAND if c[0] == r0) for r0 in range(PEER_TOPK)]
_CAND_ROWS = -(-len(_CAND) // SUBLANE) * SUBLANE


def _top_rows(s, n_top, break_ties):
    rows, cols = s.shape
    iota = lax.broadcasted_iota(I32, (rows, cols), 0).astype(F32)
    rank = jnp.full((rows, cols), float(n_top), F32)
    vals = []
    for r in range(n_top):
        m = jnp.max(s, axis=0, keepdims=True)
        hit = s == m
        if break_ties:
            hit = iota == jnp.min(jnp.where(hit, iota, float(rows)), axis=0, keepdims=True)
        vals.append(m)
        s = jnp.where(hit, NEG_INF, s)
        rank = jnp.where(hit, float(r), rank)
    n_ranked = jnp.sum((rank < float(n_top)).astype(F32), axis=0, keepdims=True)
    return jnp.concatenate(vals, axis=0), rank, n_ranked == float(n_top)


def _peer_kernel(h_ref, wq_ref, sk_ref, u_ref, vt_ref, g_ref, b_ref, y_ref,
                 hb_scr, q_scr, rank1_scr, bt_scr, nt_scr, at_scr, s1_scr,
                 s_scr, p_scr, acc_scr, *, alpha):
    tt = h_ref.shape[0]
    eb = u_ref.shape[0]
    e = pl.program_id(1)
    nk = PEER_NKEYS
    kk = PEER_TOPK
    tchunks = tt // LANE

    @pl.when(e == 0)
    def _prep():
        hb = h_ref[...].astype(BF16)
        hb_scr[...] = hb
        q_scr[...] = _mm(hb, wq_ref[...]).astype(BF16)
        acc_scr[...] = jnp.zeros(acc_scr.shape, F32)

        def scores(hd, carry):
            q0 = q_scr[:, pl.ds(pl.multiple_of(hd * 2 * nk, nk), nk)]
            q1 = q_scr[:, pl.ds(pl.multiple_of(hd * 2 * nk + nk, nk), nk)]
            at_scr[hd] = _nt(sk_ref[hd, 0], q0)
            s1_scr[hd] = _nt(sk_ref[hd, 1], q1)
            return carry

        lax.fori_loop(0, PEER_HEADS, scores, 0)

        def select_chunk(hd, c0, s0, s1, break_ties):
            a, rank0, ok0 = _top_rows(s0, kk, break_ties)
            b, rank1, ok1 = _top_rows(s1, kk, break_ties)
            pad = [jnp.full((_CAND_ROWS - len(_CAND), LANE), NEG_INF, F32)]
            cand = jnp.concatenate([a[r0:r0 + 1] + b[r1:r1 + 1] for r0, r1 in _CAND] + pad, axis=0)
            top, crank, okc = _top_rows(cand, kk, break_ties)
            zsum = jnp.sum(jnp.exp(top - top[0:1]), axis=0, keepdims=True)
            chosen = (crank < float(kk)).astype(F32)
            crow = lax.broadcasted_iota(I32, (_CAND_ROWS, 1), 0)
            nsel = jnp.zeros((nk, LANE), F32)
            for r0 in range(kk):
                in_group = (crow >= _CAND_OFF[r0]) & (crow < _CAND_OFF[r0] + _CAND_LEN[r0])
                cnt = jnp.sum(jnp.where(in_group, chosen, 0.0), axis=0, keepdims=True)
                nsel = nsel + jnp.where(rank0 == float(r0), cnt, 0.0)
            rank1_scr[hd, :, pl.ds(c0, LANE)] = rank1.astype(BF16)
            nt_scr[hd, :, pl.ds(c0, LANE)] = nsel
            at_scr[hd, :, pl.ds(c0, LANE)] = jnp.exp(s0 - a[0:1])
            bt_scr[hd, :, pl.ds(c0, LANE)] = (jnp.exp(s1 - b[0:1]) / zsum).astype(BF16)
            return jnp.min((ok0 & ok1 & okc).astype(F32)) > 0.5

        def select(idx, carry):
            hd = idx // tchunks
            c0 = pl.multiple_of((idx % tchunks) * LANE, LANE)
            s0 = at_scr[hd, :, pl.ds(c0, LANE)]
            s1 = s1_scr[hd, :, pl.ds(c0, LANE)]
            tie_free = select_chunk(hd, c0, s0, s1, False)

            @pl.when(jnp.logical_not(tie_free))
            def _():
                select_chunk(hd, c0, s0, s1, True)

            return carry

        lax.fori_loop(0, PEER_HEADS * tchunks, select, 0)

    s_scr[...] = _nt(u_ref[...], hb_scr[...])
    groups = eb // nk

    def tile(idx, carry):
        ii = idx // tchunks
        tc = idx % tchunks
        i = e * groups + ii
        r0 = pl.multiple_of(ii * nk, nk)
        c0 = pl.multiple_of(tc * LANE, LANE)
        gate = jnp.zeros((nk, LANE), BF16)
        base = pl.multiple_of((i // SUBLANE) * SUBLANE, SUBLANE)
        pick = lax.broadcasted_iota(I32, (SUBLANE, LANE), 0) == i % SUBLANE
        for hd in range(PEER_HEADS):
            nblk = nt_scr[hd, pl.ds(base, SUBLANE), pl.ds(c0, LANE)]
            ablk = at_scr[hd, pl.ds(base, SUBLANE), pl.ds(c0, LANE)]
            nrow = jnp.sum(jnp.where(pick, nblk, 0.0), axis=0, keepdims=True).astype(BF16)
            arow = jnp.sum(jnp.where(pick, ablk, 0.0), axis=0, keepdims=True).astype(BF16)
            bsel = jnp.where(rank1_scr[hd, :, pl.ds(c0, LANE)] < nrow, bt_scr[hd, :, pl.ds(c0, LANE)],
                             jnp.zeros((nk, LANE), BF16))
            gate = gate + bsel * arow
        s = s_scr[pl.ds(r0, nk), pl.ds(c0, LANE)]
        act = 0.5 * s * (1.0 + lax.erf(s * (2.0 ** -0.5)))
        p_scr[pl.ds(r0, nk), pl.ds(c0, LANE)] = gate * act.astype(BF16)
        return carry

    lax.fori_loop(0, groups * tchunks, tile, 0, unroll=2)
    acc_scr[...] += _mm(vt_ref[...], p_scr[...])

    @pl.when(e == pl.num_programs(1) - 1)
    def _():
        y = alpha * h_ref[...] + acc_scr[...].T
        y_ref[...] = _layer_norm(y, g_ref[...], b_ref[...])


def _peer(h, wq, sk, u, vt, g, b, *, alpha, tt, eb):
    n = h.shape[0]
    n_exp = u.shape[0]
    nk = PEER_NKEYS
    return pl.pallas_call(
        functools.partial(_peer_kernel, alpha=alpha),
        grid=(n // tt, n_exp // eb),
        in_specs=[pl.BlockSpec((tt, D), lambda t, e: (t, 0)),
                  pl.BlockSpec(wq.shape, lambda t, e: (0, 0)),
                  pl.BlockSpec(sk.shape, lambda t, e: (0, 0, 0, 0)),
                  pl.BlockSpec((eb, D), lambda t, e: (e, 0)),
                  pl.BlockSpec((D, eb), lambda t, e: (0, e)),
                  pl.BlockSpec((1, D), lambda t, e: (0, 0)),
                  pl.BlockSpec((1, D), lambda t, e: (0, 0))],
        out_specs=pl.BlockSpec((tt, D), lambda t, e: (t, 0)),
        out_shape=jax.ShapeDtypeStruct((n, D), F32),
        scratch_shapes=[pltpu.VMEM((tt, D), BF16), pltpu.VMEM((tt, wq.shape[1]), BF16),
                        pltpu.VMEM((PEER_HEADS, nk, tt), BF16), pltpu.VMEM((PEER_HEADS, nk, tt), BF16),
                        pltpu.VMEM((PEER_HEADS, nk, tt), F32), pltpu.VMEM((PEER_HEADS, nk, tt), F32),
                        pltpu.VMEM((PEER_HEADS, nk, tt), F32),
                        pltpu.VMEM((eb, tt), F32), pltpu.VMEM((eb, tt), BF16), pltpu.VMEM((D, tt), F32)],
        compiler_params=_cparams(("arbitrary", "arbitrary")),
        name="peer_ffn",
    )(h, wq, sk, u, vt, g, b)


def _permute_w_in(w):
    gq = 3 * GDN_HEADS * GDN_D
    gv = GDN_HEADS * GDN_D
    aq = ATT_HEADS * ATT_DH
    akv = ATT_KV_HEADS * ATT_DH
    iqw = IDX_HEADS * IDX_DIM
    sizes = (gq, gv, GDN_HEADS, GDN_HEADS, aq, akv, akv, iqw, IDX_DIM, IDX_HEADS, D, D)
    offs = [0]
    for s in sizes:
        offs.append(offs[-1] + s)
    seg = [w[:, offs[i]:offs[i + 1]] for i in range(len(sizes))]
    qkv, z, b, a, q, k, v, iq, ik, iw, ga, gb = seg
    pad = jnp.zeros((w.shape[0], LANE - IDX_DIM - 2 * GDN_HEADS - IDX_HEADS), w.dtype)
    return jnp.concatenate([qkv, z, q, iq, ga, gb, k, v, ik, b, a, iw, pad], axis=1).astype(BF16)


def kernel(x_prompt, x_sample, cache_k, cache_v, cache_idx_k, state_conv, state_delta, page_table, meta_tokens, w_in, conv_w, a_log, dt_bias, gdn_norm_g, w_branch_gdn, w_branch_attn, w_out, ln1_g, ln1_b, peer_wq, peer_subkeys, peer_u, peer_v, ln2_g, ln2_b):
    depth = w_in.shape[0]
    assert depth == 1, "single-layer step"
    batch, seq, d = x_prompt.shape
    db, t_new, _ = x_sample.shape
    n_meta = meta_tokens.shape[0]
    assert d == D and seq % GDN_CHUNK == 0 and n_meta % SUBLANE == 0 and n_meta <= GDN_CHUNK
    alpha = (2 * depth) ** 0.25
    qkv_w = 3 * GDN_HEADS * GDN_D
    kvw = ATT_KV_HEADS * ATT_DH

    w_r = _permute_w_in(w_in[0])
    xp = x_prompt.reshape(batch * seq, D)
    xs = x_sample.reshape(db * t_new, D)
    tm = 512 if (batch * seq) % 512 == 0 else 256
    parts_p = _project(xp, w_r, tm)
    parts_s = _project(xs, w_r, min(tm, db * t_new))
    parts_m = _project(meta_tokens.astype(F32), w_r, n_meta)

    hp = jnp.zeros((2, LANE), F32).at[0, SM_A:SM_A + GDN_HEADS].set(a_log[0]).at[1, SM_A:SM_A + GDN_HEADS].set(dt_bias[0])
    cw = conv_w[0]
    ng = gdn_norm_g[0].reshape(1, GDN_D)
    zero_prev = jnp.zeros((1, SUBLANE, qkv_w), F32)
    zero_state = jnp.zeros((1, GDN_HEADS, GDN_D, GDN_D), F32)
    _, s_meta = _gdn(parts_m, zero_prev, zero_state, cw, hp, ng, batch=1, n_chunks=1, c_in=n_meta, n_valid=n_meta,
                     n_seq=1, shared_init=True)
    meta_prev = parts_m[n_meta - SUBLANE:, :qkv_w].reshape(1, SUBLANE, qkv_w)
    og_p, p_delta = _gdn(parts_p, meta_prev, s_meta, cw, hp, ng, batch=batch, n_chunks=seq // GDN_CHUNK,
                         c_in=GDN_CHUNK, n_valid=GDN_CHUNK, n_seq=1, shared_init=True)
    per = GDN_CHUNK // t_new
    assert per * t_new == GDN_CHUNK and t_new % SUBLANE == 0 and db % per == 0
    samp_prev = jnp.pad(state_conv.reshape(db, CONV_W - 1, qkv_w), ((0, 0), (SUBLANE - (CONV_W - 1), 0), (0, 0)))
    og_s, s_delta = _gdn(parts_s, samp_prev, state_delta.reshape(db, GDN_HEADS, GDN_D, GDN_D).astype(F32), cw, hp, ng,
                         batch=db // per, n_chunks=1, c_in=GDN_CHUNK, n_valid=GDN_CHUNK, n_seq=per, shared_init=False)

    tq = 128
    oa_p = _attn_prompt(parts_p, parts_m, batch=batch, seq=seq, n_meta=n_meta, tq=tq, kt=256)
    oa_s = _attn_sample(parts_s, cache_k.reshape(cache_k.shape[1:]), cache_v.reshape(cache_v.shape[1:]),
                        cache_idx_k.reshape(cache_idx_k.shape[1:]), page_table, t_new=t_new)

    wbg = w_branch_gdn[0].astype(BF16)
    wba = w_branch_attn[0].astype(BF16)
    wo = w_out[0].astype(BF16)
    g1 = ln1_g[0].reshape(1, D)
    b1 = ln1_b[0].reshape(1, D)
    h_p = _finish(xp, og_p, oa_p, parts_p, wbg, wba, wo, g1, b1, alpha=alpha, tm=256)
    h_s = _finish(xs, og_s, oa_s, parts_s, wbg, wba, wo, g1, b1, alpha=alpha, tm=256)

    wq = peer_wq[0].astype(BF16)
    sk = peer_subkeys[0].astype(BF16)
    u = peer_u[0].astype(BF16)
    vt = peer_v[0].astype(BF16).T
    g2 = ln2_g[0].reshape(1, D)
    b2 = ln2_b[0].reshape(1, D)
    tt = 512 if (batch * seq) % 512 == 0 else 256
    y_p = _peer(h_p, wq, sk, u, vt, g2, b2, alpha=alpha, tt=tt, eb=512)
    y_s = _peer(h_s, wq, sk, u, vt, g2, b2, alpha=alpha, tt=min(tt, db * t_new), eb=512)

    y_prompt = y_p.reshape(batch, seq, D)
    y_sample = y_s.reshape(db, t_new, D)
    pp = parts_p.reshape(batch, seq, N_COLS)
    ps = parts_s.reshape(db, t_new, N_COLS)
    p_conv = pp[:, seq - (CONV_W - 1):, :qkv_w][None]
    s_conv = jnp.concatenate([state_conv[0].astype(F32), ps[:, :, :qkv_w]], axis=1)[:, -(CONV_W - 1):][None]

    def with_meta(col, width):
        m = jnp.broadcast_to(parts_m[None, :, col:col + width], (batch, n_meta, width))
        return jnp.concatenate([m, pp[:, :, col:col + width]], axis=1)

    p_k = with_meta(COL_AK, kvw).reshape(1, batch, seq + n_meta, ATT_KV_HEADS, ATT_DH)
    p_v = with_meta(COL_AV, kvw).reshape(1, batch, seq + n_meta, ATT_KV_HEADS, ATT_DH)
    p_idx_k = with_meta(COL_SM + SM_IK, IDX_DIM)[None]
    s_k = ps[:, :, COL_AK:COL_AK + kvw].reshape(1, db, t_new, ATT_KV_HEADS, ATT_DH)
    s_v = ps[:, :, COL_AV:COL_AV + kvw].reshape(1, db, t_new, ATT_KV_HEADS, ATT_DH)
    s_idx_k = ps[:, :, COL_SM + SM_IK:COL_SM + SM_IK + IDX_DIM][None]
    return (y_prompt, y_sample, p_conv, p_delta[None], p_k, p_v, p_idx_k, s_conv, s_delta[None].astype(state_delta.dtype),
            s_k, s_v, s_idx_k)
```

```python
import functools

import jax
import jax.numpy as jnp
from jax import lax
from jax.experimental import pallas as pl
from jax.experimental.pallas import tpu as pltpu

F32 = jnp.float32
BF16 = jnp.bfloat16
I32 = jnp.int32

GDN_HEADS = 8
GDN_D = 128
CONV_W = 4
GDN_CHUNK = 64
ATT_HEADS = 8
ATT_KV_HEADS = 2
ATT_DH = 128
IDX_HEADS = 16
IDX_DIM = 64
TOPK_MAX = 256
PEER_HEADS = 8
PEER_NKEYS = 128
PEER_TOPK = 16
LN_EPS = 1e-5
RMS_EPS = 1e-6

LANE = 128
SUBLANE = 8
VMEM_LIMIT = 56 * 1024 * 1024

D = 1024
COL_QKV = 0
COL_Z = 3072
COL_AQ = 4096
COL_IQ = 5120
COL_GA = 6144
COL_GB = 7168
COL_AK = 8192
COL_AV = 8448
COL_SM = 8704
N_COLS = 8832
SM_IK = 0
SM_B = 64
SM_A = 72
SM_IW = 80

NEG_INF = float("-inf")
INT_MIN = -(2 ** 31)


def _nt(a, b):
    return lax.dot_general(a, b, (((1,), (1,)), ((), ())), preferred_element_type=F32)


def _mm(a, b):
    return jnp.dot(a, b, preferred_element_type=F32)


def _mm_f32(a, b):
    return jnp.dot(a, b, preferred_element_type=F32, precision=lax.Precision.HIGHEST)


def _cparams(sem):
    return pltpu.CompilerParams(dimension_semantics=sem, vmem_limit_bytes=VMEM_LIMIT)


def _proj_kernel(x_ref, w_ref, o_ref):
    o_ref[...] = _mm(x_ref[...].astype(BF16), w_ref[...])


def _project(x, w_bf, tm):
    n = x.shape[0]
    tn = N_COLS // 3
    return pl.pallas_call(
        _proj_kernel,
        grid=(3, n // tm),
        in_specs=[pl.BlockSpec((tm, D), lambda j, i: (i, 0)),
                  pl.BlockSpec((D, tn), lambda j, i: (0, j))],
        out_specs=pl.BlockSpec((tm, tn), lambda j, i: (i, j)),
        out_shape=jax.ShapeDtypeStruct((n, N_COLS), F32),
        compiler_params=_cparams(("arbitrary", "arbitrary")),
        name="in_projection",
    )(x, w_bf)


def _sibling_mask(ri, ci, lvl):
    return ((ri >> (lvl + 1)) == (ci >> (lvl + 1))) & (((ri >> lvl) & 1) == 1) & (((ci >> lvl) & 1) == 0)


_BATCH0 = ((0,), (0,))


def _bmm(a, b):
    return lax.dot_general(a, b, (((2,), (1,)), _BATCH0), preferred_element_type=F32)


def _bnt(a, b):
    return lax.dot_general(a, b, (((2,), (2,)), _BATCH0), preferred_element_type=F32)


def _btn(a, b):
    return lax.dot_general(a, b, (((1,), (1,)), _BATCH0), preferred_element_type=F32)


def _split(x):
    hi = x.astype(BF16)
    return hi, (x - hi.astype(F32)).astype(BF16)


def _bmm3(a, b):
    ah, al = a
    bh, bl = b
    return _bmm(ah, bh) + (_bmm(ah, bl) + _bmm(al, bh))


def _gdn_kernel(qkv_ref, z_ref, sm_ref, prev_ref, s0_ref, cw_ref, hp_ref, ng_ref,
                o_ref, sfin_ref, s_scr, ext_scr, *, c_in, n_valid, n_seq):
    nh = GDN_HEADS
    hd = GDN_D
    stride = ext_scr.shape[0] // n_seq
    L = stride - SUBLANE
    C = L * n_seq
    ls = L.bit_length() - 1
    c = pl.program_id(1)

    @pl.when(c == 0)
    def _():
        s_scr[...] = s0_ref[...]
        for b in range(n_seq):
            ext_scr[b * stride:b * stride + SUBLANE, :] = prev_ref[b]

    u = qkv_ref[...]
    sm = sm_ref[...]
    z = z_ref[...]
    if c_in < C:
        u = jnp.concatenate([u, jnp.zeros((C - c_in, u.shape[1]), F32)], axis=0)
        sm = jnp.concatenate([sm, jnp.zeros((C - c_in, LANE), F32)], axis=0)
        z = jnp.concatenate([z, jnp.zeros((C - c_in, z.shape[1]), F32)], axis=0)
    cw = cw_ref[...]
    pieces = []
    for b in range(n_seq):
        base = b * stride + SUBLANE
        ub_ = u[b * L:(b + 1) * L]
        ext_scr[base:base + L, :] = ub_
        acc = ub_ * cw[CONV_W - 1:CONV_W, :]
        for i in range(CONV_W - 1):
            off = base - (CONV_W - 1) + i
            acc = acc + ext_scr[off:off + L, :] * cw[i:i + 1, :]
        pieces.append(acc)
    conv = pieces[0] if n_seq == 1 else jnp.concatenate(pieces, axis=0)
    if n_seq == 1:
        tail = ext_scr[c_in:c_in + SUBLANE, :]
        ext_scr[0:SUBLANE, :] = tail
    qkv = conv * jax.nn.sigmoid(conv)

    row = lax.broadcasted_iota(I32, (C, 1), 0)
    hp = hp_ref[...]
    xs = sm + hp[1:2, :]
    softplus = jnp.maximum(xs, 0.0) + jnp.log1p(jnp.exp(-jnp.abs(xs)))
    la = -jnp.exp(hp[0:1, :]) * softplus
    beta = jax.nn.sigmoid(sm)
    if n_valid < C:
        valid = row < n_valid
        qkv = jnp.where(valid, qkv, 0.0)
        la = jnp.where(valid, la, 0.0)
        beta = jnp.where(valid, beta, 0.0)

    ri = lax.broadcasted_iota(I32, (C, C), 0)
    ci = lax.broadcasted_iota(I32, (C, C), 1)
    same = (ri >> ls) == (ci >> ls)
    incl = same & (ri >= ci)
    strict = same & (ri > ci)
    g = _mm_f32(incl.astype(F32), la)
    g_end = _mm_f32((ci == (ri | (L - 1))).astype(F32), g)
    gt = g.T
    eye = (ri == ci).astype(F32)

    q_l, k_l, kd_l, dec_l, lows_l, rhs_l, eg_l, z_l = [], [], [], [], [], [], [], []
    for h in range(nh):
        col = SM_A + h
        gcol = g[:, col:col + 1]
        grow = gt[col:col + 1, :]
        bcol = beta[:, SM_B + h:SM_B + h + 1]
        q = qkv[:, h * hd:(h + 1) * hd]
        k = qkv[:, nh * hd + h * hd:nh * hd + (h + 1) * hd]
        v = qkv[:, 2 * nh * hd + h * hd:2 * nh * hd + (h + 1) * hd]
        q = q * lax.rsqrt(jnp.sum(q * q, axis=-1, keepdims=True) + RMS_EPS) * (hd ** -0.5)
        k = k * lax.rsqrt(jnp.sum(k * k, axis=-1, keepdims=True) + RMS_EPS)
        dec = jnp.exp(jnp.where(incl, gcol - grow, NEG_INF))
        eg = jnp.exp(gcol)
        q_l.append(q.astype(BF16))
        k_l.append(k.astype(BF16))
        kd_l.append((k * jnp.exp(g_end[:, col:col + 1] - gcol)).astype(BF16))
        dec_l.append(dec)
        lows_l.append(bcol * jnp.where(strict, dec, 0.0))
        rhs_l.append(jnp.concatenate([bcol * v, (bcol * eg) * k], axis=1))
        eg_l.append(eg)
        z_l.append(z[:, h * hd:(h + 1) * hd])
    qb = jnp.stack(q_l)
    kb = jnp.stack(k_l)
    kd = jnp.stack(kd_l)
    dec = jnp.stack(dec_l)
    eg = jnp.stack(eg_l)

    low = jnp.stack(lows_l) * _bnt(kb, kb)
    qk = _bnt(qb, kb)
    inv = eye[None] - jnp.where(_sibling_mask(ri, ci, 0)[None], low, 0.0)
    for lvl in range(1, ls):
        off = jnp.where(_sibling_mask(ri, ci, lvl)[None], low, 0.0)
        inv_s = _split(inv)
        inv = inv - _bmm3(inv_s, _split(_bmm3(_split(off), inv_s)))
    uw = _bmm3(_split(inv), _split(jnp.stack(rhs_l)))
    w_b = uw[:, :, hd:].astype(BF16)

    ws = None
    qs = None
    s_old = []
    for b in range(n_seq):
        S = s_scr[b]
        s_old.append(S)
        Sb = S.astype(BF16)
        if n_seq == 1:
            wm, qm = w_b, qb
        else:
            mine = ((row >> ls) == b)[None]
            wm = jnp.where(mine, w_b, jnp.zeros_like(w_b))
            qm = jnp.where(mine, qb, jnp.zeros_like(qb))
        ws = _bmm(wm, Sb) if ws is None else ws + _bmm(wm, Sb)
        qs = _bmm(qm, Sb) if qs is None else qs + _bmm(qm, Sb)
    ub = (uw[:, :, :hd] - ws).astype(BF16)
    o = eg * qs + _bmm((dec * qk).astype(BF16), ub)
    for b in range(n_seq):
        last = b * L + L - 1
        glast = jnp.stack([g[last:last + 1, SM_A + h:SM_A + h + 1] for h in range(nh)])
        kdm = kd if n_seq == 1 else jnp.where(((row >> ls) == b)[None], kd, jnp.zeros_like(kd))
        s_scr[b] = jnp.exp(glast) * s_old[b] + _btn(kdm, ub)

    on = o * lax.rsqrt(jnp.mean(o * o, axis=-1, keepdims=True) + RMS_EPS) * ng_ref[...]
    zz = jnp.stack(z_l)
    res = on * (zz * jax.nn.sigmoid(zz))
    for h in range(nh):
        o_ref[:, h * hd:(h + 1) * hd] = res[h, :c_in]

    @pl.when(c == pl.num_programs(1) - 1)
    def _():
        sfin_ref[...] = s_scr[...]


def _gdn(parts, prev, s0, cw, hp, ng, *, batch, n_chunks, c_in, n_valid, n_seq, shared_init):
    n = parts.shape[0]
    C = max(c_in, GDN_CHUNK)
    assert n_seq == 1 or (n_chunks == 1 and c_in == C)
    L = C // n_seq
    init = (lambda b, c: (0, 0, 0)) if shared_init else (lambda b, c: (b, 0, 0))
    init4 = (lambda b, c: (0, 0, 0, 0)) if shared_init else (lambda b, c: (b, 0, 0, 0))
    qkv_w = 3 * GDN_HEADS * GDN_D
    v_w = GDN_HEADS * GDN_D
    return pl.pallas_call(
        functools.partial(_gdn_kernel, c_in=c_in, n_valid=n_valid, n_seq=n_seq),
        grid=(batch, n_chunks),
        in_specs=[
            pl.BlockSpec((c_in, qkv_w), lambda b, c: (b * n_chunks + c, COL_QKV // qkv_w)),
            pl.BlockSpec((c_in, v_w), lambda b, c: (b * n_chunks + c, COL_Z // v_w)),
            pl.BlockSpec((c_in, LANE), lambda b, c: (b * n_chunks + c, COL_SM // LANE)),
            pl.BlockSpec((n_seq, SUBLANE, qkv_w), init),
            pl.BlockSpec((n_seq, GDN_HEADS, GDN_D, GDN_D), init4),
            pl.BlockSpec((CONV_W, qkv_w), lambda b, c: (0, 0)),
            pl.BlockSpec((2, LANE), lambda b, c: (0, 0)),
            pl.BlockSpec((1, GDN_D), lambda b, c: (0, 0)),
        ],
        out_specs=[
            pl.BlockSpec((c_in, v_w), lambda b, c: (b * n_chunks + c, 0)),
            pl.BlockSpec((n_seq, GDN_HEADS, GDN_D, GDN_D), lambda b, c: (b, 0, 0, 0)),
        ],
        out_shape=[jax.ShapeDtypeStruct((n, v_w), F32),
                   jax.ShapeDtypeStruct((batch * n_seq, GDN_HEADS, GDN_D, GDN_D), F32)],
        scratch_shapes=[pltpu.VMEM((n_seq, GDN_HEADS, GDN_D, GDN_D), F32),
                        pltpu.VMEM((n_seq * (SUBLANE + L), qkv_w), F32)],
        compiler_params=_cparams(("arbitrary", "arbitrary")),
        name="gated_deltanet",
    )(parts, parts, parts, prev, s0, cw, hp, ng)


def _sort_key(x):
    b = pltpu.bitcast(x + 0.0, I32)
    return b ^ ((b >> 31) & 0x7FFFFFFF)


KEY_NEG_INF = -(2 ** 31) + 0x7FFFFF
SEARCH_BITS_PER_TRIP = 4


def _count_ge(key_tiles, cand):
    tot = None
    for kt in key_tiles:
        c = jnp.sum((kt >= cand).astype(F32), axis=1, keepdims=True)
        tot = c if tot is None else tot + c
    return tot


def _rows8(x):
    return x.reshape(x.shape[0] // SUBLANE, SUBLANE, x.shape[1])


def _attn_prompt_kernel(q_ref, iq_ref, smq_ref, k_ref, v_ref, smk_ref, mk_ref, mv_ref, msm_ref,
                        o_ref, kbf, vtb, kibf, key_scr, bias_scr, s_scr, cut_scr, *, n_sel, n_meta, kt):
    tq = q_ref.shape[0]
    n_real = k_ref.shape[0]
    kvw = k_ref.shape[1]
    qb_i = pl.program_id(1)
    nt = (qb_i * tq + tq + kt - 1) // kt
    group = ATT_HEADS // ATT_KV_HEADS

    @pl.when(qb_i == 0)
    def _():
        zrows = LANE - n_meta
        kbf[0:n_real, :] = k_ref[...].astype(BF16)
        kbf[n_real:, :] = jnp.concatenate([mk_ref[...], jnp.zeros((zrows, kvw), F32)], axis=0).astype(BF16)
        vtb[:, 0:n_real] = v_ref[...].T.astype(BF16)
        vtb[:, n_real:] = jnp.concatenate([mv_ref[...], jnp.zeros((zrows, kvw), F32)], axis=0).T.astype(BF16)
        kibf[0:n_real, :] = smk_ref[...][:, SM_IK:SM_IK + IDX_DIM].astype(BF16)
        kibf[n_real:, :] = jnp.concatenate(
            [msm_ref[...][:, SM_IK:SM_IK + IDX_DIM], jnp.zeros((zrows, IDX_DIM), F32)], axis=0).astype(BF16)

    iq = iq_ref[...]
    iq_stack = jnp.concatenate([iq[:, h * IDX_DIM:(h + 1) * IDX_DIM] for h in range(IDX_HEADS)], axis=0).astype(BF16)
    w_t = smq_ref[...].T * ((IDX_DIM ** -0.5) * (IDX_HEADS ** -0.5))
    qpos = qb_i * tq + lax.broadcasted_iota(I32, (1, tq), 1)

    def index_scores(start, width):
        s_all = _nt(kibf[pl.ds(start, width), :], iq_stack)
        acc = jnp.zeros((width, tq), F32)
        for h in range(IDX_HEADS):
            acc = acc + jnp.maximum(s_all[:, h * tq:(h + 1) * tq], 0.0) * w_t[SM_IW + h:SM_IW + h + 1, :]
        return acc

    def real_tile(t, carry):
        start = pl.multiple_of(t * kt, kt)
        kpos = start + lax.broadcasted_iota(I32, (kt, 1), 0)
        key_scr[pl.ds(start, kt), :] = _sort_key(jnp.where(kpos <= qpos, index_scores(start, kt), NEG_INF))
        return carry

    lax.fori_loop(0, nt, real_tile, 0)
    mrow = lax.broadcasted_iota(I32, (LANE, 1), 0)
    key_scr[n_real:, :] = _sort_key(jnp.where(mrow < n_meta, index_scores(n_real, LANE), NEG_INF))

    def count_ge(cand):
        def body(t, acc):
            kk = key_scr[pl.ds(pl.multiple_of(t * kt, kt), kt), :]
            return acc + jnp.sum(_rows8((kk >= cand).astype(F32)), axis=0)

        acc = lax.fori_loop(0, nt, body, jnp.zeros((SUBLANE, tq), F32))
        acc = acc + jnp.sum(_rows8((key_scr[n_real:, :] >= cand).astype(F32)), axis=0)
        return jnp.sum(acc, axis=0, keepdims=True)

    few = qpos + 1 + n_meta <= n_sel

    def unsettled(state):
        it, _, cnt = state
        return (it < 32) & (jnp.max(jnp.where(few | (cnt == n_sel), 0.0, 1.0)) > 0.0)

    def refine(state):
        it, thr, cnt = state
        for step in range(SEARCH_BITS_PER_TRIP):
            cand = thr + lax.shift_left(jnp.int32(1), 31 - step - it)
            c = count_ge(cand)
            ok = c >= n_sel
            thr = jnp.where(ok, cand, thr)
            cnt = jnp.where(ok, c, cnt)
        return it + SEARCH_BITS_PER_TRIP, thr, cnt

    n_all = (nt * kt + LANE).astype(F32)
    _, thr, cnt = lax.while_loop(
        unsettled, refine, (jnp.int32(0), jnp.full((1, tq), INT_MIN, I32), jnp.full((1, tq), 1.0, F32) * n_all))

    tie = (cnt > n_sel) & (thr > KEY_NEG_INF) & jnp.logical_not(few)
    n_pos = n_real + n_meta
    cut_scr[...] = jnp.full((SUBLANE, tq), float(n_pos), F32)

    def pos_of(start, width, is_meta):
        r = lax.broadcasted_iota(I32, (width, 1), 0)
        return r if is_meta else n_meta + start + r

    @pl.when(jnp.max(tie.astype(F32)) > 0.0)
    def _tie_cut():
        need = n_sel - count_ge(thr + 1)
        nbits = max(1, (n_pos - 1).bit_length())

        def count_eq_le(cut):
            def body(t, acc):
                start = pl.multiple_of(t * kt, kt)
                kk = key_scr[pl.ds(start, kt), :]
                return acc + jnp.sum(_rows8(((kk == thr) & (pos_of(start, kt, False) <= cut)).astype(F32)), axis=0)

            acc = lax.fori_loop(0, nt, body, jnp.zeros((SUBLANE, tq), F32))
            mk = key_scr[n_real:, :]
            acc = acc + jnp.sum(_rows8(((mk == thr) & (pos_of(0, LANE, True) <= cut)).astype(F32)), axis=0)
            return jnp.sum(acc, axis=0, keepdims=True)

        def body(it, cut):
            cand = cut - lax.shift_left(jnp.int32(1), nbits - 1 - it)
            ok = (cand >= 0) & (count_eq_le(cand) >= need)
            return jnp.where(ok, cand, cut)

        cut = lax.fori_loop(0, nbits, body, jnp.full((1, tq), (1 << nbits) - 1, I32))
        cut = jnp.where(tie, cut, n_pos).astype(F32)
        cut_scr[...] = jnp.broadcast_to(cut, (SUBLANE, tq))

    cut = cut_scr[0:1, :].astype(I32)

    def bias_tile(start, width, is_meta):
        kk = key_scr[pl.ds(start, width), :]
        sel = (kk > thr) | ((kk == thr) & (pos_of(start, width, is_meta) <= cut))
        sel = sel & (kk > KEY_NEG_INF)
        bias_scr[pl.ds(start, width), :] = jnp.where(sel, 0.0, NEG_INF)

    def bias_body(t, carry):
        bias_tile(pl.multiple_of(t * kt, kt), kt, False)
        return carry

    lax.fori_loop(0, nt, bias_body, 0)
    bias_tile(n_real, LANE, True)

    scale = ATT_DH ** -0.5
    for g in range(ATT_KV_HEADS):
        q4 = jnp.concatenate([q_ref[:, (g * group + j) * ATT_DH:(g * group + j + 1) * ATT_DH] for j in range(group)],
                             axis=0).astype(BF16)

        def score_tile(start, width):
            s = _nt(kbf[pl.ds(start, width), g * ATT_DH:(g + 1) * ATT_DH], q4) * scale
            b = bias_scr[pl.ds(start, width), :]
            s = s + jnp.concatenate([b] * group, axis=1)
            s_scr[pl.ds(start, width), :] = s
            return jnp.max(_rows8(s), axis=0)

        def pass1(t, m):
            return jnp.maximum(m, score_tile(pl.multiple_of(t * kt, kt), kt))

        m = jnp.max(lax.fori_loop(0, nt, pass1, score_tile(n_real, LANE)), axis=0, keepdims=True)

        def prob_tile(start, width):
            p = jnp.exp(s_scr[pl.ds(start, width), :] - m)
            pv = _mm(vtb[g * ATT_DH:(g + 1) * ATT_DH, pl.ds(start, width)], p.astype(BF16))
            return jnp.sum(_rows8(p), axis=0), pv

        def pass2(t, carry):
            l, acc = carry
            dl, dacc = prob_tile(pl.multiple_of(t * kt, kt), kt)
            return l + dl, acc + dacc

        l, acc = lax.fori_loop(0, nt, pass2, prob_tile(n_real, LANE))
        out_t = acc / jnp.sum(l, axis=0, keepdims=True)
        for j in range(group):
            hh = g * group + j
            o_ref[:, hh * ATT_DH:(hh + 1) * ATT_DH] = out_t[:, j * tq:(j + 1) * tq].T


def _attn_prompt(parts, mparts, *, batch, seq, n_meta, tq, kt):
    nqb = seq // tq
    n_sel = min(TOPK_MAX, (seq + n_meta) // 4)
    kvw = ATT_KV_HEADS * ATT_DH
    nk = seq + LANE
    return pl.pallas_call(
        functools.partial(_attn_prompt_kernel, n_sel=n_sel, n_meta=n_meta, kt=kt),
        grid=(batch, nqb),
        in_specs=[
            pl.BlockSpec((tq, D), lambda b, i: (b * nqb + i, COL_AQ // D)),
            pl.BlockSpec((tq, D), lambda b, i: (b * nqb + i, COL_IQ // D)),
            pl.BlockSpec((tq, LANE), lambda b, i: (b * nqb + i, COL_SM // LANE)),
            pl.BlockSpec((seq, kvw), lambda b, i: (b, COL_AK // kvw)),
            pl.BlockSpec((seq, kvw), lambda b, i: (b, COL_AV // kvw)),
            pl.BlockSpec((seq, LANE), lambda b, i: (b, COL_SM // LANE)),
            pl.BlockSpec((n_meta, kvw), lambda b, i: (0, COL_AK // kvw)),
            pl.BlockSpec((n_meta, kvw), lambda b, i: (0, COL_AV // kvw)),
            pl.BlockSpec((n_meta, LANE), lambda b, i: (0, COL_SM // LANE)),
        ],
        out_specs=pl.BlockSpec((tq, D), lambda b, i: (b * nqb + i, 0)),
        out_shape=jax.ShapeDtypeStruct((batch * seq, D), F32),
        scratch_shapes=[pltpu.VMEM((nk, kvw), BF16), pltpu.VMEM((kvw, nk), BF16), pltpu.VMEM((nk, IDX_DIM), BF16),
                        pltpu.VMEM((nk, tq), I32), pltpu.VMEM((nk, tq), F32),
                        pltpu.VMEM((nk, (ATT_HEADS // ATT_KV_HEADS) * tq), F32), pltpu.VMEM((SUBLANE, tq), F32)],
        compiler_params=_cparams(("arbitrary", "arbitrary")),
        name="sparse_attention_prompt",
    )(parts, parts, parts, parts, parts, parts, mparts, mparts, mparts)


def _attn_sample_kernel(pt_ref, *refs, n_pages, n_sel):
    kp = refs[0:n_pages]
    vp = refs[n_pages:2 * n_pages]
    ip = refs[2 * n_pages:3 * n_pages]
    q_ref, iq_ref, kn_ref, vn_ref, sm_ref, o_ref, kil, vil, kibf, cut_scr = refs[3 * n_pages:]
    del pt_ref
    t_new = q_ref.shape[0]
    page = ip[0].shape[0]
    past = n_pages * page
    nkv = ATT_KV_HEADS
    for j in range(n_pages):
        kil[j * nkv * page:(j + 1) * nkv * page, :] = kp[j][...].astype(BF16)
        vil[j * nkv * page:(j + 1) * nkv * page, :] = vp[j][...].astype(BF16)
        kibf[j * page:(j + 1) * page, :] = ip[j][...].astype(BF16)
    sm = sm_ref[...]
    zpad = jnp.zeros((LANE - t_new, nkv * ATT_DH), BF16)
    knew = jnp.concatenate([kn_ref[...].astype(BF16), zpad], axis=0)
    vnew = jnp.concatenate([vn_ref[...].astype(BF16), zpad], axis=0)
    kibf[past:, :] = jnp.concatenate(
        [sm[:, SM_IK:SM_IK + IDX_DIM].astype(BF16), jnp.zeros((LANE - t_new, IDX_DIM), BF16)], axis=0)

    iq = iq_ref[...]
    qi = jnp.concatenate([iq[:, h * IDX_DIM:(h + 1) * IDX_DIM] for h in range(IDX_HEADS)], axis=0).astype(BF16)
    wcol = jnp.concatenate([sm[:, SM_IW + h:SM_IW + h + 1] for h in range(IDX_HEADS)], axis=0)
    wcol = wcol * ((IDX_DIM ** -0.5) * (IDX_HEADS ** -0.5))

    def idx_scores(start, width):
        s = jnp.maximum(_nt(qi, kibf[start:start + width, :]), 0.0) * wcol
        acc = s[0:t_new]
        for h in range(1, IDX_HEADS):
            acc = acc + s[h * t_new:(h + 1) * t_new]
        return acc

    i_past = idx_scores(0, past)
    i_new = idx_scores(past, LANE)
    trow = lax.broadcasted_iota(I32, (t_new, LANE), 0)
    tlane = lax.broadcasted_iota(I32, (t_new, LANE), 1)
    key_past = _sort_key(i_past)
    key_new = _sort_key(jnp.where(tlane <= trow, i_new, NEG_INF))
    tiles = [key_past, key_new]

    def digit_body(it, thr):
        step = lax.shift_left(jnp.int32(1), 28 - 4 * it)
        digit = jnp.zeros((t_new, 1), I32)
        for j in range(1, 16):
            digit = digit + (_count_ge(tiles, thr + j * step) >= n_sel).astype(I32)
        return thr + digit * step

    thr = lax.fori_loop(0, 8, digit_body, jnp.full((t_new, 1), INT_MIN, I32))
    n_ge = _count_ge(tiles, thr)
    tie = (n_ge > n_sel) & (thr > KEY_NEG_INF)
    pos_past = lax.broadcasted_iota(I32, (1, past), 1)
    pos_new = past + lax.broadcasted_iota(I32, (1, LANE), 1)
    n_pos = past + t_new
    cut_scr[...] = jnp.full(cut_scr.shape, n_pos, I32)

    @pl.when(jnp.max(tie.astype(F32)) > 0.0)
    def _tie_cut():
        need = n_sel - _count_ge(tiles, thr + 1)
        nbits = max(1, (n_pos - 1).bit_length())

        def cut_body(it, cut):
            cand = cut - lax.shift_left(jnp.int32(1), nbits - 1 - it)
            cnt = (jnp.sum(((key_past == thr) & (pos_past <= cand)).astype(F32), axis=1, keepdims=True)
                   + jnp.sum(((key_new == thr) & (pos_new <= cand)).astype(F32), axis=1, keepdims=True))
            ok = (cand >= 0) & (cnt >= need)
            return jnp.where(ok, cand, cut)

        cut = lax.fori_loop(0, nbits, cut_body, jnp.full((t_new, 1), (1 << nbits) - 1, I32))
        cut_scr[...] = jnp.broadcast_to(jnp.where(tie, cut, n_pos), cut_scr.shape)

    cut = cut_scr[:, 0:1]

    def chosen(kk, pos):
        return ((kk > thr) | ((kk == thr) & (pos <= cut))) & (kk > KEY_NEG_INF)

    group = ATT_HEADS // ATT_KV_HEADS
    b_new = jnp.concatenate([jnp.where(chosen(key_new, pos_new), 0.0, NEG_INF)] * group, axis=0)
    sel01 = chosen(key_past, pos_past).astype(BF16)
    er = lax.broadcasted_iota(I32, (page, nkv * page), 0)
    ec = lax.broadcasted_iota(I32, (page, nkv * page), 1)
    expand = ((ec >= er * nkv) & (ec < (er + 1) * nkv)).astype(BF16)
    sel_il = jnp.concatenate([_mm(sel01[:, j * page:(j + 1) * page], expand) for j in range(n_pages)], axis=1)
    il_lane = lax.broadcasted_iota(I32, (1, nkv * past), 1)
    assert nkv & (nkv - 1) == 0
    il_head = il_lane & (nkv - 1)
    scale = ATT_DH ** -0.5
    q = q_ref[...]
    for g in range(ATT_KV_HEADS):
        qg = jnp.concatenate(
            [q[:, (g * group + j) * ATT_DH:(g * group + j + 1) * ATT_DH] for j in range(group)], axis=0).astype(BF16)
        b_g = jnp.where((sel_il > 0.5) & (il_head == g), 0.0, NEG_INF)
        s_p = _nt(qg, kil[...]) * scale + jnp.concatenate([b_g] * group, axis=0)
        s_n = _nt(qg, knew[:, g * ATT_DH:(g + 1) * ATT_DH]) * scale + b_new
        m = jnp.maximum(jnp.max(s_p, axis=1, keepdims=True), jnp.max(s_n, axis=1, keepdims=True))
        p_p = jnp.exp(s_p - m)
        p_n = jnp.exp(s_n - m)
        l = jnp.sum(p_p, axis=1, keepdims=True) + jnp.sum(p_n, axis=1, keepdims=True)
        acc = _mm(p_p.astype(BF16), vil[...]) + _mm(p_n.astype(BF16), vnew[:, g * ATT_DH:(g + 1) * ATT_DH])
        res = acc / l
        for j in range(group):
            hh = g * group + j
            o_ref[:, hh * ATT_DH:(hh + 1) * ATT_DH] = res[j * t_new:(j + 1) * t_new]


def _attn_sample(parts, cache_k, cache_v, cache_ik, page_table, *, t_new):
    db, n_pages = page_table.shape
    page = cache_k.shape[2]
    kvw = ATT_KV_HEADS * ATT_DH
    past = n_pages * page
    n_sel = min(TOPK_MAX, (past + t_new) // 4)

    n_pool = cache_k.shape[1]
    cache_k = cache_k.reshape(n_pool, page * ATT_KV_HEADS, ATT_DH)
    cache_v = cache_v.reshape(n_pool, page * ATT_KV_HEADS, ATT_DH)
    cache_ik = cache_ik.reshape(n_pool, page, IDX_DIM)

    def kv_page(j):
        return pl.BlockSpec((None, page * ATT_KV_HEADS, ATT_DH), lambda b, pt, j=j: (pt[b, j], 0, 0))

    def ik_page(j):
        return pl.BlockSpec((None, page, IDX_DIM), lambda b, pt, j=j: (pt[b, j], 0, 0))

    in_specs = ([kv_page(j) for j in range(n_pages)] + [kv_page(j) for j in range(n_pages)]
                + [ik_page(j) for j in range(n_pages)]
                + [pl.BlockSpec((t_new, D), lambda b, pt: (b, COL_AQ // D)),
                   pl.BlockSpec((t_new, D), lambda b, pt: (b, COL_IQ // D)),
                   pl.BlockSpec((t_new, kvw), lambda b, pt: (b, COL_AK // kvw)),
                   pl.BlockSpec((t_new, kvw), lambda b, pt: (b, COL_AV // kvw)),
                   pl.BlockSpec((t_new, LANE), lambda b, pt: (b, COL_SM // LANE))])
    grid_spec = pltpu.PrefetchScalarGridSpec(
        num_scalar_prefetch=1, grid=(db,), in_specs=in_specs,
        out_specs=pl.BlockSpec((t_new, D), lambda b, pt: (b, 0)),
        scratch_shapes=[pltpu.VMEM((past * ATT_KV_HEADS, ATT_DH), BF16), pltpu.VMEM((past * ATT_KV_HEADS, ATT_DH), BF16),
                        pltpu.VMEM((past + LANE, IDX_DIM), BF16), pltpu.VMEM((t_new, LANE), I32)])
    return pl.pallas_call(
        functools.partial(_attn_sample_kernel, n_pages=n_pages, n_sel=n_sel),
        grid_spec=grid_spec,
        out_shape=jax.ShapeDtypeStruct((db * t_new, D), F32),
        compiler_params=_cparams(("arbitrary",)),
        name="sparse_attention_sample",
    )(page_table, *([cache_k] * n_pages), *([cache_v] * n_pages), *([cache_ik] * n_pages),
      parts, parts, parts, parts, parts)


def _layer_norm(x, g, b):
    mu = jnp.mean(x, axis=-1, keepdims=True)
    xc = x - mu
    var = jnp.mean(xc * xc, axis=-1, keepdims=True)
    return xc * lax.rsqrt(var + LN_EPS) * g + b


def _finish_kernel(x_ref, og_ref, oa_ref, ga_ref, gb_ref, wbg_ref, wba_ref, wo_ref, g_ref, b_ref, h_ref, *, alpha):
    a = _mm(og_ref[...].astype(BF16), wbg_ref[...])
    b = _mm(oa_ref[...].astype(BF16), wba_ref[...])
    merged = jax.nn.sigmoid(ga_ref[...]) * a + jax.nn.sigmoid(gb_ref[...]) * b
    y = alpha * x_ref[...] + _mm(merged.astype(BF16), wo_ref[...])
    h_ref[...] = _layer_norm(y, g_ref[...], b_ref[...])


def _finish(x, o_gdn, o_att, parts, wbg, wba, wo, g, b, *, alpha, tm):
    n = x.shape[0]
    tm = min(tm, n)
    row = lambda i: (i, 0)
    full = lambda i: (0, 0)
    return pl.pallas_call(
        functools.partial(_finish_kernel, alpha=alpha),
        grid=(n // tm,),
        in_specs=[pl.BlockSpec((tm, D), row), pl.BlockSpec((tm, D), row), pl.BlockSpec((tm, D), row),
                  pl.BlockSpec((tm, D), lambda i: (i, COL_GA // D)), pl.BlockSpec((tm, D), lambda i: (i, COL_GB // D)),
                  pl.BlockSpec((D, D), full), pl.BlockSpec((D, D), full), pl.BlockSpec((D, D), full),
                  pl.BlockSpec((1, D), full), pl.BlockSpec((1, D), full)],
        out_specs=pl.BlockSpec((tm, D), row),
        out_shape=jax.ShapeDtypeStruct((n, D), F32),
        compiler_params=_cparams(("arbitrary",)),
        name="merge_layernorm",
    )(x, o_gdn, o_att, parts, parts, wbg, wba, wo, g, b)


_CAND = [(r0, r1) for r0 in range(PEER_TOPK) for r1 in range(PEER_TOPK) if (r0 + 1) * (r1 + 1) <= PEER_TOPK]
_CAND_OFF = [next(i for i, c in enumerate(_CAND) if c[0] == r0) for r0 in range(PEER_TOPK)]
_CAND_LEN = [sum(1 for c in _CAND if c[0] == r0) for r0 in range(PEER_TOPK)]
_CAND_ROWS = -(-len(_CAND) // SUBLANE) * SUBLANE


def _top_rows(s, n_top, break_ties):
    rows, cols = s.shape
    iota = lax.broadcasted_iota(I32, (rows, cols), 0).astype(F32)
    rank = jnp.full((rows, cols), float(n_top), F32)
    vals = []
    for r in range(n_top):
        m = jnp.max(s, axis=0, keepdims=True)
        hit = s == m
        if break_ties:
            hit = iota == jnp.min(jnp.where(hit, iota, float(rows)), axis=0, keepdims=True)
        vals.append(m)
        s = jnp.where(hit, NEG_INF, s)
        rank = jnp.where(hit, float(r), rank)
    n_ranked = jnp.sum((rank < float(n_top)).astype(F32), axis=0, keepdims=True)
    return jnp.concatenate(vals, axis=0), rank, n_ranked == float(n_top)


def _peer_kernel(h_ref, wq_ref, sk_ref, u_ref, vt_ref, g_ref, b_ref, y_ref,
                 hb_scr, q_scr, rank1_scr, bt_scr, nt_scr, at_scr, s1_scr,
                 s_scr, p_scr, acc_scr, *, alpha):
    tt = h_ref.shape[0]
    eb = u_ref.shape[0]
    e = pl.program_id(1)
    nk = PEER_NKEYS
    kk = PEER_TOPK
    tchunks = tt // LANE

    @pl.when(e == 0)
    def _prep():
        hb = h_ref[...].astype(BF16)
        hb_scr[...] = hb
        q_scr[...] = _mm(hb, wq_ref[...]).astype(BF16)
        acc_scr[...] = jnp.zeros(acc_scr.shape, F32)

        def scores(hd, carry):
            q0 = q_scr[:, pl.ds(pl.multiple_of(hd * 2 * nk, nk), nk)]
            q1 = q_scr[:, pl.ds(pl.multiple_of(hd * 2 * nk + nk, nk), nk)]
            at_scr[hd] = _nt(sk_ref[hd, 0], q0)
            s1_scr[hd] = _nt(sk_ref[hd, 1], q1)
            return carry

        lax.fori_loop(0, PEER_HEADS, scores, 0)

        def select_chunk(hd, c0, s0, s1, break_ties):
            a, rank0, ok0 = _top_rows(s0, kk, break_ties)
            b, rank1, ok1 = _top_rows(s1, kk, break_ties)
            pad = [jnp.full((_CAND_ROWS - len(_CAND), LANE), NEG_INF, F32)]
            cand = jnp.concatenate([a[r0:r0 + 1] + b[r1:r1 + 1] for r0, r1 in _CAND] + pad, axis=0)
            top, crank, okc = _top_rows(cand, kk, break_ties)
            zsum = jnp.sum(jnp.exp(top - top[0:1]), axis=0, keepdims=True)
            chosen = (crank < float(kk)).astype(F32)
            crow = lax.broadcasted_iota(I32, (_CAND_ROWS, 1), 0)
            nsel = jnp.zeros((nk, LANE), F32)
            for r0 in range(kk):
                in_group = (crow >= _CAND_OFF[r0]) & (crow < _CAND_OFF[r0] + _CAND_LEN[r0])
                cnt = jnp.sum(jnp.where(in_group, chosen, 0.0), axis=0, keepdims=True)
                nsel = nsel + jnp.where(rank0 == float(r0), cnt, 0.0)
            rank1_scr[hd, :, pl.ds(c0, LANE)] = rank1.astype(BF16)
            nt_scr[hd, :, pl.ds(c0, LANE)] = nsel
            at_scr[hd, :, pl.ds(c0, LANE)] = jnp.exp(s0 - a[0:1])
            bt_scr[hd, :, pl.ds(c0, LANE)] = (jnp.exp(s1 - b[0:1]) / zsum).astype(BF16)
            return jnp.min((ok0 & ok1 & okc).astype(F32)) > 0.5

        def select(idx, carry):
            hd = idx // tchunks
            c0 = pl.multiple_of((idx % tchunks) * LANE, LANE)
            s0 = at_scr[hd, :, pl.ds(c0, LANE)]
            s1 = s1_scr[hd, :, pl.ds(c0, LANE)]
            tie_free = select_chunk(hd, c0, s0, s1, False)

            @pl.when(jnp.logical_not(tie_free))
            def _():
                select_chunk(hd, c0, s0, s1, True)

            return carry

        lax.fori_loop(0, PEER_HEADS * tchunks, select, 0)

    s_scr[...] = _nt(u_ref[...], hb_scr[...])
    groups = eb // nk

    def tile(idx, carry):
        ii = idx // tchunks
        tc = idx % tchunks
        i = e * groups + ii
        r0 = pl.multiple_of(ii * nk, nk)
        c0 = pl.multiple_of(tc * LANE, LANE)
        gate = jnp.zeros((nk, LANE), BF16)
        base = pl.multiple_of((i // SUBLANE) * SUBLANE, SUBLANE)
        pick = lax.broadcasted_iota(I32, (SUBLANE, LANE), 0) == i % SUBLANE
        for hd in range(PEER_HEADS):
            nblk = nt_scr[hd, pl.ds(base, SUBLANE), pl.ds(c0, LANE)]
            ablk = at_scr[hd, pl.ds(base, SUBLANE), pl.ds(c0, LANE)]
            nrow = jnp.sum(jnp.where(pick, nblk, 0.0), axis=0, keepdims=True).astype(BF16)
            arow = jnp.sum(jnp.where(pick, ablk, 0.0), axis=0, keepdims=True).astype(BF16)
            bsel = jnp.where(rank1_scr[hd, :, pl.ds(c0, LANE)] < nrow, bt_scr[hd, :, pl.ds(c0, LANE)],
                             jnp.zeros((nk, LANE), BF16))
            gate = gate + bsel * arow
        s = s_scr[pl.ds(r0, nk), pl.ds(c0, LANE)]
        act = 0.5 * s * (1.0 + lax.erf(s * (2.0 ** -0.5)))
        p_scr[pl.ds(r0, nk), pl.ds(c0, LANE)] = gate * act.astype(BF16)
        return carry

    lax.fori_loop(0, groups * tchunks, tile, 0, unroll=2)
    acc_scr[...] += _mm(vt_ref[...], p_scr[...])

    @pl.when(e == pl.num_programs(1) - 1)
    def _():
        y = alpha * h_ref[...] + acc_scr[...].T
        y_ref[...] = _layer_norm(y, g_ref[...], b_ref[...])


def _peer(h, wq, sk, u, vt, g, b, *, alpha, tt, eb):
    n = h.shape[0]
    n_exp = u.shape[0]
    nk = PEER_NKEYS
    return pl.pallas_call(
        functools.partial(_peer_kernel, alpha=alpha),
        grid=(n // tt, n_exp // eb),
        in_specs=[pl.BlockSpec((tt, D), lambda t, e: (t, 0)),
                  pl.BlockSpec(wq.shape, lambda t, e: (0, 0)),
                  pl.BlockSpec(sk.shape, lambda t, e: (0, 0, 0, 0)),
                  pl.BlockSpec((eb, D), lambda t, e: (e, 0)),
                  pl.BlockSpec((D, eb), lambda t, e: (0, e)),
                  pl.BlockSpec((1, D), lambda t, e: (0, 0)),
                  pl.BlockSpec((1, D), lambda t, e: (0, 0))],
        out_specs=pl.BlockSpec((tt, D), lambda t, e: (t, 0)),
        out_shape=jax.ShapeDtypeStruct((n, D), F32),
        scratch_shapes=[pltpu.VMEM((tt, D), BF16), pltpu.VMEM((tt, wq.shape[1]), BF16),
                        pltpu.VMEM((PEER_HEADS, nk, tt), BF16), pltpu.VMEM((PEER_HEADS, nk, tt), BF16),
                        pltpu.VMEM((PEER_HEADS, nk, tt), F32), pltpu.VMEM((PEER_HEADS, nk, tt), F32),
                        pltpu.VMEM((PEER_HEADS, nk, tt), F32),
                        pltpu.VMEM((eb, tt), F32), pltpu.VMEM((eb, tt), BF16), pltpu.VMEM((D, tt), F32)],
        compiler_params=_cparams(("arbitrary", "arbitrary")),
        name="peer_ffn",
    )(h, wq, sk, u, vt, g, b)


def _permute_w_in(w):
    gq = 3 * GDN_HEADS * GDN_D
    gv = GDN_HEADS * GDN_D
    aq = ATT_HEADS * ATT_DH
    akv = ATT_KV_HEADS * ATT_DH
    iqw = IDX_HEADS * IDX_DIM
    sizes = (gq, gv, GDN_HEADS, GDN_HEADS, aq, akv, akv, iqw, IDX_DIM, IDX_HEADS, D, D)
    offs = [0]
    for s in sizes:
        offs.append(offs[-1] + s)
    seg = [w[:, offs[i]:offs[i + 1]] for i in range(len(sizes))]
    qkv, z, b, a, q, k, v, iq, ik, iw, ga, gb = seg
    pad = jnp.zeros((w.shape[0], LANE - IDX_DIM - 2 * GDN_HEADS - IDX_HEADS), w.dtype)
    return jnp.concatenate([qkv, z, q, iq, ga, gb, k, v, ik, b, a, iw, pad], axis=1).astype(BF16)


def kernel(x_prompt, x_sample, cache_k, cache_v, cache_idx_k, state_conv, state_delta, page_table, meta_tokens, w_in, conv_w, a_log, dt_bias, gdn_norm_g, w_branch_gdn, w_branch_attn, w_out, ln1_g, ln1_b, peer_wq, peer_subkeys, peer_u, peer_v, ln2_g, ln2_b):
    depth = w_in.shape[0]
    assert depth == 1, "single-layer step"
    batch, seq, d = x_prompt.shape
    db, t_new, _ = x_sample.shape
    n_meta = meta_tokens.shape[0]
    assert d == D and seq % GDN_CHUNK == 0 and n_meta % SUBLANE == 0 and n_meta <= GDN_CHUNK
    alpha = (2 * depth) ** 0.25
    qkv_w = 3 * GDN_HEADS * GDN_D
    kvw = ATT_KV_HEADS * ATT_DH

    w_r = _permute_w_in(w_in[0])
    xp = x_prompt.reshape(batch * seq, D)
    xs = x_sample.reshape(db * t_new, D)
    tm = 512 if (batch * seq) % 512 == 0 else 256
    parts_p = _project(xp, w_r, tm)
    parts_s = _project(xs, w_r, min(tm, db * t_new))
    parts_m = _project(meta_tokens.astype(F32), w_r, n_meta)

    hp = jnp.zeros((2, LANE), F32).at[0, SM_A:SM_A + GDN_HEADS].set(a_log[0]).at[1, SM_A:SM_A + GDN_HEADS].set(dt_bias[0])
    cw = conv_w[0]
    ng = gdn_norm_g[0].reshape(1, GDN_D)
    zero_prev = jnp.zeros((1, SUBLANE, qkv_w), F32)
    zero_state = jnp.zeros((1, GDN_HEADS, GDN_D, GDN_D), F32)
    _, s_meta = _gdn(parts_m, zero_prev, zero_state, cw, hp, ng, batch=1, n_chunks=1, c_in=n_meta, n_valid=n_meta,
                     n_seq=1, shared_init=True)
    meta_prev = parts_m[n_meta - SUBLANE:, :qkv_w].reshape(1, SUBLANE, qkv_w)
    og_p, p_delta = _gdn(parts_p, meta_prev, s_meta, cw, hp, ng, batch=batch, n_chunks=seq // GDN_CHUNK,
                         c_in=GDN_CHUNK, n_valid=GDN_CHUNK, n_seq=1, shared_init=True)
    per = GDN_CHUNK // t_new
    assert per * t_new == GDN_CHUNK and t_new % SUBLANE == 0 and db % per == 0
    samp_prev = jnp.pad(state_conv.reshape(db, CONV_W - 1, qkv_w), ((0, 0), (SUBLANE - (CONV_W - 1), 0), (0, 0)))
    og_s, s_delta = _gdn(parts_s, samp_prev, state_delta.reshape(db, GDN_HEADS, GDN_D, GDN_D).astype(F32), cw, hp, ng,
                         batch=db // per, n_chunks=1, c_in=GDN_CHUNK, n_valid=GDN_CHUNK, n_seq=per, shared_init=False)

    tq = 128
    oa_p = _attn_prompt(parts_p, parts_m, batch=batch, seq=seq, n_meta=n_meta, tq=tq, kt=min(512, seq))
    oa_s = _attn_sample(parts_s, cache_k, cache_v, cache_idx_k, page_table, t_new=t_new)

    wbg = w_branch_gdn[0].astype(BF16)
    wba = w_branch_attn[0].astype(BF16)
    wo = w_out[0].astype(BF16)
    g1 = ln1_g[0].reshape(1, D)
    b1 = ln1_b[0].reshape(1, D)
    h_p = _finish(xp, og_p, oa_p, parts_p, wbg, wba, wo, g1, b1, alpha=alpha, tm=256)
    h_s = _finish(xs, og_s, oa_s, parts_s, wbg, wba, wo, g1, b1, alpha=alpha, tm=256)

    wq = peer_wq[0].astype(BF16)
    sk = peer_subkeys[0].astype(BF16)
    u = peer_u[0].astype(BF16)
    vt = peer_v[0].astype(BF16).T
    g2 = ln2_g[0].reshape(1, D)
    b2 = ln2_b[0].reshape(1, D)
    tt = 512 if (batch * seq) % 512 == 0 else 256
    y_p = _peer(h_p, wq, sk, u, vt, g2, b2, alpha=alpha, tt=tt, eb=512)
    y_s = _peer(h_s, wq, sk, u, vt, g2, b2, alpha=alpha, tt=min(tt, db * t_new), eb=512)

    y_prompt = y_p.reshape(batch, seq, D)
    y_sample = y_s.reshape(db, t_new, D)
    pp = parts_p.reshape(batch, seq, N_COLS)
    ps = parts_s.reshape(db, t_new, N_COLS)
    p_conv = pp[:, seq - (CONV_W - 1):, :qkv_w][None]
    s_conv = jnp.concatenate([state_conv[0].astype(F32), ps[:, :, :qkv_w]], axis=1)[:, -(CONV_W - 1):][None]

    def with_meta(col, width):
        m = jnp.broadcast_to(parts_m[None, :, col:col + width], (batch, n_meta, width))
        return jnp.concatenate([m, pp[:, :, col:col + width]], axis=1)

    p_k = with_meta(COL_AK, kvw).reshape(1, batch, seq + n_meta, ATT_KV_HEADS, ATT_DH)
    p_v = with_meta(COL_AV, kvw).reshape(1, batch, seq + n_meta, ATT_KV_HEADS, ATT_DH)
    p_idx_k = with_meta(COL_SM + SM_IK, IDX_DIM)[None]
    s_k = ps[:, :, COL_AK:COL_AK + kvw].reshape(1, db, t_new, ATT_KV_HEADS, ATT_DH)
    s_v = ps[:, :, COL_AV:COL_AV + kvw].reshape(1, db, t_new, ATT_KV_HEADS, ATT_DH)
    s_idx_k = ps[:, :, COL_SM + SM_IK:COL_SM + SM_IK + IDX_DIM][None]
    return (y_prompt, y_sample, p_conv, p_delta[None], p_k, p_v, p_idx_k, s_conv, s_delta[None].astype(state_delta.dtype),
            s_k, s_v, s_idx_k)
```

```python
import functools

import jax
import jax.numpy as jnp
from jax import lax
from jax.experimental import pallas as pl
from jax.experimental.pallas import tpu as pltpu

F32 = jnp.float32
BF16 = jnp.bfloat16
I32 = jnp.int32

GDN_HEADS = 8
GDN_D = 128
CONV_W = 4
GDN_CHUNK = 64
ATT_HEADS = 8
ATT_KV_HEADS = 2
ATT_DH = 128
IDX_HEADS = 16
IDX_DIM = 64
TOPK_MAX = 256
PEER_HEADS = 8
PEER_NKEYS = 128
PEER_TOPK = 16
LN_EPS = 1e-5
RMS_EPS = 1e-6

LANE = 128
SUBLANE = 8
VMEM_LIMIT = 56 * 1024 * 1024

D = 1024
COL_QKV = 0
COL_Z = 3072
COL_AQ = 4096
COL_IQ = 5120
COL_GA = 6144
COL_GB = 7168
COL_AK = 8192
COL_AV = 8448
COL_SM = 8704
N_COLS = 8832
SM_IK = 0
SM_B = 64
SM_A = 72
SM_IW = 80

NEG_INF = float("-inf")
INT_MIN = -(2 ** 31)


def _nt(a, b):
    return lax.dot_general(a, b, (((1,), (1,)), ((), ())), preferred_element_type=F32)


def _mm(a, b):
    return jnp.dot(a, b, preferred_element_type=F32)


def _mm_f32(a, b):
    return jnp.dot(a, b, preferred_element_type=F32, precision=lax.Precision.HIGHEST)


def _cparams(sem):
    return pltpu.CompilerParams(dimension_semantics=sem, vmem_limit_bytes=VMEM_LIMIT)


def _proj_kernel(x_ref, w_ref, o_ref):
    o_ref[...] = _mm(x_ref[...].astype(BF16), w_ref[...])


def _project(x, w_bf, tm):
    n = x.shape[0]
    tn = N_COLS // 3
    return pl.pallas_call(
        _proj_kernel,
        grid=(3, n // tm),
        in_specs=[pl.BlockSpec((tm, D), lambda j, i: (i, 0)),
                  pl.BlockSpec((D, tn), lambda j, i: (0, j))],
        out_specs=pl.BlockSpec((tm, tn), lambda j, i: (i, j)),
        out_shape=jax.ShapeDtypeStruct((n, N_COLS), F32),
        compiler_params=_cparams(("arbitrary", "arbitrary")),
        name="in_projection",
    )(x, w_bf)


def _sibling_mask(ri, ci, lvl):
    return ((ri >> (lvl + 1)) == (ci >> (lvl + 1))) & (((ri >> lvl) & 1) == 1) & (((ci >> lvl) & 1) == 0)


_BATCH0 = ((0,), (0,))


def _bmm(a, b):
    return lax.dot_general(a, b, (((2,), (1,)), _BATCH0), preferred_element_type=F32)


def _bnt(a, b):
    return lax.dot_general(a, b, (((2,), (2,)), _BATCH0), preferred_element_type=F32)


def _btn(a, b):
    return lax.dot_general(a, b, (((1,), (1,)), _BATCH0), preferred_element_type=F32)


def _split(x):
    hi = x.astype(BF16)
    return hi, (x - hi.astype(F32)).astype(BF16)


def _bmm3(a, b):
    ah, al = a
    bh, bl = b
    return _bmm(ah, bh) + (_bmm(ah, bl) + _bmm(al, bh))


def _gdn_kernel(qkv_ref, z_ref, sm_ref, prev_ref, s0_ref, cw_ref, hp_ref, ng_ref,
                o_ref, sfin_ref, s_scr, ext_scr, *, c_in, n_valid, n_seq):
    nh = GDN_HEADS
    hd = GDN_D
    stride = ext_scr.shape[0] // n_seq
    L = stride - SUBLANE
    C = L * n_seq
    ls = L.bit_length() - 1
    c = pl.program_id(1)

    @pl.when(c == 0)
    def _():
        s_scr[...] = s0_ref[...]
        for b in range(n_seq):
            ext_scr[b * stride:b * stride + SUBLANE, :] = prev_ref[b]

    u = qkv_ref[...]
    sm = sm_ref[...]
    z = z_ref[...]
    if c_in < C:
        u = jnp.concatenate([u, jnp.zeros((C - c_in, u.shape[1]), F32)], axis=0)
        sm = jnp.concatenate([sm, jnp.zeros((C - c_in, LANE), F32)], axis=0)
        z = jnp.concatenate([z, jnp.zeros((C - c_in, z.shape[1]), F32)], axis=0)
    cw = cw_ref[...]
    pieces = []
    for b in range(n_seq):
        base = b * stride + SUBLANE
        ub_ = u[b * L:(b + 1) * L]
        ext_scr[base:base + L, :] = ub_
        acc = ub_ * cw[CONV_W - 1:CONV_W, :]
        for i in range(CONV_W - 1):
            off = base - (CONV_W - 1) + i
            acc = acc + ext_scr[off:off + L, :] * cw[i:i + 1, :]
        pieces.append(acc)
    conv = pieces[0] if n_seq == 1 else jnp.concatenate(pieces, axis=0)
    if n_seq == 1:
        tail = ext_scr[c_in:c_in + SUBLANE, :]
        ext_scr[0:SUBLANE, :] = tail
    qkv = conv * jax.nn.sigmoid(conv)

    row = lax.broadcasted_iota(I32, (C, 1), 0)
    hp = hp_ref[...]
    xs = sm + hp[1:2, :]
    softplus = jnp.maximum(xs, 0.0) + jnp.log1p(jnp.exp(-jnp.abs(xs)))
    la = -jnp.exp(hp[0:1, :]) * softplus
    beta = jax.nn.sigmoid(sm)
    if n_valid < C:
        valid = row < n_valid
        qkv = jnp.where(valid, qkv, 0.0)
        la = jnp.where(valid, la, 0.0)
        beta = jnp.where(valid, beta, 0.0)

    ri = lax.broadcasted_iota(I32, (C, C), 0)
    ci = lax.broadcasted_iota(I32, (C, C), 1)
    same = (ri >> ls) == (ci >> ls)
    incl = same & (ri >= ci)
    strict = same & (ri > ci)
    g = _mm_f32(incl.astype(F32), la)
    g_end = _mm_f32((ci == (ri | (L - 1))).astype(F32), g)
    gt = g.T
    eye = (ri == ci).astype(F32)

    q_l, k_l, kd_l, dec_l, lows_l, rhs_l, eg_l, z_l = [], [], [], [], [], [], [], []
    for h in range(nh):
        col = SM_A + h
        gcol = g[:, col:col + 1]
        grow = gt[col:col + 1, :]
        bcol = beta[:, SM_B + h:SM_B + h + 1]
        q = qkv[:, h * hd:(h + 1) * hd]
        k = qkv[:, nh * hd + h * hd:nh * hd + (h + 1) * hd]
        v = qkv[:, 2 * nh * hd + h * hd:2 * nh * hd + (h + 1) * hd]
        q = q * lax.rsqrt(jnp.sum(q * q, axis=-1, keepdims=True) + RMS_EPS) * (hd ** -0.5)
        k = k * lax.rsqrt(jnp.sum(k * k, axis=-1, keepdims=True) + RMS_EPS)
        dec = jnp.exp(jnp.where(incl, gcol - grow, NEG_INF))
        eg = jnp.exp(gcol)
        q_l.append(q.astype(BF16))
        k_l.append(k.astype(BF16))
        kd_l.append((k * jnp.exp(g_end[:, col:col + 1] - gcol)).astype(BF16))
        dec_l.append(dec)
        lows_l.append(bcol * jnp.where(strict, dec, 0.0))
        rhs_l.append(jnp.concatenate([bcol * v, (bcol * eg) * k], axis=1))
        eg_l.append(eg)
        z_l.append(z[:, h * hd:(h + 1) * hd])
    qb = jnp.stack(q_l)
    kb = jnp.stack(k_l)
    kd = jnp.stack(kd_l)
    dec = jnp.stack(dec_l)
    eg = jnp.stack(eg_l)

    low = jnp.stack(lows_l) * _bnt(kb, kb)
    qk = _bnt(qb, kb)
    inv = eye[None] - jnp.where(_sibling_mask(ri, ci, 0)[None], low, 0.0)
    for lvl in range(1, ls):
        off = jnp.where(_sibling_mask(ri, ci, lvl)[None], low, 0.0)
        inv_s = _split(inv)
        inv = inv - _bmm3(inv_s, _split(_bmm3(_split(off), inv_s)))
    uw = _bmm3(_split(inv), _split(jnp.stack(rhs_l)))
    w_b = uw[:, :, hd:].astype(BF16)

    ws = None
    qs = None
    s_old = []
    for b in range(n_seq):
        S = s_scr[b]
        s_old.append(S)
        Sb = S.astype(BF16)
        if n_seq == 1:
            wm, qm = w_b, qb
        else:
            mine = ((row >> ls) == b)[None]
            wm = jnp.where(mine, w_b, jnp.zeros_like(w_b))
            qm = jnp.where(mine, qb, jnp.zeros_like(qb))
        ws = _bmm(wm, Sb) if ws is None else ws + _bmm(wm, Sb)
        qs = _bmm(qm, Sb) if qs is None else qs + _bmm(qm, Sb)
    ub = (uw[:, :, :hd] - ws).astype(BF16)
    o = eg * qs + _bmm((dec * qk).astype(BF16), ub)
    for b in range(n_seq):
        last = b * L + L - 1
        glast = jnp.stack([g[last:last + 1, SM_A + h:SM_A + h + 1] for h in range(nh)])
        kdm = kd if n_seq == 1 else jnp.where(((row >> ls) == b)[None], kd, jnp.zeros_like(kd))
        s_scr[b] = jnp.exp(glast) * s_old[b] + _btn(kdm, ub)

    on = o * lax.rsqrt(jnp.mean(o * o, axis=-1, keepdims=True) + RMS_EPS) * ng_ref[...]
    zz = jnp.stack(z_l)
    res = on * (zz * jax.nn.sigmoid(zz))
    for h in range(nh):
        o_ref[:, h * hd:(h + 1) * hd] = res[h, :c_in]

    @pl.when(c == pl.num_programs(1) - 1)
    def _():
        sfin_ref[...] = s_scr[...]


def _gdn(parts, prev, s0, cw, hp, ng, *, batch, n_chunks, c_in, n_valid, n_seq, shared_init):
    n = parts.shape[0]
    C = max(c_in, GDN_CHUNK)
    assert n_seq == 1 or (n_chunks == 1 and c_in == C)
    L = C // n_seq
    init = (lambda b, c: (0, 0, 0)) if shared_init else (lambda b, c: (b, 0, 0))
    init4 = (lambda b, c: (0, 0, 0, 0)) if shared_init else (lambda b, c: (b, 0, 0, 0))
    qkv_w = 3 * GDN_HEADS * GDN_D
    v_w = GDN_HEADS * GDN_D
    return pl.pallas_call(
        functools.partial(_gdn_kernel, c_in=c_in, n_valid=n_valid, n_seq=n_seq),
        grid=(batch, n_chunks),
        in_specs=[
            pl.BlockSpec((c_in, qkv_w), lambda b, c: (b * n_chunks + c, COL_QKV // qkv_w)),
            pl.BlockSpec((c_in, v_w), lambda b, c: (b * n_chunks + c, COL_Z // v_w)),
            pl.BlockSpec((c_in, LANE), lambda b, c: (b * n_chunks + c, COL_SM // LANE)),
            pl.BlockSpec((n_seq, SUBLANE, qkv_w), init),
            pl.BlockSpec((n_seq, GDN_HEADS, GDN_D, GDN_D), init4),
            pl.BlockSpec((CONV_W, qkv_w), lambda b, c: (0, 0)),
            pl.BlockSpec((2, LANE), lambda b, c: (0, 0)),
            pl.BlockSpec((1, GDN_D), lambda b, c: (0, 0)),
        ],
        out_specs=[
            pl.BlockSpec((c_in, v_w), lambda b, c: (b * n_chunks + c, 0)),
            pl.BlockSpec((n_seq, GDN_HEADS, GDN_D, GDN_D), lambda b, c: (b, 0, 0, 0)),
        ],
        out_shape=[jax.ShapeDtypeStruct((n, v_w), F32),
                   jax.ShapeDtypeStruct((batch * n_seq, GDN_HEADS, GDN_D, GDN_D), F32)],
        scratch_shapes=[pltpu.VMEM((n_seq, GDN_HEADS, GDN_D, GDN_D), F32),
                        pltpu.VMEM((n_seq * (SUBLANE + L), qkv_w), F32)],
        compiler_params=_cparams(("arbitrary", "arbitrary")),
        name="gated_deltanet",
    )(parts, parts, parts, prev, s0, cw, hp, ng)


def _sort_key(x):
    b = pltpu.bitcast(x + 0.0, I32)
    return b ^ ((b >> 31) & 0x7FFFFFFF)


KEY_NEG_INF = -(2 ** 31) + 0x7FFFFF
SEARCH_BITS_PER_TRIP = 4


def _count_ge(key_tiles, cand):
    tot = None
    for kt in key_tiles:
        c = jnp.sum((kt >= cand).astype(F32), axis=1, keepdims=True)
        tot = c if tot is None else tot + c
    return tot


def _rows8(x):
    return x.reshape(x.shape[0] // SUBLANE, SUBLANE, x.shape[1])


def _attn_prompt_kernel(q_ref, iq_ref, smq_ref, k_ref, v_ref, smk_ref, mk_ref, mv_ref, msm_ref,
                        o_ref, kbf, vtb, kibf, key_scr, bias_scr, s_scr, cut_scr, *, n_sel, n_meta, kt):
    tq = q_ref.shape[0]
    n_real = k_ref.shape[0]
    kvw = k_ref.shape[1]
    qb_i = pl.program_id(1)
    nt = (qb_i * tq + tq + kt - 1) // kt
    group = ATT_HEADS // ATT_KV_HEADS

    @pl.when(qb_i == 0)
    def _():
        zrows = LANE - n_meta
        kbf[0:n_real, :] = k_ref[...].astype(BF16)
        kbf[n_real:, :] = jnp.concatenate([mk_ref[...], jnp.zeros((zrows, kvw), F32)], axis=0).astype(BF16)
        vtb[:, 0:n_real] = v_ref[...].T.astype(BF16)
        vtb[:, n_real:] = jnp.concatenate([mv_ref[...], jnp.zeros((zrows, kvw), F32)], axis=0).T.astype(BF16)
        kibf[0:n_real, :] = smk_ref[...][:, SM_IK:SM_IK + IDX_DIM].astype(BF16)
        kibf[n_real:, :] = jnp.concatenate(
            [msm_ref[...][:, SM_IK:SM_IK + IDX_DIM], jnp.zeros((zrows, IDX_DIM), F32)], axis=0).astype(BF16)

    iq = iq_ref[...]
    iq_stack = jnp.concatenate([iq[:, h * IDX_DIM:(h + 1) * IDX_DIM] for h in range(IDX_HEADS)], axis=0).astype(BF16)
    w_t = smq_ref[...].T * ((IDX_DIM ** -0.5) * (IDX_HEADS ** -0.5))
    qpos = qb_i * tq + lax.broadcasted_iota(I32, (1, tq), 1)

    def index_scores(start, width):
        s_all = _nt(kibf[pl.ds(start, width), :], iq_stack)
        acc = jnp.zeros((width, tq), F32)
        for h in range(IDX_HEADS):
            acc = acc + jnp.maximum(s_all[:, h * tq:(h + 1) * tq], 0.0) * w_t[SM_IW + h:SM_IW + h + 1, :]
        return acc

    def real_tile(t, carry):
        start = pl.multiple_of(t * kt, kt)
        kpos = start + lax.broadcasted_iota(I32, (kt, 1), 0)
        key_scr[pl.ds(start, kt), :] = _sort_key(jnp.where(kpos <= qpos, index_scores(start, kt), NEG_INF))
        return carry

    lax.fori_loop(0, nt, real_tile, 0)
    mrow = lax.broadcasted_iota(I32, (LANE, 1), 0)
    key_scr[n_real:, :] = _sort_key(jnp.where(mrow < n_meta, index_scores(n_real, LANE), NEG_INF))

    def count_ge(cand):
        def body(t, acc):
            kk = key_scr[pl.ds(pl.multiple_of(t * kt, kt), kt), :]
            return acc + jnp.sum(_rows8((kk >= cand).astype(F32)), axis=0)

        acc = lax.fori_loop(0, nt, body, jnp.zeros((SUBLANE, tq), F32))
        acc = acc + jnp.sum(_rows8((key_scr[n_real:, :] >= cand).astype(F32)), axis=0)
        return jnp.sum(acc, axis=0, keepdims=True)

    few = qpos + 1 + n_meta <= n_sel

    def unsettled(state):
        it, _, cnt = state
        return (it < 32) & (jnp.max(jnp.where(few | (cnt == n_sel), 0.0, 1.0)) > 0.0)

    def refine(state):
        it, thr, cnt = state
        for step in range(SEARCH_BITS_PER_TRIP):
            cand = thr + lax.shift_left(jnp.int32(1), 31 - step - it)
            c = count_ge(cand)
            ok = c >= n_sel
            thr = jnp.where(ok, cand, thr)
            cnt = jnp.where(ok, c, cnt)
        return it + SEARCH_BITS_PER_TRIP, thr, cnt

    n_all = (nt * kt + LANE).astype(F32)
    _, thr, cnt = lax.while_loop(
        unsettled, refine, (jnp.int32(0), jnp.full((1, tq), INT_MIN, I32), jnp.full((1, tq), 1.0, F32) * n_all))

    tie = (cnt > n_sel) & (thr > KEY_NEG_INF) & jnp.logical_not(few)
    n_pos = n_real + n_meta
    cut_scr[...] = jnp.full((SUBLANE, tq), float(n_pos), F32)

    def pos_of(start, width, is_meta):
        r = lax.broadcasted_iota(I32, (width, 1), 0)
        return r if is_meta else n_meta + start + r

    @pl.when(jnp.max(tie.astype(F32)) > 0.0)
    def _tie_cut():
        need = n_sel - count_ge(thr + 1)
        nbits = max(1, (n_pos - 1).bit_length())

        def count_eq_le(cut):
            def body(t, acc):
                start = pl.multiple_of(t * kt, kt)
                kk = key_scr[pl.ds(start, kt), :]
                return acc + jnp.sum(_rows8(((kk == thr) & (pos_of(start, kt, False) <= cut)).astype(F32)), axis=0)

            acc = lax.fori_loop(0, nt, body, jnp.zeros((SUBLANE, tq), F32))
            mk = key_scr[n_real:, :]
            acc = acc + jnp.sum(_rows8(((mk == thr) & (pos_of(0, LANE, True) <= cut)).astype(F32)), axis=0)
            return jnp.sum(acc, axis=0, keepdims=True)

        def body(it, cut):
            cand = cut - lax.shift_left(jnp.int32(1), nbits - 1 - it)
            ok = (cand >= 0) & (count_eq_le(cand) >= need)
            return jnp.where(ok, cand, cut)

        cut = lax.fori_loop(0, nbits, body, jnp.full((1, tq), (1 << nbits) - 1, I32))
        cut = jnp.where(tie, cut, n_pos).astype(F32)
        cut_scr[...] = jnp.broadcast_to(cut, (SUBLANE, tq))

    cut = cut_scr[0:1, :].astype(I32)

    def bias_tile(start, width, is_meta):
        kk = key_scr[pl.ds(start, width), :]
        sel = (kk > thr) | ((kk == thr) & (pos_of(start, width, is_meta) <= cut))
        sel = sel & (kk > KEY_NEG_INF)
        bias_scr[pl.ds(start, width), :] = jnp.where(sel, 0.0, NEG_INF)

    def bias_body(t, carry):
        bias_tile(pl.multiple_of(t * kt, kt), kt, False)
        return carry

    lax.fori_loop(0, nt, bias_body, 0)
    bias_tile(n_real, LANE, True)

    scale = ATT_DH ** -0.5
    for g in range(ATT_KV_HEADS):
        q4 = jnp.concatenate([q_ref[:, (g * group + j) * ATT_DH:(g * group + j + 1) * ATT_DH] for j in range(group)],
                             axis=0).astype(BF16)

        def score_tile(start, width):
            s = _nt(kbf[pl.ds(start, width), g * ATT_DH:(g + 1) * ATT_DH], q4) * scale
            b = bias_scr[pl.ds(start, width), :]
            s = s + jnp.concatenate([b] * group, axis=1)
            s_scr[pl.ds(start, width), :] = s
            return jnp.max(_rows8(s), axis=0)

        def pass1(t, m):
            return jnp.maximum(m, score_tile(pl.multiple_of(t * kt, kt), kt))

        m = jnp.max(lax.fori_loop(0, nt, pass1, score_tile(n_real, LANE)), axis=0, keepdims=True)

        def prob_tile(start, width):
            p = jnp.exp(s_scr[pl.ds(start, width), :] - m)
            pv = _mm(vtb[g * ATT_DH:(g + 1) * ATT_DH, pl.ds(start, width)], p.astype(BF16))
            return jnp.sum(_rows8(p), axis=0), pv

        def pass2(t, carry):
            l, acc = carry
            dl, dacc = prob_tile(pl.multiple_of(t * kt, kt), kt)
            return l + dl, acc + dacc

        l, acc = lax.fori_loop(0, nt, pass2, prob_tile(n_real, LANE))
        out_t = acc / jnp.sum(l, axis=0, keepdims=True)
        for j in range(group):
            hh = g * group + j
            o_ref[:, hh * ATT_DH:(hh + 1) * ATT_DH] = out_t[:, j * tq:(j + 1) * tq].T


def _attn_prompt(parts, mparts, *, batch, seq, n_meta, tq, kt):
    nqb = seq // tq
    n_sel = min(TOPK_MAX, (seq + n_meta) // 4)
    kvw = ATT_KV_HEADS * ATT_DH
    nk = seq + LANE
    return pl.pallas_call(
        functools.partial(_attn_prompt_kernel, n_sel=n_sel, n_meta=n_meta, kt=kt),
        grid=(batch, nqb),
        in_specs=[
            pl.BlockSpec((tq, D), lambda b, i: (b * nqb + i, COL_AQ // D)),
            pl.BlockSpec((tq, D), lambda b, i: (b * nqb + i, COL_IQ // D)),
            pl.BlockSpec((tq, LANE), lambda b, i: (b * nqb + i, COL_SM // LANE)),
            pl.BlockSpec((seq, kvw), lambda b, i: (b, COL_AK // kvw)),
            pl.BlockSpec((seq, kvw), lambda b, i: (b, COL_AV // kvw)),
            pl.BlockSpec((seq, LANE), lambda b, i: (b, COL_SM // LANE)),
            pl.BlockSpec((n_meta, kvw), lambda b, i: (0, COL_AK // kvw)),
            pl.BlockSpec((n_meta, kvw), lambda b, i: (0, COL_AV // kvw)),
            pl.BlockSpec((n_meta, LANE), lambda b, i: (0, COL_SM // LANE)),
        ],
        out_specs=pl.BlockSpec((tq, D), lambda b, i: (b * nqb + i, 0)),
        out_shape=jax.ShapeDtypeStruct((batch * seq, D), F32),
        scratch_shapes=[pltpu.VMEM((nk, kvw), BF16), pltpu.VMEM((kvw, nk), BF16), pltpu.VMEM((nk, IDX_DIM), BF16),
                        pltpu.VMEM((nk, tq), I32), pltpu.VMEM((nk, tq), F32),
                        pltpu.VMEM((nk, (ATT_HEADS // ATT_KV_HEADS) * tq), F32), pltpu.VMEM((SUBLANE, tq), F32)],
        compiler_params=_cparams(("arbitrary", "arbitrary")),
        name="sparse_attention_prompt",
    )(parts, parts, parts, parts, parts, parts, mparts, mparts, mparts)


def _attn_sample_kernel(pt_ref, *refs, n_pages, n_sel):
    kp = refs[0:n_pages]
    vp = refs[n_pages:2 * n_pages]
    ip = refs[2 * n_pages:3 * n_pages]
    q_ref, iq_ref, kn_ref, vn_ref, sm_ref, o_ref, kil, vil, kibf, cut_scr = refs[3 * n_pages:]
    del pt_ref
    t_new = q_ref.shape[0]
    page = ip[0].shape[0]
    past = n_pages * page
    nkv = ATT_KV_HEADS
    for j in range(n_pages):
        kil[j * nkv * page:(j + 1) * nkv * page, :] = kp[j][...].astype(BF16)
        vil[j * nkv * page:(j + 1) * nkv * page, :] = vp[j][...].astype(BF16)
        kibf[j * page:(j + 1) * page, :] = ip[j][...].astype(BF16)
    sm = sm_ref[...]
    zpad = jnp.zeros((LANE - t_new, nkv * ATT_DH), BF16)
    knew = jnp.concatenate([kn_ref[...].astype(BF16), zpad], axis=0)
    vnew = jnp.concatenate([vn_ref[...].astype(BF16), zpad], axis=0)
    kibf[past:, :] = jnp.concatenate(
        [sm[:, SM_IK:SM_IK + IDX_DIM].astype(BF16), jnp.zeros((LANE - t_new, IDX_DIM), BF16)], axis=0)

    iq = iq_ref[...]
    qi = jnp.concatenate([iq[:, h * IDX_DIM:(h + 1) * IDX_DIM] for h in range(IDX_HEADS)], axis=0).astype(BF16)
    wcol = jnp.concatenate([sm[:, SM_IW + h:SM_IW + h + 1] for h in range(IDX_HEADS)], axis=0)
    wcol = wcol * ((IDX_DIM ** -0.5) * (IDX_HEADS ** -0.5))

    def idx_scores(start, width):
        s = jnp.maximum(_nt(qi, kibf[start:start + width, :]), 0.0) * wcol
        acc = s[0:t_new]
        for h in range(1, IDX_HEADS):
            acc = acc + s[h * t_new:(h + 1) * t_new]
        return acc

    i_past = idx_scores(0, past)
    i_new = idx_scores(past, LANE)
    trow = lax.broadcasted_iota(I32, (t_new, LANE), 0)
    tlane = lax.broadcasted_iota(I32, (t_new, LANE), 1)
    key_past = _sort_key(i_past)
    key_new = _sort_key(jnp.where(tlane <= trow, i_new, NEG_INF))
    tiles = [key_past, key_new]

    def digit_body(it, thr):
        step = lax.shift_left(jnp.int32(1), 28 - 4 * it)
        digit = jnp.zeros((t_new, 1), I32)
        for j in range(1, 16):
            digit = digit + (_count_ge(tiles, thr + j * step) >= n_sel).astype(I32)
        return thr + digit * step

    thr = lax.fori_loop(0, 8, digit_body, jnp.full((t_new, 1), INT_MIN, I32))
    n_ge = _count_ge(tiles, thr)
    tie = (n_ge > n_sel) & (thr > KEY_NEG_INF)
    pos_past = lax.broadcasted_iota(I32, (1, past), 1)
    pos_new = past + lax.broadcasted_iota(I32, (1, LANE), 1)
    n_pos = past + t_new
    cut_scr[...] = jnp.full(cut_scr.shape, n_pos, I32)

    @pl.when(jnp.max(tie.astype(F32)) > 0.0)
    def _tie_cut():
        need = n_sel - _count_ge(tiles, thr + 1)
        nbits = max(1, (n_pos - 1).bit_length())

        def cut_body(it, cut):
            cand = cut - lax.shift_left(jnp.int32(1), nbits - 1 - it)
            cnt = (jnp.sum(((key_past == thr) & (pos_past <= cand)).astype(F32), axis=1, keepdims=True)
                   + jnp.sum(((key_new == thr) & (pos_new <= cand)).astype(F32), axis=1, keepdims=True))
            ok = (cand >= 0) & (cnt >= need)
            return jnp.where(ok, cand, cut)

        cut = lax.fori_loop(0, nbits, cut_body, jnp.full((t_new, 1), (1 << nbits) - 1, I32))
        cut_scr[...] = jnp.broadcast_to(jnp.where(tie, cut, n_pos), cut_scr.shape)

    cut = cut_scr[:, 0:1]

    def chosen(kk, pos):
        return ((kk > thr) | ((kk == thr) & (pos <= cut))) & (kk > KEY_NEG_INF)

    group = ATT_HEADS // ATT_KV_HEADS
    b_new = jnp.concatenate([jnp.where(chosen(key_new, pos_new), 0.0, NEG_INF)] * group, axis=0)
    sel01 = chosen(key_past, pos_past).astype(BF16)
    er = lax.broadcasted_iota(I32, (page, nkv * page), 0)
    ec = lax.broadcasted_iota(I32, (page, nkv * page), 1)
    expand = ((ec >= er * nkv) & (ec < (er + 1) * nkv)).astype(BF16)
    sel_il = jnp.concatenate([_mm(sel01[:, j * page:(j + 1) * page], expand) for j in range(n_pages)], axis=1)
    il_lane = lax.broadcasted_iota(I32, (1, nkv * past), 1)
    assert nkv & (nkv - 1) == 0
    il_head = il_lane & (nkv - 1)
    scale = ATT_DH ** -0.5
    q = q_ref[...]
    for g in range(ATT_KV_HEADS):
        qg = jnp.concatenate(
            [q[:, (g * group + j) * ATT_DH:(g * group + j + 1) * ATT_DH] for j in range(group)], axis=0).astype(BF16)
        b_g = jnp.where((sel_il > 0.5) & (il_head == g), 0.0, NEG_INF)
        s_p = _nt(qg, kil[...]) * scale + jnp.concatenate([b_g] * group, axis=0)
        s_n = _nt(qg, knew[:, g * ATT_DH:(g + 1) * ATT_DH]) * scale + b_new
        m = jnp.maximum(jnp.max(s_p, axis=1, keepdims=True), jnp.max(s_n, axis=1, keepdims=True))
        p_p = jnp.exp(s_p - m)
        p_n = jnp.exp(s_n - m)
        l = jnp.sum(p_p, axis=1, keepdims=True) + jnp.sum(p_n, axis=1, keepdims=True)
        acc = _mm(p_p.astype(BF16), vil[...]) + _mm(p_n.astype(BF16), vnew[:, g * ATT_DH:(g + 1) * ATT_DH])
        res = acc / l
        for j in range(group):
            hh = g * group + j
            o_ref[:, hh * ATT_DH:(hh + 1) * ATT_DH] = res[j * t_new:(j + 1) * t_new]


def _attn_sample(parts, cache_k, cache_v, cache_ik, page_table, *, t_new):
    db, n_pages = page_table.shape
    page = cache_k.shape[2]
    kvw = ATT_KV_HEADS * ATT_DH
    past = n_pages * page
    n_sel = min(TOPK_MAX, (past + t_new) // 4)

    n_pool = cache_k.shape[1]
    cache_k = cache_k.reshape(n_pool, page * ATT_KV_HEADS, ATT_DH)
    cache_v = cache_v.reshape(n_pool, page * ATT_KV_HEADS, ATT_DH)
    cache_ik = cache_ik.reshape(n_pool, page, IDX_DIM)

    def kv_page(j):
        return pl.BlockSpec((None, page * ATT_KV_HEADS, ATT_DH), lambda b, pt, j=j: (pt[b, j], 0, 0))

    def ik_page(j):
        return pl.BlockSpec((None, page, IDX_DIM), lambda b, pt, j=j: (pt[b, j], 0, 0))

    in_specs = ([kv_page(j) for j in range(n_pages)] + [kv_page(j) for j in range(n_pages)]
                + [ik_page(j) for j in range(n_pages)]
                + [pl.BlockSpec((t_new, D), lambda b, pt: (b, COL_AQ // D)),
                   pl.BlockSpec((t_new, D), lambda b, pt: (b, COL_IQ // D)),
                   pl.BlockSpec((t_new, kvw), lambda b, pt: (b, COL_AK // kvw)),
                   pl.BlockSpec((t_new, kvw), lambda b, pt: (b, COL_AV // kvw)),
                   pl.BlockSpec((t_new, LANE), lambda b, pt: (b, COL_SM // LANE))])
    grid_spec = pltpu.PrefetchScalarGridSpec(
        num_scalar_prefetch=1, grid=(db,), in_specs=in_specs,
        out_specs=pl.BlockSpec((t_new, D), lambda b, pt: (b, 0)),
        scratch_shapes=[pltpu.VMEM((past * ATT_KV_HEADS, ATT_DH), BF16), pltpu.VMEM((past * ATT_KV_HEADS, ATT_DH), BF16),
                        pltpu.VMEM((past + LANE, IDX_DIM), BF16), pltpu.VMEM((t_new, LANE), I32)])
    return pl.pallas_call(
        functools.partial(_attn_sample_kernel, n_pages=n_pages, n_sel=n_sel),
        grid_spec=grid_spec,
        out_shape=jax.ShapeDtypeStruct((db * t_new, D), F32),
        compiler_params=_cparams(("arbitrary",)),
        name="sparse_attention_sample",
    )(page_table, *([cache_k] * n_pages), *([cache_v] * n_pages), *([cache_ik] * n_pages),
      parts, parts, parts, parts, parts)


def _layer_norm(x, g, b):
    mu = jnp.mean(x, axis=-1, keepdims=True)
    xc = x - mu
    var = jnp.mean(xc * xc, axis=-1, keepdims=True)
    return xc * lax.rsqrt(var + LN_EPS) * g + b


def _finish_kernel(x_ref, og_ref, oa_ref, ga_ref, gb_ref, wbg_ref, wba_ref, wo_ref, g_ref, b_ref, h_ref, *, alpha):
    a = _mm(og_ref[...].astype(BF16), wbg_ref[...])
    b = _mm(oa_ref[...].astype(BF16), wba_ref[...])
    merged = jax.nn.sigmoid(ga_ref[...]) * a + jax.nn.sigmoid(gb_ref[...]) * b
    y = alpha * x_ref[...] + _mm(merged.astype(BF16), wo_ref[...])
    h_ref[...] = _layer_norm(y, g_ref[...], b_ref[...])


def _finish(x, o_gdn, o_att, parts, wbg, wba, wo, g, b, *, alpha, tm):
    n = x.shape[0]
    tm = min(tm, n)
    row = lambda i: (i, 0)
    full = lambda i: (0, 0)
    return pl.pallas_call(
        functools.partial(_finish_kernel, alpha=alpha),
        grid=(n // tm,),
        in_specs=[pl.BlockSpec((tm, D), row), pl.BlockSpec((tm, D), row), pl.BlockSpec((tm, D), row),
                  pl.BlockSpec((tm, D), lambda i: (i, COL_GA // D)), pl.BlockSpec((tm, D), lambda i: (i, COL_GB // D)),
                  pl.BlockSpec((D, D), full), pl.BlockSpec((D, D), full), pl.BlockSpec((D, D), full),
                  pl.BlockSpec((1, D), full), pl.BlockSpec((1, D), full)],
        out_specs=pl.BlockSpec((tm, D), row),
        out_shape=jax.ShapeDtypeStruct((n, D), F32),
        compiler_params=_cparams(("arbitrary",)),
        name="merge_layernorm",
    )(x, o_gdn, o_att, parts, parts, wbg, wba, wo, g, b)


_CAND = [(r0, r1) for r0 in range(PEER_TOPK) for r1 in range(PEER_TOPK) if (r0 + 1) * (r1 + 1) <= PEER_TOPK]
_CAND_OFF = [next(i for i, c in enumerate(_CAND) if c[0] == r0) for r0 in range(PEER_TOPK)]
_CAND_LEN = [sum(1 for c in _CAND if c[0] == r0) for r0 in range(PEER_TOPK)]
_CAND_ROWS = -(-len(_CAND) // SUBLANE) * SUBLANE


def _top_rows(s, n_top, break_ties):
    rows, cols = s.shape
    iota = lax.broadcasted_iota(I32, (rows, cols), 0).astype(F32)
    rank = jnp.full((rows, cols), float(n_top), F32)
    vals = []
    for r in range(n_top):
        m = jnp.max(s, axis=0, keepdims=True)
        hit = s == m
        if break_ties:
            hit = iota == jnp.min(jnp.where(hit, iota, float(rows)), axis=0, keepdims=True)
        vals.append(m)
        s = jnp.where(hit, NEG_INF, s)
        rank = jnp.where(hit, float(r), rank)
    n_ranked = jnp.sum((rank < float(n_top)).astype(F32), axis=0, keepdims=True)
    return jnp.concatenate(vals, axis=0), rank, n_ranked == float(n_top)


def _peer_kernel(h_ref, wq_ref, sk_ref, u0_ref, ub_ref, un_ref, vtp_ref, vta_ref, vtl_ref, g_ref, b_ref, y_ref,
                 hb_scr, q_scr, rank1_scr, bt_scr, nt_scr, at_scr, s1_scr,
                 sa_scr, sb_scr, pa_scr, pb_scr, acc_scr, *, alpha, n_steps):
    tt = h_ref.shape[0]
    eb = ub_ref.shape[0]
    e = pl.program_id(1)
    nk = PEER_NKEYS
    kk = PEER_TOPK
    tchunks = tt // LANE

    @pl.when(e == 0)
    def _prep():
        hb = h_ref[...].astype(BF16)
        hb_scr[...] = hb
        q_scr[...] = _mm(hb, wq_ref[...]).astype(BF16)
        acc_scr[...] = jnp.zeros(acc_scr.shape, F32)

        def scores(hd, carry):
            q0 = q_scr[:, pl.ds(pl.multiple_of(hd * 2 * nk, nk), nk)]
            q1 = q_scr[:, pl.ds(pl.multiple_of(hd * 2 * nk + nk, nk), nk)]
            at_scr[hd] = _nt(sk_ref[hd, 0], q0)
            s1_scr[hd] = _nt(sk_ref[hd, 1], q1)
            return carry

        lax.fori_loop(0, PEER_HEADS, scores, 0)

        def select_chunk(hd, c0, s0, s1, break_ties):
            a, rank0, ok0 = _top_rows(s0, kk, break_ties)
            b, rank1, ok1 = _top_rows(s1, kk, break_ties)
            pad = [jnp.full((_CAND_ROWS - len(_CAND), LANE), NEG_INF, F32)]
            cand = jnp.concatenate([a[r0:r0 + 1] + b[r1:r1 + 1] for r0, r1 in _CAND] + pad, axis=0)
            top, crank, okc = _top_rows(cand, kk, break_ties)
            zsum = jnp.sum(jnp.exp(top - top[0:1]), axis=0, keepdims=True)
            chosen = (crank < float(kk)).astype(F32)
            crow = lax.broadcasted_iota(I32, (_CAND_ROWS, 1), 0)
            nsel = jnp.zeros((nk, LANE), F32)
            for r0 in range(kk):
                in_group = (crow >= _CAND_OFF[r0]) & (crow < _CAND_OFF[r0] + _CAND_LEN[r0])
                cnt = jnp.sum(jnp.where(in_group, chosen, 0.0), axis=0, keepdims=True)
                nsel = nsel + jnp.where(rank0 == float(r0), cnt, 0.0)
            rank1_scr[hd, :, pl.ds(c0, LANE)] = rank1.astype(BF16)
            nt_scr[hd, :, pl.ds(c0, LANE)] = nsel
            at_scr[hd, :, pl.ds(c0, LANE)] = jnp.exp(s0 - a[0:1])
            bt_scr[hd, :, pl.ds(c0, LANE)] = (jnp.exp(s1 - b[0:1]) / zsum).astype(BF16)
            return jnp.min((ok0 & ok1 & okc).astype(F32)) > 0.5

        def select(idx, carry):
            hd = idx // tchunks
            c0 = pl.multiple_of((idx % tchunks) * LANE, LANE)
            s0 = at_scr[hd, :, pl.ds(c0, LANE)]
            s1 = s1_scr[hd, :, pl.ds(c0, LANE)]
            tie_free = select_chunk(hd, c0, s0, s1, False)

            @pl.when(jnp.logical_not(tie_free))
            def _():
                select_chunk(hd, c0, s0, s1, True)

            return carry

        lax.fori_loop(0, PEER_HEADS * tchunks, select, 0)

        sa_scr[1] = _nt(u0_ref[...], hb_scr[...])
        pb_scr[1] = jnp.zeros(pb_scr.shape[1:], BF16)

    groups = eb // nk

    def weights(block, s_ref, p_ref):
        for ii in range(groups):
            i = block * groups + ii
            base = pl.multiple_of((i // SUBLANE) * SUBLANE, SUBLANE)
            pick = lax.broadcasted_iota(I32, (SUBLANE, LANE), 0) == i % SUBLANE
            for tc in range(tchunks):
                rows = slice(ii * nk, (ii + 1) * nk)
                cols = slice(tc * LANE, (tc + 1) * LANE)
                gate = jnp.zeros((nk, LANE), BF16)
                for hd in range(PEER_HEADS):
                    nblk = nt_scr[hd, pl.ds(base, SUBLANE), cols]
                    ablk = at_scr[hd, pl.ds(base, SUBLANE), cols]
                    nrow = jnp.sum(jnp.where(pick, nblk, 0.0), axis=0, keepdims=True).astype(BF16)
                    arow = jnp.sum(jnp.where(pick, ablk, 0.0), axis=0, keepdims=True).astype(BF16)
                    bsel = jnp.where(rank1_scr[hd, :, cols] < nrow, bt_scr[hd, :, cols], jnp.zeros((nk, LANE), BF16))
                    gate = gate + bsel * arow
                s = s_ref[rows, cols]
                act = 0.5 * s * (1.0 + lax.erf(s * (2.0 ** -0.5)))
                p_ref[rows, cols] = gate * act.astype(BF16)

    def step(cur, nxt):
        sb_scr[...] = _nt(ub_ref[...], hb_scr[...])
        weights(2 * e, sa_scr.at[cur], pa_scr)
        acc_scr[...] += _mm(vtp_ref[...], pb_scr[cur])
        weights(2 * e + 1, sb_scr, pb_scr.at[nxt])
        sa_scr[nxt] = _nt(un_ref[...], hb_scr[...])
        acc_scr[...] += _mm(vta_ref[...], pa_scr[...])

    @pl.when(e % 2 == 0)
    def _():
        step(1, 0)

    @pl.when(e % 2 == 1)
    def _():
        step(0, 1)

    last_slot = (n_steps - 1) % 2

    @pl.when(e == n_steps - 1)
    def _():
        y = alpha * h_ref[...] + (acc_scr[...] + _mm(vtl_ref[...], pb_scr[last_slot])).T
        y_ref[...] = _layer_norm(y, g_ref[...], b_ref[...])


def _peer(h, wq, sk, u, vt, g, b, *, alpha, tt, eb):
    n = h.shape[0]
    n_exp = u.shape[0]
    nk = PEER_NKEYS
    nb = n_exp // eb
    assert nb % 2 == 0
    return pl.pallas_call(
        functools.partial(_peer_kernel, alpha=alpha, n_steps=nb // 2),
        grid=(n // tt, nb // 2),
        in_specs=[pl.BlockSpec((tt, D), lambda t, e: (t, 0)),
                  pl.BlockSpec(wq.shape, lambda t, e: (0, 0)),
                  pl.BlockSpec(sk.shape, lambda t, e: (0, 0, 0, 0)),
                  pl.BlockSpec((eb, D), lambda t, e: (0, 0)),
                  pl.BlockSpec((eb, D), lambda t, e: (2 * e + 1, 0)),
                  pl.BlockSpec((eb, D), lambda t, e: (jnp.minimum(2 * e + 2, nb - 1), 0)),
                  pl.BlockSpec((D, eb), lambda t, e: (0, jnp.maximum(2 * e - 1, 0))),
                  pl.BlockSpec((D, eb), lambda t, e: (0, 2 * e)),
                  pl.BlockSpec((D, eb), lambda t, e: (0, nb - 1)),
                  pl.BlockSpec((1, D), lambda t, e: (0, 0)),
                  pl.BlockSpec((1, D), lambda t, e: (0, 0))],
        out_specs=pl.BlockSpec((tt, D), lambda t, e: (t, 0)),
        out_shape=jax.ShapeDtypeStruct((n, D), F32),
        scratch_shapes=[pltpu.VMEM((tt, D), BF16), pltpu.VMEM((tt, wq.shape[1]), BF16),
                        pltpu.VMEM((PEER_HEADS, nk, tt), BF16), pltpu.VMEM((PEER_HEADS, nk, tt), BF16),
                        pltpu.VMEM((PEER_HEADS, nk, tt), F32), pltpu.VMEM((PEER_HEADS, nk, tt), F32),
                        pltpu.VMEM((PEER_HEADS, nk, tt), F32),
                        pltpu.VMEM((2, eb, tt), F32), pltpu.VMEM((eb, tt), F32),
                        pltpu.VMEM((eb, tt), BF16), pltpu.VMEM((2, eb, tt), BF16), pltpu.VMEM((D, tt), F32)],
        compiler_params=_cparams(("arbitrary", "arbitrary")),
        name="peer_ffn",
    )(h, wq, sk, u, u, u, vt, vt, vt, g, b)


def _permute_w_in(w):
    gq = 3 * GDN_HEADS * GDN_D
    gv = GDN_HEADS * GDN_D
    aq = ATT_HEADS * ATT_DH
    akv = ATT_KV_HEADS * ATT_DH
    iqw = IDX_HEADS * IDX_DIM
    sizes = (gq, gv, GDN_HEADS, GDN_HEADS, aq, akv, akv, iqw, IDX_DIM, IDX_HEADS, D, D)
    offs = [0]
    for s in sizes:
        offs.append(offs[-1] + s)
    seg = [w[:, offs[i]:offs[i + 1]] for i in range(len(sizes))]
    qkv, z, b, a, q, k, v, iq, ik, iw, ga, gb = seg
    pad = jnp.zeros((w.shape[0], LANE - IDX_DIM - 2 * GDN_HEADS - IDX_HEADS), w.dtype)
    return jnp.concatenate([qkv, z, q, iq, ga, gb, k, v, ik, b, a, iw, pad], axis=1).astype(BF16)


def kernel(x_prompt, x_sample, cache_k, cache_v, cache_idx_k, state_conv, state_delta, page_table, meta_tokens, w_in, conv_w, a_log, dt_bias, gdn_norm_g, w_branch_gdn, w_branch_attn, w_out, ln1_g, ln1_b, peer_wq, peer_subkeys, peer_u, peer_v, ln2_g, ln2_b):
    depth = w_in.shape[0]
    assert depth == 1, "single-layer step"
    batch, seq, d = x_prompt.shape
    db, t_new, _ = x_sample.shape
    n_meta = meta_tokens.shape[0]
    assert d == D and seq % GDN_CHUNK == 0 and n_meta % SUBLANE == 0 and n_meta <= GDN_CHUNK
    alpha = (2 * depth) ** 0.25
    qkv_w = 3 * GDN_HEADS * GDN_D
    kvw = ATT_KV_HEADS * ATT_DH

    w_r = _permute_w_in(w_in[0])
    xp = x_prompt.reshape(batch * seq, D)
    xs = x_sample.reshape(db * t_new, D)
    tm = 512 if (batch * seq) % 512 == 0 else 256
    parts_p = _project(xp, w_r, tm)
    parts_s = _project(xs, w_r, min(tm, db * t_new))
    parts_m = _project(meta_tokens.astype(F32), w_r, n_meta)

    hp = jnp.zeros((2, LANE), F32).at[0, SM_A:SM_A + GDN_HEADS].set(a_log[0]).at[1, SM_A:SM_A + GDN_HEADS].set(dt_bias[0])
    cw = conv_w[0]
    ng = gdn_norm_g[0].reshape(1, GDN_D)
    zero_prev = jnp.zeros((1, SUBLANE, qkv_w), F32)
    zero_state = jnp.zeros((1, GDN_HEADS, GDN_D, GDN_D), F32)
    _, s_meta = _gdn(parts_m, zero_prev, zero_state, cw, hp, ng, batch=1, n_chunks=1, c_in=n_meta, n_valid=n_meta,
                     n_seq=1, shared_init=True)
    meta_prev = parts_m[n_meta - SUBLANE:, :qkv_w].reshape(1, SUBLANE, qkv_w)
    og_p, p_delta = _gdn(parts_p, meta_prev, s_meta, cw, hp, ng, batch=batch, n_chunks=seq // GDN_CHUNK,
                         c_in=GDN_CHUNK, n_valid=GDN_CHUNK, n_seq=1, shared_init=True)
    per = GDN_CHUNK // t_new
    assert per * t_new == GDN_CHUNK and t_new % SUBLANE == 0 and db % per == 0
    samp_prev = jnp.pad(state_conv.reshape(db, CONV_W - 1, qkv_w), ((0, 0), (SUBLANE - (CONV_W - 1), 0), (0, 0)))
    og_s, s_delta = _gdn(parts_s, samp_prev, state_delta.reshape(db, GDN_HEADS, GDN_D, GDN_D).astype(F32), cw, hp, ng,
                         batch=db // per, n_chunks=1, c_in=GDN_CHUNK, n_valid=GDN_CHUNK, n_seq=per, shared_init=False)

    tq = 128
    oa_p = _attn_prompt(parts_p, parts_m, batch=batch, seq=seq, n_meta=n_meta, tq=tq, kt=min(512, seq))
    oa_s = _attn_sample(parts_s, cache_k, cache_v, cache_idx_k, page_table, t_new=t_new)

    wbg = w_branch_gdn[0].astype(BF16)
    wba = w_branch_attn[0].astype(BF16)
    wo = w_out[0].astype(BF16)
    g1 = ln1_g[0].reshape(1, D)
    b1 = ln1_b[0].reshape(1, D)
    h_p = _finish(xp, og_p, oa_p, parts_p, wbg, wba, wo, g1, b1, alpha=alpha, tm=256)
    h_s = _finish(xs, og_s, oa_s, parts_s, wbg, wba, wo, g1, b1, alpha=alpha, tm=256)

    wq = peer_wq[0].astype(BF16)
    sk = peer_subkeys[0].astype(BF16)
    u = peer_u[0].astype(BF16)
    vt = peer_v[0].astype(BF16).T
    g2 = ln2_g[0].reshape(1, D)
    b2 = ln2_b[0].reshape(1, D)
    tt = 512 if (batch * seq) % 512 == 0 else 256
    y_p = _peer(h_p, wq, sk, u, vt, g2, b2, alpha=alpha, tt=tt, eb=512)
    y_s = _peer(h_s, wq, sk, u, vt, g2, b2, alpha=alpha, tt=min(tt, db * t_new), eb=512)

    y_prompt = y_p.reshape(batch, seq, D)
    y_sample = y_s.reshape(db, t_new, D)
    pp = parts_p.reshape(batch, seq, N_COLS)
    ps = parts_s.reshape(db, t_new, N_COLS)
    p_conv = pp[:, seq - (CONV_W - 1):, :qkv_w][None]
    s_conv = jnp.concatenate([state_conv[0].astype(F32), ps[:, :, :qkv_w]], axis=1)[:, -(CONV_W - 1):][None]

    def with_meta(col, width):
        m = jnp.broadcast_to(parts_m[None, :, col:col + width], (batch, n_meta, width))
        return jnp.concatenate([m, pp[:, :, col:col + width]], axis=1)

    p_k = with_meta(COL_AK, kvw).reshape(1, batch, seq + n_meta, ATT_KV_HEADS, ATT_DH)
    p_v = with_meta(COL_AV, kvw).reshape(1, batch, seq + n_meta, ATT_KV_HEADS, ATT_DH)
    p_idx_k = with_meta(COL_SM + SM_IK, IDX_DIM)[None]
    s_k = ps[:, :, COL_AK:COL_AK + kvw].reshape(1, db, t_new, ATT_KV_HEADS, ATT_DH)
    s_v = ps[:, :, COL_AV:COL_AV + kvw].reshape(1, db, t_new, ATT_KV_HEADS, ATT_DH)
    s_idx_k = ps[:, :, COL_SM + SM_IK:COL_SM + SM_IK + IDX_DIM][None]
    return (y_prompt, y_sample, p_conv, p_delta[None], p_k, p_v, p_idx_k, s_conv, s_delta[None].astype(state_delta.dtype),
            s_k, s_v, s_idx_k)
```

```python
import functools

import jax
import jax.numpy as jnp
from jax import lax
from jax.experimental import pallas as pl
from jax.experimental.pallas import tpu as pltpu

F32 = jnp.float32
BF16 = jnp.bfloat16
I32 = jnp.int32

GDN_HEADS = 8
GDN_D = 128
CONV_W = 4
GDN_CHUNK = 64
ATT_HEADS = 8
ATT_KV_HEADS = 2
ATT_DH = 128
IDX_HEADS = 16
IDX_DIM = 64
TOPK_MAX = 256
PEER_HEADS = 8
PEER_NKEYS = 128
PEER_TOPK = 16
LN_EPS = 1e-5
RMS_EPS = 1e-6

LANE = 128
SUBLANE = 8
VMEM_LIMIT = 56 * 1024 * 1024

D = 1024
COL_QKV = 0
COL_Z = 3072
COL_AQ = 4096
COL_IQ = 5120
COL_GA = 6144
COL_GB = 7168
COL_AK = 8192
COL_AV = 8448
COL_SM = 8704
N_COLS = 8832
SM_IK = 0
SM_B = 64
SM_A = 72
SM_IW = 80

NEG_INF = float("-inf")
INT_MIN = -(2 ** 31)


def _nt(a, b):
    return lax.dot_general(a, b, (((1,), (1,)), ((), ())), preferred_element_type=F32)


def _mm(a, b):
    return jnp.dot(a, b, preferred_element_type=F32)


def _mm_f32(a, b):
    return jnp.dot(a, b, preferred_element_type=F32, precision=lax.Precision.HIGHEST)


def _cparams(sem):
    return pltpu.CompilerParams(dimension_semantics=sem, vmem_limit_bytes=VMEM_LIMIT)


def _proj_kernel(x_ref, w_ref, o_ref):
    o_ref[...] = _mm(x_ref[...].astype(BF16), w_ref[...])


def _project(x, w_bf, tm):
    n = x.shape[0]
    tn = N_COLS // 3
    return pl.pallas_call(
        _proj_kernel,
        grid=(3, n // tm),
        in_specs=[pl.BlockSpec((tm, D), lambda j, i: (i, 0)),
                  pl.BlockSpec((D, tn), lambda j, i: (0, j))],
        out_specs=pl.BlockSpec((tm, tn), lambda j, i: (i, j)),
        out_shape=jax.ShapeDtypeStruct((n, N_COLS), F32),
        compiler_params=_cparams(("arbitrary", "arbitrary")),
        name="in_projection",
    )(x, w_bf)


def _sibling_mask(ri, ci, lvl):
    return ((ri >> (lvl + 1)) == (ci >> (lvl + 1))) & (((ri >> lvl) & 1) == 1) & (((ci >> lvl) & 1) == 0)


_BATCH0 = ((0,), (0,))


def _bmm(a, b):
    return lax.dot_general(a, b, (((2,), (1,)), _BATCH0), preferred_element_type=F32)


def _bnt(a, b):
    return lax.dot_general(a, b, (((2,), (2,)), _BATCH0), preferred_element_type=F32)


def _btn(a, b):
    return lax.dot_general(a, b, (((1,), (1,)), _BATCH0), preferred_element_type=F32)


def _split(x):
    hi = x.astype(BF16)
    return hi, (x - hi.astype(F32)).astype(BF16)


def _bmm3(a, b):
    ah, al = a
    bh, bl = b
    return _bmm(ah, bh) + (_bmm(ah, bl) + _bmm(al, bh))


def _gdn_kernel(qkv_ref, z_ref, sm_ref, prev_ref, s0_ref, cw_ref, hp_ref, ng_ref,
                o_ref, sfin_ref, s_scr, ext_scr, *, c_in, n_valid, n_seq):
    nh = GDN_HEADS
    hd = GDN_D
    stride = ext_scr.shape[0] // n_seq
    L = stride - SUBLANE
    C = L * n_seq
    ls = L.bit_length() - 1
    c = pl.program_id(1)

    @pl.when(c == 0)
    def _():
        s_scr[...] = s0_ref[...]
        for b in range(n_seq):
            ext_scr[b * stride:b * stride + SUBLANE, :] = prev_ref[b]

    u = qkv_ref[...]
    sm = sm_ref[...]
    z = z_ref[...]
    if c_in < C:
        u = jnp.concatenate([u, jnp.zeros((C - c_in, u.shape[1]), F32)], axis=0)
        sm = jnp.concatenate([sm, jnp.zeros((C - c_in, LANE), F32)], axis=0)
        z = jnp.concatenate([z, jnp.zeros((C - c_in, z.shape[1]), F32)], axis=0)
    cw = cw_ref[...]
    pieces = []
    for b in range(n_seq):
        base = b * stride + SUBLANE
        ub_ = u[b * L:(b + 1) * L]
        ext_scr[base:base + L, :] = ub_
        acc = ub_ * cw[CONV_W - 1:CONV_W, :]
        for i in range(CONV_W - 1):
            off = base - (CONV_W - 1) + i
            acc = acc + ext_scr[off:off + L, :] * cw[i:i + 1, :]
        pieces.append(acc)
    conv = pieces[0] if n_seq == 1 else jnp.concatenate(pieces, axis=0)
    if n_seq == 1:
        tail = ext_scr[c_in:c_in + SUBLANE, :]
        ext_scr[0:SUBLANE, :] = tail
    qkv = conv * jax.nn.sigmoid(conv)

    row = lax.broadcasted_iota(I32, (C, 1), 0)
    hp = hp_ref[...]
    xs = sm + hp[1:2, :]
    softplus = jnp.maximum(xs, 0.0) + jnp.log1p(jnp.exp(-jnp.abs(xs)))
    la = -jnp.exp(hp[0:1, :]) * softplus
    beta = jax.nn.sigmoid(sm)
    if n_valid < C:
        valid = row < n_valid
        qkv = jnp.where(valid, qkv, 0.0)
        la = jnp.where(valid, la, 0.0)
        beta = jnp.where(valid, beta, 0.0)

    ri = lax.broadcasted_iota(I32, (C, C), 0)
    ci = lax.broadcasted_iota(I32, (C, C), 1)
    same = (ri >> ls) == (ci >> ls)
    incl = same & (ri >= ci)
    strict = same & (ri > ci)
    g = _mm_f32(incl.astype(F32), la)
    g_end = _mm_f32((ci == (ri | (L - 1))).astype(F32), g)
    gt = g.T
    eye = (ri == ci).astype(F32)

    q_l, k_l, kd_l, dec_l, lows_l, rhs_l, eg_l, z_l = [], [], [], [], [], [], [], []
    for h in range(nh):
        col = SM_A + h
        gcol = g[:, col:col + 1]
        grow = gt[col:col + 1, :]
        bcol = beta[:, SM_B + h:SM_B + h + 1]
        q = qkv[:, h * hd:(h + 1) * hd]
        k = qkv[:, nh * hd + h * hd:nh * hd + (h + 1) * hd]
        v = qkv[:, 2 * nh * hd + h * hd:2 * nh * hd + (h + 1) * hd]
        q = q * lax.rsqrt(jnp.sum(q * q, axis=-1, keepdims=True) + RMS_EPS) * (hd ** -0.5)
        k = k * lax.rsqrt(jnp.sum(k * k, axis=-1, keepdims=True) + RMS_EPS)
        dec = jnp.exp(jnp.where(incl, gcol - grow, NEG_INF))
        eg = jnp.exp(gcol)
        q_l.append(q.astype(BF16))
        k_l.append(k.astype(BF16))
        kd_l.append((k * jnp.exp(g_end[:, col:col + 1] - gcol)).astype(BF16))
        dec_l.append(dec)
        lows_l.append(bcol * jnp.where(strict, dec, 0.0))
        rhs_l.append(jnp.concatenate([bcol * v, (bcol * eg) * k], axis=1))
        eg_l.append(eg)
        z_l.append(z[:, h * hd:(h + 1) * hd])
    qb = jnp.stack(q_l)
    kb = jnp.stack(k_l)
    kd = jnp.stack(kd_l)
    dec = jnp.stack(dec_l)
    eg = jnp.stack(eg_l)

    low = jnp.stack(lows_l) * _bnt(kb, kb)
    qk = _bnt(qb, kb)
    inv = eye[None] - jnp.where(_sibling_mask(ri, ci, 0)[None], low, 0.0)
    for lvl in range(1, ls):
        off = jnp.where(_sibling_mask(ri, ci, lvl)[None], low, 0.0)
        inv_s = _split(inv)
        inv = inv - _bmm3(inv_s, _split(_bmm3(_split(off), inv_s)))
    uw = _bmm3(_split(inv), _split(jnp.stack(rhs_l)))
    w_b = uw[:, :, hd:].astype(BF16)

    ws = None
    qs = None
    s_old = []
    for b in range(n_seq):
        S = s_scr[b]
        s_old.append(S)
        Sb = S.astype(BF16)
        if n_seq == 1:
            wm, qm = w_b, qb
        else:
            mine = ((row >> ls) == b)[None]
            wm = jnp.where(mine, w_b, jnp.zeros_like(w_b))
            qm = jnp.where(mine, qb, jnp.zeros_like(qb))
        ws = _bmm(wm, Sb) if ws is None else ws + _bmm(wm, Sb)
        qs = _bmm(qm, Sb) if qs is None else qs + _bmm(qm, Sb)
    ub = (uw[:, :, :hd] - ws).astype(BF16)
    o = eg * qs + _bmm((dec * qk).astype(BF16), ub)
    for b in range(n_seq):
        last = b * L + L - 1
        glast = jnp.stack([g[last:last + 1, SM_A + h:SM_A + h + 1] for h in range(nh)])
        kdm = kd if n_seq == 1 else jnp.where(((row >> ls) == b)[None], kd, jnp.zeros_like(kd))
        s_scr[b] = jnp.exp(glast) * s_old[b] + _btn(kdm, ub)

    on = o * lax.rsqrt(jnp.mean(o * o, axis=-1, keepdims=True) + RMS_EPS) * ng_ref[...]
    zz = jnp.stack(z_l)
    res = on * (zz * jax.nn.sigmoid(zz))
    for h in range(nh):
        o_ref[:, h * hd:(h + 1) * hd] = res[h, :c_in]

    @pl.when(c == pl.num_programs(1) - 1)
    def _():
        sfin_ref[...] = s_scr[...]


def _gdn(parts, prev, s0, cw, hp, ng, *, batch, n_chunks, c_in, n_valid, n_seq, shared_init):
    n = parts.shape[0]
    C = max(c_in, GDN_CHUNK)
    assert n_seq == 1 or (n_chunks == 1 and c_in == C)
    L = C // n_seq
    init = (lambda b, c: (0, 0, 0)) if shared_init else (lambda b, c: (b, 0, 0))
    init4 = (lambda b, c: (0, 0, 0, 0)) if shared_init else (lambda b, c: (b, 0, 0, 0))
    qkv_w = 3 * GDN_HEADS * GDN_D
    v_w = GDN_HEADS * GDN_D
    return pl.pallas_call(
        functools.partial(_gdn_kernel, c_in=c_in, n_valid=n_valid, n_seq=n_seq),
        grid=(batch, n_chunks),
        in_specs=[
            pl.BlockSpec((c_in, qkv_w), lambda b, c: (b * n_chunks + c, COL_QKV // qkv_w)),
            pl.BlockSpec((c_in, v_w), lambda b, c: (b * n_chunks + c, COL_Z // v_w)),
            pl.BlockSpec((c_in, LANE), lambda b, c: (b * n_chunks + c, COL_SM // LANE)),
            pl.BlockSpec((n_seq, SUBLANE, qkv_w), init),
            pl.BlockSpec((n_seq, GDN_HEADS, GDN_D, GDN_D), init4),
            pl.BlockSpec((CONV_W, qkv_w), lambda b, c: (0, 0)),
            pl.BlockSpec((2, LANE), lambda b, c: (0, 0)),
            pl.BlockSpec((1, GDN_D), lambda b, c: (0, 0)),
        ],
        out_specs=[
            pl.BlockSpec((c_in, v_w), lambda b, c: (b * n_chunks + c, 0)),
            pl.BlockSpec((n_seq, GDN_HEADS, GDN_D, GDN_D), lambda b, c: (b, 0, 0, 0)),
        ],
        out_shape=[jax.ShapeDtypeStruct((n, v_w), F32),
                   jax.ShapeDtypeStruct((batch * n_seq, GDN_HEADS, GDN_D, GDN_D), F32)],
        scratch_shapes=[pltpu.VMEM((n_seq, GDN_HEADS, GDN_D, GDN_D), F32),
                        pltpu.VMEM((n_seq * (SUBLANE + L), qkv_w), F32)],
        compiler_params=_cparams(("arbitrary", "arbitrary")),
        name="gated_deltanet",
    )(parts, parts, parts, prev, s0, cw, hp, ng)


def _sort_key(x):
    b = pltpu.bitcast(x + 0.0, I32)
    return b ^ ((b >> 31) & 0x7FFFFFFF)


KEY_NEG_INF = -(2 ** 31) + 0x7FFFFF
SEARCH_BITS_PER_TRIP = 4


def _count_ge(key_tiles, cand):
    tot = None
    for kt in key_tiles:
        c = jnp.sum((kt >= cand).astype(F32), axis=1, keepdims=True)
        tot = c if tot is None else tot + c
    return tot


def _rows8(x):
    return x.reshape(x.shape[0] // SUBLANE, SUBLANE, x.shape[1])


def _sum_rows8(x, chains=4):
    rows = x.shape[0]
    if rows % (chains * SUBLANE):
        return jnp.sum(_rows8(x), axis=0)
    part = rows // chains
    sums = [jnp.sum(_rows8(x[k * part:(k + 1) * part]), axis=0) for k in range(chains)]
    while len(sums) > 1:
        sums = [a + b for a, b in zip(sums[0::2], sums[1::2])]
    return sums[0]


def _attn_prompt_kernel(q_ref, iq_ref, smq_ref, k_ref, v_ref, smk_ref, mk_ref, mv_ref, msm_ref,
                        o_ref, kbf, vtb, kibf, key_scr, bias_scr, s_scr, cut_scr, *, n_sel, n_meta, kt):
    tq = q_ref.shape[0]
    n_real = k_ref.shape[0]
    kvw = k_ref.shape[1]
    qb_i = pl.program_id(1)
    nt = (qb_i * tq + tq + kt - 1) // kt
    group = ATT_HEADS // ATT_KV_HEADS

    @pl.when(qb_i == 0)
    def _():
        zrows = LANE - n_meta
        kbf[0:n_real, :] = k_ref[...].astype(BF16)
        kbf[n_real:, :] = jnp.concatenate([mk_ref[...], jnp.zeros((zrows, kvw), F32)], axis=0).astype(BF16)
        vtb[:, 0:n_real] = v_ref[...].T.astype(BF16)
        vtb[:, n_real:] = jnp.concatenate([mv_ref[...], jnp.zeros((zrows, kvw), F32)], axis=0).T.astype(BF16)
        kibf[0:n_real, :] = smk_ref[...][:, SM_IK:SM_IK + IDX_DIM].astype(BF16)
        kibf[n_real:, :] = jnp.concatenate(
            [msm_ref[...][:, SM_IK:SM_IK + IDX_DIM], jnp.zeros((zrows, IDX_DIM), F32)], axis=0).astype(BF16)

    iq = iq_ref[...]
    iq_stack = jnp.concatenate([iq[:, h * IDX_DIM:(h + 1) * IDX_DIM] for h in range(IDX_HEADS)], axis=0).astype(BF16)
    w_t = smq_ref[...].T * ((IDX_DIM ** -0.5) * (IDX_HEADS ** -0.5))
    qpos = qb_i * tq + lax.broadcasted_iota(I32, (1, tq), 1)

    def index_scores(start, width):
        s_all = _nt(kibf[pl.ds(start, width), :], iq_stack)
        acc = jnp.zeros((width, tq), F32)
        for h in range(IDX_HEADS):
            acc = acc + jnp.maximum(s_all[:, h * tq:(h + 1) * tq], 0.0) * w_t[SM_IW + h:SM_IW + h + 1, :]
        return acc

    def real_tile(t, carry):
        start = pl.multiple_of(t * kt, kt)
        kpos = start + lax.broadcasted_iota(I32, (kt, 1), 0)
        key_scr[pl.ds(start, kt), :] = _sort_key(jnp.where(kpos <= qpos, index_scores(start, kt), NEG_INF))
        return carry

    lax.fori_loop(0, nt, real_tile, 0)
    mrow = lax.broadcasted_iota(I32, (LANE, 1), 0)
    key_scr[n_real:, :] = _sort_key(jnp.where(mrow < n_meta, index_scores(n_real, LANE), NEG_INF))

    def count_ge(cand):
        def body(t, acc):
            kk = key_scr[pl.ds(pl.multiple_of(t * kt, kt), kt), :]
            return acc + _sum_rows8((kk >= cand).astype(F32))

        acc = lax.fori_loop(0, nt, body, jnp.zeros((SUBLANE, tq), F32))
        acc = acc + jnp.sum(_rows8((key_scr[n_real:, :] >= cand).astype(F32)), axis=0)
        return jnp.sum(acc, axis=0, keepdims=True)

    few = qpos + 1 + n_meta <= n_sel

    def unsettled(state):
        it, _, cnt = state
        return (it < 32) & (jnp.max(jnp.where(few | (cnt == n_sel), 0.0, 1.0)) > 0.0)

    def refine(state):
        it, thr, cnt = state
        for step in range(SEARCH_BITS_PER_TRIP):
            cand = thr + lax.shift_left(jnp.int32(1), 31 - step - it)
            c = count_ge(cand)
            ok = c >= n_sel
            thr = jnp.where(ok, cand, thr)
            cnt = jnp.where(ok, c, cnt)
        return it + SEARCH_BITS_PER_TRIP, thr, cnt

    n_all = (nt * kt + LANE).astype(F32)
    _, thr, cnt = lax.while_loop(
        unsettled, refine, (jnp.int32(0), jnp.full((1, tq), INT_MIN, I32), jnp.full((1, tq), 1.0, F32) * n_all))

    tie = (cnt > n_sel) & (thr > KEY_NEG_INF) & jnp.logical_not(few)
    n_pos = n_real + n_meta
    cut_scr[...] = jnp.full((SUBLANE, tq), float(n_pos), F32)

    def pos_of(start, width, is_meta):
        r = lax.broadcasted_iota(I32, (width, 1), 0)
        return r if is_meta else n_meta + start + r

    @pl.when(jnp.max(tie.astype(F32)) > 0.0)
    def _tie_cut():
        need = n_sel - count_ge(thr + 1)
        nbits = max(1, (n_pos - 1).bit_length())

        def count_eq_le(cut):
            def body(t, acc):
                start = pl.multiple_of(t * kt, kt)
                kk = key_scr[pl.ds(start, kt), :]
                return acc + jnp.sum(_rows8(((kk == thr) & (pos_of(start, kt, False) <= cut)).astype(F32)), axis=0)

            acc = lax.fori_loop(0, nt, body, jnp.zeros((SUBLANE, tq), F32))
            mk = key_scr[n_real:, :]
            acc = acc + jnp.sum(_rows8(((mk == thr) & (pos_of(0, LANE, True) <= cut)).astype(F32)), axis=0)
            return jnp.sum(acc, axis=0, keepdims=True)

        def body(it, cut):
            cand = cut - lax.shift_left(jnp.int32(1), nbits - 1 - it)
            ok = (cand >= 0) & (count_eq_le(cand) >= need)
            return jnp.where(ok, cand, cut)

        cut = lax.fori_loop(0, nbits, body, jnp.full((1, tq), (1 << nbits) - 1, I32))
        cut = jnp.where(tie, cut, n_pos).astype(F32)
        cut_scr[...] = jnp.broadcast_to(cut, (SUBLANE, tq))

    cut = cut_scr[0:1, :].astype(I32)

    def bias_tile(start, width, is_meta):
        kk = key_scr[pl.ds(start, width), :]
        sel = (kk > thr) | ((kk == thr) & (pos_of(start, width, is_meta) <= cut))
        sel = sel & (kk > KEY_NEG_INF)
        bias_scr[pl.ds(start, width), :] = jnp.where(sel, 0.0, NEG_INF)

    def bias_body(t, carry):
        bias_tile(pl.multiple_of(t * kt, kt), kt, False)
        return carry

    lax.fori_loop(0, nt, bias_body, 0)
    bias_tile(n_real, LANE, True)

    scale = ATT_DH ** -0.5
    for g in range(ATT_KV_HEADS):
        q4 = jnp.concatenate([q_ref[:, (g * group + j) * ATT_DH:(g * group + j + 1) * ATT_DH] for j in range(group)],
                             axis=0).astype(BF16)

        def score_tile(start, width):
            s = _nt(kbf[pl.ds(start, width), g * ATT_DH:(g + 1) * ATT_DH], q4) * scale
            b = bias_scr[pl.ds(start, width), :]
            s = s + jnp.concatenate([b] * group, axis=1)
            s_scr[pl.ds(start, width), :] = s
            return jnp.max(_rows8(s), axis=0)

        def pass1(t, m):
            return jnp.maximum(m, score_tile(pl.multiple_of(t * kt, kt), kt))

        m = jnp.max(lax.fori_loop(0, nt, pass1, score_tile(n_real, LANE)), axis=0, keepdims=True)

        def prob_tile(start, width):
            p = jnp.exp(s_scr[pl.ds(start, width), :] - m)
            pv = _mm(vtb[g * ATT_DH:(g + 1) * ATT_DH, pl.ds(start, width)], p.astype(BF16))
            return _sum_rows8(p), pv

        def pass2(t, carry):
            l, acc = carry
            dl, dacc = prob_tile(pl.multiple_of(t * kt, kt), kt)
            return l + dl, acc + dacc

        l, acc = lax.fori_loop(0, nt, pass2, prob_tile(n_real, LANE))
        out_t = acc / jnp.sum(l, axis=0, keepdims=True)
        for j in range(group):
            hh = g * group + j
            o_ref[:, hh * ATT_DH:(hh + 1) * ATT_DH] = out_t[:, j * tq:(j + 1) * tq].T


def _attn_prompt(parts, mparts, *, batch, seq, n_meta, tq, kt):
    nqb = seq // tq
    n_sel = min(TOPK_MAX, (seq + n_meta) // 4)
    kvw = ATT_KV_HEADS * ATT_DH
    nk = seq + LANE
    return pl.pallas_call(
        functools.partial(_attn_prompt_kernel, n_sel=n_sel, n_meta=n_meta, kt=kt),
        grid=(batch, nqb),
        in_specs=[
            pl.BlockSpec((tq, D), lambda b, i: (b * nqb + i, COL_AQ // D)),
            pl.BlockSpec((tq, D), lambda b, i: (b * nqb + i, COL_IQ // D)),
            pl.BlockSpec((tq, LANE), lambda b, i: (b * nqb + i, COL_SM // LANE)),
            pl.BlockSpec((seq, kvw), lambda b, i: (b, COL_AK // kvw)),
            pl.BlockSpec((seq, kvw), lambda b, i: (b, COL_AV // kvw)),
            pl.BlockSpec((seq, LANE), lambda b, i: (b, COL_SM // LANE)),
            pl.BlockSpec((n_meta, kvw), lambda b, i: (0, COL_AK // kvw)),
            pl.BlockSpec((n_meta, kvw), lambda b, i: (0, COL_AV // kvw)),
            pl.BlockSpec((n_meta, LANE), lambda b, i: (0, COL_SM // LANE)),
        ],
        out_specs=pl.BlockSpec((tq, D), lambda b, i: (b * nqb + i, 0)),
        out_shape=jax.ShapeDtypeStruct((batch * seq, D), F32),
        scratch_shapes=[pltpu.VMEM((nk, kvw), BF16), pltpu.VMEM((kvw, nk), BF16), pltpu.VMEM((nk, IDX_DIM), BF16),
                        pltpu.VMEM((nk, tq), I32), pltpu.VMEM((nk, tq), F32),
                        pltpu.VMEM((nk, (ATT_HEADS // ATT_KV_HEADS) * tq), F32), pltpu.VMEM((SUBLANE, tq), F32)],
        compiler_params=_cparams(("arbitrary", "arbitrary")),
        name="sparse_attention_prompt",
    )(parts, parts, parts, parts, parts, parts, mparts, mparts, mparts)


def _attn_sample_kernel(pt_ref, *refs, n_pages, n_sel):
    kp = refs[0:n_pages]
    vp = refs[n_pages:2 * n_pages]
    ip = refs[2 * n_pages:3 * n_pages]
    q_ref, iq_ref, kn_ref, vn_ref, sm_ref, o_ref, kil, vil, kibf, cut_scr = refs[3 * n_pages:]
    del pt_ref
    t_new = q_ref.shape[0]
    page = ip[0].shape[0]
    past = n_pages * page
    nkv = ATT_KV_HEADS
    for j in range(n_pages):
        kil[j * nkv * page:(j + 1) * nkv * page, :] = kp[j][...].astype(BF16)
        vil[j * nkv * page:(j + 1) * nkv * page, :] = vp[j][...].astype(BF16)
        kibf[j * page:(j + 1) * page, :] = ip[j][...].astype(BF16)
    sm = sm_ref[...]
    zpad = jnp.zeros((LANE - t_new, nkv * ATT_DH), BF16)
    knew = jnp.concatenate([kn_ref[...].astype(BF16), zpad], axis=0)
    vnew = jnp.concatenate([vn_ref[...].astype(BF16), zpad], axis=0)
    kibf[past:, :] = jnp.concatenate(
        [sm[:, SM_IK:SM_IK + IDX_DIM].astype(BF16), jnp.zeros((LANE - t_new, IDX_DIM), BF16)], axis=0)

    iq = iq_ref[...]
    qi = jnp.concatenate([iq[:, h * IDX_DIM:(h + 1) * IDX_DIM] for h in range(IDX_HEADS)], axis=0).astype(BF16)
    wcol = jnp.concatenate([sm[:, SM_IW + h:SM_IW + h + 1] for h in range(IDX_HEADS)], axis=0)
    wcol = wcol * ((IDX_DIM ** -0.5) * (IDX_HEADS ** -0.5))

    def idx_scores(start, width):
        s = jnp.maximum(_nt(qi, kibf[start:start + width, :]), 0.0) * wcol
        acc = s[0:t_new]
        for h in range(1, IDX_HEADS):
            acc = acc + s[h * t_new:(h + 1) * t_new]
        return acc

    i_past = idx_scores(0, past)
    i_new = idx_scores(past, LANE)
    trow = lax.broadcasted_iota(I32, (t_new, LANE), 0)
    tlane = lax.broadcasted_iota(I32, (t_new, LANE), 1)
    key_past = _sort_key(i_past)
    key_new = _sort_key(jnp.where(tlane <= trow, i_new, NEG_INF))
    tiles = [key_past, key_new]

    def digit_body(it, thr):
        step = lax.shift_left(jnp.int32(1), 28 - 4 * it)
        digit = jnp.zeros((t_new, 1), I32)
        for j in range(1, 16):
            digit = digit + (_count_ge(tiles, thr + j * step) >= n_sel).astype(I32)
        return thr + digit * step

    thr = lax.fori_loop(0, 8, digit_body, jnp.full((t_new, 1), INT_MIN, I32))
    n_ge = _count_ge(tiles, thr)
    tie = (n_ge > n_sel) & (thr > KEY_NEG_INF)
    pos_past = lax.broadcasted_iota(I32, (1, past), 1)
    pos_new = past + lax.broadcasted_iota(I32, (1, LANE), 1)
    n_pos = past + t_new
    cut_scr[...] = jnp.full(cut_scr.shape, n_pos, I32)

    @pl.when(jnp.max(tie.astype(F32)) > 0.0)
    def _tie_cut():
        need = n_sel - _count_ge(tiles, thr + 1)
        nbits = max(1, (n_pos - 1).bit_length())

        def cut_body(it, cut):
            cand = cut - lax.shift_left(jnp.int32(1), nbits - 1 - it)
            cnt = (jnp.sum(((key_past == thr) & (pos_past <= cand)).astype(F32), axis=1, keepdims=True)
                   + jnp.sum(((key_new == thr) & (pos_new <= cand)).astype(F32), axis=1, keepdims=True))
            ok = (cand >= 0) & (cnt >= need)
            return jnp.where(ok, cand, cut)

        cut = lax.fori_loop(0, nbits, cut_body, jnp.full((t_new, 1), (1 << nbits) - 1, I32))
        cut_scr[...] = jnp.broadcast_to(jnp.where(tie, cut, n_pos), cut_scr.shape)

    cut = cut_scr[:, 0:1]

    def chosen(kk, pos):
        return ((kk > thr) | ((kk == thr) & (pos <= cut))) & (kk > KEY_NEG_INF)

    group = ATT_HEADS // ATT_KV_HEADS
    b_new = jnp.concatenate([jnp.where(chosen(key_new, pos_new), 0.0, NEG_INF)] * group, axis=0)
    sel01 = chosen(key_past, pos_past).astype(BF16)
    er = lax.broadcasted_iota(I32, (page, nkv * page), 0)
    ec = lax.broadcasted_iota(I32, (page, nkv * page), 1)
    expand = ((ec >= er * nkv) & (ec < (er + 1) * nkv)).astype(BF16)
    sel_il = jnp.concatenate([_mm(sel01[:, j * page:(j + 1) * page], expand) for j in range(n_pages)], axis=1)
    il_lane = lax.broadcasted_iota(I32, (1, nkv * past), 1)
    assert nkv & (nkv - 1) == 0
    il_head = il_lane & (nkv - 1)
    scale = ATT_DH ** -0.5
    q = q_ref[...]
    for g in range(ATT_KV_HEADS):
        qg = jnp.concatenate(
            [q[:, (g * group + j) * ATT_DH:(g * group + j + 1) * ATT_DH] for j in range(group)], axis=0).astype(BF16)
        b_g = jnp.where((sel_il > 0.5) & (il_head == g), 0.0, NEG_INF)
        s_p = _nt(qg, kil[...]) * scale + jnp.concatenate([b_g] * group, axis=0)
        s_n = _nt(qg, knew[:, g * ATT_DH:(g + 1) * ATT_DH]) * scale + b_new
        m = jnp.maximum(jnp.max(s_p, axis=1, keepdims=True), jnp.max(s_n, axis=1, keepdims=True))
        p_p = jnp.exp(s_p - m)
        p_n = jnp.exp(s_n - m)
        l = jnp.sum(p_p, axis=1, keepdims=True) + jnp.sum(p_n, axis=1, keepdims=True)
        acc = _mm(p_p.astype(BF16), vil[...]) + _mm(p_n.astype(BF16), vnew[:, g * ATT_DH:(g + 1) * ATT_DH])
        res = acc / l
        for j in range(group):
            hh = g * group + j
            o_ref[:, hh * ATT_DH:(hh + 1) * ATT_DH] = res[j * t_new:(j + 1) * t_new]


def _attn_sample(parts, cache_k, cache_v, cache_ik, page_table, *, t_new):
    db, n_pages = page_table.shape
    page = cache_k.shape[2]
    kvw = ATT_KV_HEADS * ATT_DH
    past = n_pages * page
    n_sel = min(TOPK_MAX, (past + t_new) // 4)

    n_pool = cache_k.shape[1]
    cache_k = cache_k.reshape(n_pool, page * ATT_KV_HEADS, ATT_DH)
    cache_v = cache_v.reshape(n_pool, page * ATT_KV_HEADS, ATT_DH)
    cache_ik = cache_ik.reshape(n_pool, page, IDX_DIM)

    def kv_page(j):
        return pl.BlockSpec((None, page * ATT_KV_HEADS, ATT_DH), lambda b, pt, j=j: (pt[b, j], 0, 0))

    def ik_page(j):
        return pl.BlockSpec((None, page, IDX_DIM), lambda b, pt, j=j: (pt[b, j], 0, 0))

    in_specs = ([kv_page(j) for j in range(n_pages)] + [kv_page(j) for j in range(n_pages)]
                + [ik_page(j) for j in range(n_pages)]
                + [pl.BlockSpec((t_new, D), lambda b, pt: (b, COL_AQ // D)),
                   pl.BlockSpec((t_new, D), lambda b, pt: (b, COL_IQ // D)),
                   pl.BlockSpec((t_new, kvw), lambda b, pt: (b, COL_AK // kvw)),
                   pl.BlockSpec((t_new, kvw), lambda b, pt: (b, COL_AV // kvw)),
                   pl.BlockSpec((t_new, LANE), lambda b, pt: (b, COL_SM // LANE))])
    grid_spec = pltpu.PrefetchScalarGridSpec(
        num_scalar_prefetch=1, grid=(db,), in_specs=in_specs,
        out_specs=pl.BlockSpec((t_new, D), lambda b, pt: (b, 0)),
        scratch_shapes=[pltpu.VMEM((past * ATT_KV_HEADS, ATT_DH), BF16), pltpu.VMEM((past * ATT_KV_HEADS, ATT_DH), BF16),
                        pltpu.VMEM((past + LANE, IDX_DIM), BF16), pltpu.VMEM((t_new, LANE), I32)])
    return pl.pallas_call(
        functools.partial(_attn_sample_kernel, n_pages=n_pages, n_sel=n_sel),
        grid_spec=grid_spec,
        out_shape=jax.ShapeDtypeStruct((db * t_new, D), F32),
        compiler_params=_cparams(("arbitrary",)),
        name="sparse_attention_sample",
    )(page_table, *([cache_k] * n_pages), *([cache_v] * n_pages), *([cache_ik] * n_pages),
      parts, parts, parts, parts, parts)


def _layer_norm(x, g, b):
    mu = jnp.mean(x, axis=-1, keepdims=True)
    xc = x - mu
    var = jnp.mean(xc * xc, axis=-1, keepdims=True)
    return xc * lax.rsqrt(var + LN_EPS) * g + b


def _finish_kernel(x_ref, og_ref, oa_ref, ga_ref, gb_ref, wbg_ref, wba_ref, wo_ref, g_ref, b_ref, h_ref, *, alpha):
    a = _mm(og_ref[...].astype(BF16), wbg_ref[...])
    b = _mm(oa_ref[...].astype(BF16), wba_ref[...])
    merged = jax.nn.sigmoid(ga_ref[...]) * a + jax.nn.sigmoid(gb_ref[...]) * b
    y = alpha * x_ref[...] + _mm(merged.astype(BF16), wo_ref[...])
    h_ref[...] = _layer_norm(y, g_ref[...], b_ref[...])


def _finish(x, o_gdn, o_att, parts, wbg, wba, wo, g, b, *, alpha, tm):
    n = x.shape[0]
    tm = min(tm, n)
    row = lambda i: (i, 0)
    full = lambda i: (0, 0)
    return pl.pallas_call(
        functools.partial(_finish_kernel, alpha=alpha),
        grid=(n // tm,),
        in_specs=[pl.BlockSpec((tm, D), row), pl.BlockSpec((tm, D), row), pl.BlockSpec((tm, D), row),
                  pl.BlockSpec((tm, D), lambda i: (i, COL_GA // D)), pl.BlockSpec((tm, D), lambda i: (i, COL_GB // D)),
                  pl.BlockSpec((D, D), full), pl.BlockSpec((D, D), full), pl.BlockSpec((D, D), full),
                  pl.BlockSpec((1, D), full), pl.BlockSpec((1, D), full)],
        out_specs=pl.BlockSpec((tm, D), row),
        out_shape=jax.ShapeDtypeStruct((n, D), F32),
        compiler_params=_cparams(("arbitrary",)),
        name="merge_layernorm",
    )(x, o_gdn, o_att, parts, parts, wbg, wba, wo, g, b)


_CAND = [(r0, r1) for r0 in range(PEER_TOPK) for r1 in range(PEER_TOPK) if (r0 + 1) * (r1 + 1) <= PEER_TOPK]
_CAND_OFF = [next(i for i, c in enumerate(_CAND) if c[0] == r0) for r0 in range(PEER_TOPK)]
_CAND_LEN = [sum(1 for c in _CAND if c[0] == r0) for r0 in range(PEER_TOPK)]
_CAND_ROWS = -(-len(_CAND) // SUBLANE) * SUBLANE


def _top_rows(s, n_top, break_ties):
    rows, cols = s.shape
    iota = lax.broadcasted_iota(I32, (rows, cols), 0).astype(F32)
    rank = jnp.full((rows, cols), float(n_top), F32)
    vals = []
    for r in range(n_top):
        m = jnp.max(s, axis=0, keepdims=True)
        hit = s == m
        if break_ties:
            hit = iota == jnp.min(jnp.where(hit, iota, float(rows)), axis=0, keepdims=True)
        vals.append(m)
        s = jnp.where(hit, NEG_INF, s)
        rank = jnp.where(hit, float(r), rank)
    n_ranked = jnp.sum((rank < float(n_top)).astype(F32), axis=0, keepdims=True)
    return jnp.concatenate(vals, axis=0), rank, n_ranked == float(n_top)


def _peer_kernel(h_ref, wq_ref, sk_ref, u0_ref, ub_ref, un_ref, vtp_ref, vta_ref, vtl_ref, g_ref, b_ref, y_ref,
                 hb_scr, q_scr, rank1_scr, bt_scr, nt_scr, at_scr, s1_scr,
                 sa_scr, sb_scr, pa_scr, pb_scr, acc_scr, *, alpha, n_steps):
    tt = h_ref.shape[0]
    eb = ub_ref.shape[0]
    e = pl.program_id(1)
    nk = PEER_NKEYS
    kk = PEER_TOPK
    tchunks = tt // LANE

    @pl.when(e == 0)
    def _prep():
        hb = h_ref[...].astype(BF16)
        hb_scr[...] = hb
        q_scr[...] = _mm(hb, wq_ref[...]).astype(BF16)
        acc_scr[...] = jnp.zeros(acc_scr.shape, F32)

        def scores(hd, carry):
            q0 = q_scr[:, pl.ds(pl.multiple_of(hd * 2 * nk, nk), nk)]
            q1 = q_scr[:, pl.ds(pl.multiple_of(hd * 2 * nk + nk, nk), nk)]
            at_scr[hd] = _nt(sk_ref[hd, 0], q0)
            s1_scr[hd] = _nt(sk_ref[hd, 1], q1)
            return carry

        lax.fori_loop(0, PEER_HEADS, scores, 0)

        def select_chunk(hd, c0, s0, s1, break_ties):
            a, rank0, ok0 = _top_rows(s0, kk, break_ties)
            b, rank1, ok1 = _top_rows(s1, kk, break_ties)
            pad = [jnp.full((_CAND_ROWS - len(_CAND), LANE), NEG_INF, F32)]
            cand = jnp.concatenate([a[r0:r0 + 1] + b[r1:r1 + 1] for r0, r1 in _CAND] + pad, axis=0)
            top, crank, okc = _top_rows(cand, kk, break_ties)
            zsum = jnp.sum(jnp.exp(top - top[0:1]), axis=0, keepdims=True)
            chosen = (crank < float(kk)).astype(F32)
            crow = lax.broadcasted_iota(I32, (_CAND_ROWS, 1), 0)
            nsel = jnp.zeros((nk, LANE), F32)
            for r0 in range(kk):
                in_group = (crow >= _CAND_OFF[r0]) & (crow < _CAND_OFF[r0] + _CAND_LEN[r0])
                cnt = jnp.sum(jnp.where(in_group, chosen, 0.0), axis=0, keepdims=True)
                nsel = nsel + jnp.where(rank0 == float(r0), cnt, 0.0)
            rank1_scr[hd, :, pl.ds(c0, LANE)] = rank1.astype(BF16)
            nt_scr[hd, :, pl.ds(c0, LANE)] = nsel
            at_scr[hd, :, pl.ds(c0, LANE)] = jnp.exp(s0 - a[0:1])
            bt_scr[hd, :, pl.ds(c0, LANE)] = (jnp.exp(s1 - b[0:1]) / zsum).astype(BF16)
            return jnp.min((ok0 & ok1 & okc).astype(F32)) > 0.5

        pair = 2 if tchunks % 2 == 0 else 1

        def select(idx, carry):
            hd = idx // (tchunks // pair)
            first = (idx % (tchunks // pair)) * pair
            tie_free = None
            chunks = []
            for k in range(pair):
                c0 = pl.multiple_of((first + k) * LANE, LANE)
                s0 = at_scr[hd, :, pl.ds(c0, LANE)]
                s1 = s1_scr[hd, :, pl.ds(c0, LANE)]
                chunks.append((c0, s0, s1))
            for c0, s0, s1 in chunks:
                ok = select_chunk(hd, c0, s0, s1, False)
                tie_free = ok if tie_free is None else tie_free & ok

            @pl.when(jnp.logical_not(tie_free))
            def _():
                for c0, s0, s1 in chunks:
                    select_chunk(hd, c0, s0, s1, True)

            return carry

        lax.fori_loop(0, PEER_HEADS * tchunks // pair, select, 0)

        sa_scr[1] = _nt(u0_ref[...], hb_scr[...])
        pb_scr[1] = jnp.zeros(pb_scr.shape[1:], BF16)

    groups = eb // nk

    assert 2 * groups == SUBLANE

    def weights(half, s_ref, p_ref):
        i0 = pl.multiple_of(e * SUBLANE, SUBLANE)
        zero = jnp.zeros((nk, LANE), BF16)

        for tc in range(tchunks):
            cols = slice(tc * LANE, (tc + 1) * LANE)
            gates = [zero] * groups
            for hd in range(PEER_HEADS):
                nblk = nt_scr[hd, pl.ds(i0, SUBLANE), cols]
                ablk = at_scr[hd, pl.ds(i0, SUBLANE), cols]
                rank1 = rank1_scr[hd, :, cols]
                bval = bt_scr[hd, :, cols]
                for ii in range(groups):
                    r = half * groups + ii
                    nrow = nblk[r:r + 1].astype(BF16)
                    arow = ablk[r:r + 1].astype(BF16)
                    gates[ii] = gates[ii] + jnp.where(rank1 < nrow, bval, zero) * arow
            for ii in range(groups):
                rows = slice(ii * nk, (ii + 1) * nk)
                s = s_ref[rows, cols]
                act = 0.5 * s * (1.0 + lax.erf(s * (2.0 ** -0.5)))
                p_ref[rows, cols] = gates[ii] * act.astype(BF16)

    def step(cur, nxt):
        sb_scr[...] = _nt(ub_ref[...], hb_scr[...])
        weights(0, sa_scr.at[cur], pa_scr)
        acc_scr[...] += _mm(vtp_ref[...], pb_scr[cur])
        weights(1, sb_scr, pb_scr.at[nxt])
        sa_scr[nxt] = _nt(un_ref[...], hb_scr[...])
        acc_scr[...] += _mm(vta_ref[...], pa_scr[...])

    @pl.when(e % 2 == 0)
    def _():
        step(1, 0)

    @pl.when(e % 2 == 1)
    def _():
        step(0, 1)

    last_slot = (n_steps - 1) % 2

    @pl.when(e == n_steps - 1)
    def _():
        y = alpha * h_ref[...] + (acc_scr[...] + _mm(vtl_ref[...], pb_scr[last_slot])).T
        y_ref[...] = _layer_norm(y, g_ref[...], b_ref[...])


def _peer(h, wq, sk, u, vt, g, b, *, alpha, tt, eb):
    n = h.shape[0]
    n_exp = u.shape[0]
    nk = PEER_NKEYS
    nb = n_exp // eb
    assert nb % 2 == 0
    return pl.pallas_call(
        functools.partial(_peer_kernel, alpha=alpha, n_steps=nb // 2),
        grid=(n // tt, nb // 2),
        in_specs=[pl.BlockSpec((tt, D), lambda t, e: (t, 0)),
                  pl.BlockSpec(wq.shape, lambda t, e: (0, 0)),
                  pl.BlockSpec(sk.shape, lambda t, e: (0, 0, 0, 0)),
                  pl.BlockSpec((eb, D), lambda t, e: (0, 0)),
                  pl.BlockSpec((eb, D), lambda t, e: (2 * e + 1, 0)),
                  pl.BlockSpec((eb, D), lambda t, e: (jnp.minimum(2 * e + 2, nb - 1), 0)),
                  pl.BlockSpec((D, eb), lambda t, e: (0, jnp.maximum(2 * e - 1, 0))),
                  pl.BlockSpec((D, eb), lambda t, e: (0, 2 * e)),
                  pl.BlockSpec((D, eb), lambda t, e: (0, nb - 1)),
                  pl.BlockSpec((1, D), lambda t, e: (0, 0)),
                  pl.BlockSpec((1, D), lambda t, e: (0, 0))],
        out_specs=pl.BlockSpec((tt, D), lambda t, e: (t, 0)),
        out_shape=jax.ShapeDtypeStruct((n, D), F32),
        scratch_shapes=[pltpu.VMEM((tt, D), BF16), pltpu.VMEM((tt, wq.shape[1]), BF16),
                        pltpu.VMEM((PEER_HEADS, nk, tt), BF16), pltpu.VMEM((PEER_HEADS, nk, tt), BF16),
                        pltpu.VMEM((PEER_HEADS, nk, tt), F32), pltpu.VMEM((PEER_HEADS, nk, tt), F32),
                        pltpu.VMEM((PEER_HEADS, nk, tt), F32),
                        pltpu.VMEM((2, eb, tt), F32), pltpu.VMEM((eb, tt), F32),
                        pltpu.VMEM((eb, tt), BF16), pltpu.VMEM((2, eb, tt), BF16), pltpu.VMEM((D, tt), F32)],
        compiler_params=_cparams(("arbitrary", "arbitrary")),
        name="peer_ffn",
    )(h, wq, sk, u, u, u, vt, vt, vt, g, b)


def _permute_w_in(w):
    gq = 3 * GDN_HEADS * GDN_D
    gv = GDN_HEADS * GDN_D
    aq = ATT_HEADS * ATT_DH
    akv = ATT_KV_HEADS * ATT_DH
    iqw = IDX_HEADS * IDX_DIM
    sizes = (gq, gv, GDN_HEADS, GDN_HEADS, aq, akv, akv, iqw, IDX_DIM, IDX_HEADS, D, D)
    offs = [0]
    for s in sizes:
        offs.append(offs[-1] + s)
    seg = [w[:, offs[i]:offs[i + 1]] for i in range(len(sizes))]
    qkv, z, b, a, q, k, v, iq, ik, iw, ga, gb = seg
    pad = jnp.zeros((w.shape[0], LANE - IDX_DIM - 2 * GDN_HEADS - IDX_HEADS), w.dtype)
    return jnp.concatenate([qkv, z, q, iq, ga, gb, k, v, ik, b, a, iw, pad], axis=1).astype(BF16)


def kernel(x_prompt, x_sample, cache_k, cache_v, cache_idx_k, state_conv, state_delta, page_table, meta_tokens, w_in, conv_w, a_log, dt_bias, gdn_norm_g, w_branch_gdn, w_branch_attn, w_out, ln1_g, ln1_b, peer_wq, peer_subkeys, peer_u, peer_v, ln2_g, ln2_b):
    depth = w_in.shape[0]
    assert depth == 1, "single-layer step"
    batch, seq, d = x_prompt.shape
    db, t_new, _ = x_sample.shape
    n_meta = meta_tokens.shape[0]
    assert d == D and seq % GDN_CHUNK == 0 and n_meta % SUBLANE == 0 and n_meta <= GDN_CHUNK
    alpha = (2 * depth) ** 0.25
    qkv_w = 3 * GDN_HEADS * GDN_D
    kvw = ATT_KV_HEADS * ATT_DH

    w_r = _permute_w_in(w_in[0])
    xp = x_prompt.reshape(batch * seq, D)
    xs = x_sample.reshape(db * t_new, D)
    tm = 512 if (batch * seq) % 512 == 0 else 256
    parts_p = _project(xp, w_r, tm)
    parts_s = _project(xs, w_r, min(tm, db * t_new))
    parts_m = _project(meta_tokens.astype(F32), w_r, n_meta)

    hp = jnp.zeros((2, LANE), F32).at[0, SM_A:SM_A + GDN_HEADS].set(a_log[0]).at[1, SM_A:SM_A + GDN_HEADS].set(dt_bias[0])
    cw = conv_w[0]
    ng = gdn_norm_g[0].reshape(1, GDN_D)
    zero_prev = jnp.zeros((1, SUBLANE, qkv_w), F32)
    zero_state = jnp.zeros((1, GDN_HEADS, GDN_D, GDN_D), F32)
    _, s_meta = _gdn(parts_m, zero_prev, zero_state, cw, hp, ng, batch=1, n_chunks=1, c_in=n_meta, n_valid=n_meta,
                     n_seq=1, shared_init=True)
    meta_prev = parts_m[n_meta - SUBLANE:, :qkv_w].reshape(1, SUBLANE, qkv_w)
    og_p, p_delta = _gdn(parts_p, meta_prev, s_meta, cw, hp, ng, batch=batch, n_chunks=seq // GDN_CHUNK,
                         c_in=GDN_CHUNK, n_valid=GDN_CHUNK, n_seq=1, shared_init=True)
    per = GDN_CHUNK // t_new
    assert per * t_new == GDN_CHUNK and t_new % SUBLANE == 0 and db % per == 0
    samp_prev = jnp.pad(state_conv.reshape(db, CONV_W - 1, qkv_w), ((0, 0), (SUBLANE - (CONV_W - 1), 0), (0, 0)))
    og_s, s_delta = _gdn(parts_s, samp_prev, state_delta.reshape(db, GDN_HEADS, GDN_D, GDN_D).astype(F32), cw, hp, ng,
                         batch=db // per, n_chunks=1, c_in=GDN_CHUNK, n_valid=GDN_CHUNK, n_seq=per, shared_init=False)

    tq = 128
    oa_p = _attn_prompt(parts_p, parts_m, batch=batch, seq=seq, n_meta=n_meta, tq=tq, kt=min(512, seq))
    oa_s = _attn_sample(parts_s, cache_k, cache_v, cache_idx_k, page_table, t_new=t_new)

    wbg = w_branch_gdn[0].astype(BF16)
    wba = w_branch_attn[0].astype(BF16)
    wo = w_out[0].astype(BF16)
    g1 = ln1_g[0].reshape(1, D)
    b1 = ln1_b[0].reshape(1, D)
    h_p = _finish(xp, og_p, oa_p, parts_p, wbg, wba, wo, g1, b1, alpha=alpha, tm=256)
    h_s = _finish(xs, og_s, oa_s, parts_s, wbg, wba, wo, g1, b1, alpha=alpha, tm=256)

    wq = peer_wq[0].astype(BF16)
    sk = peer_subkeys[0].astype(BF16)
    u = peer_u[0].astype(BF16)
    vt = peer_v[0].astype(BF16).T
    g2 = ln2_g[0].reshape(1, D)
    b2 = ln2_b[0].reshape(1, D)
    tt = 512 if (batch * seq) % 512 == 0 else 256
    y_p = _peer(h_p, wq, sk, u, vt, g2, b2, alpha=alpha, tt=tt, eb=512)
    y_s = _peer(h_s, wq, sk, u, vt, g2, b2, alpha=alpha, tt=min(tt, db * t_new), eb=512)

    y_prompt = y_p.reshape(batch, seq, D)
    y_sample = y_s.reshape(db, t_new, D)
    pp = parts_p.reshape(batch, seq, N_COLS)
    ps = parts_s.reshape(db, t_new, N_COLS)
    p_conv = pp[:, seq - (CONV_W - 1):, :qkv_w][None]
    s_conv = jnp.concatenate([state_conv[0].astype(F32), ps[:, :, :qkv_w]], axis=1)[:, -(CONV_W - 1):][None]

    def with_meta(col, width):
        m = jnp.broadcast_to(parts_m[None, :, col:col + width], (batch, n_meta, width))
        return jnp.concatenate([m, pp[:, :, col:col + width]], axis=1)

    p_k = with_meta(COL_AK, kvw).reshape(1, batch, seq + n_meta, ATT_KV_HEADS, ATT_DH)
    p_v = with_meta(COL_AV, kvw).reshape(1, batch, seq + n_meta, ATT_KV_HEADS, ATT_DH)
    p_idx_k = with_meta(COL_SM + SM_IK, IDX_DIM)[None]
    s_k = ps[:, :, COL_AK:COL_AK + kvw].reshape(1, db, t_new, ATT_KV_HEADS, ATT_DH)
    s_v = ps[:, :, COL_AV:COL_AV + kvw].reshape(1, db, t_new, ATT_KV_HEADS, ATT_DH)
    s_idx_k = ps[:, :, COL_SM + SM_IK:COL_SM + SM_IK + IDX_DIM][None]
    return (y_prompt, y_sample, p_conv, p_delta[None], p_k, p_v, p_idx_k, s_conv, s_delta[None].astype(state_delta.dtype),
            s_k, s_v, s_idx_k)
```

```python
import functools

import jax
import jax.numpy as jnp
from jax import lax
from jax.experimental import pallas as pl
from jax.experimental.pallas import tpu as pltpu

F32 = jnp.float32
BF16 = jnp.bfloat16
I32 = jnp.int32

GDN_HEADS = 8
GDN_D = 128
CONV_W = 4
GDN_CHUNK = 64
ATT_HEADS = 8
ATT_KV_HEADS = 2
ATT_DH = 128
IDX_HEADS = 16
IDX_DIM = 64
TOPK_MAX = 256
PEER_HEADS = 8
PEER_NKEYS = 128
PEER_TOPK = 16
LN_EPS = 1e-5
RMS_EPS = 1e-6

LANE = 128
SUBLANE = 8
VMEM_LIMIT = 56 * 1024 * 1024

D = 1024
COL_QKV = 0
COL_Z = 3072
COL_AQ = 4096
COL_IQ = 5120
COL_GA = 6144
COL_GB = 7168
COL_AK = 8192
COL_AV = 8448
COL_SM = 8704
N_COLS = 8832
SM_IK = 0
SM_B = 64
SM_A = 72
SM_IW = 80

NEG_INF = float("-inf")
INT_MIN = -(2 ** 31)


def _nt(a, b):
    return lax.dot_general(a, b, (((1,), (1,)), ((), ())), preferred_element_type=F32)


def _tn(a, b):
    return lax.dot_general(a, b, (((0,), (0,)), ((), ())), preferred_element_type=F32)


def _mm(a, b):
    return jnp.dot(a, b, preferred_element_type=F32)


def _mm_f32(a, b):
    return jnp.dot(a, b, preferred_element_type=F32, precision=lax.Precision.HIGHEST)


def _cparams(sem):
    return pltpu.CompilerParams(dimension_semantics=sem, vmem_limit_bytes=VMEM_LIMIT)


def _proj_kernel(x_ref, w_ref, o_ref):
    o_ref[...] = _mm(x_ref[...].astype(BF16), w_ref[...])


def _project(x, w_bf, tm):
    n = x.shape[0]
    tn = N_COLS // 3
    return pl.pallas_call(
        _proj_kernel,
        grid=(3, n // tm),
        in_specs=[pl.BlockSpec((tm, D), lambda j, i: (i, 0)),
                  pl.BlockSpec((D, tn), lambda j, i: (0, j))],
        out_specs=pl.BlockSpec((tm, tn), lambda j, i: (i, j)),
        out_shape=jax.ShapeDtypeStruct((n, N_COLS), F32),
        compiler_params=_cparams(("arbitrary", "arbitrary")),
        name="in_projection",
    )(x, w_bf)


def _sibling_mask(ri, ci, lvl):
    return ((ri >> (lvl + 1)) == (ci >> (lvl + 1))) & (((ri >> lvl) & 1) == 1) & (((ci >> lvl) & 1) == 0)


_BATCH0 = ((0,), (0,))


def _bmm(a, b):
    return lax.dot_general(a, b, (((2,), (1,)), _BATCH0), preferred_element_type=F32)


def _bnt(a, b):
    return lax.dot_general(a, b, (((2,), (2,)), _BATCH0), preferred_element_type=F32)


def _btn(a, b):
    return lax.dot_general(a, b, (((1,), (1,)), _BATCH0), preferred_element_type=F32)


def _gdn_kernel(qkv_ref, z_ref, sm_ref, prev_ref, s0_ref, cw_ref, hp_ref, ng_ref,
                o_ref, sfin_ref, s_scr, ext_scr, *, c_in, n_valid, n_seq):
    nh = GDN_HEADS
    hd = GDN_D
    stride = ext_scr.shape[0] // n_seq
    L = stride - SUBLANE
    C = L * n_seq
    ls = L.bit_length() - 1
    c = pl.program_id(1)

    @pl.when(c == 0)
    def _():
        s_scr[...] = s0_ref[...]
        for b in range(n_seq):
            ext_scr[b * stride:b * stride + SUBLANE, :] = prev_ref[b]

    u = qkv_ref[...]
    sm = sm_ref[...]
    z = z_ref[...]
    if c_in < C:
        u = jnp.concatenate([u, jnp.zeros((C - c_in, u.shape[1]), F32)], axis=0)
        sm = jnp.concatenate([sm, jnp.zeros((C - c_in, LANE), F32)], axis=0)
        z = jnp.concatenate([z, jnp.zeros((C - c_in, z.shape[1]), F32)], axis=0)
    cw = cw_ref[...]
    pieces = []
    for b in range(n_seq):
        base = b * stride + SUBLANE
        ub_ = u[b * L:(b + 1) * L]
        ext_scr[base:base + L, :] = ub_
        acc = ub_ * cw[CONV_W - 1:CONV_W, :]
        for i in range(CONV_W - 1):
            off = base - (CONV_W - 1) + i
            acc = acc + ext_scr[off:off + L, :] * cw[i:i + 1, :]
        pieces.append(acc)
    conv = pieces[0] if n_seq == 1 else jnp.concatenate(pieces, axis=0)
    if n_seq == 1:
        tail = ext_scr[c_in:c_in + SUBLANE, :]
        ext_scr[0:SUBLANE, :] = tail
    qkv = conv * jax.nn.sigmoid(conv)

    row = lax.broadcasted_iota(I32, (C, 1), 0)
    hp = hp_ref[...]
    xs = sm + hp[1:2, :]
    softplus = jnp.maximum(xs, 0.0) + jnp.log1p(jnp.exp(-jnp.abs(xs)))
    la = -jnp.exp(hp[0:1, :]) * softplus
    beta = jax.nn.sigmoid(sm)
    if n_valid < C:
        valid = row < n_valid
        qkv = jnp.where(valid, qkv, 0.0)
        la = jnp.where(valid, la, 0.0)
        beta = jnp.where(valid, beta, 0.0)

    ri = lax.broadcasted_iota(I32, (C, C), 0)
    ci = lax.broadcasted_iota(I32, (C, C), 1)
    same = (ri >> ls) == (ci >> ls)
    incl = same & (ri >= ci)
    strict = same & (ri > ci)
    g = _mm_f32(incl.astype(F32), la)
    g_end = _mm_f32((ci == (ri | (L - 1))).astype(F32), g)
    gt = g.T
    eye = (ri == ci).astype(F32)

    q_l, k_l, kd_l, dec_l, lows_l, rhs_l, eg_l, z_l = [], [], [], [], [], [], [], []
    for h in range(nh):
        col = SM_A + h
        gcol = g[:, col:col + 1]
        grow = gt[col:col + 1, :]
        bcol = beta[:, SM_B + h:SM_B + h + 1]
        q = qkv[:, h * hd:(h + 1) * hd]
        k = qkv[:, nh * hd + h * hd:nh * hd + (h + 1) * hd]
        v = qkv[:, 2 * nh * hd + h * hd:2 * nh * hd + (h + 1) * hd]
        q = q * lax.rsqrt(jnp.sum(q * q, axis=-1, keepdims=True) + RMS_EPS) * (hd ** -0.5)
        k = k * lax.rsqrt(jnp.sum(k * k, axis=-1, keepdims=True) + RMS_EPS)
        dec = jnp.exp(jnp.where(incl, gcol - grow, NEG_INF))
        eg = jnp.exp(gcol)
        q_l.append(q.astype(BF16))
        k_l.append(k.astype(BF16))
        kd_l.append((k * jnp.exp(g_end[:, col:col + 1] - gcol)).astype(BF16))
        dec_l.append(dec)
        lows_l.append(bcol * jnp.where(strict, dec, 0.0))
        rhs_l.append(jnp.concatenate([bcol * v, (bcol * eg) * k], axis=1))
        eg_l.append(eg)
        z_l.append(z[:, h * hd:(h + 1) * hd])
    qb = jnp.stack(q_l)
    kb = jnp.stack(k_l)
    kd = jnp.stack(kd_l)
    dec = jnp.stack(dec_l)
    eg = jnp.stack(eg_l)

    low = jnp.stack(lows_l) * _bnt(kb, kb)
    qk = _bnt(qb, kb)
    inv = eye[None] - jnp.where(_sibling_mask(ri, ci, 0)[None], low, 0.0)
    for lvl in range(1, ls):
        off = jnp.where(_sibling_mask(ri, ci, lvl)[None], low, 0.0)
        inv_b = inv.astype(BF16)
        inv = inv - _bmm(inv_b, _bmm(off.astype(BF16), inv_b).astype(BF16))
    uw = _bmm(inv.astype(BF16), jnp.stack(rhs_l).astype(BF16))
    w_b = uw[:, :, hd:].astype(BF16)

    ws = None
    qs = None
    s_old = []
    for b in range(n_seq):
        S = s_scr[b]
        s_old.append(S)
        Sb = S.astype(BF16)
        if n_seq == 1:
            wm, qm = w_b, qb
        else:
            mine = ((row >> ls) == b)[None]
            wm = jnp.where(mine, w_b, jnp.zeros_like(w_b))
            qm = jnp.where(mine, qb, jnp.zeros_like(qb))
        ws = _bmm(wm, Sb) if ws is None else ws + _bmm(wm, Sb)
        qs = _bmm(qm, Sb) if qs is None else qs + _bmm(qm, Sb)
    ub = (uw[:, :, :hd] - ws).astype(BF16)
    o = eg * qs + _bmm((dec * qk).astype(BF16), ub)
    for b in range(n_seq):
        last = b * L + L - 1
        glast = jnp.stack([g[last:last + 1, SM_A + h:SM_A + h + 1] for h in range(nh)])
        kdm = kd if n_seq == 1 else jnp.where(((row >> ls) == b)[None], kd, jnp.zeros_like(kd))
        s_scr[b] = jnp.exp(glast) * s_old[b] + _btn(kdm, ub)

    on = o * lax.rsqrt(jnp.mean(o * o, axis=-1, keepdims=True) + RMS_EPS) * ng_ref[...]
    zz = jnp.stack(z_l)
    res = on * (zz * jax.nn.sigmoid(zz))
    for h in range(nh):
        o_ref[:, h * hd:(h + 1) * hd] = res[h, :c_in]

    @pl.when(c == pl.num_programs(1) - 1)
    def _():
        sfin_ref[...] = s_scr[...]


def _gdn(parts, prev, s0, cw, hp, ng, *, batch, n_chunks, c_in, n_valid, n_seq, shared_init):
    n = parts.shape[0]
    C = max(c_in, GDN_CHUNK)
    assert n_seq == 1 or (n_chunks == 1 and c_in == C)
    L = C // n_seq
    init = (lambda b, c: (0, 0, 0)) if shared_init else (lambda b, c: (b, 0, 0))
    init4 = (lambda b, c: (0, 0, 0, 0)) if shared_init else (lambda b, c: (b, 0, 0, 0))
    qkv_w = 3 * GDN_HEADS * GDN_D
    v_w = GDN_HEADS * GDN_D
    return pl.pallas_call(
        functools.partial(_gdn_kernel, c_in=c_in, n_valid=n_valid, n_seq=n_seq),
        grid=(batch, n_chunks),
        in_specs=[
            pl.BlockSpec((c_in, qkv_w), lambda b, c: (b * n_chunks + c, COL_QKV // qkv_w)),
            pl.BlockSpec((c_in, v_w), lambda b, c: (b * n_chunks + c, COL_Z // v_w)),
            pl.BlockSpec((c_in, LANE), lambda b, c: (b * n_chunks + c, COL_SM // LANE)),
            pl.BlockSpec((n_seq, SUBLANE, qkv_w), init),
            pl.BlockSpec((n_seq, GDN_HEADS, GDN_D, GDN_D), init4),
            pl.BlockSpec((CONV_W, qkv_w), lambda b, c: (0, 0)),
            pl.BlockSpec((2, LANE), lambda b, c: (0, 0)),
            pl.BlockSpec((1, GDN_D), lambda b, c: (0, 0)),
        ],
        out_specs=[
            pl.BlockSpec((c_in, v_w), lambda b, c: (b * n_chunks + c, 0)),
            pl.BlockSpec((n_seq, GDN_HEADS, GDN_D, GDN_D), lambda b, c: (b, 0, 0, 0)),
        ],
        out_shape=[jax.ShapeDtypeStruct((n, v_w), F32),
                   jax.ShapeDtypeStruct((batch * n_seq, GDN_HEADS, GDN_D, GDN_D), F32)],
        scratch_shapes=[pltpu.VMEM((n_seq, GDN_HEADS, GDN_D, GDN_D), F32),
                        pltpu.VMEM((n_seq * (SUBLANE + L), qkv_w), F32)],
        compiler_params=_cparams(("arbitrary", "arbitrary")),
        name="gated_deltanet",
    )(parts, parts, parts, prev, s0, cw, hp, ng)


def _sort_key(x):
    b = pltpu.bitcast(x + 0.0, I32)
    return b ^ ((b >> 31) & 0x7FFFFFFF)


KEY_NEG_INF = -(2 ** 31) + 0x7FFFFF
SEARCH_BITS_PER_TRIP = 4


def _count_ge(key_tiles, cand):
    tot = None
    for kt in key_tiles:
        c = jnp.sum((kt >= cand).astype(F32), axis=1, keepdims=True)
        tot = c if tot is None else tot + c
    return tot


def _rows8(x):
    return x.reshape(x.shape[0] // SUBLANE, SUBLANE, x.shape[1])


def _sum_rows8(x, chains=4):
    rows = x.shape[0]
    if rows % (chains * SUBLANE):
        return jnp.sum(_rows8(x), axis=0)
    part = rows // chains
    sums = [jnp.sum(_rows8(x[k * part:(k + 1) * part]), axis=0) for k in range(chains)]
    while len(sums) > 1:
        sums = [a + b for a, b in zip(sums[0::2], sums[1::2])]
    return sums[0]


def _attn_prompt_kernel(q_ref, iq_ref, smq_ref, k_ref, v_ref, smk_ref, mk_ref, mv_ref, msm_ref,
                        o_ref, kbf, vtb, kibf, key_scr, bias_scr, s_scr, cut_scr, *, n_sel, n_meta, kt):
    tq = q_ref.shape[0]
    n_real = k_ref.shape[0]
    kvw = k_ref.shape[1]
    qb_i = pl.program_id(1)
    nt = (qb_i * tq + tq + kt - 1) // kt
    group = ATT_HEADS // ATT_KV_HEADS

    @pl.when(qb_i == 0)
    def _():
        zrows = LANE - n_meta
        kbf[0:n_real, :] = k_ref[...].astype(BF16)
        kbf[n_real:, :] = jnp.concatenate([mk_ref[...], jnp.zeros((zrows, kvw), F32)], axis=0).astype(BF16)
        vtb[:, 0:n_real] = v_ref[...].T.astype(BF16)
        vtb[:, n_real:] = jnp.concatenate([mv_ref[...], jnp.zeros((zrows, kvw), F32)], axis=0).T.astype(BF16)
        kibf[0:n_real, :] = smk_ref[...][:, SM_IK:SM_IK + IDX_DIM].astype(BF16)
        kibf[n_real:, :] = jnp.concatenate(
            [msm_ref[...][:, SM_IK:SM_IK + IDX_DIM], jnp.zeros((zrows, IDX_DIM), F32)], axis=0).astype(BF16)

    iq = iq_ref[...]
    iq_stack = jnp.concatenate([iq[:, h * IDX_DIM:(h + 1) * IDX_DIM] for h in range(IDX_HEADS)], axis=0).astype(BF16)
    w_t = smq_ref[...].T * ((IDX_DIM ** -0.5) * (IDX_HEADS ** -0.5))
    qpos = qb_i * tq + lax.broadcasted_iota(I32, (1, tq), 1)

    def index_scores(start, width):
        s_all = _nt(kibf[pl.ds(start, width), :], iq_stack)
        acc = jnp.zeros((width, tq), F32)
        for h in range(IDX_HEADS):
            acc = acc + jnp.maximum(s_all[:, h * tq:(h + 1) * tq], 0.0) * w_t[SM_IW + h:SM_IW + h + 1, :]
        return acc

    def real_tile(t, carry):
        start = pl.multiple_of(t * kt, kt)
        kpos = start + lax.broadcasted_iota(I32, (kt, 1), 0)
        key_scr[pl.ds(start, kt), :] = _sort_key(jnp.where(kpos <= qpos, index_scores(start, kt), NEG_INF))
        return carry

    lax.fori_loop(0, nt, real_tile, 0)
    mrow = lax.broadcasted_iota(I32, (LANE, 1), 0)
    key_scr[n_real:, :] = _sort_key(jnp.where(mrow < n_meta, index_scores(n_real, LANE), NEG_INF))

    def count_ge(cand):
        def body(t, acc):
            kk = key_scr[pl.ds(pl.multiple_of(t * kt, kt), kt), :]
            return acc + _sum_rows8((kk >= cand).astype(F32))

        acc = lax.fori_loop(0, nt, body, jnp.zeros((SUBLANE, tq), F32))
        acc = acc + jnp.sum(_rows8((key_scr[n_real:, :] >= cand).astype(F32)), axis=0)
        return jnp.sum(acc, axis=0, keepdims=True)

    few = qpos + 1 + n_meta <= n_sel

    def unsettled(state):
        it, _, cnt = state
        return (it < 32) & (jnp.max(jnp.where(few | (cnt == n_sel), 0.0, 1.0)) > 0.0)

    def refine(state):
        it, thr, cnt = state
        for step in range(SEARCH_BITS_PER_TRIP):
            cand = thr + lax.shift_left(jnp.int32(1), 31 - step - it)
            c = count_ge(cand)
            ok = c >= n_sel
            thr = jnp.where(ok, cand, thr)
            cnt = jnp.where(ok, c, cnt)
        return it + SEARCH_BITS_PER_TRIP, thr, cnt

    n_all = (nt * kt + LANE).astype(F32)
    _, thr, cnt = lax.while_loop(
        unsettled, refine, (jnp.int32(0), jnp.full((1, tq), INT_MIN, I32), jnp.full((1, tq), 1.0, F32) * n_all))

    tie = (cnt > n_sel) & (thr > KEY_NEG_INF) & jnp.logical_not(few)
    n_pos = n_real + n_meta
    cut_scr[...] = jnp.full((SUBLANE, tq), float(n_pos), F32)

    def pos_of(start, width, is_meta):
        r = lax.broadcasted_iota(I32, (width, 1), 0)
        return r if is_meta else n_meta + start + r

    @pl.when(jnp.max(tie.astype(F32)) > 0.0)
    def _tie_cut():
        need = n_sel - count_ge(thr + 1)
        nbits = max(1, (n_pos - 1).bit_length())

        def count_eq_le(cut):
            def body(t, acc):
                start = pl.multiple_of(t * kt, kt)
                kk = key_scr[pl.ds(start, kt), :]
                return acc + jnp.sum(_rows8(((kk == thr) & (pos_of(start, kt, False) <= cut)).astype(F32)), axis=0)

            acc = lax.fori_loop(0, nt, body, jnp.zeros((SUBLANE, tq), F32))
            mk = key_scr[n_real:, :]
            acc = acc + jnp.sum(_rows8(((mk == thr) & (pos_of(0, LANE, True) <= cut)).astype(F32)), axis=0)
            return jnp.sum(acc, axis=0, keepdims=True)

        def body(it, cut):
            cand = cut - lax.shift_left(jnp.int32(1), nbits - 1 - it)
            ok = (cand >= 0) & (count_eq_le(cand) >= need)
            return jnp.where(ok, cand, cut)

        cut = lax.fori_loop(0, nbits, body, jnp.full((1, tq), (1 << nbits) - 1, I32))
        cut = jnp.where(tie, cut, n_pos).astype(F32)
        cut_scr[...] = jnp.broadcast_to(cut, (SUBLANE, tq))

    cut = cut_scr[0:1, :].astype(I32)

    def bias_tile(start, width, is_meta):
        kk = key_scr[pl.ds(start, width), :]
        sel = (kk > thr) | ((kk == thr) & (pos_of(start, width, is_meta) <= cut))
        sel = sel & (kk > KEY_NEG_INF)
        bias_scr[pl.ds(start, width), :] = jnp.where(sel, 0.0, NEG_INF)

    def bias_body(t, carry):
        bias_tile(pl.multiple_of(t * kt, kt), kt, False)
        return carry

    lax.fori_loop(0, nt, bias_body, 0)
    bias_tile(n_real, LANE, True)

    scale = ATT_DH ** -0.5
    for g in range(ATT_KV_HEADS):
        q4 = jnp.concatenate([q_ref[:, (g * group + j) * ATT_DH:(g * group + j + 1) * ATT_DH] for j in range(group)],
                             axis=0).astype(BF16)

        def score_tile(start, width):
            s = _nt(kbf[pl.ds(start, width), g * ATT_DH:(g + 1) * ATT_DH], q4) * scale
            b = bias_scr[pl.ds(start, width), :]
            s = s + jnp.concatenate([b] * group, axis=1)
            s_scr[pl.ds(start, width), :] = s
            return jnp.max(_rows8(s), axis=0)

        def pass1(t, m):
            return jnp.maximum(m, score_tile(pl.multiple_of(t * kt, kt), kt))

        m = jnp.max(lax.fori_loop(0, nt, pass1, score_tile(n_real, LANE)), axis=0, keepdims=True)

        def prob_tile(start, width):
            p = jnp.exp(s_scr[pl.ds(start, width), :] - m)
            pv = _mm(vtb[g * ATT_DH:(g + 1) * ATT_DH, pl.ds(start, width)], p.astype(BF16))
            return _sum_rows8(p), pv

        def pass2(t, carry):
            l, acc = carry
            dl, dacc = prob_tile(pl.multiple_of(t * kt, kt), kt)
            return l + dl, acc + dacc

        l, acc = lax.fori_loop(0, nt, pass2, prob_tile(n_real, LANE))
        out_t = acc / jnp.sum(l, axis=0, keepdims=True)
        for j in range(group):
            hh = g * group + j
            o_ref[:, hh * ATT_DH:(hh + 1) * ATT_DH] = out_t[:, j * tq:(j + 1) * tq].T


def _attn_prompt(parts, mparts, *, batch, seq, n_meta, tq, kt):
    nqb = seq // tq
    n_sel = min(TOPK_MAX, (seq + n_meta) // 4)
    kvw = ATT_KV_HEADS * ATT_DH
    nk = seq + LANE
    return pl.pallas_call(
        functools.partial(_attn_prompt_kernel, n_sel=n_sel, n_meta=n_meta, kt=kt),
        grid=(batch, nqb),
        in_specs=[
            pl.BlockSpec((tq, D), lambda b, i: (b * nqb + i, COL_AQ // D)),
            pl.BlockSpec((tq, D), lambda b, i: (b * nqb + i, COL_IQ // D)),
            pl.BlockSpec((tq, LANE), lambda b, i: (b * nqb + i, COL_SM // LANE)),
            pl.BlockSpec((seq, kvw), lambda b, i: (b, COL_AK // kvw)),
            pl.BlockSpec((seq, kvw), lambda b, i: (b, COL_AV // kvw)),
            pl.BlockSpec((seq, LANE), lambda b, i: (b, COL_SM // LANE)),
            pl.BlockSpec((n_meta, kvw), lambda b, i: (0, COL_AK // kvw)),
            pl.BlockSpec((n_meta, kvw), lambda b, i: (0, COL_AV // kvw)),
            pl.BlockSpec((n_meta, LANE), lambda b, i: (0, COL_SM // LANE)),
        ],
        out_specs=pl.BlockSpec((tq, D), lambda b, i: (b * nqb + i, 0)),
        out_shape=jax.ShapeDtypeStruct((batch * seq, D), F32),
        scratch_shapes=[pltpu.VMEM((nk, kvw), BF16), pltpu.VMEM((kvw, nk), BF16), pltpu.VMEM((nk, IDX_DIM), BF16),
                        pltpu.VMEM((nk, tq), I32), pltpu.VMEM((nk, tq), F32),
                        pltpu.VMEM((nk, (ATT_HEADS // ATT_KV_HEADS) * tq), F32), pltpu.VMEM((SUBLANE, tq), F32)],
        compiler_params=_cparams(("arbitrary", "arbitrary")),
        name="sparse_attention_prompt",
    )(parts, parts, parts, parts, parts, parts, mparts, mparts, mparts)


def _attn_sample_kernel(pt_ref, *refs, n_pages, n_sel):
    kp = refs[0:n_pages]
    vp = refs[n_pages:2 * n_pages]
    ip = refs[2 * n_pages:3 * n_pages]
    q_ref, iq_ref, kn_ref, vn_ref, sm_ref, o_ref, kil, vil, kibf, cut_scr = refs[3 * n_pages:]
    del pt_ref
    t_new = q_ref.shape[0]
    page = ip[0].shape[0]
    past = n_pages * page
    nkv = ATT_KV_HEADS
    for j in range(n_pages):
        kil[j * nkv * page:(j + 1) * nkv * page, :] = kp[j][...].astype(BF16)
        vil[j * nkv * page:(j + 1) * nkv * page, :] = vp[j][...].astype(BF16)
        kibf[j * page:(j + 1) * page, :] = ip[j][...].astype(BF16)
    sm = sm_ref[...]
    zpad = jnp.zeros((LANE - t_new, nkv * ATT_DH), BF16)
    knew = jnp.concatenate([kn_ref[...].astype(BF16), zpad], axis=0)
    vnew = jnp.concatenate([vn_ref[...].astype(BF16), zpad], axis=0)
    kibf[past:, :] = jnp.concatenate(
        [sm[:, SM_IK:SM_IK + IDX_DIM].astype(BF16), jnp.zeros((LANE - t_new, IDX_DIM), BF16)], axis=0)

    iq = iq_ref[...]
    qi = jnp.concatenate([iq[:, h * IDX_DIM:(h + 1) * IDX_DIM] for h in range(IDX_HEADS)], axis=0).astype(BF16)
    wcol = jnp.concatenate([sm[:, SM_IW + h:SM_IW + h + 1] for h in range(IDX_HEADS)], axis=0)
    wcol = wcol * ((IDX_DIM ** -0.5) * (IDX_HEADS ** -0.5))

    def idx_scores(start, width):
        s = jnp.maximum(_nt(qi, kibf[start:start + width, :]), 0.0) * wcol
        acc = s[0:t_new]
        for h in range(1, IDX_HEADS):
            acc = acc + s[h * t_new:(h + 1) * t_new]
        return acc

    i_past = idx_scores(0, past)
    i_new = idx_scores(past, LANE)
    trow = lax.broadcasted_iota(I32, (t_new, LANE), 0)
    tlane = lax.broadcasted_iota(I32, (t_new, LANE), 1)
    key_past = _sort_key(i_past)
    key_new = _sort_key(jnp.where(tlane <= trow, i_new, NEG_INF))
    tiles = [key_past, key_new]

    def digit_body(it, thr):
        step = lax.shift_left(jnp.int32(1), 28 - 4 * it)
        digit = jnp.zeros((t_new, 1), I32)
        for j in range(1, 16):
            digit = digit + (_count_ge(tiles, thr + j * step) >= n_sel).astype(I32)
        return thr + digit * step

    thr = lax.fori_loop(0, 8, digit_body, jnp.full((t_new, 1), INT_MIN, I32))
    n_ge = _count_ge(tiles, thr)
    tie = (n_ge > n_sel) & (thr > KEY_NEG_INF)
    pos_past = lax.broadcasted_iota(I32, (1, past), 1)
    pos_new = past + lax.broadcasted_iota(I32, (1, LANE), 1)
    n_pos = past + t_new
    cut_scr[...] = jnp.full(cut_scr.shape, n_pos, I32)

    @pl.when(jnp.max(tie.astype(F32)) > 0.0)
    def _tie_cut():
        need = n_sel - _count_ge(tiles, thr + 1)
        nbits = max(1, (n_pos - 1).bit_length())

        def cut_body(it, cut):
            cand = cut - lax.shift_left(jnp.int32(1), nbits - 1 - it)
            cnt = (jnp.sum(((key_past == thr) & (pos_past <= cand)).astype(F32), axis=1, keepdims=True)
                   + jnp.sum(((key_new == thr) & (pos_new <= cand)).astype(F32), axis=1, keepdims=True))
            ok = (cand >= 0) & (cnt >= need)
            return jnp.where(ok, cand, cut)

        cut = lax.fori_loop(0, nbits, cut_body, jnp.full((t_new, 1), (1 << nbits) - 1, I32))
        cut_scr[...] = jnp.broadcast_to(jnp.where(tie, cut, n_pos), cut_scr.shape)

    cut = cut_scr[:, 0:1]

    def chosen(kk, pos):
        return ((kk > thr) | ((kk == thr) & (pos <= cut))) & (kk > KEY_NEG_INF)

    group = ATT_HEADS // ATT_KV_HEADS
    b_new = jnp.concatenate([jnp.where(chosen(key_new, pos_new), 0.0, NEG_INF)] * group, axis=0)
    sel01 = chosen(key_past, pos_past).astype(BF16)
    er = lax.broadcasted_iota(I32, (page, nkv * page), 0)
    ec = lax.broadcasted_iota(I32, (page, nkv * page), 1)
    expand = ((ec >= er * nkv) & (ec < (er + 1) * nkv)).astype(BF16)
    sel_il = jnp.concatenate([_mm(sel01[:, j * page:(j + 1) * page], expand) for j in range(n_pages)], axis=1)
    il_lane = lax.broadcasted_iota(I32, (1, nkv * past), 1)
    assert nkv & (nkv - 1) == 0
    il_head = il_lane & (nkv - 1)
    scale = ATT_DH ** -0.5
    q = q_ref[...]
    for g in range(ATT_KV_HEADS):
        qg = jnp.concatenate(
            [q[:, (g * group + j) * ATT_DH:(g * group + j + 1) * ATT_DH] for j in range(group)], axis=0).astype(BF16)
        b_g = jnp.where((sel_il > 0.5) & (il_head == g), 0.0, NEG_INF)
        s_p = _nt(qg, kil[...]) * scale + jnp.concatenate([b_g] * group, axis=0)
        s_n = _nt(qg, knew[:, g * ATT_DH:(g + 1) * ATT_DH]) * scale + b_new
        m = jnp.maximum(jnp.max(s_p, axis=1, keepdims=True), jnp.max(s_n, axis=1, keepdims=True))
        p_p = jnp.exp(s_p - m)
        p_n = jnp.exp(s_n - m)
        l = jnp.sum(p_p, axis=1, keepdims=True) + jnp.sum(p_n, axis=1, keepdims=True)
        acc = _mm(p_p.astype(BF16), vil[...]) + _mm(p_n.astype(BF16), vnew[:, g * ATT_DH:(g + 1) * ATT_DH])
        res = acc / l
        for j in range(group):
            hh = g * group + j
            o_ref[:, hh * ATT_DH:(hh + 1) * ATT_DH] = res[j * t_new:(j + 1) * t_new]


def _attn_sample(parts, cache_k, cache_v, cache_ik, page_table, *, t_new):
    db, n_pages = page_table.shape
    page = cache_k.shape[2]
    kvw = ATT_KV_HEADS * ATT_DH
    past = n_pages * page
    n_sel = min(TOPK_MAX, (past + t_new) // 4)

    n_pool = cache_k.shape[1]
    cache_k = cache_k.reshape(n_pool, page * ATT_KV_HEADS, ATT_DH)
    cache_v = cache_v.reshape(n_pool, page * ATT_KV_HEADS, ATT_DH)
    cache_ik = cache_ik.reshape(n_pool, page, IDX_DIM)

    def kv_page(j):
        return pl.BlockSpec((None, page * ATT_KV_HEADS, ATT_DH), lambda b, pt, j=j: (pt[b, j], 0, 0))

    def ik_page(j):
        return pl.BlockSpec((None, page, IDX_DIM), lambda b, pt, j=j: (pt[b, j], 0, 0))

    in_specs = ([kv_page(j) for j in range(n_pages)] + [kv_page(j) for j in range(n_pages)]
                + [ik_page(j) for j in range(n_pages)]
                + [pl.BlockSpec((t_new, D), lambda b, pt: (b, COL_AQ // D)),
                   pl.BlockSpec((t_new, D), lambda b, pt: (b, COL_IQ // D)),
                   pl.BlockSpec((t_new, kvw), lambda b, pt: (b, COL_AK // kvw)),
                   pl.BlockSpec((t_new, kvw), lambda b, pt: (b, COL_AV // kvw)),
                   pl.BlockSpec((t_new, LANE), lambda b, pt: (b, COL_SM // LANE))])
    grid_spec = pltpu.PrefetchScalarGridSpec(
        num_scalar_prefetch=1, grid=(db,), in_specs=in_specs,
        out_specs=pl.BlockSpec((t_new, D), lambda b, pt: (b, 0)),
        scratch_shapes=[pltpu.VMEM((past * ATT_KV_HEADS, ATT_DH), BF16), pltpu.VMEM((past * ATT_KV_HEADS, ATT_DH), BF16),
                        pltpu.VMEM((past + LANE, IDX_DIM), BF16), pltpu.VMEM((t_new, LANE), I32)])
    return pl.pallas_call(
        functools.partial(_attn_sample_kernel, n_pages=n_pages, n_sel=n_sel),
        grid_spec=grid_spec,
        out_shape=jax.ShapeDtypeStruct((db * t_new, D), F32),
        compiler_params=_cparams(("arbitrary",)),
        name="sparse_attention_sample",
    )(page_table, *([cache_k] * n_pages), *([cache_v] * n_pages), *([cache_ik] * n_pages),
      parts, parts, parts, parts, parts)


def _layer_norm(x, g, b):
    mu = jnp.mean(x, axis=-1, keepdims=True)
    xc = x - mu
    var = jnp.mean(xc * xc, axis=-1, keepdims=True)
    return xc * lax.rsqrt(var + LN_EPS) * g + b


def _finish_kernel(x_ref, og_ref, oa_ref, ga_ref, gb_ref, wbg_ref, wba_ref, wo_ref, g_ref, b_ref, h_ref, *, alpha):
    a = _mm(og_ref[...].astype(BF16), wbg_ref[...])
    b = _mm(oa_ref[...].astype(BF16), wba_ref[...])
    merged = jax.nn.sigmoid(ga_ref[...]) * a + jax.nn.sigmoid(gb_ref[...]) * b
    y = alpha * x_ref[...] + _mm(merged.astype(BF16), wo_ref[...])
    h_ref[...] = _layer_norm(y, g_ref[...], b_ref[...])


def _finish(x, o_gdn, o_att, parts, wbg, wba, wo, g, b, *, alpha, tm):
    n = x.shape[0]
    tm = min(tm, n)
    row = lambda i: (i, 0)
    full = lambda i: (0, 0)
    return pl.pallas_call(
        functools.partial(_finish_kernel, alpha=alpha),
        grid=(n // tm,),
        in_specs=[pl.BlockSpec((tm, D), row), pl.BlockSpec((tm, D), row), pl.BlockSpec((tm, D), row),
                  pl.BlockSpec((tm, D), lambda i: (i, COL_GA // D)), pl.BlockSpec((tm, D), lambda i: (i, COL_GB // D)),
                  pl.BlockSpec((D, D), full), pl.BlockSpec((D, D), full), pl.BlockSpec((D, D), full),
                  pl.BlockSpec((1, D), full), pl.BlockSpec((1, D), full)],
        out_specs=pl.BlockSpec((tm, D), row),
        out_shape=jax.ShapeDtypeStruct((n, D), F32),
        compiler_params=_cparams(("arbitrary",)),
        name="merge_layernorm",
    )(x, o_gdn, o_att, parts, parts, wbg, wba, wo, g, b)


_CAND = [(r0, r1) for r0 in range(PEER_TOPK) for r1 in range(PEER_TOPK) if (r0 + 1) * (r1 + 1) <= PEER_TOPK]
_CAND_OFF = [next(i for i, c in enumerate(_CAND) if c[0] == r0) for r0 in range(PEER_TOPK)]
_CAND_LEN = [sum(1 for c in _CAND if c[0] == r0) for r0 in range(PEER_TOPK)]
_CAND_ROWS = -(-len(_CAND) // SUBLANE) * SUBLANE


def _top_rows(s, n_top, break_ties):
    rows, cols = s.shape
    iota = lax.broadcasted_iota(I32, (rows, cols), 0).astype(F32)
    rank = jnp.full((rows, cols), float(n_top), F32)
    vals = []
    for r in range(n_top):
        m = jnp.max(s, axis=0, keepdims=True)
        hit = s == m
        if break_ties:
            hit = iota == jnp.min(jnp.where(hit, iota, float(rows)), axis=0, keepdims=True)
        vals.append(m)
        s = jnp.where(hit, NEG_INF, s)
        rank = jnp.where(hit, float(r), rank)
    n_ranked = jnp.sum((rank < float(n_top)).astype(F32), axis=0, keepdims=True)
    return jnp.concatenate(vals, axis=0), rank, n_ranked == float(n_top)


def _peer_kernel(h_ref, wq_ref, sk_ref, u0_ref, ub_ref, un_ref, vp_ref, va_ref, vl_ref, g_ref, b_ref, y_ref,
                 hb_scr, q_scr, rank1_scr, bt_scr, nt_scr, at_scr, s1_scr,
                 sa_scr, sb_scr, pa_scr, pb_scr, acc_scr, *, alpha, n_steps):
    tt = h_ref.shape[0]
    eb = ub_ref.shape[0]
    e = pl.program_id(1)
    nk = PEER_NKEYS
    kk = PEER_TOPK
    tchunks = tt // LANE

    @pl.when(e == 0)
    def _prep():
        hb = h_ref[...].astype(BF16)
        hb_scr[...] = hb
        q_scr[...] = _mm(hb, wq_ref[...]).astype(BF16)
        acc_scr[...] = jnp.zeros(acc_scr.shape, F32)

        def scores(hd, carry):
            q0 = q_scr[:, pl.ds(pl.multiple_of(hd * 2 * nk, nk), nk)]
            q1 = q_scr[:, pl.ds(pl.multiple_of(hd * 2 * nk + nk, nk), nk)]
            at_scr[hd] = _nt(sk_ref[hd, 0], q0)
            s1_scr[hd] = _nt(sk_ref[hd, 1], q1)
            return carry

        lax.fori_loop(0, PEER_HEADS, scores, 0)

        def select_chunk(hd, c0, s0, s1, break_ties):
            a, rank0, ok0 = _top_rows(s0, kk, break_ties)
            b, rank1, ok1 = _top_rows(s1, kk, break_ties)
            pad = [jnp.full((_CAND_ROWS - len(_CAND), LANE), NEG_INF, F32)]
            cand = jnp.concatenate([a[r0:r0 + 1] + b[r1:r1 + 1] for r0, r1 in _CAND] + pad, axis=0)
            top, crank, okc = _top_rows(cand, kk, break_ties)
            zsum = jnp.sum(jnp.exp(top - top[0:1]), axis=0, keepdims=True)
            chosen = (crank < float(kk)).astype(F32)
            crow = lax.broadcasted_iota(I32, (_CAND_ROWS, 1), 0)
            nsel = jnp.zeros((nk, LANE), F32)
            for r0 in range(kk):
                in_group = (crow >= _CAND_OFF[r0]) & (crow < _CAND_OFF[r0] + _CAND_LEN[r0])
                cnt = jnp.sum(jnp.where(in_group, chosen, 0.0), axis=0, keepdims=True)
                nsel = nsel + jnp.where(rank0 == float(r0), cnt, 0.0)
            rank1_scr[hd, :, pl.ds(c0, LANE)] = rank1.astype(BF16)
            nt_scr[hd, :, pl.ds(c0, LANE)] = nsel
            at_scr[hd, :, pl.ds(c0, LANE)] = jnp.exp(s0 - a[0:1])
            bt_scr[hd, :, pl.ds(c0, LANE)] = (jnp.exp(s1 - b[0:1]) / zsum).astype(BF16)
            return jnp.min((ok0 & ok1 & okc).astype(F32)) > 0.5

        pair = 2 if tchunks % 2 == 0 else 1

        def select(idx, carry):
            hd = idx // (tchunks // pair)
            first = (idx % (tchunks // pair)) * pair
            tie_free = None
            chunks = []
            for k in range(pair):
                c0 = pl.multiple_of((first + k) * LANE, LANE)
                s0 = at_scr[hd, :, pl.ds(c0, LANE)]
                s1 = s1_scr[hd, :, pl.ds(c0, LANE)]
                chunks.append((c0, s0, s1))
            for c0, s0, s1 in chunks:
                ok = select_chunk(hd, c0, s0, s1, False)
                tie_free = ok if tie_free is None else tie_free & ok

            @pl.when(jnp.logical_not(tie_free))
            def _():
                for c0, s0, s1 in chunks:
                    select_chunk(hd, c0, s0, s1, True)

            return carry

        lax.fori_loop(0, PEER_HEADS * tchunks // pair, select, 0)

        sa_scr[1] = _nt(u0_ref[...], hb_scr[...])
        pb_scr[1] = jnp.zeros(pb_scr.shape[1:], BF16)

    groups = eb // nk

    assert 2 * groups == SUBLANE

    def weights(half, s_ref, p_ref):
        i0 = pl.multiple_of(e * SUBLANE, SUBLANE)
        zero = jnp.zeros((nk, LANE), BF16)

        for tc in range(tchunks):
            cols = slice(tc * LANE, (tc + 1) * LANE)
            gates = [zero] * groups
            for hd in range(PEER_HEADS):
                nblk = nt_scr[hd, pl.ds(i0, SUBLANE), cols]
                ablk = at_scr[hd, pl.ds(i0, SUBLANE), cols]
                rank1 = rank1_scr[hd, :, cols]
                bval = bt_scr[hd, :, cols]
                for ii in range(groups):
                    r = half * groups + ii
                    nrow = nblk[r:r + 1].astype(BF16)
                    arow = ablk[r:r + 1].astype(BF16)
                    gates[ii] = gates[ii] + jnp.where(rank1 < nrow, bval, zero) * arow
            for ii in range(groups):
                rows = slice(ii * nk, (ii + 1) * nk)
                s = s_ref[rows, cols]
                act = 0.5 * s * (1.0 + lax.erf(s * (2.0 ** -0.5)))
                p_ref[rows, cols] = gates[ii] * act.astype(BF16)

    def step(cur, nxt):
        sb_scr[...] = _nt(ub_ref[...], hb_scr[...])
        weights(0, sa_scr.at[cur], pa_scr)
        acc_scr[...] += _tn(pb_scr[cur], vp_ref[...])
        weights(1, sb_scr, pb_scr.at[nxt])
        sa_scr[nxt] = _nt(un_ref[...], hb_scr[...])
        acc_scr[...] += _tn(pa_scr[...], va_ref[...])

    @pl.when(e % 2 == 0)
    def _():
        step(1, 0)

    @pl.when(e % 2 == 1)
    def _():
        step(0, 1)

    last_slot = (n_steps - 1) % 2

    @pl.when(e == n_steps - 1)
    def _():
        y = alpha * h_ref[...] + (acc_scr[...] + _tn(pb_scr[last_slot], vl_ref[...]))
        y_ref[...] = _layer_norm(y, g_ref[...], b_ref[...])


def _peer(h, wq, sk, u, v, g, b, *, alpha, tt, eb):
    n = h.shape[0]
    n_exp = u.shape[0]
    nk = PEER_NKEYS
    nb = n_exp // eb
    assert nb % 2 == 0
    return pl.pallas_call(
        functools.partial(_peer_kernel, alpha=alpha, n_steps=nb // 2),
        grid=(n // tt, nb // 2),
        in_specs=[pl.BlockSpec((tt, D), lambda t, e: (t, 0)),
                  pl.BlockSpec(wq.shape, lambda t, e: (0, 0)),
                  pl.BlockSpec(sk.shape, lambda t, e: (0, 0, 0, 0)),
                  pl.BlockSpec((eb, D), lambda t, e: (0, 0)),
                  pl.BlockSpec((eb, D), lambda t, e: (2 * e + 1, 0)),
                  pl.BlockSpec((eb, D), lambda t, e: (jnp.minimum(2 * e + 2, nb - 1), 0)),
                  pl.BlockSpec((eb, D), lambda t, e: (jnp.maximum(2 * e - 1, 0), 0)),
                  pl.BlockSpec((eb, D), lambda t, e: (2 * e, 0)),
                  pl.BlockSpec((eb, D), lambda t, e: (nb - 1, 0)),
                  pl.BlockSpec((1, D), lambda t, e: (0, 0)),
                  pl.BlockSpec((1, D), lambda t, e: (0, 0))],
        out_specs=pl.BlockSpec((tt, D), lambda t, e: (t, 0)),
        out_shape=jax.ShapeDtypeStruct((n, D), F32),
        scratch_shapes=[pltpu.VMEM((tt, D), BF16), pltpu.VMEM((tt, wq.shape[1]), BF16),
                        pltpu.VMEM((PEER_HEADS, nk, tt), BF16), pltpu.VMEM((PEER_HEADS, nk, tt), BF16),
                        pltpu.VMEM((PEER_HEADS, nk, tt), F32), pltpu.VMEM((PEER_HEADS, nk, tt), F32),
                        pltpu.VMEM((PEER_HEADS, nk, tt), F32),
                        pltpu.VMEM((2, eb, tt), F32), pltpu.VMEM((eb, tt), F32),
                        pltpu.VMEM((eb, tt), BF16), pltpu.VMEM((2, eb, tt), BF16), pltpu.VMEM((tt, D), F32)],
        compiler_params=_cparams(("arbitrary", "arbitrary")),
        name="peer_ffn",
    )(h, wq, sk, u, u, u, v, v, v, g, b)


def _permute_w_in(w):
    gq = 3 * GDN_HEADS * GDN_D
    gv = GDN_HEADS * GDN_D
    aq = ATT_HEADS * ATT_DH
    akv = ATT_KV_HEADS * ATT_DH
    iqw = IDX_HEADS * IDX_DIM
    sizes = (gq, gv, GDN_HEADS, GDN_HEADS, aq, akv, akv, iqw, IDX_DIM, IDX_HEADS, D, D)
    offs = [0]
    for s in sizes:
        offs.append(offs[-1] + s)
    seg = [w[:, offs[i]:offs[i + 1]] for i in range(len(sizes))]
    qkv, z, b, a, q, k, v, iq, ik, iw, ga, gb = seg
    pad = jnp.zeros((w.shape[0], LANE - IDX_DIM - 2 * GDN_HEADS - IDX_HEADS), w.dtype)
    return jnp.concatenate([qkv, z, q, iq, ga, gb, k, v, ik, b, a, iw, pad], axis=1).astype(BF16)


def kernel(x_prompt, x_sample, cache_k, cache_v, cache_idx_k, state_conv, state_delta, page_table, meta_tokens, w_in, conv_w, a_log, dt_bias, gdn_norm_g, w_branch_gdn, w_branch_attn, w_out, ln1_g, ln1_b, peer_wq, peer_subkeys, peer_u, peer_v, ln2_g, ln2_b):
    depth = w_in.shape[0]
    assert depth == 1, "single-layer step"
    batch, seq, d = x_prompt.shape
    db, t_new, _ = x_sample.shape
    n_meta = meta_tokens.shape[0]
    assert d == D and seq % GDN_CHUNK == 0 and n_meta % SUBLANE == 0 and n_meta <= GDN_CHUNK
    alpha = (2 * depth) ** 0.25
    qkv_w = 3 * GDN_HEADS * GDN_D
    kvw = ATT_KV_HEADS * ATT_DH

    w_r = _permute_w_in(w_in[0])
    xp = x_prompt.reshape(batch * seq, D)
    xs = x_sample.reshape(db * t_new, D)
    tm = 512 if (batch * seq) % 512 == 0 else 256
    parts_p = _project(xp, w_r, tm)
    parts_s = _project(xs, w_r, min(tm, db * t_new))
    parts_m = _project(meta_tokens.astype(F32), w_r, n_meta)

    hp = jnp.zeros((2, LANE), F32).at[0, SM_A:SM_A + GDN_HEADS].set(a_log[0]).at[1, SM_A:SM_A + GDN_HEADS].set(dt_bias[0])
    cw = conv_w[0]
    ng = gdn_norm_g[0].reshape(1, GDN_D)
    zero_prev = jnp.zeros((1, SUBLANE, qkv_w), F32)
    zero_state = jnp.zeros((1, GDN_HEADS, GDN_D, GDN_D), F32)
    _, s_meta = _gdn(parts_m, zero_prev, zero_state, cw, hp, ng, batch=1, n_chunks=1, c_in=n_meta, n_valid=n_meta,
                     n_seq=1, shared_init=True)
    meta_prev = parts_m[n_meta - SUBLANE:, :qkv_w].reshape(1, SUBLANE, qkv_w)
    og_p, p_delta = _gdn(parts_p, meta_prev, s_meta, cw, hp, ng, batch=batch, n_chunks=seq // GDN_CHUNK,
                         c_in=GDN_CHUNK, n_valid=GDN_CHUNK, n_seq=1, shared_init=True)
    per = GDN_CHUNK // t_new
    assert per * t_new == GDN_CHUNK and t_new % SUBLANE == 0 and db % per == 0
    samp_prev = jnp.pad(state_conv.reshape(db, CONV_W - 1, qkv_w), ((0, 0), (SUBLANE - (CONV_W - 1), 0), (0, 0)))
    og_s, s_delta = _gdn(parts_s, samp_prev, state_delta.reshape(db, GDN_HEADS, GDN_D, GDN_D).astype(F32), cw, hp, ng,
                         batch=db // per, n_chunks=1, c_in=GDN_CHUNK, n_valid=GDN_CHUNK, n_seq=per, shared_init=False)

    tq = 128
    oa_p = _attn_prompt(parts_p, parts_m, batch=batch, seq=seq, n_meta=n_meta, tq=tq, kt=min(512, seq))
    oa_s = _attn_sample(parts_s, cache_k, cache_v, cache_idx_k, page_table, t_new=t_new)

    wbg = w_branch_gdn[0].astype(BF16)
    wba = w_branch_attn[0].astype(BF16)
    wo = w_out[0].astype(BF16)
    g1 = ln1_g[0].reshape(1, D)
    b1 = ln1_b[0].reshape(1, D)
    h_p = _finish(xp, og_p, oa_p, parts_p, wbg, wba, wo, g1, b1, alpha=alpha, tm=256)
    h_s = _finish(xs, og_s, oa_s, parts_s, wbg, wba, wo, g1, b1, alpha=alpha, tm=256)

    wq = peer_wq[0].astype(BF16)
    sk = peer_subkeys[0].astype(BF16)
    u = peer_u[0].astype(BF16)
    v = peer_v[0].astype(BF16)
    g2 = ln2_g[0].reshape(1, D)
    b2 = ln2_b[0].reshape(1, D)
    tt = 512 if (batch * seq) % 512 == 0 else 256
    y_p = _peer(h_p, wq, sk, u, v, g2, b2, alpha=alpha, tt=tt, eb=512)
    y_s = _peer(h_s, wq, sk, u, v, g2, b2, alpha=alpha, tt=min(tt, db * t_new), eb=512)

    y_prompt = y_p.reshape(batch, seq, D)
    y_sample = y_s.reshape(db, t_new, D)
    pp = parts_p.reshape(batch, seq, N_COLS)
    ps = parts_s.reshape(db, t_new, N_COLS)
    p_conv = pp[:, seq - (CONV_W - 1):, :qkv_w][None]
    s_conv = jnp.concatenate([state_conv[0].astype(F32), ps[:, :, :qkv_w]], axis=1)[:, -(CONV_W - 1):][None]

    def with_meta(col, width):
        m = jnp.broadcast_to(parts_m[None, :, col:col + width], (batch, n_meta, width))
        return jnp.concatenate([m, pp[:, :, col:col + width]], axis=1)

    p_k = with_meta(COL_AK, kvw).reshape(1, batch, seq + n_meta, ATT_KV_HEADS, ATT_DH)
    p_v = with_meta(COL_AV, kvw).reshape(1, batch, seq + n_meta, ATT_KV_HEADS, ATT_DH)
    p_idx_k = with_meta(COL_SM + SM_IK, IDX_DIM)[None]
    s_k = ps[:, :, COL_AK:COL_AK + kvw].reshape(1, db, t_new, ATT_KV_HEADS, ATT_DH)
    s_v = ps[:, :, COL_AV:COL_AV + kvw].reshape(1, db, t_new, ATT_KV_HEADS, ATT_DH)
    s_idx_k = ps[:, :, COL_SM + SM_IK:COL_SM + SM_IK + IDX_DIM][None]
    return (y_prompt, y_sample, p_conv, p_delta[None], p_k, p_v, p_idx_k, s_conv, s_delta[None].astype(state_delta.dtype),
            s_k, s_v, s_idx_k)
```

```python
import functools

import jax
import jax.numpy as jnp
from jax import lax
from jax.experimental import pallas as pl
from jax.experimental.pallas import tpu as pltpu

F32 = jnp.float32
BF16 = jnp.bfloat16
I32 = jnp.int32

GDN_HEADS = 8
GDN_D = 128
CONV_W = 4
GDN_CHUNK = 64
ATT_HEADS = 8
ATT_KV_HEADS = 2
ATT_DH = 128
IDX_HEADS = 16
IDX_DIM = 64
TOPK_MAX = 256
PEER_HEADS = 8
PEER_NKEYS = 128
PEER_TOPK = 16
LN_EPS = 1e-5
RMS_EPS = 1e-6

LANE = 128
SUBLANE = 8
VMEM_LIMIT = 56 * 1024 * 1024

D = 1024
COL_QKV = 0
COL_Z = 3072
COL_AQ = 4096
COL_IQ = 5120
COL_GA = 6144
COL_GB = 7168
COL_AK = 8192
COL_AV = 8448
COL_SM = 8704
N_COLS = 8832
SM_IK = 0
SM_B = 64
SM_A = 72
SM_IW = 80

NEG_INF = float("-inf")
INT_MIN = -(2 ** 31)


def _nt(a, b):
    return lax.dot_general(a, b, (((1,), (1,)), ((), ())), preferred_element_type=F32)


def _tn(a, b):
    return lax.dot_general(a, b, (((0,), (0,)), ((), ())), preferred_element_type=F32)


def _mm(a, b):
    return jnp.dot(a, b, preferred_element_type=F32)


def _mm_f32(a, b):
    return jnp.dot(a, b, preferred_element_type=F32, precision=lax.Precision.HIGHEST)


def _cparams(sem):
    return pltpu.CompilerParams(dimension_semantics=sem, vmem_limit_bytes=VMEM_LIMIT)


def _proj_kernel(x_ref, w_ref, o_ref):
    o_ref[...] = _mm(x_ref[...].astype(BF16), w_ref[...])


def _project(x, w_bf, tm):
    n = x.shape[0]
    tn = N_COLS // 3
    return pl.pallas_call(
        _proj_kernel,
        grid=(3, n // tm),
        in_specs=[pl.BlockSpec((tm, D), lambda j, i: (i, 0)),
                  pl.BlockSpec((D, tn), lambda j, i: (0, j))],
        out_specs=pl.BlockSpec((tm, tn), lambda j, i: (i, j)),
        out_shape=jax.ShapeDtypeStruct((n, N_COLS), F32),
        compiler_params=_cparams(("arbitrary", "arbitrary")),
        name="in_projection",
    )(x, w_bf)


def _sibling_mask(ri, ci, lvl):
    return ((ri >> (lvl + 1)) == (ci >> (lvl + 1))) & (((ri >> lvl) & 1) == 1) & (((ci >> lvl) & 1) == 0)


_BATCH0 = ((0,), (0,))


def _bmm(a, b):
    return lax.dot_general(a, b, (((2,), (1,)), _BATCH0), preferred_element_type=F32)


def _bnt(a, b):
    return lax.dot_general(a, b, (((2,), (2,)), _BATCH0), preferred_element_type=F32)


def _btn(a, b):
    return lax.dot_general(a, b, (((1,), (1,)), _BATCH0), preferred_element_type=F32)


def _gdn_kernel(qkv_ref, z_ref, sm_ref, prev_ref, s0_ref, cw_ref, hp_ref, ng_ref,
                o_ref, sfin_ref, s_scr, ext_scr, *, c_in, n_valid, n_seq):
    nh = GDN_HEADS
    hd = GDN_D
    stride = ext_scr.shape[0] // n_seq
    L = stride - SUBLANE
    C = L * n_seq
    ls = L.bit_length() - 1
    c = pl.program_id(1)

    @pl.when(c == 0)
    def _():
        s_scr[...] = s0_ref[...]
        for b in range(n_seq):
            ext_scr[b * stride:b * stride + SUBLANE, :] = prev_ref[b]

    u = qkv_ref[...]
    sm = sm_ref[...]
    z = z_ref[...]
    if c_in < C:
        u = jnp.concatenate([u, jnp.zeros((C - c_in, u.shape[1]), F32)], axis=0)
        sm = jnp.concatenate([sm, jnp.zeros((C - c_in, LANE), F32)], axis=0)
        z = jnp.concatenate([z, jnp.zeros((C - c_in, z.shape[1]), F32)], axis=0)
    cw = cw_ref[...]
    pieces = []
    for b in range(n_seq):
        base = b * stride + SUBLANE
        ub_ = u[b * L:(b + 1) * L]
        ext_scr[base:base + L, :] = ub_
        acc = ub_ * cw[CONV_W - 1:CONV_W, :]
        for i in range(CONV_W - 1):
            off = base - (CONV_W - 1) + i
            acc = acc + ext_scr[off:off + L, :] * cw[i:i + 1, :]
        pieces.append(acc)
    conv = pieces[0] if n_seq == 1 else jnp.concatenate(pieces, axis=0)
    if n_seq == 1:
        tail = ext_scr[c_in:c_in + SUBLANE, :]
        ext_scr[0:SUBLANE, :] = tail
    qkv = conv * jax.nn.sigmoid(conv)

    row = lax.broadcasted_iota(I32, (C, 1), 0)
    hp = hp_ref[...]
    xs = sm + hp[1:2, :]
    softplus = jnp.maximum(xs, 0.0) + jnp.log1p(jnp.exp(-jnp.abs(xs)))
    la = -jnp.exp(hp[0:1, :]) * softplus
    beta = jax.nn.sigmoid(sm)
    if n_valid < C:
        valid = row < n_valid
        qkv = jnp.where(valid, qkv, 0.0)
        la = jnp.where(valid, la, 0.0)
        beta = jnp.where(valid, beta, 0.0)

    ri = lax.broadcasted_iota(I32, (C, C), 0)
    ci = lax.broadcasted_iota(I32, (C, C), 1)
    same = (ri >> ls) == (ci >> ls)
    incl = same & (ri >= ci)
    strict = same & (ri > ci)
    g = _mm_f32(incl.astype(F32), la)
    g_end = _mm_f32((ci == (ri | (L - 1))).astype(F32), g)
    gt = g.T
    eye = (ri == ci).astype(F32)

    q_l, k_l, kd_l, dec_l, lows_l, rhs_l, eg_l, z_l = [], [], [], [], [], [], [], []
    for h in range(nh):
        col = SM_A + h
        gcol = g[:, col:col + 1]
        grow = gt[col:col + 1, :]
        bcol = beta[:, SM_B + h:SM_B + h + 1]
        q = qkv[:, h * hd:(h + 1) * hd]
        k = qkv[:, nh * hd + h * hd:nh * hd + (h + 1) * hd]
        v = qkv[:, 2 * nh * hd + h * hd:2 * nh * hd + (h + 1) * hd]
        q = q * lax.rsqrt(jnp.sum(q * q, axis=-1, keepdims=True) + RMS_EPS) * (hd ** -0.5)
        k = k * lax.rsqrt(jnp.sum(k * k, axis=-1, keepdims=True) + RMS_EPS)
        dec = jnp.exp(jnp.where(incl, gcol - grow, NEG_INF))
        eg = jnp.exp(gcol)
        q_l.append(q.astype(BF16))
        k_l.append(k.astype(BF16))
        kd_l.append((k * jnp.exp(g_end[:, col:col + 1] - gcol)).astype(BF16))
        dec_l.append(dec)
        lows_l.append(bcol * jnp.where(strict, dec, 0.0))
        rhs_l.append(jnp.concatenate([bcol * v, (bcol * eg) * k], axis=1))
        eg_l.append(eg)
        z_l.append(z[:, h * hd:(h + 1) * hd])
    qb = jnp.stack(q_l)
    kb = jnp.stack(k_l)
    kd = jnp.stack(kd_l)
    dec = jnp.stack(dec_l)
    eg = jnp.stack(eg_l)

    low = jnp.stack(lows_l) * _bnt(kb, kb)
    qk = _bnt(qb, kb)
    inv = eye[None] - jnp.where(_sibling_mask(ri, ci, 0)[None], low, 0.0)
    for lvl in range(1, ls):
        off = jnp.where(_sibling_mask(ri, ci, lvl)[None], low, 0.0)
        inv_b = inv.astype(BF16)
        inv = inv - _bmm(inv_b, _bmm(off.astype(BF16), inv_b).astype(BF16))
    uw = _bmm(inv.astype(BF16), jnp.stack(rhs_l).astype(BF16))
    w_b = uw[:, :, hd:].astype(BF16)

    ws = None
    qs = None
    s_old = []
    for b in range(n_seq):
        S = s_scr[b]
        s_old.append(S)
        Sb = S.astype(BF16)
        if n_seq == 1:
            wm, qm = w_b, qb
        else:
            mine = ((row >> ls) == b)[None]
            wm = jnp.where(mine, w_b, jnp.zeros_like(w_b))
            qm = jnp.where(mine, qb, jnp.zeros_like(qb))
        ws = _bmm(wm, Sb) if ws is None else ws + _bmm(wm, Sb)
        qs = _bmm(qm, Sb) if qs is None else qs + _bmm(qm, Sb)
    ub = (uw[:, :, :hd] - ws).astype(BF16)
    o = eg * qs + _bmm((dec * qk).astype(BF16), ub)
    for b in range(n_seq):
        last = b * L + L - 1
        glast = jnp.stack([g[last:last + 1, SM_A + h:SM_A + h + 1] for h in range(nh)])
        kdm = kd if n_seq == 1 else jnp.where(((row >> ls) == b)[None], kd, jnp.zeros_like(kd))
        s_scr[b] = jnp.exp(glast) * s_old[b] + _btn(kdm, ub)

    on = o * lax.rsqrt(jnp.mean(o * o, axis=-1, keepdims=True) + RMS_EPS) * ng_ref[...]
    zz = jnp.stack(z_l)
    res = on * (zz * jax.nn.sigmoid(zz))
    for h in range(nh):
        o_ref[:, h * hd:(h + 1) * hd] = res[h, :c_in]

    @pl.when(c == pl.num_programs(1) - 1)
    def _():
        sfin_ref[...] = s_scr[...]


def _gdn(parts, prev, s0, cw, hp, ng, *, batch, n_chunks, c_in, n_valid, n_seq, shared_init):
    n = parts.shape[0]
    C = max(c_in, GDN_CHUNK)
    assert n_seq == 1 or (n_chunks == 1 and c_in == C)
    L = C // n_seq
    init = (lambda b, c: (0, 0, 0)) if shared_init else (lambda b, c: (b, 0, 0))
    init4 = (lambda b, c: (0, 0, 0, 0)) if shared_init else (lambda b, c: (b, 0, 0, 0))
    qkv_w = 3 * GDN_HEADS * GDN_D
    v_w = GDN_HEADS * GDN_D
    return pl.pallas_call(
        functools.partial(_gdn_kernel, c_in=c_in, n_valid=n_valid, n_seq=n_seq),
        grid=(batch, n_chunks),
        in_specs=[
            pl.BlockSpec((c_in, qkv_w), lambda b, c: (b * n_chunks + c, COL_QKV // qkv_w)),
            pl.BlockSpec((c_in, v_w), lambda b, c: (b * n_chunks + c, COL_Z // v_w)),
            pl.BlockSpec((c_in, LANE), lambda b, c: (b * n_chunks + c, COL_SM // LANE)),
            pl.BlockSpec((n_seq, SUBLANE, qkv_w), init),
            pl.BlockSpec((n_seq, GDN_HEADS, GDN_D, GDN_D), init4),
            pl.BlockSpec((CONV_W, qkv_w), lambda b, c: (0, 0)),
            pl.BlockSpec((2, LANE), lambda b, c: (0, 0)),
            pl.BlockSpec((1, GDN_D), lambda b, c: (0, 0)),
        ],
        out_specs=[
            pl.BlockSpec((c_in, v_w), lambda b, c: (b * n_chunks + c, 0)),
            pl.BlockSpec((n_seq, GDN_HEADS, GDN_D, GDN_D), lambda b, c: (b, 0, 0, 0)),
        ],
        out_shape=[jax.ShapeDtypeStruct((n, v_w), F32),
                   jax.ShapeDtypeStruct((batch * n_seq, GDN_HEADS, GDN_D, GDN_D), F32)],
        scratch_shapes=[pltpu.VMEM((n_seq, GDN_HEADS, GDN_D, GDN_D), F32),
                        pltpu.VMEM((n_seq * (SUBLANE + L), qkv_w), F32)],
        compiler_params=_cparams(("arbitrary", "arbitrary")),
        name="gated_deltanet",
    )(parts, parts, parts, prev, s0, cw, hp, ng)


def _sort_key(x):
    b = pltpu.bitcast(x + 0.0, I32)
    return b ^ ((b >> 31) & 0x7FFFFFFF)


KEY_NEG_INF = -(2 ** 31) + 0x7FFFFF
SEARCH_BITS_PER_TRIP = 4


def _count_ge(key_tiles, cand):
    tot = None
    for kt in key_tiles:
        c = jnp.sum((kt >= cand).astype(F32), axis=1, keepdims=True)
        tot = c if tot is None else tot + c
    return tot


def _rows8(x):
    return x.reshape(x.shape[0] // SUBLANE, SUBLANE, x.shape[1])


def _sum_rows8(x, chains=4):
    rows = x.shape[0]
    if rows % (chains * SUBLANE):
        return jnp.sum(_rows8(x), axis=0)
    part = rows // chains
    sums = [jnp.sum(_rows8(x[k * part:(k + 1) * part]), axis=0) for k in range(chains)]
    while len(sums) > 1:
        sums = [a + b for a, b in zip(sums[0::2], sums[1::2])]
    return sums[0]


def _attn_prompt_kernel(q_ref, iq_ref, smq_ref, k_ref, v_ref, smk_ref, mk_ref, mv_ref, msm_ref,
                        o_ref, kbf, vtb, kibf, key_scr, bias_scr, s_scr, cut_scr, *, n_sel, n_meta, kt):
    tq = q_ref.shape[0]
    n_real = k_ref.shape[0]
    kvw = k_ref.shape[1]
    qb_i = pl.program_id(1)
    nt = (qb_i * tq + tq + kt - 1) // kt
    group = ATT_HEADS // ATT_KV_HEADS

    @pl.when(qb_i == 0)
    def _():
        zrows = LANE - n_meta
        kbf[0:n_real, :] = k_ref[...].astype(BF16)
        kbf[n_real:, :] = jnp.concatenate([mk_ref[...], jnp.zeros((zrows, kvw), F32)], axis=0).astype(BF16)
        vtb[:, 0:n_real] = v_ref[...].T.astype(BF16)
        vtb[:, n_real:] = jnp.concatenate([mv_ref[...], jnp.zeros((zrows, kvw), F32)], axis=0).T.astype(BF16)
        kibf[0:n_real, :] = smk_ref[...][:, SM_IK:SM_IK + IDX_DIM].astype(BF16)
        kibf[n_real:, :] = jnp.concatenate(
            [msm_ref[...][:, SM_IK:SM_IK + IDX_DIM], jnp.zeros((zrows, IDX_DIM), F32)], axis=0).astype(BF16)

    iq = iq_ref[...]
    iq_stack = jnp.concatenate([iq[:, h * IDX_DIM:(h + 1) * IDX_DIM] for h in range(IDX_HEADS)], axis=0).astype(BF16)
    w_t = smq_ref[...].T * ((IDX_DIM ** -0.5) * (IDX_HEADS ** -0.5))
    qpos = qb_i * tq + lax.broadcasted_iota(I32, (1, tq), 1)

    def index_scores(start, width):
        s_all = _nt(kibf[pl.ds(start, width), :], iq_stack)
        acc = jnp.zeros((width, tq), F32)
        for h in range(IDX_HEADS):
            acc = acc + jnp.maximum(s_all[:, h * tq:(h + 1) * tq], 0.0) * w_t[SM_IW + h:SM_IW + h + 1, :]
        return acc

    def real_tile(t, carry):
        start = pl.multiple_of(t * kt, kt)
        kpos = start + lax.broadcasted_iota(I32, (kt, 1), 0)
        key_scr[pl.ds(start, kt), :] = _sort_key(jnp.where(kpos <= qpos, index_scores(start, kt), NEG_INF))
        return carry

    lax.fori_loop(0, nt, real_tile, 0)
    mrow = lax.broadcasted_iota(I32, (LANE, 1), 0)
    key_scr[n_real:, :] = _sort_key(jnp.where(mrow < n_meta, index_scores(n_real, LANE), NEG_INF))

    def count_ge(cand):
        def body(t, acc):
            kk = key_scr[pl.ds(pl.multiple_of(t * kt, kt), kt), :]
            return acc + _sum_rows8((kk >= cand).astype(F32))

        acc = lax.fori_loop(0, nt, body, jnp.zeros((SUBLANE, tq), F32))
        acc = acc + jnp.sum(_rows8((key_scr[n_real:, :] >= cand).astype(F32)), axis=0)
        return jnp.sum(acc, axis=0, keepdims=True)

    few = qpos + 1 + n_meta <= n_sel

    def unsettled(state):
        it, _, cnt = state
        return (it < 32) & (jnp.max(jnp.where(few | (cnt == n_sel), 0.0, 1.0)) > 0.0)

    def refine(state):
        it, thr, cnt = state
        for step in range(SEARCH_BITS_PER_TRIP):
            cand = thr + lax.shift_left(jnp.int32(1), 31 - step - it)
            c = count_ge(cand)
            ok = c >= n_sel
            thr = jnp.where(ok, cand, thr)
            cnt = jnp.where(ok, c, cnt)
        return it + SEARCH_BITS_PER_TRIP, thr, cnt

    n_all = (nt * kt + LANE).astype(F32)
    _, thr, cnt = lax.while_loop(
        unsettled, refine, (jnp.int32(0), jnp.full((1, tq), INT_MIN, I32), jnp.full((1, tq), 1.0, F32) * n_all))

    tie = (cnt > n_sel) & (thr > KEY_NEG_INF) & jnp.logical_not(few)
    n_pos = n_real + n_meta
    cut_scr[...] = jnp.full((SUBLANE, tq), float(n_pos), F32)

    def pos_of(start, width, is_meta):
        r = lax.broadcasted_iota(I32, (width, 1), 0)
        return r if is_meta else n_meta + start + r

    @pl.when(jnp.max(tie.astype(F32)) > 0.0)
    def _tie_cut():
        need = n_sel - count_ge(thr + 1)
        nbits = max(1, (n_pos - 1).bit_length())

        def count_eq_le(cut):
            def body(t, acc):
                start = pl.multiple_of(t * kt, kt)
                kk = key_scr[pl.ds(start, kt), :]
                return acc + jnp.sum(_rows8(((kk == thr) & (pos_of(start, kt, False) <= cut)).astype(F32)), axis=0)

            acc = lax.fori_loop(0, nt, body, jnp.zeros((SUBLANE, tq), F32))
            mk = key_scr[n_real:, :]
            acc = acc + jnp.sum(_rows8(((mk == thr) & (pos_of(0, LANE, True) <= cut)).astype(F32)), axis=0)
            return jnp.sum(acc, axis=0, keepdims=True)

        def body(it, cut):
            cand = cut - lax.shift_left(jnp.int32(1), nbits - 1 - it)
            ok = (cand >= 0) & (count_eq_le(cand) >= need)
            return jnp.where(ok, cand, cut)

        cut = lax.fori_loop(0, nbits, body, jnp.full((1, tq), (1 << nbits) - 1, I32))
        cut = jnp.where(tie, cut, n_pos).astype(F32)
        cut_scr[...] = jnp.broadcast_to(cut, (SUBLANE, tq))

    cut = cut_scr[0:1, :].astype(I32)

    def bias_tile(start, width, is_meta):
        kk = key_scr[pl.ds(start, width), :]
        sel = (kk > thr) | ((kk == thr) & (pos_of(start, width, is_meta) <= cut))
        sel = sel & (kk > KEY_NEG_INF)
        bias_scr[pl.ds(start, width), :] = jnp.where(sel, 0.0, NEG_INF)

    def bias_body(t, carry):
        bias_tile(pl.multiple_of(t * kt, kt), kt, False)
        return carry

    lax.fori_loop(0, nt, bias_body, 0)
    bias_tile(n_real, LANE, True)

    scale = ATT_DH ** -0.5
    for g in range(ATT_KV_HEADS):
        q4 = jnp.concatenate([q_ref[:, (g * group + j) * ATT_DH:(g * group + j + 1) * ATT_DH] for j in range(group)],
                             axis=0).astype(BF16)

        def score_tile(start, width):
            s = _nt(kbf[pl.ds(start, width), g * ATT_DH:(g + 1) * ATT_DH], q4) * scale
            b = bias_scr[pl.ds(start, width), :]
            s = s + jnp.concatenate([b] * group, axis=1)
            s_scr[pl.ds(start, width), :] = s
            return jnp.max(_rows8(s), axis=0)

        def pass1(t, m):
            return jnp.maximum(m, score_tile(pl.multiple_of(t * kt, kt), kt))

        m = jnp.max(lax.fori_loop(0, nt, pass1, score_tile(n_real, LANE)), axis=0, keepdims=True)

        def prob_tile(start, width):
            p = jnp.exp(s_scr[pl.ds(start, width), :] - m)
            pv = _mm(vtb[g * ATT_DH:(g + 1) * ATT_DH, pl.ds(start, width)], p.astype(BF16))
            return _sum_rows8(p), pv

        def pass2(t, carry):
            l, acc = carry
            dl, dacc = prob_tile(pl.multiple_of(t * kt, kt), kt)
            return l + dl, acc + dacc

        l, acc = lax.fori_loop(0, nt, pass2, prob_tile(n_real, LANE))
        out_t = acc / jnp.sum(l, axis=0, keepdims=True)
        for j in range(group):
            hh = g * group + j
            o_ref[:, hh * ATT_DH:(hh + 1) * ATT_DH] = out_t[:, j * tq:(j + 1) * tq].T


def _attn_prompt(parts, mparts, *, batch, seq, n_meta, tq, kt):
    nqb = seq // tq
    n_sel = min(TOPK_MAX, (seq + n_meta) // 4)
    kvw = ATT_KV_HEADS * ATT_DH
    nk = seq + LANE
    return pl.pallas_call(
        functools.partial(_attn_prompt_kernel, n_sel=n_sel, n_meta=n_meta, kt=kt),
        grid=(batch, nqb),
        in_specs=[
            pl.BlockSpec((tq, D), lambda b, i: (b * nqb + i, COL_AQ // D)),
            pl.BlockSpec((tq, D), lambda b, i: (b * nqb + i, COL_IQ // D)),
            pl.BlockSpec((tq, LANE), lambda b, i: (b * nqb + i, COL_SM // LANE)),
            pl.BlockSpec((seq, kvw), lambda b, i: (b, COL_AK // kvw)),
            pl.BlockSpec((seq, kvw), lambda b, i: (b, COL_AV // kvw)),
            pl.BlockSpec((seq, LANE), lambda b, i: (b, COL_SM // LANE)),
            pl.BlockSpec((n_meta, kvw), lambda b, i: (0, COL_AK // kvw)),
            pl.BlockSpec((n_meta, kvw), lambda b, i: (0, COL_AV // kvw)),
            pl.BlockSpec((n_meta, LANE), lambda b, i: (0, COL_SM // LANE)),
        ],
        out_specs=pl.BlockSpec((tq, D), lambda b, i: (b * nqb + i, 0)),
        out_shape=jax.ShapeDtypeStruct((batch * seq, D), F32),
        scratch_shapes=[pltpu.VMEM((nk, kvw), BF16), pltpu.VMEM((kvw, nk), BF16), pltpu.VMEM((nk, IDX_DIM), BF16),
                        pltpu.VMEM((nk, tq), I32), pltpu.VMEM((nk, tq), F32),
                        pltpu.VMEM((nk, (ATT_HEADS // ATT_KV_HEADS) * tq), F32), pltpu.VMEM((SUBLANE, tq), F32)],
        compiler_params=_cparams(("arbitrary", "arbitrary")),
        name="sparse_attention_prompt",
    )(parts, parts, parts, parts, parts, parts, mparts, mparts, mparts)


def _attn_sample_kernel(pt_ref, *refs, n_pages, n_sel):
    kp = refs[0:n_pages]
    vp = refs[n_pages:2 * n_pages]
    ip = refs[2 * n_pages:3 * n_pages]
    q_ref, iq_ref, kn_ref, vn_ref, sm_ref, o_ref, kil, vil, kit, cut_scr = refs[3 * n_pages:]
    del pt_ref
    t_new = q_ref.shape[0]
    page = ip[0].shape[1]
    past = n_pages * page
    nkv = ATT_KV_HEADS
    for j in range(n_pages):
        kil[j * nkv * page:(j + 1) * nkv * page, :] = kp[j][...].astype(BF16)
        vil[j * nkv * page:(j + 1) * nkv * page, :] = vp[j][...].astype(BF16)
        kit[:, j * page:(j + 1) * page] = ip[j][...].astype(BF16)
    sm = sm_ref[...]
    zpad = jnp.zeros((LANE - t_new, nkv * ATT_DH), BF16)
    knew = jnp.concatenate([kn_ref[...].astype(BF16), zpad], axis=0)
    vnew = jnp.concatenate([vn_ref[...].astype(BF16), zpad], axis=0)
    ki_new = jnp.concatenate(
        [sm[:, SM_IK:SM_IK + IDX_DIM].astype(BF16), jnp.zeros((LANE - t_new, IDX_DIM), BF16)], axis=0)

    iq = iq_ref[...]
    qi = jnp.concatenate([iq[:, h * IDX_DIM:(h + 1) * IDX_DIM] for h in range(IDX_HEADS)], axis=0).astype(BF16)
    wcol = jnp.concatenate([sm[:, SM_IW + h:SM_IW + h + 1] for h in range(IDX_HEADS)], axis=0)
    wcol = wcol * ((IDX_DIM ** -0.5) * (IDX_HEADS ** -0.5))

    def idx_scores(qk):
        s = jnp.maximum(qk, 0.0) * wcol
        acc = s[0:t_new]
        for h in range(1, IDX_HEADS):
            acc = acc + s[h * t_new:(h + 1) * t_new]
        return acc

    i_past = idx_scores(_mm(qi, kit[...]))
    i_new = idx_scores(_nt(qi, ki_new))
    trow = lax.broadcasted_iota(I32, (t_new, LANE), 0)
    tlane = lax.broadcasted_iota(I32, (t_new, LANE), 1)
    key_past = _sort_key(i_past)
    key_new = _sort_key(jnp.where(tlane <= trow, i_new, NEG_INF))
    tiles = [key_past, key_new]

    def digit_body(it, thr):
        step = lax.shift_left(jnp.int32(1), 28 - 4 * it)
        digit = jnp.zeros((t_new, 1), I32)
        for j in range(1, 16):
            digit = digit + (_count_ge(tiles, thr + j * step) >= n_sel).astype(I32)
        return thr + digit * step

    thr = lax.fori_loop(0, 8, digit_body, jnp.full((t_new, 1), INT_MIN, I32))
    n_ge = _count_ge(tiles, thr)
    tie = (n_ge > n_sel) & (thr > KEY_NEG_INF)
    pos_past = lax.broadcasted_iota(I32, (1, past), 1)
    pos_new = past + lax.broadcasted_iota(I32, (1, LANE), 1)
    n_pos = past + t_new
    cut_scr[...] = jnp.full(cut_scr.shape, n_pos, I32)

    @pl.when(jnp.max(tie.astype(F32)) > 0.0)
    def _tie_cut():
        need = n_sel - _count_ge(tiles, thr + 1)
        nbits = max(1, (n_pos - 1).bit_length())

        def cut_body(it, cut):
            cand = cut - lax.shift_left(jnp.int32(1), nbits - 1 - it)
            cnt = (jnp.sum(((key_past == thr) & (pos_past <= cand)).astype(F32), axis=1, keepdims=True)
                   + jnp.sum(((key_new == thr) & (pos_new <= cand)).astype(F32), axis=1, keepdims=True))
            ok = (cand >= 0) & (cnt >= need)
            return jnp.where(ok, cand, cut)

        cut = lax.fori_loop(0, nbits, cut_body, jnp.full((t_new, 1), (1 << nbits) - 1, I32))
        cut_scr[...] = jnp.broadcast_to(jnp.where(tie, cut, n_pos), cut_scr.shape)

    cut = cut_scr[:, 0:1]

    def chosen(kk, pos):
        return ((kk > thr) | ((kk == thr) & (pos <= cut))) & (kk > KEY_NEG_INF)

    group = ATT_HEADS // ATT_KV_HEADS
    b_new = jnp.concatenate([jnp.where(chosen(key_new, pos_new), 0.0, NEG_INF)] * group, axis=0)
    sel01 = chosen(key_past, pos_past).astype(BF16)
    er = lax.broadcasted_iota(I32, (page, nkv * page), 0)
    ec = lax.broadcasted_iota(I32, (page, nkv * page), 1)
    expand = ((ec >= er * nkv) & (ec < (er + 1) * nkv)).astype(BF16)
    sel_il = jnp.concatenate([_mm(sel01[:, j * page:(j + 1) * page], expand) for j in range(n_pages)], axis=1)
    il_lane = lax.broadcasted_iota(I32, (1, nkv * past), 1)
    assert nkv & (nkv - 1) == 0
    il_head = il_lane & (nkv - 1)
    scale = ATT_DH ** -0.5
    q = q_ref[...]
    for g in range(ATT_KV_HEADS):
        qg = jnp.concatenate(
            [q[:, (g * group + j) * ATT_DH:(g * group + j + 1) * ATT_DH] for j in range(group)], axis=0).astype(BF16)
        b_g = jnp.where((sel_il > 0.5) & (il_head == g), 0.0, NEG_INF)
        s_p = _nt(qg, kil[...]) * scale + jnp.concatenate([b_g] * group, axis=0)
        s_n = _nt(qg, knew[:, g * ATT_DH:(g + 1) * ATT_DH]) * scale + b_new
        m = jnp.maximum(jnp.max(s_p, axis=1, keepdims=True), jnp.max(s_n, axis=1, keepdims=True))
        p_p = jnp.exp(s_p - m)
        p_n = jnp.exp(s_n - m)
        l = jnp.sum(p_p, axis=1, keepdims=True) + jnp.sum(p_n, axis=1, keepdims=True)
        acc = _mm(p_p.astype(BF16), vil[...]) + _mm(p_n.astype(BF16), vnew[:, g * ATT_DH:(g + 1) * ATT_DH])
        res = acc / l
        for j in range(group):
            hh = g * group + j
            o_ref[:, hh * ATT_DH:(hh + 1) * ATT_DH] = res[j * t_new:(j + 1) * t_new]


def _attn_sample(parts, cache_k, cache_v, cache_ik, page_table, *, t_new):
    db, n_pages = page_table.shape
    page = cache_k.shape[2]
    kvw = ATT_KV_HEADS * ATT_DH
    past = n_pages * page
    n_sel = min(TOPK_MAX, (past + t_new) // 4)

    n_pool = cache_k.shape[1]
    cache_k = cache_k.reshape(n_pool, page * ATT_KV_HEADS, ATT_DH)
    cache_v = cache_v.reshape(n_pool, page * ATT_KV_HEADS, ATT_DH)
    cache_ik = jnp.swapaxes(cache_ik, 2, 3).reshape(n_pool, IDX_DIM, page)

    def kv_page(j):
        return pl.BlockSpec((None, page * ATT_KV_HEADS, ATT_DH), lambda b, pt, j=j: (pt[b, j], 0, 0))

    def ik_page(j):
        return pl.BlockSpec((None, IDX_DIM, page), lambda b, pt, j=j: (pt[b, j], 0, 0))

    in_specs = ([kv_page(j) for j in range(n_pages)] + [kv_page(j) for j in range(n_pages)]
                + [ik_page(j) for j in range(n_pages)]
                + [pl.BlockSpec((t_new, D), lambda b, pt: (b, COL_AQ // D)),
                   pl.BlockSpec((t_new, D), lambda b, pt: (b, COL_IQ // D)),
                   pl.BlockSpec((t_new, kvw), lambda b, pt: (b, COL_AK // kvw)),
                   pl.BlockSpec((t_new, kvw), lambda b, pt: (b, COL_AV // kvw)),
                   pl.BlockSpec((t_new, LANE), lambda b, pt: (b, COL_SM // LANE))])
    grid_spec = pltpu.PrefetchScalarGridSpec(
        num_scalar_prefetch=1, grid=(db,), in_specs=in_specs,
        out_specs=pl.BlockSpec((t_new, D), lambda b, pt: (b, 0)),
        scratch_shapes=[pltpu.VMEM((past * ATT_KV_HEADS, ATT_DH), BF16), pltpu.VMEM((past * ATT_KV_HEADS, ATT_DH), BF16),
                        pltpu.VMEM((IDX_DIM, past), BF16), pltpu.VMEM((t_new, LANE), I32)])
    return pl.pallas_call(
        functools.partial(_attn_sample_kernel, n_pages=n_pages, n_sel=n_sel),
        grid_spec=grid_spec,
        out_shape=jax.ShapeDtypeStruct((db * t_new, D), F32),
        compiler_params=_cparams(("arbitrary",)),
        name="sparse_attention_sample",
    )(page_table, *([cache_k] * n_pages), *([cache_v] * n_pages), *([cache_ik] * n_pages),
      parts, parts, parts, parts, parts)


def _layer_norm(x, g, b):
    mu = jnp.mean(x, axis=-1, keepdims=True)
    xc = x - mu
    var = jnp.mean(xc * xc, axis=-1, keepdims=True)
    return xc * lax.rsqrt(var + LN_EPS) * g + b


def _finish_kernel(x_ref, og_ref, oa_ref, ga_ref, gb_ref, wbg_ref, wba_ref, wo_ref, g_ref, b_ref, h_ref, *, alpha):
    a = _mm(og_ref[...].astype(BF16), wbg_ref[...])
    b = _mm(oa_ref[...].astype(BF16), wba_ref[...])
    merged = jax.nn.sigmoid(ga_ref[...]) * a + jax.nn.sigmoid(gb_ref[...]) * b
    y = alpha * x_ref[...] + _mm(merged.astype(BF16), wo_ref[...])
    h_ref[...] = _layer_norm(y, g_ref[...], b_ref[...])


def _finish(x, o_gdn, o_att, parts, wbg, wba, wo, g, b, *, alpha, tm):
    n = x.shape[0]
    tm = min(tm, n)
    row = lambda i: (i, 0)
    full = lambda i: (0, 0)
    return pl.pallas_call(
        functools.partial(_finish_kernel, alpha=alpha),
        grid=(n // tm,),
        in_specs=[pl.BlockSpec((tm, D), row), pl.BlockSpec((tm, D), row), pl.BlockSpec((tm, D), row),
                  pl.BlockSpec((tm, D), lambda i: (i, COL_GA // D)), pl.BlockSpec((tm, D), lambda i: (i, COL_GB // D)),
                  pl.BlockSpec((D, D), full), pl.BlockSpec((D, D), full), pl.BlockSpec((D, D), full),
                  pl.BlockSpec((1, D), full), pl.BlockSpec((1, D), full)],
        out_specs=pl.BlockSpec((tm, D), row),
        out_shape=jax.ShapeDtypeStruct((n, D), F32),
        compiler_params=_cparams(("arbitrary",)),
        name="merge_layernorm",
    )(x, o_gdn, o_att, parts, parts, wbg, wba, wo, g, b)


_CAND = [(r0, r1) for r0 in range(PEER_TOPK) for r1 in range(PEER_TOPK) if (r0 + 1) * (r1 + 1) <= PEER_TOPK]
_CAND_OFF = [next(i for i, c in enumerate(_CAND) if c[0] == r0) for r0 in range(PEER_TOPK)]
_CAND_LEN = [sum(1 for c in _CAND if c[0] == r0) for r0 in range(PEER_TOPK)]
_CAND_ROWS = -(-len(_CAND) // SUBLANE) * SUBLANE


def _top_rows(s, n_top, break_ties):
    rows, cols = s.shape
    iota = lax.broadcasted_iota(I32, (rows, cols), 0).astype(F32)
    rank = jnp.full((rows, cols), float(n_top), F32)
    vals = []
    for r in range(n_top):
        m = jnp.max(s, axis=0, keepdims=True)
        hit = s == m
        if break_ties:
            hit = iota == jnp.min(jnp.where(hit, iota, float(rows)), axis=0, keepdims=True)
        vals.append(m)
        s = jnp.where(hit, NEG_INF, s)
        rank = jnp.where(hit, float(r), rank)
    n_ranked = jnp.sum((rank < float(n_top)).astype(F32), axis=0, keepdims=True)
    return jnp.concatenate(vals, axis=0), rank, n_ranked == float(n_top)


def _peer_kernel(h_ref, wq_ref, sk_ref, u0_ref, un_ref, vp_ref, vl_ref, g_ref, b_ref, y_ref,
                 hb_scr, q_scr, rank1_scr, bt_scr, nt_scr, at_scr, s1_scr, s_scr, p_scr, acc_scr, *, alpha, n_steps):
    tt = h_ref.shape[0]
    eb = un_ref.shape[0] // 2
    e = pl.program_id(1)
    nk = PEER_NKEYS
    kk = PEER_TOPK
    tchunks = tt // LANE

    @pl.when(e == 0)
    def _prep():
        hb = h_ref[...].astype(BF16)
        hb_scr[...] = hb
        q_scr[...] = _mm(hb, wq_ref[...]).astype(BF16)
        acc_scr[...] = jnp.zeros(acc_scr.shape, F32)

        def scores(hd, carry):
            q0 = q_scr[:, pl.ds(pl.multiple_of(hd * 2 * nk, nk), nk)]
            q1 = q_scr[:, pl.ds(pl.multiple_of(hd * 2 * nk + nk, nk), nk)]
            at_scr[hd] = _nt(sk_ref[hd, 0], q0)
            s1_scr[hd] = _nt(sk_ref[hd, 1], q1)
            return carry

        lax.fori_loop(0, PEER_HEADS, scores, 0)

        def select_chunk(hd, c0, s0, s1, break_ties):
            a, rank0, ok0 = _top_rows(s0, kk, break_ties)
            b, rank1, ok1 = _top_rows(s1, kk, break_ties)
            pad = [jnp.full((_CAND_ROWS - len(_CAND), LANE), NEG_INF, F32)]
            cand = jnp.concatenate([a[r0:r0 + 1] + b[r1:r1 + 1] for r0, r1 in _CAND] + pad, axis=0)
            top, crank, okc = _top_rows(cand, kk, break_ties)
            zsum = jnp.sum(jnp.exp(top - top[0:1]), axis=0, keepdims=True)
            chosen = (crank < float(kk)).astype(F32)
            crow = lax.broadcasted_iota(I32, (_CAND_ROWS, 1), 0)
            nsel = jnp.zeros((nk, LANE), F32)
            for r0 in range(kk):
                in_group = (crow >= _CAND_OFF[r0]) & (crow < _CAND_OFF[r0] + _CAND_LEN[r0])
                cnt = jnp.sum(jnp.where(in_group, chosen, 0.0), axis=0, keepdims=True)
                nsel = nsel + jnp.where(rank0 == float(r0), cnt, 0.0)
            rank1_scr[hd, :, pl.ds(c0, LANE)] = rank1.astype(BF16)
            nt_scr[hd, :, pl.ds(c0, LANE)] = nsel
            at_scr[hd, :, pl.ds(c0, LANE)] = jnp.exp(s0 - a[0:1])
            bt_scr[hd, :, pl.ds(c0, LANE)] = (jnp.exp(s1 - b[0:1]) / zsum).astype(BF16)
            return jnp.min((ok0 & ok1 & okc).astype(F32)) > 0.5

        pair = 2 if tchunks % 2 == 0 else 1

        def select(idx, carry):
            hd = idx // (tchunks // pair)
            first = (idx % (tchunks // pair)) * pair
            tie_free = None
            chunks = []
            for k in range(pair):
                c0 = pl.multiple_of((first + k) * LANE, LANE)
                s0 = at_scr[hd, :, pl.ds(c0, LANE)]
                s1 = s1_scr[hd, :, pl.ds(c0, LANE)]
                chunks.append((c0, s0, s1))
            for c0, s0, s1 in chunks:
                ok = select_chunk(hd, c0, s0, s1, False)
                tie_free = ok if tie_free is None else tie_free & ok

            @pl.when(jnp.logical_not(tie_free))
            def _():
                for c0, s0, s1 in chunks:
                    select_chunk(hd, c0, s0, s1, True)

            return carry

        lax.fori_loop(0, PEER_HEADS * tchunks // pair, select, 0)

        s_scr[1] = _nt(u0_ref[...], hb_scr[...])
        p_scr[1] = jnp.zeros(p_scr.shape[1:], BF16)

    groups = eb // nk

    assert 2 * groups == SUBLANE

    def weights(half, s_ref, p_ref):
        i0 = pl.multiple_of(e * SUBLANE, SUBLANE)
        zero = jnp.zeros((nk, LANE), BF16)

        for tc in range(tchunks):
            cols = slice(tc * LANE, (tc + 1) * LANE)
            gates = [zero] * groups
            for hd in range(PEER_HEADS):
                nblk = nt_scr[hd, pl.ds(i0, SUBLANE), cols]
                ablk = at_scr[hd, pl.ds(i0, SUBLANE), cols]
                rank1 = rank1_scr[hd, :, cols]
                bval = bt_scr[hd, :, cols]
                for ii in range(groups):
                    r = half * groups + ii
                    nrow = nblk[r:r + 1].astype(BF16)
                    arow = ablk[r:r + 1].astype(BF16)
                    gates[ii] = gates[ii] + jnp.where(rank1 < nrow, bval, zero) * arow
            for ii in range(groups):
                rows = slice(ii * nk, (ii + 1) * nk)
                s = s_ref[rows, cols]
                act = 0.5 * s * (1.0 + lax.erf(s * (2.0 ** -0.5)))
                p_ref[rows, cols] = gates[ii] * act.astype(BF16)

    def step(cur, nxt):
        s_scr[nxt] = _nt(un_ref[...], hb_scr[...])
        for half in range(2):
            rows = pl.ds(half * eb, eb)
            weights(half, s_scr.at[cur, rows], p_scr.at[nxt, rows])
        acc_scr[...] += _tn(p_scr[cur], vp_ref[...])

    @pl.when(e % 2 == 0)
    def _():
        step(1, 0)

    @pl.when(e % 2 == 1)
    def _():
        step(0, 1)

    last_slot = (n_steps - 1) % 2

    @pl.when(e == n_steps - 1)
    def _():
        y = alpha * h_ref[...] + (acc_scr[...] + _tn(p_scr[last_slot], vl_ref[...]))
        y_ref[...] = _layer_norm(y, g_ref[...], b_ref[...])


def _peer(h, wq, sk, u, v, g, b, *, alpha, tt, eb):
    n = h.shape[0]
    n_exp = u.shape[0]
    nk = PEER_NKEYS
    nb = n_exp // eb
    return pl.pallas_call(
        functools.partial(_peer_kernel, alpha=alpha, n_steps=nb),
        grid=(n // tt, nb),
        in_specs=[pl.BlockSpec((tt, D), lambda t, e: (t, 0)),
                  pl.BlockSpec(wq.shape, lambda t, e: (0, 0)),
                  pl.BlockSpec(sk.shape, lambda t, e: (0, 0, 0, 0)),
                  pl.BlockSpec((eb, D), lambda t, e: (0, 0)),
                  pl.BlockSpec((eb, D), lambda t, e: (jnp.minimum(e + 1, nb - 1), 0)),
                  pl.BlockSpec((eb, D), lambda t, e: (jnp.maximum(e - 1, 0), 0)),
                  pl.BlockSpec((eb, D), lambda t, e: (nb - 1, 0)),
                  pl.BlockSpec((1, D), lambda t, e: (0, 0)),
                  pl.BlockSpec((1, D), lambda t, e: (0, 0))],
        out_specs=pl.BlockSpec((tt, D), lambda t, e: (t, 0)),
        out_shape=jax.ShapeDtypeStruct((n, D), F32),
        scratch_shapes=[pltpu.VMEM((tt, D), BF16), pltpu.VMEM((tt, wq.shape[1]), BF16),
                        pltpu.VMEM((PEER_HEADS, nk, tt), BF16), pltpu.VMEM((PEER_HEADS, nk, tt), BF16),
                        pltpu.VMEM((PEER_HEADS, nk, tt), F32), pltpu.VMEM((PEER_HEADS, nk, tt), F32),
                        pltpu.VMEM((PEER_HEADS, nk, tt), F32),
                        pltpu.VMEM((2, eb, tt), F32), pltpu.VMEM((2, eb, tt), BF16), pltpu.VMEM((tt, D), F32)],
        compiler_params=_cparams(("arbitrary", "arbitrary")),
        name="peer_ffn",
    )(h, wq, sk, u, u, v, v, g, b)


def _permute_w_in(w):
    gq = 3 * GDN_HEADS * GDN_D
    gv = GDN_HEADS * GDN_D
    aq = ATT_HEADS * ATT_DH
    akv = ATT_KV_HEADS * ATT_DH
    iqw = IDX_HEADS * IDX_DIM
    sizes = (gq, gv, GDN_HEADS, GDN_HEADS, aq, akv, akv, iqw, IDX_DIM, IDX_HEADS, D, D)
    offs = [0]
    for s in sizes:
        offs.append(offs[-1] + s)
    seg = [w[:, offs[i]:offs[i + 1]] for i in range(len(sizes))]
    qkv, z, b, a, q, k, v, iq, ik, iw, ga, gb = seg
    pad = jnp.zeros((w.shape[0], LANE - IDX_DIM - 2 * GDN_HEADS - IDX_HEADS), w.dtype)
    return jnp.concatenate([qkv, z, q, iq, ga, gb, k, v, ik, b, a, iw, pad], axis=1).astype(BF16)


def kernel(x_prompt, x_sample, cache_k, cache_v, cache_idx_k, state_conv, state_delta, page_table, meta_tokens, w_in, conv_w, a_log, dt_bias, gdn_norm_g, w_branch_gdn, w_branch_attn, w_out, ln1_g, ln1_b, peer_wq, peer_subkeys, peer_u, peer_v, ln2_g, ln2_b):
    depth = w_in.shape[0]
    assert depth == 1, "single-layer step"
    batch, seq, d = x_prompt.shape
    db, t_new, _ = x_sample.shape
    n_meta = meta_tokens.shape[0]
    assert d == D and seq % GDN_CHUNK == 0 and n_meta % SUBLANE == 0 and n_meta <= GDN_CHUNK
    alpha = (2 * depth) ** 0.25
    qkv_w = 3 * GDN_HEADS * GDN_D
    kvw = ATT_KV_HEADS * ATT_DH

    w_r = _permute_w_in(w_in[0])
    xp = x_prompt.reshape(batch * seq, D)
    xs = x_sample.reshape(db * t_new, D)
    tm = 512 if (batch * seq) % 512 == 0 else 256
    parts_p = _project(xp, w_r, tm)
    parts_s = _project(xs, w_r, min(tm, db * t_new))
    parts_m = _project(meta_tokens.astype(F32), w_r, n_meta)

    hp = jnp.zeros((2, LANE), F32).at[0, SM_A:SM_A + GDN_HEADS].set(a_log[0]).at[1, SM_A:SM_A + GDN_HEADS].set(dt_bias[0])
    cw = conv_w[0]
    ng = gdn_norm_g[0].reshape(1, GDN_D)
    zero_prev = jnp.zeros((1, SUBLANE, qkv_w), F32)
    zero_state = jnp.zeros((1, GDN_HEADS, GDN_D, GDN_D), F32)
    _, s_meta = _gdn(parts_m, zero_prev, zero_state, cw, hp, ng, batch=1, n_chunks=1, c_in=n_meta, n_valid=n_meta,
                     n_seq=1, shared_init=True)
    meta_prev = parts_m[n_meta - SUBLANE:, :qkv_w].reshape(1, SUBLANE, qkv_w)
    og_p, p_delta = _gdn(parts_p, meta_prev, s_meta, cw, hp, ng, batch=batch, n_chunks=seq // GDN_CHUNK,
                         c_in=GDN_CHUNK, n_valid=GDN_CHUNK, n_seq=1, shared_init=True)
    per = GDN_CHUNK // t_new
    assert per * t_new == GDN_CHUNK and t_new % SUBLANE == 0 and db % per == 0
    samp_prev = jnp.pad(state_conv.reshape(db, CONV_W - 1, qkv_w), ((0, 0), (SUBLANE - (CONV_W - 1), 0), (0, 0)))
    og_s, s_delta = _gdn(parts_s, samp_prev, state_delta.reshape(db, GDN_HEADS, GDN_D, GDN_D).astype(F32), cw, hp, ng,
                         batch=db // per, n_chunks=1, c_in=GDN_CHUNK, n_valid=GDN_CHUNK, n_seq=per, shared_init=False)

    tq = 128
    oa_p = _attn_prompt(parts_p, parts_m, batch=batch, seq=seq, n_meta=n_meta, tq=tq, kt=min(512, seq))
    oa_s = _attn_sample(parts_s, cache_k, cache_v, cache_idx_k, page_table, t_new=t_new)

    wbg = w_branch_gdn[0].astype(BF16)
    wba = w_branch_attn[0].astype(BF16)
    wo = w_out[0].astype(BF16)
    g1 = ln1_g[0].reshape(1, D)
    b1 = ln1_b[0].reshape(1, D)
    h_p = _finish(xp, og_p, oa_p, parts_p, wbg, wba, wo, g1, b1, alpha=alpha, tm=256)
    h_s = _finish(xs, og_s, oa_s, parts_s, wbg, wba, wo, g1, b1, alpha=alpha, tm=256)

    wq = peer_wq[0].astype(BF16)
    sk = peer_subkeys[0].astype(BF16)
    u = peer_u[0].astype(BF16)
    v = peer_v[0].astype(BF16)
    g2 = ln2_g[0].reshape(1, D)
    b2 = ln2_b[0].reshape(1, D)
    tt = 512 if (batch * seq) % 512 == 0 else 256
    eb = SUBLANE * PEER_NKEYS
    y_p = _peer(h_p, wq, sk, u, v, g2, b2, alpha=alpha, tt=tt, eb=eb)
    y_s = _peer(h_s, wq, sk, u, v, g2, b2, alpha=alpha, tt=min(tt, db * t_new), eb=eb)

    y_prompt = y_p.reshape(batch, seq, D)
    y_sample = y_s.reshape(db, t_new, D)
    pp = parts_p.reshape(batch, seq, N_COLS)
    ps = parts_s.reshape(db, t_new, N_COLS)
    p_conv = pp[:, seq - (CONV_W - 1):, :qkv_w][None]
    s_conv = jnp.concatenate([state_conv[0].astype(F32), ps[:, :, :qkv_w]], axis=1)[:, -(CONV_W - 1):][None]

    def with_meta(col, width):
        m = jnp.broadcast_to(parts_m[None, :, col:col + width], (batch, n_meta, width))
        return jnp.concatenate([m, pp[:, :, col:col + width]], axis=1)

    p_k = with_meta(COL_AK, kvw).reshape(1, batch, seq + n_meta, ATT_KV_HEADS, ATT_DH)
    p_v = with_meta(COL_AV, kvw).reshape(1, batch, seq + n_meta, ATT_KV_HEADS, ATT_DH)
    p_idx_k = with_meta(COL_SM + SM_IK, IDX_DIM)[None]
    s_k = ps[:, :, COL_AK:COL_AK + kvw].reshape(1, db, t_new, ATT_KV_HEADS, ATT_DH)
    s_v = ps[:, :, COL_AV:COL_AV + kvw].reshape(1, db, t_new, ATT_KV_HEADS, ATT_DH)
    s_idx_k = ps[:, :, COL_SM + SM_IK:COL_SM + SM_IK + IDX_DIM][None]
    return (y_prompt, y_sample, p_conv, p_delta[None], p_k, p_v, p_idx_k, s_conv, s_delta[None].astype(state_delta.dtype),
            s_k, s_v, s_idx_k)
```

```python
import functools

import jax
import jax.numpy as jnp
from jax import lax
from jax.experimental import pallas as pl
from jax.experimental.pallas import tpu as pltpu

F32 = jnp.float32
BF16 = jnp.bfloat16
I32 = jnp.int32

GDN_HEADS = 8
GDN_D = 128
CONV_W = 4
GDN_CHUNK = 64
ATT_HEADS = 8
ATT_KV_HEADS = 2
ATT_DH = 128
IDX_HEADS = 16
IDX_DIM = 64
TOPK_MAX = 256
PEER_HEADS = 8
PEER_NKEYS = 128
PEER_TOPK = 16
LN_EPS = 1e-5
RMS_EPS = 1e-6

LANE = 128
SUBLANE = 8
VMEM_LIMIT = 56 * 1024 * 1024

D = 1024
COL_QKV = 0
COL_Z = 3072
COL_AQ = 4096
COL_IQ = 5120
COL_GA = 6144
COL_GB = 7168
COL_AK = 8192
COL_AV = 8448
COL_SM = 8704
N_COLS = 8832
SM_IK = 0
SM_B = 64
SM_A = 72
SM_IW = 80

NEG_INF = float("-inf")
INT_MIN = -(2 ** 31)


def _nt(a, b):
    return lax.dot_general(a, b, (((1,), (1,)), ((), ())), preferred_element_type=F32)


def _tn(a, b):
    return lax.dot_general(a, b, (((0,), (0,)), ((), ())), preferred_element_type=F32)


def _mm(a, b):
    return jnp.dot(a, b, preferred_element_type=F32)


def _mm_f32(a, b):
    return jnp.dot(a, b, preferred_element_type=F32, precision=lax.Precision.HIGHEST)


def _cparams(sem):
    return pltpu.CompilerParams(dimension_semantics=sem, vmem_limit_bytes=VMEM_LIMIT)


def _proj_kernel(x_ref, w_ref, o_ref):
    o_ref[...] = _mm(x_ref[...].astype(BF16), w_ref[...])


def _project(x, w_bf, tm):
    n = x.shape[0]
    tn = N_COLS // 3
    return pl.pallas_call(
        _proj_kernel,
        grid=(3, n // tm),
        in_specs=[pl.BlockSpec((tm, D), lambda j, i: (i, 0)),
                  pl.BlockSpec((D, tn), lambda j, i: (0, j))],
        out_specs=pl.BlockSpec((tm, tn), lambda j, i: (i, j)),
        out_shape=jax.ShapeDtypeStruct((n, N_COLS), F32),
        compiler_params=_cparams(("arbitrary", "arbitrary")),
        name="in_projection",
    )(x, w_bf)


def _sibling_mask(ri, ci, lvl):
    return ((ri >> (lvl + 1)) == (ci >> (lvl + 1))) & (((ri >> lvl) & 1) == 1) & (((ci >> lvl) & 1) == 0)


_BATCH0 = ((0,), (0,))


def _bmm(a, b):
    return lax.dot_general(a, b, (((2,), (1,)), _BATCH0), preferred_element_type=F32)


def _bnt(a, b):
    return lax.dot_general(a, b, (((2,), (2,)), _BATCH0), preferred_element_type=F32)


def _btn(a, b):
    return lax.dot_general(a, b, (((1,), (1,)), _BATCH0), preferred_element_type=F32)


def _gdn_kernel(qkv_ref, z_ref, sm_ref, prev_ref, s0_ref, cw_ref, hp_ref, ng_ref,
                o_ref, sfin_ref, s_scr, ext_scr, *, c_in, n_valid, n_seq):
    nh = GDN_HEADS
    hd = GDN_D
    stride = ext_scr.shape[0] // n_seq
    L = stride - SUBLANE
    C = L * n_seq
    ls = L.bit_length() - 1
    c = pl.program_id(1)

    @pl.when(c == 0)
    def _():
        s_scr[...] = s0_ref[...]
        for b in range(n_seq):
            ext_scr[b * stride:b * stride + SUBLANE, :] = prev_ref[b]

    u = qkv_ref[...]
    sm = sm_ref[...]
    z = z_ref[...]
    if c_in < C:
        u = jnp.concatenate([u, jnp.zeros((C - c_in, u.shape[1]), F32)], axis=0)
        sm = jnp.concatenate([sm, jnp.zeros((C - c_in, LANE), F32)], axis=0)
        z = jnp.concatenate([z, jnp.zeros((C - c_in, z.shape[1]), F32)], axis=0)
    cw = cw_ref[...]
    pieces = []
    for b in range(n_seq):
        base = b * stride + SUBLANE
        ub_ = u[b * L:(b + 1) * L]
        ext_scr[base:base + L, :] = ub_
        acc = ub_ * cw[CONV_W - 1:CONV_W, :]
        for i in range(CONV_W - 1):
            off = base - (CONV_W - 1) + i
            acc = acc + ext_scr[off:off + L, :] * cw[i:i + 1, :]
        pieces.append(acc)
    conv = pieces[0] if n_seq == 1 else jnp.concatenate(pieces, axis=0)
    if n_seq == 1:
        tail = ext_scr[c_in:c_in + SUBLANE, :]
        ext_scr[0:SUBLANE, :] = tail
    qkv = conv * jax.nn.sigmoid(conv)

    row = lax.broadcasted_iota(I32, (C, 1), 0)
    hp = hp_ref[...]
    xs = sm + hp[1:2, :]
    softplus = jnp.maximum(xs, 0.0) + jnp.log1p(jnp.exp(-jnp.abs(xs)))
    la = -jnp.exp(hp[0:1, :]) * softplus
    beta = jax.nn.sigmoid(sm)
    if n_valid < C:
        valid = row < n_valid
        qkv = jnp.where(valid, qkv, 0.0)
        la = jnp.where(valid, la, 0.0)
        beta = jnp.where(valid, beta, 0.0)

    ri = lax.broadcasted_iota(I32, (C, C), 0)
    ci = lax.broadcasted_iota(I32, (C, C), 1)
    same = (ri >> ls) == (ci >> ls)
    incl = same & (ri >= ci)
    strict = same & (ri > ci)
    g = _mm_f32(incl.astype(F32), la)
    g_end = _mm_f32((ci == (ri | (L - 1))).astype(F32), g)
    gt = g.T
    eye = (ri == ci).astype(F32)

    q_l, k_l, kd_l, dec_l, lows_l, rhs_l, eg_l, z_l = [], [], [], [], [], [], [], []
    for h in range(nh):
        col = SM_A + h
        gcol = g[:, col:col + 1]
        grow = gt[col:col + 1, :]
        bcol = beta[:, SM_B + h:SM_B + h + 1]
        q = qkv[:, h * hd:(h + 1) * hd]
        k = qkv[:, nh * hd + h * hd:nh * hd + (h + 1) * hd]
        v = qkv[:, 2 * nh * hd + h * hd:2 * nh * hd + (h + 1) * hd]
        q = q * lax.rsqrt(jnp.sum(q * q, axis=-1, keepdims=True) + RMS_EPS) * (hd ** -0.5)
        k = k * lax.rsqrt(jnp.sum(k * k, axis=-1, keepdims=True) + RMS_EPS)
        dec = jnp.exp(jnp.where(incl, gcol - grow, NEG_INF))
        eg = jnp.exp(gcol)
        q_l.append(q.astype(BF16))
        k_l.append(k.astype(BF16))
        kd_l.append((k * jnp.exp(g_end[:, col:col + 1] - gcol)).astype(BF16))
        dec_l.append(dec)
        lows_l.append(bcol * jnp.where(strict, dec, 0.0))
        rhs_l.append(jnp.concatenate([bcol * v, (bcol * eg) * k], axis=1))
        eg_l.append(eg)
        z_l.append(z[:, h * hd:(h + 1) * hd])
    qb = jnp.stack(q_l)
    kb = jnp.stack(k_l)
    kd = jnp.stack(kd_l)
    dec = jnp.stack(dec_l)
    eg = jnp.stack(eg_l)

    low = jnp.stack(lows_l) * _bnt(kb, kb)
    qk = _bnt(qb, kb)
    inv = eye[None] - jnp.where(_sibling_mask(ri, ci, 0)[None], low, 0.0)
    for lvl in range(1, ls):
        off = jnp.where(_sibling_mask(ri, ci, lvl)[None], low, 0.0)
        inv_b = inv.astype(BF16)
        inv = inv - _bmm(inv_b, _bmm(off.astype(BF16), inv_b).astype(BF16))
    uw = _bmm(inv.astype(BF16), jnp.stack(rhs_l).astype(BF16))
    w_b = uw[:, :, hd:].astype(BF16)

    ws = None
    qs = None
    s_old = []
    for b in range(n_seq):
        S = s_scr[b]
        s_old.append(S)
        Sb = S.astype(BF16)
        if n_seq == 1:
            wm, qm = w_b, qb
        else:
            mine = ((row >> ls) == b)[None]
            wm = jnp.where(mine, w_b, jnp.zeros_like(w_b))
            qm = jnp.where(mine, qb, jnp.zeros_like(qb))
        ws = _bmm(wm, Sb) if ws is None else ws + _bmm(wm, Sb)
        qs = _bmm(qm, Sb) if qs is None else qs + _bmm(qm, Sb)
    ub = (uw[:, :, :hd] - ws).astype(BF16)
    o = eg * qs + _bmm((dec * qk).astype(BF16), ub)
    for b in range(n_seq):
        last = b * L + L - 1
        glast = jnp.stack([g[last:last + 1, SM_A + h:SM_A + h + 1] for h in range(nh)])
        kdm = kd if n_seq == 1 else jnp.where(((row >> ls) == b)[None], kd, jnp.zeros_like(kd))
        s_scr[b] = jnp.exp(glast) * s_old[b] + _btn(kdm, ub)

    on = o * lax.rsqrt(jnp.mean(o * o, axis=-1, keepdims=True) + RMS_EPS) * ng_ref[...]
    zz = jnp.stack(z_l)
    res = on * (zz * jax.nn.sigmoid(zz))
    for h in range(nh):
        o_ref[:, h * hd:(h + 1) * hd] = res[h, :c_in]

    @pl.when(c == pl.num_programs(1) - 1)
    def _():
        sfin_ref[...] = s_scr[...]


def _gdn(parts, prev, s0, cw, hp, ng, *, batch, n_chunks, c_in, n_valid, n_seq, shared_init):
    n = parts.shape[0]
    C = max(c_in, GDN_CHUNK)
    assert n_seq == 1 or (n_chunks == 1 and c_in == C)
    L = C // n_seq
    init = (lambda b, c: (0, 0, 0)) if shared_init else (lambda b, c: (b, 0, 0))
    init4 = (lambda b, c: (0, 0, 0, 0)) if shared_init else (lambda b, c: (b, 0, 0, 0))
    qkv_w = 3 * GDN_HEADS * GDN_D
    v_w = GDN_HEADS * GDN_D
    return pl.pallas_call(
        functools.partial(_gdn_kernel, c_in=c_in, n_valid=n_valid, n_seq=n_seq),
        grid=(batch, n_chunks),
        in_specs=[
            pl.BlockSpec((c_in, qkv_w), lambda b, c: (b * n_chunks + c, COL_QKV // qkv_w)),
            pl.BlockSpec((c_in, v_w), lambda b, c: (b * n_chunks + c, COL_Z // v_w)),
            pl.BlockSpec((c_in, LANE), lambda b, c: (b * n_chunks + c, COL_SM // LANE)),
            pl.BlockSpec((n_seq, SUBLANE, qkv_w), init),
            pl.BlockSpec((n_seq, GDN_HEADS, GDN_D, GDN_D), init4),
            pl.BlockSpec((CONV_W, qkv_w), lambda b, c: (0, 0)),
            pl.BlockSpec((2, LANE), lambda b, c: (0, 0)),
            pl.BlockSpec((1, GDN_D), lambda b, c: (0, 0)),
        ],
        out_specs=[
            pl.BlockSpec((c_in, v_w), lambda b, c: (b * n_chunks + c, 0)),
            pl.BlockSpec((n_seq, GDN_HEADS, GDN_D, GDN_D), lambda b, c: (b, 0, 0, 0)),
        ],
        out_shape=[jax.ShapeDtypeStruct((n, v_w), F32),
                   jax.ShapeDtypeStruct((batch * n_seq, GDN_HEADS, GDN_D, GDN_D), F32)],
        scratch_shapes=[pltpu.VMEM((n_seq, GDN_HEADS, GDN_D, GDN_D), F32),
                        pltpu.VMEM((n_seq * (SUBLANE + L), qkv_w), F32)],
        compiler_params=_cparams(("arbitrary", "arbitrary")),
        name="gated_deltanet",
    )(parts, parts, parts, prev, s0, cw, hp, ng)


def _sort_key(x):
    b = pltpu.bitcast(x + 0.0, I32)
    return b ^ ((b >> 31) & 0x7FFFFFFF)


KEY_NEG_INF = -(2 ** 31) + 0x7FFFFF
SEARCH_BITS_PER_TRIP = 4


def _count_ge(key_tiles, cand):
    tot = None
    for kt in key_tiles:
        c = jnp.sum((kt >= cand).astype(F32), axis=1, keepdims=True)
        tot = c if tot is None else tot + c
    return tot


def _rows8(x):
    return x.reshape(x.shape[0] // SUBLANE, SUBLANE, x.shape[1])


def _sum_rows8(x, chains=4):
    rows = x.shape[0]
    if rows % (chains * SUBLANE):
        return jnp.sum(_rows8(x), axis=0)
    part = rows // chains
    sums = [jnp.sum(_rows8(x[k * part:(k + 1) * part]), axis=0) for k in range(chains)]
    while len(sums) > 1:
        sums = [a + b for a, b in zip(sums[0::2], sums[1::2])]
    return sums[0]


def _attn_prompt_kernel(q_ref, iq_ref, smq_ref, k_ref, v_ref, smk_ref, mk_ref, mv_ref, msm_ref,
                        o_ref, kbf, vtb, kibf, key_scr, bias_scr, s_scr, cut_scr, *, n_sel, n_meta, kt):
    tq = q_ref.shape[0]
    n_real = k_ref.shape[0]
    kvw = k_ref.shape[1]
    qb_i = pl.program_id(1)
    nt = (qb_i * tq + tq + kt - 1) // kt
    group = ATT_HEADS // ATT_KV_HEADS

    @pl.when(qb_i == 0)
    def _():
        zrows = LANE - n_meta
        kbf[0:n_real, :] = k_ref[...].astype(BF16)
        kbf[n_real:, :] = jnp.concatenate([mk_ref[...], jnp.zeros((zrows, kvw), F32)], axis=0).astype(BF16)
        vtb[:, 0:n_real] = v_ref[...].T.astype(BF16)
        vtb[:, n_real:] = jnp.concatenate([mv_ref[...], jnp.zeros((zrows, kvw), F32)], axis=0).T.astype(BF16)
        kibf[0:n_real, :] = smk_ref[...][:, SM_IK:SM_IK + IDX_DIM].astype(BF16)
        kibf[n_real:, :] = jnp.concatenate(
            [msm_ref[...][:, SM_IK:SM_IK + IDX_DIM], jnp.zeros((zrows, IDX_DIM), F32)], axis=0).astype(BF16)

    iq = iq_ref[...]
    iq_stack = jnp.concatenate([iq[:, h * IDX_DIM:(h + 1) * IDX_DIM] for h in range(IDX_HEADS)], axis=0).astype(BF16)
    w_t = smq_ref[...].T * ((IDX_DIM ** -0.5) * (IDX_HEADS ** -0.5))
    qpos = qb_i * tq + lax.broadcasted_iota(I32, (1, tq), 1)

    def index_scores(start, width):
        s_all = _nt(kibf[pl.ds(start, width), :], iq_stack)
        acc = jnp.zeros((width, tq), F32)
        for h in range(IDX_HEADS):
            acc = acc + jnp.maximum(s_all[:, h * tq:(h + 1) * tq], 0.0) * w_t[SM_IW + h:SM_IW + h + 1, :]
        return acc

    def real_tile(t, carry):
        start = pl.multiple_of(t * kt, kt)
        kpos = start + lax.broadcasted_iota(I32, (kt, 1), 0)
        key_scr[pl.ds(start, kt), :] = _sort_key(jnp.where(kpos <= qpos, index_scores(start, kt), NEG_INF))
        return carry

    lax.fori_loop(0, nt, real_tile, 0)
    mrow = lax.broadcasted_iota(I32, (LANE, 1), 0)
    key_scr[n_real:, :] = _sort_key(jnp.where(mrow < n_meta, index_scores(n_real, LANE), NEG_INF))

    def count_ge(cand):
        def body(t, acc):
            kk = key_scr[pl.ds(pl.multiple_of(t * kt, kt), kt), :]
            return acc + _sum_rows8((kk >= cand).astype(F32))

        acc = lax.fori_loop(0, nt, body, jnp.zeros((SUBLANE, tq), F32))
        acc = acc + jnp.sum(_rows8((key_scr[n_real:, :] >= cand).astype(F32)), axis=0)
        return jnp.sum(acc, axis=0, keepdims=True)

    few = qpos + 1 + n_meta <= n_sel

    def unsettled(state):
        it, _, cnt = state
        return (it < 32) & (jnp.max(jnp.where(few | (cnt == n_sel), 0.0, 1.0)) > 0.0)

    def refine(state):
        it, thr, cnt = state
        for step in range(SEARCH_BITS_PER_TRIP):
            cand = thr + lax.shift_left(jnp.int32(1), 31 - step - it)
            c = count_ge(cand)
            ok = c >= n_sel
            thr = jnp.where(ok, cand, thr)
            cnt = jnp.where(ok, c, cnt)
        return it + SEARCH_BITS_PER_TRIP, thr, cnt

    n_all = (nt * kt + LANE).astype(F32)
    _, thr, cnt = lax.while_loop(
        unsettled, refine, (jnp.int32(0), jnp.full((1, tq), INT_MIN, I32), jnp.full((1, tq), 1.0, F32) * n_all))

    tie = (cnt > n_sel) & (thr > KEY_NEG_INF) & jnp.logical_not(few)
    n_pos = n_real + n_meta
    cut_scr[...] = jnp.full((SUBLANE, tq), float(n_pos), F32)

    def pos_of(start, width, is_meta):
        r = lax.broadcasted_iota(I32, (width, 1), 0)
        return r if is_meta else n_meta + start + r

    @pl.when(jnp.max(tie.astype(F32)) > 0.0)
    def _tie_cut():
        need = n_sel - count_ge(thr + 1)
        nbits = max(1, (n_pos - 1).bit_length())

        def count_eq_le(cut):
            def body(t, acc):
                start = pl.multiple_of(t * kt, kt)
                kk = key_scr[pl.ds(start, kt), :]
                return acc + jnp.sum(_rows8(((kk == thr) & (pos_of(start, kt, False) <= cut)).astype(F32)), axis=0)

            acc = lax.fori_loop(0, nt, body, jnp.zeros((SUBLANE, tq), F32))
            mk = key_scr[n_real:, :]
            acc = acc + jnp.sum(_rows8(((mk == thr) & (pos_of(0, LANE, True) <= cut)).astype(F32)), axis=0)
            return jnp.sum(acc, axis=0, keepdims=True)

        def body(it, cut):
            cand = cut - lax.shift_left(jnp.int32(1), nbits - 1 - it)
            ok = (cand >= 0) & (count_eq_le(cand) >= need)
            return jnp.where(ok, cand, cut)

        cut = lax.fori_loop(0, nbits, body, jnp.full((1, tq), (1 << nbits) - 1, I32))
        cut = jnp.where(tie, cut, n_pos).astype(F32)
        cut_scr[...] = jnp.broadcast_to(cut, (SUBLANE, tq))

    cut = cut_scr[0:1, :].astype(I32)

    def bias_tile(start, width, is_meta):
        kk = key_scr[pl.ds(start, width), :]
        sel = (kk > thr) | ((kk == thr) & (pos_of(start, width, is_meta) <= cut))
        sel = sel & (kk > KEY_NEG_INF)
        bias_scr[pl.ds(start, width), :] = jnp.where(sel, 0.0, NEG_INF)

    def bias_body(t, carry):
        bias_tile(pl.multiple_of(t * kt, kt), kt, False)
        return carry

    lax.fori_loop(0, nt, bias_body, 0)
    bias_tile(n_real, LANE, True)

    scale = ATT_DH ** -0.5
    for g in range(ATT_KV_HEADS):
        q4 = jnp.concatenate([q_ref[:, (g * group + j) * ATT_DH:(g * group + j + 1) * ATT_DH] for j in range(group)],
                             axis=0).astype(BF16)

        def score_tile(start, width):
            s = _nt(kbf[pl.ds(start, width), g * ATT_DH:(g + 1) * ATT_DH], q4) * scale
            b = bias_scr[pl.ds(start, width), :]
            s = s + jnp.concatenate([b] * group, axis=1)
            s_scr[pl.ds(start, width), :] = s
            return jnp.max(_rows8(s), axis=0)

        def pass1(t, m):
            return jnp.maximum(m, score_tile(pl.multiple_of(t * kt, kt), kt))

        m = jnp.max(lax.fori_loop(0, nt, pass1, score_tile(n_real, LANE)), axis=0, keepdims=True)

        def prob_tile(start, width):
            p = jnp.exp(s_scr[pl.ds(start, width), :] - m)
            pv = _mm(vtb[g * ATT_DH:(g + 1) * ATT_DH, pl.ds(start, width)], p.astype(BF16))
            return _sum_rows8(p), pv

        def pass2(t, carry):
            l, acc = carry
            dl, dacc = prob_tile(pl.multiple_of(t * kt, kt), kt)
            return l + dl, acc + dacc

        l, acc = lax.fori_loop(0, nt, pass2, prob_tile(n_real, LANE))
        out_t = acc / jnp.sum(l, axis=0, keepdims=True)
        for j in range(group):
            hh = g * group + j
            o_ref[:, hh * ATT_DH:(hh + 1) * ATT_DH] = out_t[:, j * tq:(j + 1) * tq].T


def _attn_prompt(parts, mparts, *, batch, seq, n_meta, tq, kt):
    nqb = seq // tq
    n_sel = min(TOPK_MAX, (seq + n_meta) // 4)
    kvw = ATT_KV_HEADS * ATT_DH
    nk = seq + LANE
    return pl.pallas_call(
        functools.partial(_attn_prompt_kernel, n_sel=n_sel, n_meta=n_meta, kt=kt),
        grid=(batch, nqb),
        in_specs=[
            pl.BlockSpec((tq, D), lambda b, i: (b * nqb + i, COL_AQ // D)),
            pl.BlockSpec((tq, D), lambda b, i: (b * nqb + i, COL_IQ // D)),
            pl.BlockSpec((tq, LANE), lambda b, i: (b * nqb + i, COL_SM // LANE)),
            pl.BlockSpec((seq, kvw), lambda b, i: (b, COL_AK // kvw)),
            pl.BlockSpec((seq, kvw), lambda b, i: (b, COL_AV // kvw)),
            pl.BlockSpec((seq, LANE), lambda b, i: (b, COL_SM // LANE)),
            pl.BlockSpec((n_meta, kvw), lambda b, i: (0, COL_AK // kvw)),
            pl.BlockSpec((n_meta, kvw), lambda b, i: (0, COL_AV // kvw)),
            pl.BlockSpec((n_meta, LANE), lambda b, i: (0, COL_SM // LANE)),
        ],
        out_specs=pl.BlockSpec((tq, D), lambda b, i: (b * nqb + i, 0)),
        out_shape=jax.ShapeDtypeStruct((batch * seq, D), F32),
        scratch_shapes=[pltpu.VMEM((nk, kvw), BF16), pltpu.VMEM((kvw, nk), BF16), pltpu.VMEM((nk, IDX_DIM), BF16),
                        pltpu.VMEM((nk, tq), I32), pltpu.VMEM((nk, tq), F32),
                        pltpu.VMEM((nk, (ATT_HEADS // ATT_KV_HEADS) * tq), F32), pltpu.VMEM((SUBLANE, tq), F32)],
        compiler_params=_cparams(("arbitrary", "arbitrary")),
        name="sparse_attention_prompt",
    )(parts, parts, parts, parts, parts, parts, mparts, mparts, mparts)


def _attn_sample_kernel(pt_ref, *refs, n_pages, n_sel):
    kp = refs[0:n_pages]
    vp = refs[n_pages:2 * n_pages]
    ip = refs[2 * n_pages:3 * n_pages]
    q_ref, iq_ref, kn_ref, vn_ref, sm_ref, o_ref, kil, vil, kit, cut_scr = refs[3 * n_pages:]
    del pt_ref
    t_new = q_ref.shape[0]
    page = ip[0].shape[1]
    past = n_pages * page
    nkv = ATT_KV_HEADS
    for j in range(n_pages):
        kil[j * nkv * page:(j + 1) * nkv * page, :] = kp[j][...].astype(BF16)
        vil[j * nkv * page:(j + 1) * nkv * page, :] = vp[j][...].astype(BF16)
        kit[:, j * page:(j + 1) * page] = ip[j][...].astype(BF16)
    sm = sm_ref[...]
    zpad = jnp.zeros((LANE - t_new, nkv * ATT_DH), BF16)
    knew = jnp.concatenate([kn_ref[...].astype(BF16), zpad], axis=0)
    vnew = jnp.concatenate([vn_ref[...].astype(BF16), zpad], axis=0)
    ki_new = jnp.concatenate(
        [sm[:, SM_IK:SM_IK + IDX_DIM].astype(BF16), jnp.zeros((LANE - t_new, IDX_DIM), BF16)], axis=0)

    iq = iq_ref[...]
    qi = jnp.concatenate([iq[:, h * IDX_DIM:(h + 1) * IDX_DIM] for h in range(IDX_HEADS)], axis=0).astype(BF16)
    wcol = jnp.concatenate([sm[:, SM_IW + h:SM_IW + h + 1] for h in range(IDX_HEADS)], axis=0)
    wcol = wcol * ((IDX_DIM ** -0.5) * (IDX_HEADS ** -0.5))

    def idx_scores(qk):
        s = jnp.maximum(qk, 0.0) * wcol
        acc = s[0:t_new]
        for h in range(1, IDX_HEADS):
            acc = acc + s[h * t_new:(h + 1) * t_new]
        return acc

    i_past = idx_scores(_mm(qi, kit[...]))
    i_new = idx_scores(_nt(qi, ki_new))
    trow = lax.broadcasted_iota(I32, (t_new, LANE), 0)
    tlane = lax.broadcasted_iota(I32, (t_new, LANE), 1)
    key_past = _sort_key(i_past)
    key_new = _sort_key(jnp.where(tlane <= trow, i_new, NEG_INF))
    tiles = [key_past, key_new]

    def digit_body(it, thr):
        step = lax.shift_left(jnp.int32(1), 28 - 4 * it)
        digit = jnp.zeros((t_new, 1), I32)
        for j in range(1, 16):
            digit = digit + (_count_ge(tiles, thr + j * step) >= n_sel).astype(I32)
        return thr + digit * step

    thr = lax.fori_loop(0, 8, digit_body, jnp.full((t_new, 1), INT_MIN, I32))
    n_ge = _count_ge(tiles, thr)
    tie = (n_ge > n_sel) & (thr > KEY_NEG_INF)
    pos_past = lax.broadcasted_iota(I32, (1, past), 1)
    pos_new = past + lax.broadcasted_iota(I32, (1, LANE), 1)
    n_pos = past + t_new
    cut_scr[...] = jnp.full(cut_scr.shape, n_pos, I32)

    @pl.when(jnp.max(tie.astype(F32)) > 0.0)
    def _tie_cut():
        need = n_sel - _count_ge(tiles, thr + 1)
        nbits = max(1, (n_pos - 1).bit_length())

        def cut_body(it, cut):
            cand = cut - lax.shift_left(jnp.int32(1), nbits - 1 - it)
            cnt = (jnp.sum(((key_past == thr) & (pos_past <= cand)).astype(F32), axis=1, keepdims=True)
                   + jnp.sum(((key_new == thr) & (pos_new <= cand)).astype(F32), axis=1, keepdims=True))
            ok = (cand >= 0) & (cnt >= need)
            return jnp.where(ok, cand, cut)

        cut = lax.fori_loop(0, nbits, cut_body, jnp.full((t_new, 1), (1 << nbits) - 1, I32))
        cut_scr[...] = jnp.broadcast_to(jnp.where(tie, cut, n_pos), cut_scr.shape)

    cut = cut_scr[:, 0:1]

    def chosen(kk, pos):
        return ((kk > thr) | ((kk == thr) & (pos <= cut))) & (kk > KEY_NEG_INF)

    group = ATT_HEADS // ATT_KV_HEADS
    b_new = jnp.concatenate([jnp.where(chosen(key_new, pos_new), 0.0, NEG_INF)] * group, axis=0)
    sel01 = chosen(key_past, pos_past).astype(BF16)
    er = lax.broadcasted_iota(I32, (page, nkv * page), 0)
    ec = lax.broadcasted_iota(I32, (page, nkv * page), 1)
    expand = ((ec >= er * nkv) & (ec < (er + 1) * nkv)).astype(BF16)
    sel_il = jnp.concatenate([_mm(sel01[:, j * page:(j + 1) * page], expand) for j in range(n_pages)], axis=1)
    il_lane = lax.broadcasted_iota(I32, (1, nkv * past), 1)
    assert nkv & (nkv - 1) == 0
    il_head = il_lane & (nkv - 1)
    scale = ATT_DH ** -0.5
    q = q_ref[...]
    for g in range(ATT_KV_HEADS):
        qg = jnp.concatenate(
            [q[:, (g * group + j) * ATT_DH:(g * group + j + 1) * ATT_DH] for j in range(group)], axis=0).astype(BF16)
        b_g = jnp.where((sel_il > 0.5) & (il_head == g), 0.0, NEG_INF)
        s_p = _nt(qg, kil[...]) * scale + jnp.concatenate([b_g] * group, axis=0)
        s_n = _nt(qg, knew[:, g * ATT_DH:(g + 1) * ATT_DH]) * scale + b_new
        m = jnp.maximum(jnp.max(s_p, axis=1, keepdims=True), jnp.max(s_n, axis=1, keepdims=True))
        p_p = jnp.exp(s_p - m)
        p_n = jnp.exp(s_n - m)
        l = jnp.sum(p_p, axis=1, keepdims=True) + jnp.sum(p_n, axis=1, keepdims=True)
        acc = _mm(p_p.astype(BF16), vil[...]) + _mm(p_n.astype(BF16), vnew[:, g * ATT_DH:(g + 1) * ATT_DH])
        res = acc / l
        for j in range(group):
            hh = g * group + j
            o_ref[:, hh * ATT_DH:(hh + 1) * ATT_DH] = res[j * t_new:(j + 1) * t_new]


def _attn_sample(parts, cache_k, cache_v, cache_ik, page_table, *, t_new):
    db, n_pages = page_table.shape
    page = cache_k.shape[2]
    kvw = ATT_KV_HEADS * ATT_DH
    past = n_pages * page
    n_sel = min(TOPK_MAX, (past + t_new) // 4)

    n_pool = cache_k.shape[1]
    cache_k = cache_k.reshape(n_pool, page * ATT_KV_HEADS, ATT_DH)
    cache_v = cache_v.reshape(n_pool, page * ATT_KV_HEADS, ATT_DH)
    cache_ik = jnp.swapaxes(cache_ik, 2, 3).reshape(n_pool, IDX_DIM, page)

    def kv_page(j):
        return pl.BlockSpec((None, page * ATT_KV_HEADS, ATT_DH), lambda b, pt, j=j: (pt[b, j], 0, 0))

    def ik_page(j):
        return pl.BlockSpec((None, IDX_DIM, page), lambda b, pt, j=j: (pt[b, j], 0, 0))

    in_specs = ([kv_page(j) for j in range(n_pages)] + [kv_page(j) for j in range(n_pages)]
                + [ik_page(j) for j in range(n_pages)]
                + [pl.BlockSpec((t_new, D), lambda b, pt: (b, COL_AQ // D)),
                   pl.BlockSpec((t_new, D), lambda b, pt: (b, COL_IQ // D)),
                   pl.BlockSpec((t_new, kvw), lambda b, pt: (b, COL_AK // kvw)),
                   pl.BlockSpec((t_new, kvw), lambda b, pt: (b, COL_AV // kvw)),
                   pl.BlockSpec((t_new, LANE), lambda b, pt: (b, COL_SM // LANE))])
    grid_spec = pltpu.PrefetchScalarGridSpec(
        num_scalar_prefetch=1, grid=(db,), in_specs=in_specs,
        out_specs=pl.BlockSpec((t_new, D), lambda b, pt: (b, 0)),
        scratch_shapes=[pltpu.VMEM((past * ATT_KV_HEADS, ATT_DH), BF16), pltpu.VMEM((past * ATT_KV_HEADS, ATT_DH), BF16),
                        pltpu.VMEM((IDX_DIM, past), BF16), pltpu.VMEM((t_new, LANE), I32)])
    return pl.pallas_call(
        functools.partial(_attn_sample_kernel, n_pages=n_pages, n_sel=n_sel),
        grid_spec=grid_spec,
        out_shape=jax.ShapeDtypeStruct((db * t_new, D), F32),
        compiler_params=_cparams(("arbitrary",)),
        name="sparse_attention_sample",
    )(page_table, *([cache_k] * n_pages), *([cache_v] * n_pages), *([cache_ik] * n_pages),
      parts, parts, parts, parts, parts)


def _layer_norm(x, g, b):
    mu = jnp.mean(x, axis=-1, keepdims=True)
    xc = x - mu
    var = jnp.mean(xc * xc, axis=-1, keepdims=True)
    return xc * lax.rsqrt(var + LN_EPS) * g + b


def _finish_kernel(x_ref, og_ref, oa_ref, ga_ref, gb_ref, wbg_ref, wba_ref, wo_ref, g_ref, b_ref, h_ref, *, alpha):
    a = _mm(og_ref[...].astype(BF16), wbg_ref[...])
    b = _mm(oa_ref[...].astype(BF16), wba_ref[...])
    merged = jax.nn.sigmoid(ga_ref[...]) * a + jax.nn.sigmoid(gb_ref[...]) * b
    y = alpha * x_ref[...] + _mm(merged.astype(BF16), wo_ref[...])
    h_ref[...] = _layer_norm(y, g_ref[...], b_ref[...])


def _finish(x, o_gdn, o_att, parts, wbg, wba, wo, g, b, *, alpha, tm):
    n = x.shape[0]
    tm = min(tm, n)
    row = lambda i: (i, 0)
    full = lambda i: (0, 0)
    return pl.pallas_call(
        functools.partial(_finish_kernel, alpha=alpha),
        grid=(n // tm,),
        in_specs=[pl.BlockSpec((tm, D), row), pl.BlockSpec((tm, D), row), pl.BlockSpec((tm, D), row),
                  pl.BlockSpec((tm, D), lambda i: (i, COL_GA // D)), pl.BlockSpec((tm, D), lambda i: (i, COL_GB // D)),
                  pl.BlockSpec((D, D), full), pl.BlockSpec((D, D), full), pl.BlockSpec((D, D), full),
                  pl.BlockSpec((1, D), full), pl.BlockSpec((1, D), full)],
        out_specs=pl.BlockSpec((tm, D), row),
        out_shape=jax.ShapeDtypeStruct((n, D), F32),
        compiler_params=_cparams(("arbitrary",)),
        name="merge_layernorm",
    )(x, o_gdn, o_att, parts, parts, wbg, wba, wo, g, b)


_CAND = [(r0, r1) for r0 in range(PEER_TOPK) for r1 in range(PEER_TOPK) if (r0 + 1) * (r1 + 1) <= PEER_TOPK]
_CAND_OFF = [next(i for i, c in enumerate(_CAND) if c[0] == r0) for r0 in range(PEER_TOPK)]
_CAND_LEN = [sum(1 for c in _CAND if c[0] == r0) for r0 in range(PEER_TOPK)]
_CAND_ROWS = -(-len(_CAND) // SUBLANE) * SUBLANE


def _top_rows(s, n_top, break_ties):
    rows, cols = s.shape
    iota = lax.broadcasted_iota(I32, (rows, cols), 0).astype(F32)
    rank = jnp.full((rows, cols), float(n_top), F32)
    vals = []
    for r in range(n_top):
        m = jnp.max(s, axis=0, keepdims=True)
        hit = s == m
        if break_ties:
            hit = iota == jnp.min(jnp.where(hit, iota, float(rows)), axis=0, keepdims=True)
        vals.append(m)
        s = jnp.where(hit, NEG_INF, s)
        rank = jnp.where(hit, float(r), rank)
    n_ranked = jnp.sum((rank < float(n_top)).astype(F32), axis=0, keepdims=True)
    return jnp.concatenate(vals, axis=0), rank, n_ranked == float(n_top)


def _peer_kernel(h_ref, wq_ref, sk_ref, u0_ref, un_ref, vp_ref, vl_ref, g_ref, b_ref, y_ref,
                 hb_scr, q_scr, rank1_scr, bt_scr, nt_scr, at_scr, s1_scr, s_scr, p_scr, acc_scr, *, alpha, n_steps):
    tt = h_ref.shape[0]
    eb = un_ref.shape[0] // 2
    e = pl.program_id(1)
    nk = PEER_NKEYS
    kk = PEER_TOPK
    tchunks = tt // LANE

    @pl.when(e == 0)
    def _prep():
        hb = h_ref[...].astype(BF16)
        hb_scr[...] = hb
        q_scr[...] = _mm(hb, wq_ref[...]).astype(BF16)
        acc_scr[...] = jnp.zeros(acc_scr.shape, F32)

        def scores(hd, carry):
            q0 = q_scr[:, pl.ds(pl.multiple_of(hd * 2 * nk, nk), nk)]
            q1 = q_scr[:, pl.ds(pl.multiple_of(hd * 2 * nk + nk, nk), nk)]
            at_scr[hd] = _nt(sk_ref[hd, 0], q0)
            s1_scr[hd] = _nt(sk_ref[hd, 1], q1)
            return carry

        lax.fori_loop(0, PEER_HEADS, scores, 0)

        def select_chunk(hd, c0, s0, s1, break_ties):
            a, rank0, ok0 = _top_rows(s0, kk, break_ties)
            b, rank1, ok1 = _top_rows(s1, kk, break_ties)
            pad = [jnp.full((_CAND_ROWS - len(_CAND), LANE), NEG_INF, F32)]
            cand = jnp.concatenate([a[r0:r0 + 1] + b[r1:r1 + 1] for r0, r1 in _CAND] + pad, axis=0)
            top, crank, okc = _top_rows(cand, kk, break_ties)
            zsum = jnp.sum(jnp.exp(top - top[0:1]), axis=0, keepdims=True)
            chosen = (crank < float(kk)).astype(F32)
            crow = lax.broadcasted_iota(I32, (_CAND_ROWS, 1), 0)
            nsel = jnp.zeros((nk, LANE), F32)
            for r0 in range(kk):
                in_group = (crow >= _CAND_OFF[r0]) & (crow < _CAND_OFF[r0] + _CAND_LEN[r0])
                cnt = jnp.sum(jnp.where(in_group, chosen, 0.0), axis=0, keepdims=True)
                nsel = nsel + jnp.where(rank0 == float(r0), cnt, 0.0)
            rank1_scr[hd, :, pl.ds(c0, LANE)] = rank1.astype(BF16)
            nt_scr[hd, :, pl.ds(c0, LANE)] = nsel
            at_scr[hd, :, pl.ds(c0, LANE)] = jnp.exp(s0 - a[0:1])
            bt_scr[hd, :, pl.ds(c0, LANE)] = (jnp.exp(s1 - b[0:1]) / zsum).astype(BF16)
            return jnp.min((ok0 & ok1 & okc).astype(F32)) > 0.5

        pair = 2 if tchunks % 2 == 0 else 1

        def select(idx, carry):
            hd = idx // (tchunks // pair)
            first = (idx % (tchunks // pair)) * pair
            tie_free = None
            chunks = []
            for k in range(pair):
                c0 = pl.multiple_of((first + k) * LANE, LANE)
                s0 = at_scr[hd, :, pl.ds(c0, LANE)]
                s1 = s1_scr[hd, :, pl.ds(c0, LANE)]
                chunks.append((c0, s0, s1))
            for c0, s0, s1 in chunks:
                ok = select_chunk(hd, c0, s0, s1, False)
                tie_free = ok if tie_free is None else tie_free & ok

            @pl.when(jnp.logical_not(tie_free))
            def _():
                for c0, s0, s1 in chunks:
                    select_chunk(hd, c0, s0, s1, True)

            return carry

        lax.fori_loop(0, PEER_HEADS * tchunks // pair, select, 0)

        s_scr[1] = _nt(u0_ref[...], hb_scr[...])
        p_scr[1] = jnp.zeros(p_scr.shape[1:], BF16)

    groups = eb // nk

    assert 2 * groups == SUBLANE

    def weights(half, s_ref, p_ref):
        i0 = pl.multiple_of(e * SUBLANE, SUBLANE)
        zero = jnp.zeros((nk, LANE), BF16)

        for tc in range(tchunks):
            cols = slice(tc * LANE, (tc + 1) * LANE)
            gates = [zero] * groups
            for hd in range(PEER_HEADS):
                nblk = nt_scr[hd, pl.ds(i0, SUBLANE), cols]
                ablk = at_scr[hd, pl.ds(i0, SUBLANE), cols]
                rank1 = rank1_scr[hd, :, cols]
                bval = bt_scr[hd, :, cols]
                for ii in range(groups):
                    r = half * groups + ii
                    nrow = nblk[r:r + 1].astype(BF16)
                    arow = ablk[r:r + 1].astype(BF16)
                    gates[ii] = gates[ii] + jnp.where(rank1 < nrow, bval, zero) * arow
            for ii in range(groups):
                rows = slice(ii * nk, (ii + 1) * nk)
                s = s_ref[rows, cols]
                act = 0.5 * s * (1.0 + lax.erf(s * (2.0 ** -0.5)))
                p_ref[rows, cols] = gates[ii] * act.astype(BF16)

    def step(cur, nxt):
        s_scr[nxt] = _nt(un_ref[...], hb_scr[...])
        for half in range(2):
            rows = pl.ds(half * eb, eb)
            weights(half, s_scr.at[cur, rows], p_scr.at[nxt, rows])
        acc_scr[...] += _tn(p_scr[cur], vp_ref[...])

    @pl.when(e % 2 == 0)
    def _():
        step(1, 0)

    @pl.when(e % 2 == 1)
    def _():
        step(0, 1)

    last_slot = (n_steps - 1) % 2

    @pl.when(e == n_steps - 1)
    def _():
        y = alpha * h_ref[...] + (acc_scr[...] + _tn(p_scr[last_slot], vl_ref[...]))
        y_ref[...] = _layer_norm(y, g_ref[...], b_ref[...])


def _peer(h, wq, sk, u, v, g, b, *, alpha, tt, eb):
    n = h.shape[0]
    n_exp = u.shape[0]
    nk = PEER_NKEYS
    nb = n_exp // eb
    return pl.pallas_call(
        functools.partial(_peer_kernel, alpha=alpha, n_steps=nb),
        grid=(n // tt, nb),
        in_specs=[pl.BlockSpec((tt, D), lambda t, e: (t, 0)),
                  pl.BlockSpec(wq.shape, lambda t, e: (0, 0)),
                  pl.BlockSpec(sk.shape, lambda t, e: (0, 0, 0, 0)),
                  pl.BlockSpec((eb, D), lambda t, e: (0, 0)),
                  pl.BlockSpec((eb, D), lambda t, e: (jnp.minimum(e + 1, nb - 1), 0)),
                  pl.BlockSpec((eb, D), lambda t, e: (jnp.maximum(e - 1, 0), 0)),
                  pl.BlockSpec((eb, D), lambda t, e: (nb - 1, 0)),
                  pl.BlockSpec((1, D), lambda t, e: (0, 0)),
                  pl.BlockSpec((1, D), lambda t, e: (0, 0))],
        out_specs=pl.BlockSpec((tt, D), lambda t, e: (t, 0)),
        out_shape=jax.ShapeDtypeStruct((n, D), F32),
        scratch_shapes=[pltpu.VMEM((tt, D), BF16), pltpu.VMEM((tt, wq.shape[1]), BF16),
                        pltpu.VMEM((PEER_HEADS, nk, tt), BF16), pltpu.VMEM((PEER_HEADS, nk, tt), BF16),
                        pltpu.VMEM((PEER_HEADS, nk, tt), F32), pltpu.VMEM((PEER_HEADS, nk, tt), F32),
                        pltpu.VMEM((PEER_HEADS, nk, tt), F32),
                        pltpu.VMEM((2, eb, tt), F32), pltpu.VMEM((2, eb, tt), BF16), pltpu.VMEM((tt, D), F32)],
        compiler_params=_cparams(("arbitrary", "arbitrary")),
        name="peer_ffn",
    )(h, wq, sk, u, u, v, v, g, b)


def _permute_w_in(w):
    gq = 3 * GDN_HEADS * GDN_D
    gv = GDN_HEADS * GDN_D
    aq = ATT_HEADS * ATT_DH
    akv = ATT_KV_HEADS * ATT_DH
    iqw = IDX_HEADS * IDX_DIM
    sizes = (gq, gv, GDN_HEADS, GDN_HEADS, aq, akv, akv, iqw, IDX_DIM, IDX_HEADS, D, D)
    offs = [0]
    for s in sizes:
        offs.append(offs[-1] + s)
    seg = [w[:, offs[i]:offs[i + 1]] for i in range(len(sizes))]
    qkv, z, b, a, q, k, v, iq, ik, iw, ga, gb = seg
    pad = jnp.zeros((w.shape[0], LANE - IDX_DIM - 2 * GDN_HEADS - IDX_HEADS), w.dtype)
    return jnp.concatenate([qkv, z, q, iq, ga, gb, k, v, ik, b, a, iw, pad], axis=1).astype(BF16)


def kernel(x_prompt, x_sample, cache_k, cache_v, cache_idx_k, state_conv, state_delta, page_table, meta_tokens, w_in, conv_w, a_log, dt_bias, gdn_norm_g, w_branch_gdn, w_branch_attn, w_out, ln1_g, ln1_b, peer_wq, peer_subkeys, peer_u, peer_v, ln2_g, ln2_b):
    depth = w_in.shape[0]
    assert depth == 1, "single-layer step"
    batch, seq, d = x_prompt.shape
    db, t_new, _ = x_sample.shape
    n_meta = meta_tokens.shape[0]
    assert d == D and seq % GDN_CHUNK == 0 and n_meta % SUBLANE == 0 and n_meta <= GDN_CHUNK
    alpha = (2 * depth) ** 0.25
    qkv_w = 3 * GDN_HEADS * GDN_D
    kvw = ATT_KV_HEADS * ATT_DH

    w_r = _permute_w_in(w_in[0])
    xp = x_prompt.reshape(batch * seq, D)
    xs = x_sample.reshape(db * t_new, D)
    tm = 512 if (batch * seq) % 512 == 0 else 256
    parts_p = _project(xp, w_r, tm)
    parts_s = _project(xs, w_r, min(tm, db * t_new))
    parts_m = _project(meta_tokens.astype(F32), w_r, n_meta)

    hp = jnp.zeros((2, LANE), F32).at[0, SM_A:SM_A + GDN_HEADS].set(a_log[0]).at[1, SM_A:SM_A + GDN_HEADS].set(dt_bias[0])
    cw = conv_w[0]
    ng = gdn_norm_g[0].reshape(1, GDN_D)
    zero_prev = jnp.zeros((1, SUBLANE, qkv_w), F32)
    zero_state = jnp.zeros((1, GDN_HEADS, GDN_D, GDN_D), F32)
    _, s_meta = _gdn(parts_m, zero_prev, zero_state, cw, hp, ng, batch=1, n_chunks=1, c_in=n_meta, n_valid=n_meta,
                     n_seq=1, shared_init=True)
    meta_prev = parts_m[n_meta - SUBLANE:, :qkv_w].reshape(1, SUBLANE, qkv_w)
    og_p, p_delta = _gdn(parts_p, meta_prev, s_meta, cw, hp, ng, batch=batch, n_chunks=seq // GDN_CHUNK,
                         c_in=GDN_CHUNK, n_valid=GDN_CHUNK, n_seq=1, shared_init=True)
    per = GDN_CHUNK // t_new
    assert per * t_new == GDN_CHUNK and t_new % SUBLANE == 0 and db % per == 0
    samp_prev = jnp.pad(state_conv.reshape(db, CONV_W - 1, qkv_w), ((0, 0), (SUBLANE - (CONV_W - 1), 0), (0, 0)))
    og_s, s_delta = _gdn(parts_s, samp_prev, state_delta.reshape(db, GDN_HEADS, GDN_D, GDN_D).astype(F32), cw, hp, ng,
                         batch=db // per, n_chunks=1, c_in=GDN_CHUNK, n_valid=GDN_CHUNK, n_seq=per, shared_init=False)

    tq = 256 if seq % 256 == 0 else 128
    oa_p = _attn_prompt(parts_p, parts_m, batch=batch, seq=seq, n_meta=n_meta, tq=tq, kt=min(512, seq))
    oa_s = _attn_sample(parts_s, cache_k, cache_v, cache_idx_k, page_table, t_new=t_new)

    wbg = w_branch_gdn[0].astype(BF16)
    wba = w_branch_attn[0].astype(BF16)
    wo = w_out[0].astype(BF16)
    g1 = ln1_g[0].reshape(1, D)
    b1 = ln1_b[0].reshape(1, D)
    h_p = _finish(xp, og_p, oa_p, parts_p, wbg, wba, wo, g1, b1, alpha=alpha, tm=512)
    h_s = _finish(xs, og_s, oa_s, parts_s, wbg, wba, wo, g1, b1, alpha=alpha, tm=512)

    wq = peer_wq[0].astype(BF16)
    sk = peer_subkeys[0].astype(BF16)
    u = peer_u[0].astype(BF16)
    v = peer_v[0].astype(BF16)
    g2 = ln2_g[0].reshape(1, D)
    b2 = ln2_b[0].reshape(1, D)
    tt = 512 if (batch * seq) % 512 == 0 else 256
    eb = SUBLANE * PEER_NKEYS
    y_p = _peer(h_p, wq, sk, u, v, g2, b2, alpha=alpha, tt=tt, eb=eb)
    y_s = _peer(h_s, wq, sk, u, v, g2, b2, alpha=alpha, tt=min(tt, db * t_new), eb=eb)

    y_prompt = y_p.reshape(batch, seq, D)
    y_sample = y_s.reshape(db, t_new, D)
    pp = parts_p.reshape(batch, seq, N_COLS)
    ps = parts_s.reshape(db, t_new, N_COLS)
    p_conv = pp[:, seq - (CONV_W - 1):, :qkv_w][None]
    s_conv = jnp.concatenate([state_conv[0].astype(F32), ps[:, :, :qkv_w]], axis=1)[:, -(CONV_W - 1):][None]

    def with_meta(col, width):
        m = jnp.broadcast_to(parts_m[None, :, col:col + width], (batch, n_meta, width))
        return jnp.concatenate([m, pp[:, :, col:col + width]], axis=1)

    p_k = with_meta(COL_AK, kvw).reshape(1, batch, seq + n_meta, ATT_KV_HEADS, ATT_DH)
    p_v = with_meta(COL_AV, kvw).reshape(1, batch, seq + n_meta, ATT_KV_HEADS, ATT_DH)
    p_idx_k = with_meta(COL_SM + SM_IK, IDX_DIM)[None]
    s_k = ps[:, :, COL_AK:COL_AK + kvw].reshape(1, db, t_new, ATT_KV_HEADS, ATT_DH)
    s_v = ps[:, :, COL_AV:COL_AV + kvw].reshape(1, db, t_new, ATT_KV_HEADS, ATT_DH)
    s_idx_k = ps[:, :, COL_SM + SM_IK:COL_SM + SM_IK + IDX_DIM][None]
    return (y_prompt, y_sample, p_conv, p_delta[None], p_k, p_v, p_idx_k, s_conv, s_delta[None].astype(state_delta.dtype),
            s_k, s_v, s_idx_k)
```

```python
import functools
from typing import NamedTuple

import jax
import jax.numpy as jnp
from jax import lax
from jax.experimental import pallas as pl
from jax.experimental.pallas import tpu as pltpu

F32 = jnp.float32
BF16 = jnp.bfloat16
I32 = jnp.int32

GDN_HEADS = 8
GDN_D = 128
CONV_W = 4
GDN_CHUNK = 64
ATT_HEADS = 8
ATT_KV_HEADS = 2
ATT_DH = 128
IDX_HEADS = 16
IDX_DIM = 64
TOPK_MAX = 256
PEER_HEADS = 8
PEER_NKEYS = 128
PEER_TOPK = 16
LN_EPS = 1e-5
RMS_EPS = 1e-6

LANE = 128
SUBLANE = 8
VMEM_LIMIT = 56 * 1024 * 1024

D = 1024
COL_QKV = 0
COL_Z = 3072
COL_AQ = 4096
COL_IQ = 5120
COL_GA = 6144
COL_GB = 7168
COL_AK = 8192
COL_AV = 8448
COL_SM = 8704
N_COLS = 8832
SM_IK = 0
SM_B = 64
SM_A = 72
SM_IW = 80

NEG_INF = float("-inf")
INT_MIN = -(2 ** 31)


def _nt(a, b):
    return lax.dot_general(a, b, (((1,), (1,)), ((), ())), preferred_element_type=F32)


def _tn(a, b):
    return lax.dot_general(a, b, (((0,), (0,)), ((), ())), preferred_element_type=F32)


def _mm(a, b):
    return jnp.dot(a, b, preferred_element_type=F32)


def _mm_f32(a, b):
    return jnp.dot(a, b, preferred_element_type=F32, precision=lax.Precision.HIGHEST)


def _cparams(sem):
    return pltpu.CompilerParams(dimension_semantics=sem, vmem_limit_bytes=VMEM_LIMIT)


def _proj_kernel(x_ref, w_ref, o_ref):
    o_ref[...] = _mm(x_ref[...].astype(BF16), w_ref[...])


def _project(x, w_bf, tm):
    n = x.shape[0]
    tn = N_COLS // 3
    return pl.pallas_call(
        _proj_kernel,
        grid=(3, n // tm),
        in_specs=[pl.BlockSpec((tm, D), lambda j, i: (i, 0)),
                  pl.BlockSpec((D, tn), lambda j, i: (0, j))],
        out_specs=pl.BlockSpec((tm, tn), lambda j, i: (i, j)),
        out_shape=jax.ShapeDtypeStruct((n, N_COLS), F32),
        compiler_params=_cparams(("arbitrary", "arbitrary")),
        name="in_projection",
    )(x, w_bf)


def _sibling_mask(ri, ci, lvl):
    return ((ri >> (lvl + 1)) == (ci >> (lvl + 1))) & (((ri >> lvl) & 1) == 1) & (((ci >> lvl) & 1) == 0)


_BATCH0 = ((0,), (0,))


def _bmm(a, b):
    return lax.dot_general(a, b, (((2,), (1,)), _BATCH0), preferred_element_type=F32)


def _bnt(a, b):
    return lax.dot_general(a, b, (((2,), (2,)), _BATCH0), preferred_element_type=F32)


def _btn(a, b):
    return lax.dot_general(a, b, (((1,), (1,)), _BATCH0), preferred_element_type=F32)


def _gdn_kernel(qkv_ref, z_ref, sm_ref, prev_ref, s0_ref, cw_ref, hp_ref, ng_ref,
                o_ref, sfin_ref, s_scr, ext_scr, *, c_in, n_valid, n_seq):
    nh = GDN_HEADS
    hd = GDN_D
    stride = ext_scr.shape[0] // n_seq
    L = stride - SUBLANE
    C = L * n_seq
    ls = L.bit_length() - 1
    c = pl.program_id(1)

    @pl.when(c == 0)
    def _():
        s_scr[...] = s0_ref[...]
        for b in range(n_seq):
            ext_scr[b * stride:b * stride + SUBLANE, :] = prev_ref[b]

    u = qkv_ref[...]
    sm = sm_ref[...]
    z = z_ref[...]
    if c_in < C:
        u = jnp.concatenate([u, jnp.zeros((C - c_in, u.shape[1]), F32)], axis=0)
        sm = jnp.concatenate([sm, jnp.zeros((C - c_in, LANE), F32)], axis=0)
        z = jnp.concatenate([z, jnp.zeros((C - c_in, z.shape[1]), F32)], axis=0)
    cw = cw_ref[...]
    pieces = []
    for b in range(n_seq):
        base = b * stride + SUBLANE
        ub_ = u[b * L:(b + 1) * L]
        ext_scr[base:base + L, :] = ub_
        acc = ub_ * cw[CONV_W - 1:CONV_W, :]
        for i in range(CONV_W - 1):
            off = base - (CONV_W - 1) + i
            acc = acc + ext_scr[off:off + L, :] * cw[i:i + 1, :]
        pieces.append(acc)
    conv = pieces[0] if n_seq == 1 else jnp.concatenate(pieces, axis=0)
    if n_seq == 1:
        tail = ext_scr[c_in:c_in + SUBLANE, :]
        ext_scr[0:SUBLANE, :] = tail
    qkv = conv * jax.nn.sigmoid(conv)

    row = lax.broadcasted_iota(I32, (C, 1), 0)
    hp = hp_ref[...]
    xs = sm + hp[1:2, :]
    softplus = jnp.maximum(xs, 0.0) + jnp.log1p(jnp.exp(-jnp.abs(xs)))
    la = -jnp.exp(hp[0:1, :]) * softplus
    beta = jax.nn.sigmoid(sm)
    if n_valid < C:
        valid = row < n_valid
        qkv = jnp.where(valid, qkv, 0.0)
        la = jnp.where(valid, la, 0.0)
        beta = jnp.where(valid, beta, 0.0)

    ri = lax.broadcasted_iota(I32, (C, C), 0)
    ci = lax.broadcasted_iota(I32, (C, C), 1)
    same = (ri >> ls) == (ci >> ls)
    incl = same & (ri >= ci)
    strict = same & (ri > ci)
    g = _mm_f32(incl.astype(F32), la)
    g_end = _mm_f32((ci == (ri | (L - 1))).astype(F32), g)
    gt = g.T
    eye = (ri == ci).astype(F32)

    q_l, k_l, kd_l, dec_l, lows_l, rhs_l, eg_l, z_l = [], [], [], [], [], [], [], []
    for h in range(nh):
        col = SM_A + h
        gcol = g[:, col:col + 1]
        grow = gt[col:col + 1, :]
        bcol = beta[:, SM_B + h:SM_B + h + 1]
        q = qkv[:, h * hd:(h + 1) * hd]
        k = qkv[:, nh * hd + h * hd:nh * hd + (h + 1) * hd]
        v = qkv[:, 2 * nh * hd + h * hd:2 * nh * hd + (h + 1) * hd]
        q = q * lax.rsqrt(jnp.sum(q * q, axis=-1, keepdims=True) + RMS_EPS) * (hd ** -0.5)
        k = k * lax.rsqrt(jnp.sum(k * k, axis=-1, keepdims=True) + RMS_EPS)
        dec = jnp.exp(jnp.where(incl, gcol - grow, NEG_INF))
        eg = jnp.exp(gcol)
        q_l.append(q.astype(BF16))
        k_l.append(k.astype(BF16))
        kd_l.append((k * jnp.exp(g_end[:, col:col + 1] - gcol)).astype(BF16))
        dec_l.append(dec)
        lows_l.append(bcol * jnp.where(strict, dec, 0.0))
        rhs_l.append(jnp.concatenate([bcol * v, (bcol * eg) * k], axis=1))
        eg_l.append(eg)
        z_l.append(z[:, h * hd:(h + 1) * hd])
    qb = jnp.stack(q_l)
    kb = jnp.stack(k_l)
    kd = jnp.stack(kd_l)
    dec = jnp.stack(dec_l)
    eg = jnp.stack(eg_l)

    low = jnp.stack(lows_l) * _bnt(kb, kb)
    qk = _bnt(qb, kb)
    inv = eye[None] - jnp.where(_sibling_mask(ri, ci, 0)[None], low, 0.0)
    for lvl in range(1, ls):
        off = jnp.where(_sibling_mask(ri, ci, lvl)[None], low, 0.0)
        inv_b = inv.astype(BF16)
        inv = inv - _bmm(inv_b, _bmm(off.astype(BF16), inv_b).astype(BF16))
    uw = _bmm(inv.astype(BF16), jnp.stack(rhs_l).astype(BF16))
    w_b = uw[:, :, hd:].astype(BF16)

    ws = None
    qs = None
    s_old = []
    for b in range(n_seq):
        S = s_scr[b]
        s_old.append(S)
        Sb = S.astype(BF16)
        if n_seq == 1:
            wm, qm = w_b, qb
        else:
            mine = ((row >> ls) == b)[None]
            wm = jnp.where(mine, w_b, jnp.zeros_like(w_b))
            qm = jnp.where(mine, qb, jnp.zeros_like(qb))
        ws = _bmm(wm, Sb) if ws is None else ws + _bmm(wm, Sb)
        qs = _bmm(qm, Sb) if qs is None else qs + _bmm(qm, Sb)
    ub = (uw[:, :, :hd] - ws).astype(BF16)
    o = eg * qs + _bmm((dec * qk).astype(BF16), ub)
    for b in range(n_seq):
        last = b * L + L - 1
        glast = jnp.stack([g[last:last + 1, SM_A + h:SM_A + h + 1] for h in range(nh)])
        kdm = kd if n_seq == 1 else jnp.where(((row >> ls) == b)[None], kd, jnp.zeros_like(kd))
        s_scr[b] = jnp.exp(glast) * s_old[b] + _btn(kdm, ub)

    on = o * lax.rsqrt(jnp.mean(o * o, axis=-1, keepdims=True) + RMS_EPS) * ng_ref[...]
    zz = jnp.stack(z_l)
    res = on * (zz * jax.nn.sigmoid(zz))
    for h in range(nh):
        o_ref[:, h * hd:(h + 1) * hd] = res[h, :c_in]

    @pl.when(c == pl.num_programs(1) - 1)
    def _():
        sfin_ref[...] = s_scr[...]


def _gdn(parts, prev, s0, cw, hp, ng, *, batch, n_chunks, c_in, n_valid, n_seq, shared_init):
    n = parts.shape[0]
    C = max(c_in, GDN_CHUNK)
    assert n_seq == 1 or (n_chunks == 1 and c_in == C)
    L = C // n_seq
    init = (lambda b, c: (0, 0, 0)) if shared_init else (lambda b, c: (b, 0, 0))
    init4 = (lambda b, c: (0, 0, 0, 0)) if shared_init else (lambda b, c: (b, 0, 0, 0))
    qkv_w = 3 * GDN_HEADS * GDN_D
    v_w = GDN_HEADS * GDN_D
    return pl.pallas_call(
        functools.partial(_gdn_kernel, c_in=c_in, n_valid=n_valid, n_seq=n_seq),
        grid=(batch, n_chunks),
        in_specs=[
            pl.BlockSpec((c_in, qkv_w), lambda b, c: (b * n_chunks + c, COL_QKV // qkv_w)),
            pl.BlockSpec((c_in, v_w), lambda b, c: (b * n_chunks + c, COL_Z // v_w)),
            pl.BlockSpec((c_in, LANE), lambda b, c: (b * n_chunks + c, COL_SM // LANE)),
            pl.BlockSpec((n_seq, SUBLANE, qkv_w), init),
            pl.BlockSpec((n_seq, GDN_HEADS, GDN_D, GDN_D), init4),
            pl.BlockSpec((CONV_W, qkv_w), lambda b, c: (0, 0)),
            pl.BlockSpec((2, LANE), lambda b, c: (0, 0)),
            pl.BlockSpec((1, GDN_D), lambda b, c: (0, 0)),
        ],
        out_specs=[
            pl.BlockSpec((c_in, v_w), lambda b, c: (b * n_chunks + c, 0)),
            pl.BlockSpec((n_seq, GDN_HEADS, GDN_D, GDN_D), lambda b, c: (b, 0, 0, 0)),
        ],
        out_shape=[jax.ShapeDtypeStruct((n, v_w), F32),
                   jax.ShapeDtypeStruct((batch * n_seq, GDN_HEADS, GDN_D, GDN_D), F32)],
        scratch_shapes=[pltpu.VMEM((n_seq, GDN_HEADS, GDN_D, GDN_D), F32),
                        pltpu.VMEM((n_seq * (SUBLANE + L), qkv_w), F32)],
        compiler_params=_cparams(("arbitrary", "arbitrary")),
        name="gated_deltanet",
    )(parts, parts, parts, prev, s0, cw, hp, ng)


def _sort_key(x):
    b = pltpu.bitcast(x + 0.0, I32)
    return b ^ ((b >> 31) & 0x7FFFFFFF)


KEY_NEG_INF = -(2 ** 31) + 0x7FFFFF
SEARCH_BITS_PER_TRIP = 4


def _count_ge(key_tiles, cand):
    tot = None
    for kt in key_tiles:
        c = jnp.sum((kt >= cand).astype(F32), axis=1, keepdims=True)
        tot = c if tot is None else tot + c
    return tot


def _rows8(x):
    return x.reshape(x.shape[0] // SUBLANE, SUBLANE, x.shape[1])


def _sum_rows8(x, chains=4):
    rows = x.shape[0]
    if rows % (chains * SUBLANE):
        return jnp.sum(_rows8(x), axis=0)
    part = rows // chains
    sums = [jnp.sum(_rows8(x[k * part:(k + 1) * part]), axis=0) for k in range(chains)]
    while len(sums) > 1:
        sums = [a + b for a, b in zip(sums[0::2], sums[1::2])]
    return sums[0]


def _attn_prompt_kernel(q_ref, iq_ref, smq_ref, k_ref, v_ref, smk_ref, mk_ref, mv_ref, msm_ref,
                        o_ref, kbf, vtb, kibf, key_scr, bias_scr, s_scr, cut_scr, *, n_sel, n_meta, kt):
    tq = q_ref.shape[0]
    n_real = k_ref.shape[0]
    kvw = k_ref.shape[1]
    qb_i = pl.program_id(1)
    nt = (qb_i * tq + tq + kt - 1) // kt
    group = ATT_HEADS // ATT_KV_HEADS

    @pl.when(qb_i == 0)
    def _():
        zrows = LANE - n_meta
        kbf[0:n_real, :] = k_ref[...].astype(BF16)
        kbf[n_real:, :] = jnp.concatenate([mk_ref[...], jnp.zeros((zrows, kvw), F32)], axis=0).astype(BF16)
        vtb[:, 0:n_real] = v_ref[...].T.astype(BF16)
        vtb[:, n_real:] = jnp.concatenate([mv_ref[...], jnp.zeros((zrows, kvw), F32)], axis=0).T.astype(BF16)
        kibf[0:n_real, :] = smk_ref[...][:, SM_IK:SM_IK + IDX_DIM].astype(BF16)
        kibf[n_real:, :] = jnp.concatenate(
            [msm_ref[...][:, SM_IK:SM_IK + IDX_DIM], jnp.zeros((zrows, IDX_DIM), F32)], axis=0).astype(BF16)

    iq = iq_ref[...]
    iq_stack = jnp.concatenate([iq[:, h * IDX_DIM:(h + 1) * IDX_DIM] for h in range(IDX_HEADS)], axis=0).astype(BF16)
    w_t = smq_ref[...].T * ((IDX_DIM ** -0.5) * (IDX_HEADS ** -0.5))
    qpos = qb_i * tq + lax.broadcasted_iota(I32, (1, tq), 1)

    def index_scores(start, width):
        s_all = _nt(kibf[pl.ds(start, width), :], iq_stack)
        acc = jnp.zeros((width, tq), F32)
        for h in range(IDX_HEADS):
            acc = acc + jnp.maximum(s_all[:, h * tq:(h + 1) * tq], 0.0) * w_t[SM_IW + h:SM_IW + h + 1, :]
        return acc

    def real_tile(t, carry):
        start = pl.multiple_of(t * kt, kt)
        kpos = start + lax.broadcasted_iota(I32, (kt, 1), 0)
        key_scr[pl.ds(start, kt), :] = _sort_key(jnp.where(kpos <= qpos, index_scores(start, kt), NEG_INF))
        return carry

    lax.fori_loop(0, nt, real_tile, 0)
    mrow = lax.broadcasted_iota(I32, (LANE, 1), 0)
    key_scr[n_real:, :] = _sort_key(jnp.where(mrow < n_meta, index_scores(n_real, LANE), NEG_INF))

    def count_ge(cand):
        def body(t, acc):
            kk = key_scr[pl.ds(pl.multiple_of(t * kt, kt), kt), :]
            return acc + _sum_rows8((kk >= cand).astype(F32))

        acc = lax.fori_loop(0, nt, body, jnp.zeros((SUBLANE, tq), F32))
        acc = acc + jnp.sum(_rows8((key_scr[n_real:, :] >= cand).astype(F32)), axis=0)
        return jnp.sum(acc, axis=0, keepdims=True)

    few = qpos + 1 + n_meta <= n_sel

    def unsettled(state):
        it, _, cnt = state
        return (it < 32) & (jnp.max(jnp.where(few | (cnt == n_sel), 0.0, 1.0)) > 0.0)

    def refine(state):
        it, thr, cnt = state
        for step in range(SEARCH_BITS_PER_TRIP):
            cand = thr + lax.shift_left(jnp.int32(1), 31 - step - it)
            c = count_ge(cand)
            ok = c >= n_sel
            thr = jnp.where(ok, cand, thr)
            cnt = jnp.where(ok, c, cnt)
        return it + SEARCH_BITS_PER_TRIP, thr, cnt

    n_all = (nt * kt + LANE).astype(F32)
    _, thr, cnt = lax.while_loop(
        unsettled, refine, (jnp.int32(0), jnp.full((1, tq), INT_MIN, I32), jnp.full((1, tq), 1.0, F32) * n_all))

    tie = (cnt > n_sel) & (thr > KEY_NEG_INF) & jnp.logical_not(few)
    n_pos = n_real + n_meta
    cut_scr[...] = jnp.full((SUBLANE, tq), float(n_pos), F32)

    def pos_of(start, width, is_meta):
        r = lax.broadcasted_iota(I32, (width, 1), 0)
        return r if is_meta else n_meta + start + r

    @pl.when(jnp.max(tie.astype(F32)) > 0.0)
    def _tie_cut():
        need = n_sel - count_ge(thr + 1)
        nbits = max(1, (n_pos - 1).bit_length())

        def count_eq_le(cut):
            def body(t, acc):
                start = pl.multiple_of(t * kt, kt)
                kk = key_scr[pl.ds(start, kt), :]
                return acc + jnp.sum(_rows8(((kk == thr) & (pos_of(start, kt, False) <= cut)).astype(F32)), axis=0)

            acc = lax.fori_loop(0, nt, body, jnp.zeros((SUBLANE, tq), F32))
            mk = key_scr[n_real:, :]
            acc = acc + jnp.sum(_rows8(((mk == thr) & (pos_of(0, LANE, True) <= cut)).astype(F32)), axis=0)
            return jnp.sum(acc, axis=0, keepdims=True)

        def body(it, cut):
            cand = cut - lax.shift_left(jnp.int32(1), nbits - 1 - it)
            ok = (cand >= 0) & (count_eq_le(cand) >= need)
            return jnp.where(ok, cand, cut)

        cut = lax.fori_loop(0, nbits, body, jnp.full((1, tq), (1 << nbits) - 1, I32))
        cut = jnp.where(tie, cut, n_pos).astype(F32)
        cut_scr[...] = jnp.broadcast_to(cut, (SUBLANE, tq))

    cut = cut_scr[0:1, :].astype(I32)

    def bias_tile(start, width, is_meta):
        kk = key_scr[pl.ds(start, width), :]
        sel = (kk > thr) | ((kk == thr) & (pos_of(start, width, is_meta) <= cut))
        sel = sel & (kk > KEY_NEG_INF)
        bias_scr[pl.ds(start, width), :] = jnp.where(sel, 0.0, NEG_INF)

    def bias_body(t, carry):
        bias_tile(pl.multiple_of(t * kt, kt), kt, False)
        return carry

    lax.fori_loop(0, nt, bias_body, 0)
    bias_tile(n_real, LANE, True)

    scale = ATT_DH ** -0.5
    for g in range(ATT_KV_HEADS):
        q4 = jnp.concatenate([q_ref[:, (g * group + j) * ATT_DH:(g * group + j + 1) * ATT_DH] for j in range(group)],
                             axis=0).astype(BF16)

        def score_tile(start, width):
            s = _nt(kbf[pl.ds(start, width), g * ATT_DH:(g + 1) * ATT_DH], q4) * scale
            b = bias_scr[pl.ds(start, width), :]
            s = s + jnp.concatenate([b] * group, axis=1)
            s_scr[pl.ds(start, width), :] = s
            return jnp.max(_rows8(s), axis=0)

        def pass1(t, m):
            return jnp.maximum(m, score_tile(pl.multiple_of(t * kt, kt), kt))

        m = jnp.max(lax.fori_loop(0, nt, pass1, score_tile(n_real, LANE)), axis=0, keepdims=True)

        def prob_tile(start, width):
            p = jnp.exp(s_scr[pl.ds(start, width), :] - m)
            pv = _mm(vtb[g * ATT_DH:(g + 1) * ATT_DH, pl.ds(start, width)], p.astype(BF16))
            return _sum_rows8(p), pv

        def pass2(t, carry):
            l, acc = carry
            dl, dacc = prob_tile(pl.multiple_of(t * kt, kt), kt)
            return l + dl, acc + dacc

        l, acc = lax.fori_loop(0, nt, pass2, prob_tile(n_real, LANE))
        out_t = acc / jnp.sum(l, axis=0, keepdims=True)
        for j in range(group):
            hh = g * group + j
            o_ref[:, hh * ATT_DH:(hh + 1) * ATT_DH] = out_t[:, j * tq:(j + 1) * tq].T


def _attn_prompt(parts, mparts, *, batch, seq, n_meta, tq, kt):
    nqb = seq // tq
    n_sel = min(TOPK_MAX, (seq + n_meta) // 4)
    kvw = ATT_KV_HEADS * ATT_DH
    nk = seq + LANE
    return pl.pallas_call(
        functools.partial(_attn_prompt_kernel, n_sel=n_sel, n_meta=n_meta, kt=kt),
        grid=(batch, nqb),
        in_specs=[
            pl.BlockSpec((tq, D), lambda b, i: (b * nqb + i, COL_AQ // D)),
            pl.BlockSpec((tq, D), lambda b, i: (b * nqb + i, COL_IQ // D)),
            pl.BlockSpec((tq, LANE), lambda b, i: (b * nqb + i, COL_SM // LANE)),
            pl.BlockSpec((seq, kvw), lambda b, i: (b, COL_AK // kvw)),
            pl.BlockSpec((seq, kvw), lambda b, i: (b, COL_AV // kvw)),
            pl.BlockSpec((seq, LANE), lambda b, i: (b, COL_SM // LANE)),
            pl.BlockSpec((n_meta, kvw), lambda b, i: (0, COL_AK // kvw)),
            pl.BlockSpec((n_meta, kvw), lambda b, i: (0, COL_AV // kvw)),
            pl.BlockSpec((n_meta, LANE), lambda b, i: (0, COL_SM // LANE)),
        ],
        out_specs=pl.BlockSpec((tq, D), lambda b, i: (b * nqb + i, 0)),
        out_shape=jax.ShapeDtypeStruct((batch * seq, D), F32),
        scratch_shapes=[pltpu.VMEM((nk, kvw), BF16), pltpu.VMEM((kvw, nk), BF16), pltpu.VMEM((nk, IDX_DIM), BF16),
                        pltpu.VMEM((nk, tq), I32), pltpu.VMEM((nk, tq), F32),
                        pltpu.VMEM((nk, (ATT_HEADS // ATT_KV_HEADS) * tq), F32), pltpu.VMEM((SUBLANE, tq), F32)],
        compiler_params=_cparams(("arbitrary", "arbitrary")),
        name="sparse_attention_prompt",
    )(parts, parts, parts, parts, parts, parts, mparts, mparts, mparts)


def _attn_sample_kernel(pt_ref, *refs, n_pages, n_sel):
    kp = refs[0:n_pages]
    vp = refs[n_pages:2 * n_pages]
    ip = refs[2 * n_pages:3 * n_pages]
    q_ref, iq_ref, kn_ref, vn_ref, sm_ref, o_ref, kil, vil, kit, cut_scr = refs[3 * n_pages:]
    del pt_ref
    t_new = q_ref.shape[0]
    page = ip[0].shape[1]
    past = n_pages * page
    nkv = ATT_KV_HEADS
    for j in range(n_pages):
        kil[j * nkv * page:(j + 1) * nkv * page, :] = kp[j][...].astype(BF16)
        vil[j * nkv * page:(j + 1) * nkv * page, :] = vp[j][...].astype(BF16)
        kit[:, j * page:(j + 1) * page] = ip[j][...].astype(BF16)
    sm = sm_ref[...]
    zpad = jnp.zeros((LANE - t_new, nkv * ATT_DH), BF16)
    knew = jnp.concatenate([kn_ref[...].astype(BF16), zpad], axis=0)
    vnew = jnp.concatenate([vn_ref[...].astype(BF16), zpad], axis=0)
    ki_new = jnp.concatenate(
        [sm[:, SM_IK:SM_IK + IDX_DIM].astype(BF16), jnp.zeros((LANE - t_new, IDX_DIM), BF16)], axis=0)

    iq = iq_ref[...]
    qi = jnp.concatenate([iq[:, h * IDX_DIM:(h + 1) * IDX_DIM] for h in range(IDX_HEADS)], axis=0).astype(BF16)
    wcol = jnp.concatenate([sm[:, SM_IW + h:SM_IW + h + 1] for h in range(IDX_HEADS)], axis=0)
    wcol = wcol * ((IDX_DIM ** -0.5) * (IDX_HEADS ** -0.5))

    def idx_scores(qk):
        s = jnp.maximum(qk, 0.0) * wcol
        acc = s[0:t_new]
        for h in range(1, IDX_HEADS):
            acc = acc + s[h * t_new:(h + 1) * t_new]
        return acc

    i_past = idx_scores(_mm(qi, kit[...]))
    i_new = idx_scores(_nt(qi, ki_new))
    trow = lax.broadcasted_iota(I32, (t_new, LANE), 0)
    tlane = lax.broadcasted_iota(I32, (t_new, LANE), 1)
    key_past = _sort_key(i_past)
    key_new = _sort_key(jnp.where(tlane <= trow, i_new, NEG_INF))
    tiles = [key_past, key_new]

    def digit_body(it, thr):
        step = lax.shift_left(jnp.int32(1), 28 - 4 * it)
        digit = jnp.zeros((t_new, 1), I32)
        for j in range(1, 16):
            digit = digit + (_count_ge(tiles, thr + j * step) >= n_sel).astype(I32)
        return thr + digit * step

    thr = lax.fori_loop(0, 8, digit_body, jnp.full((t_new, 1), INT_MIN, I32))
    n_ge = _count_ge(tiles, thr)
    tie = (n_ge > n_sel) & (thr > KEY_NEG_INF)
    pos_past = lax.broadcasted_iota(I32, (1, past), 1)
    pos_new = past + lax.broadcasted_iota(I32, (1, LANE), 1)
    n_pos = past + t_new
    cut_scr[...] = jnp.full(cut_scr.shape, n_pos, I32)

    @pl.when(jnp.max(tie.astype(F32)) > 0.0)
    def _tie_cut():
        need = n_sel - _count_ge(tiles, thr + 1)
        nbits = max(1, (n_pos - 1).bit_length())

        def cut_body(it, cut):
            cand = cut - lax.shift_left(jnp.int32(1), nbits - 1 - it)
            cnt = (jnp.sum(((key_past == thr) & (pos_past <= cand)).astype(F32), axis=1, keepdims=True)
                   + jnp.sum(((key_new == thr) & (pos_new <= cand)).astype(F32), axis=1, keepdims=True))
            ok = (cand >= 0) & (cnt >= need)
            return jnp.where(ok, cand, cut)

        cut = lax.fori_loop(0, nbits, cut_body, jnp.full((t_new, 1), (1 << nbits) - 1, I32))
        cut_scr[...] = jnp.broadcast_to(jnp.where(tie, cut, n_pos), cut_scr.shape)

    cut = cut_scr[:, 0:1]

    def chosen(kk, pos):
        return ((kk > thr) | ((kk == thr) & (pos <= cut))) & (kk > KEY_NEG_INF)

    group = ATT_HEADS // ATT_KV_HEADS
    b_new = jnp.concatenate([jnp.where(chosen(key_new, pos_new), 0.0, NEG_INF)] * group, axis=0)
    sel01 = chosen(key_past, pos_past).astype(BF16)
    er = lax.broadcasted_iota(I32, (page, nkv * page), 0)
    ec = lax.broadcasted_iota(I32, (page, nkv * page), 1)
    expand = ((ec >= er * nkv) & (ec < (er + 1) * nkv)).astype(BF16)
    sel_il = jnp.concatenate([_mm(sel01[:, j * page:(j + 1) * page], expand) for j in range(n_pages)], axis=1)
    il_lane = lax.broadcasted_iota(I32, (1, nkv * past), 1)
    assert nkv & (nkv - 1) == 0
    il_head = il_lane & (nkv - 1)
    scale = ATT_DH ** -0.5
    q = q_ref[...]
    for g in range(ATT_KV_HEADS):
        qg = jnp.concatenate(
            [q[:, (g * group + j) * ATT_DH:(g * group + j + 1) * ATT_DH] for j in range(group)], axis=0).astype(BF16)
        b_g = jnp.where((sel_il > 0.5) & (il_head == g), 0.0, NEG_INF)
        s_p = _nt(qg, kil[...]) * scale + jnp.concatenate([b_g] * group, axis=0)
        s_n = _nt(qg, knew[:, g * ATT_DH:(g + 1) * ATT_DH]) * scale + b_new
        m = jnp.maximum(jnp.max(s_p, axis=1, keepdims=True), jnp.max(s_n, axis=1, keepdims=True))
        p_p = jnp.exp(s_p - m)
        p_n = jnp.exp(s_n - m)
        l = jnp.sum(p_p, axis=1, keepdims=True) + jnp.sum(p_n, axis=1, keepdims=True)
        acc = _mm(p_p.astype(BF16), vil[...]) + _mm(p_n.astype(BF16), vnew[:, g * ATT_DH:(g + 1) * ATT_DH])
        res = acc / l
        for j in range(group):
            hh = g * group + j
            o_ref[:, hh * ATT_DH:(hh + 1) * ATT_DH] = res[j * t_new:(j + 1) * t_new]


def _attn_sample(parts, cache_k, cache_v, cache_ik, page_table, *, t_new):
    db, n_pages = page_table.shape
    page = cache_k.shape[2]
    kvw = ATT_KV_HEADS * ATT_DH
    past = n_pages * page
    n_sel = min(TOPK_MAX, (past + t_new) // 4)

    n_pool = cache_k.shape[1]
    cache_k = cache_k.reshape(n_pool, page * ATT_KV_HEADS, ATT_DH)
    cache_v = cache_v.reshape(n_pool, page * ATT_KV_HEADS, ATT_DH)
    cache_ik = jnp.swapaxes(cache_ik, 2, 3).reshape(n_pool, IDX_DIM, page)

    def kv_page(j):
        return pl.BlockSpec((None, page * ATT_KV_HEADS, ATT_DH), lambda b, pt, j=j: (pt[b, j], 0, 0))

    def ik_page(j):
        return pl.BlockSpec((None, IDX_DIM, page), lambda b, pt, j=j: (pt[b, j], 0, 0))

    in_specs = ([kv_page(j) for j in range(n_pages)] + [kv_page(j) for j in range(n_pages)]
                + [ik_page(j) for j in range(n_pages)]
                + [pl.BlockSpec((t_new, D), lambda b, pt: (b, COL_AQ // D)),
                   pl.BlockSpec((t_new, D), lambda b, pt: (b, COL_IQ // D)),
                   pl.BlockSpec((t_new, kvw), lambda b, pt: (b, COL_AK // kvw)),
                   pl.BlockSpec((t_new, kvw), lambda b, pt: (b, COL_AV // kvw)),
                   pl.BlockSpec((t_new, LANE), lambda b, pt: (b, COL_SM // LANE))])
    grid_spec = pltpu.PrefetchScalarGridSpec(
        num_scalar_prefetch=1, grid=(db,), in_specs=in_specs,
        out_specs=pl.BlockSpec((t_new, D), lambda b, pt: (b, 0)),
        scratch_shapes=[pltpu.VMEM((past * ATT_KV_HEADS, ATT_DH), BF16), pltpu.VMEM((past * ATT_KV_HEADS, ATT_DH), BF16),
                        pltpu.VMEM((IDX_DIM, past), BF16), pltpu.VMEM((t_new, LANE), I32)])
    return pl.pallas_call(
        functools.partial(_attn_sample_kernel, n_pages=n_pages, n_sel=n_sel),
        grid_spec=grid_spec,
        out_shape=jax.ShapeDtypeStruct((db * t_new, D), F32),
        compiler_params=_cparams(("arbitrary",)),
        name="sparse_attention_sample",
    )(page_table, *([cache_k] * n_pages), *([cache_v] * n_pages), *([cache_ik] * n_pages),
      parts, parts, parts, parts, parts)


def _layer_norm(x, g, b):
    mu = jnp.mean(x, axis=-1, keepdims=True)
    xc = x - mu
    var = jnp.mean(xc * xc, axis=-1, keepdims=True)
    return xc * lax.rsqrt(var + LN_EPS) * g + b


def _finish_kernel(x_ref, og_ref, oa_ref, ga_ref, gb_ref, wbg_ref, wba_ref, wo_ref, g_ref, b_ref, h_ref, *, alpha):
    a = _mm(og_ref[...].astype(BF16), wbg_ref[...])
    b = _mm(oa_ref[...].astype(BF16), wba_ref[...])
    merged = jax.nn.sigmoid(ga_ref[...]) * a + jax.nn.sigmoid(gb_ref[...]) * b
    y = alpha * x_ref[...] + _mm(merged.astype(BF16), wo_ref[...])
    h_ref[...] = _layer_norm(y, g_ref[...], b_ref[...])


def _finish(x, o_gdn, o_att, parts, wbg, wba, wo, g, b, *, alpha, tm):
    n = x.shape[0]
    tm = min(tm, n)
    row = lambda i: (i, 0)
    full = lambda i: (0, 0)
    return pl.pallas_call(
        functools.partial(_finish_kernel, alpha=alpha),
        grid=(n // tm,),
        in_specs=[pl.BlockSpec((tm, D), row), pl.BlockSpec((tm, D), row), pl.BlockSpec((tm, D), row),
                  pl.BlockSpec((tm, D), lambda i: (i, COL_GA // D)), pl.BlockSpec((tm, D), lambda i: (i, COL_GB // D)),
                  pl.BlockSpec((D, D), full), pl.BlockSpec((D, D), full), pl.BlockSpec((D, D), full),
                  pl.BlockSpec((1, D), full), pl.BlockSpec((1, D), full)],
        out_specs=pl.BlockSpec((tm, D), row),
        out_shape=jax.ShapeDtypeStruct((n, D), F32),
        compiler_params=_cparams(("arbitrary",)),
        name="merge_layernorm",
    )(x, o_gdn, o_att, parts, parts, wbg, wba, wo, g, b)


_CAND = [(r0, r1) for r0 in range(PEER_TOPK) for r1 in range(PEER_TOPK) if (r0 + 1) * (r1 + 1) <= PEER_TOPK]
_CAND_OFF = [next(i for i, c in enumerate(_CAND) if c[0] == r0) for r0 in range(PEER_TOPK)]
_CAND_LEN = [sum(1 for c in _CAND if c[0] == r0) for r0 in range(PEER_TOPK)]
_CAND_ROWS = -(-len(_CAND) // SUBLANE) * SUBLANE


def _top_rows(s, n_top, break_ties):
    rows, cols = s.shape
    iota = lax.broadcasted_iota(I32, (rows, cols), 0).astype(F32)
    rank = jnp.full((rows, cols), float(n_top), F32)
    vals = []
    for r in range(n_top):
        m = jnp.max(s, axis=0, keepdims=True)
        hit = s == m
        if break_ties:
            hit = iota == jnp.min(jnp.where(hit, iota, float(rows)), axis=0, keepdims=True)
        vals.append(m)
        s = jnp.where(hit, NEG_INF, s)
        rank = jnp.where(hit, float(r), rank)
    n_ranked = jnp.sum((rank < float(n_top)).astype(F32), axis=0, keepdims=True)
    return jnp.concatenate(vals, axis=0), rank, n_ranked == float(n_top)


def _peer_kernel(h_ref, wq_ref, sk_ref, u0_ref, un_ref, vp_ref, vl_ref, g_ref, b_ref, y_ref,
                 hb_scr, q_scr, rank1_scr, bt_scr, nt_scr, at_scr, s1_scr, s_scr, p_scr, acc_scr, *, alpha, n_steps):
    tt = h_ref.shape[0]
    eb = un_ref.shape[0] // 2
    e = pl.program_id(1)
    nk = PEER_NKEYS
    kk = PEER_TOPK
    tchunks = tt // LANE

    @pl.when(e == 0)
    def _prep():
        hb = h_ref[...].astype(BF16)
        hb_scr[...] = hb
        q_scr[...] = _mm(hb, wq_ref[...]).astype(BF16)
        acc_scr[...] = jnp.zeros(acc_scr.shape, F32)

        def scores(hd, carry):
            q0 = q_scr[:, pl.ds(pl.multiple_of(hd * 2 * nk, nk), nk)]
            q1 = q_scr[:, pl.ds(pl.multiple_of(hd * 2 * nk + nk, nk), nk)]
            at_scr[hd] = _nt(sk_ref[hd, 0], q0)
            s1_scr[hd] = _nt(sk_ref[hd, 1], q1)
            return carry

        lax.fori_loop(0, PEER_HEADS, scores, 0)

        def select_chunk(hd, c0, s0, s1, break_ties):
            a, rank0, ok0 = _top_rows(s0, kk, break_ties)
            b, rank1, ok1 = _top_rows(s1, kk, break_ties)
            pad = [jnp.full((_CAND_ROWS - len(_CAND), LANE), NEG_INF, F32)]
            cand = jnp.concatenate([a[r0:r0 + 1] + b[r1:r1 + 1] for r0, r1 in _CAND] + pad, axis=0)
            top, crank, okc = _top_rows(cand, kk, break_ties)
            zsum = jnp.sum(jnp.exp(top - top[0:1]), axis=0, keepdims=True)
            chosen = (crank < float(kk)).astype(F32)
            crow = lax.broadcasted_iota(I32, (_CAND_ROWS, 1), 0)
            nsel = jnp.zeros((nk, LANE), F32)
            for r0 in range(kk):
                in_group = (crow >= _CAND_OFF[r0]) & (crow < _CAND_OFF[r0] + _CAND_LEN[r0])
                cnt = jnp.sum(jnp.where(in_group, chosen, 0.0), axis=0, keepdims=True)
                nsel = nsel + jnp.where(rank0 == float(r0), cnt, 0.0)
            rank1_scr[hd, :, pl.ds(c0, LANE)] = rank1.astype(BF16)
            nt_scr[hd, :, pl.ds(c0, LANE)] = nsel
            at_scr[hd, :, pl.ds(c0, LANE)] = jnp.exp(s0 - a[0:1])
            bt_scr[hd, :, pl.ds(c0, LANE)] = (jnp.exp(s1 - b[0:1]) / zsum).astype(BF16)
            return jnp.min((ok0 & ok1 & okc).astype(F32)) > 0.5

        pair = 2 if tchunks % 2 == 0 else 1

        def select(idx, carry):
            hd = idx // (tchunks // pair)
            first = (idx % (tchunks // pair)) * pair
            tie_free = None
            chunks = []
            for k in range(pair):
                c0 = pl.multiple_of((first + k) * LANE, LANE)
                s0 = at_scr[hd, :, pl.ds(c0, LANE)]
                s1 = s1_scr[hd, :, pl.ds(c0, LANE)]
                chunks.append((c0, s0, s1))
            for c0, s0, s1 in chunks:
                ok = select_chunk(hd, c0, s0, s1, False)
                tie_free = ok if tie_free is None else tie_free & ok

            @pl.when(jnp.logical_not(tie_free))
            def _():
                for c0, s0, s1 in chunks:
                    select_chunk(hd, c0, s0, s1, True)

            return carry

        lax.fori_loop(0, PEER_HEADS * tchunks // pair, select, 0)

        s_scr[1] = _nt(u0_ref[...], hb_scr[...])
        p_scr[1] = jnp.zeros(p_scr.shape[1:], BF16)

    groups = eb // nk

    assert 2 * groups == SUBLANE

    def weights(half, s_ref, p_ref):
        i0 = pl.multiple_of(e * SUBLANE, SUBLANE)
        zero = jnp.zeros((nk, LANE), BF16)

        for tc in range(tchunks):
            cols = slice(tc * LANE, (tc + 1) * LANE)
            gates = [zero] * groups
            for hd in range(PEER_HEADS):
                nblk = nt_scr[hd, pl.ds(i0, SUBLANE), cols]
                ablk = at_scr[hd, pl.ds(i0, SUBLANE), cols]
                rank1 = rank1_scr[hd, :, cols]
                bval = bt_scr[hd, :, cols]
                for ii in range(groups):
                    r = half * groups + ii
                    nrow = nblk[r:r + 1].astype(BF16)
                    arow = ablk[r:r + 1].astype(BF16)
                    gates[ii] = gates[ii] + jnp.where(rank1 < nrow, bval, zero) * arow
            for ii in range(groups):
                rows = slice(ii * nk, (ii + 1) * nk)
                s = s_ref[rows, cols]
                act = 0.5 * s * (1.0 + lax.erf(s * (2.0 ** -0.5)))
                p_ref[rows, cols] = gates[ii] * act.astype(BF16)

    def step(cur, nxt):
        s_scr[nxt] = _nt(un_ref[...], hb_scr[...])
        for half in range(2):
            rows = pl.ds(half * eb, eb)
            weights(half, s_scr.at[cur, rows], p_scr.at[nxt, rows])
        acc_scr[...] += _tn(p_scr[cur], vp_ref[...])

    @pl.when(e % 2 == 0)
    def _():
        step(1, 0)

    @pl.when(e % 2 == 1)
    def _():
        step(0, 1)

    last_slot = (n_steps - 1) % 2

    @pl.when(e == n_steps - 1)
    def _():
        y = alpha * h_ref[...] + (acc_scr[...] + _tn(p_scr[last_slot], vl_ref[...]))
        y_ref[...] = _layer_norm(y, g_ref[...], b_ref[...])


def _peer(h, wq, sk, u, v, g, b, *, alpha, tt, eb):
    n = h.shape[0]
    n_exp = u.shape[0]
    nk = PEER_NKEYS
    nb = n_exp // eb
    return pl.pallas_call(
        functools.partial(_peer_kernel, alpha=alpha, n_steps=nb),
        grid=(n // tt, nb),
        in_specs=[pl.BlockSpec((tt, D), lambda t, e: (t, 0)),
                  pl.BlockSpec(wq.shape, lambda t, e: (0, 0)),
                  pl.BlockSpec(sk.shape, lambda t, e: (0, 0, 0, 0)),
                  pl.BlockSpec((eb, D), lambda t, e: (0, 0)),
                  pl.BlockSpec((eb, D), lambda t, e: (jnp.minimum(e + 1, nb - 1), 0)),
                  pl.BlockSpec((eb, D), lambda t, e: (jnp.maximum(e - 1, 0), 0)),
                  pl.BlockSpec((eb, D), lambda t, e: (nb - 1, 0)),
                  pl.BlockSpec((1, D), lambda t, e: (0, 0)),
                  pl.BlockSpec((1, D), lambda t, e: (0, 0))],
        out_specs=pl.BlockSpec((tt, D), lambda t, e: (t, 0)),
        out_shape=jax.ShapeDtypeStruct((n, D), F32),
        scratch_shapes=[pltpu.VMEM((tt, D), BF16), pltpu.VMEM((tt, wq.shape[1]), BF16),
                        pltpu.VMEM((PEER_HEADS, nk, tt), BF16), pltpu.VMEM((PEER_HEADS, nk, tt), BF16),
                        pltpu.VMEM((PEER_HEADS, nk, tt), F32), pltpu.VMEM((PEER_HEADS, nk, tt), F32),
                        pltpu.VMEM((PEER_HEADS, nk, tt), F32),
                        pltpu.VMEM((2, eb, tt), F32), pltpu.VMEM((2, eb, tt), BF16), pltpu.VMEM((tt, D), F32)],
        compiler_params=_cparams(("arbitrary", "arbitrary")),
        name="peer_ffn",
    )(h, wq, sk, u, u, v, v, g, b)


class _Tiles(NamedTuple):
    rows: int
    attn_q: int
    attn_k: int
    peer_experts: int


def _tiles(n_rows, seq):
    rows = 512 if n_rows % 512 == 0 else 256
    return _Tiles(rows=rows, attn_q=256 if seq % 256 == 0 else 128, attn_k=min(512, seq),
                  peer_experts=SUBLANE * PEER_NKEYS)


def _permute_w_in(w):
    gq = 3 * GDN_HEADS * GDN_D
    gv = GDN_HEADS * GDN_D
    aq = ATT_HEADS * ATT_DH
    akv = ATT_KV_HEADS * ATT_DH
    iqw = IDX_HEADS * IDX_DIM
    sizes = (gq, gv, GDN_HEADS, GDN_HEADS, aq, akv, akv, iqw, IDX_DIM, IDX_HEADS, D, D)
    offs = [0]
    for s in sizes:
        offs.append(offs[-1] + s)
    seg = [w[:, offs[i]:offs[i + 1]] for i in range(len(sizes))]
    qkv, z, b, a, q, k, v, iq, ik, iw, ga, gb = seg
    pad = jnp.zeros((w.shape[0], LANE - IDX_DIM - 2 * GDN_HEADS - IDX_HEADS), w.dtype)
    return jnp.concatenate([qkv, z, q, iq, ga, gb, k, v, ik, b, a, iw, pad], axis=1).astype(BF16)


def kernel(x_prompt, x_sample, cache_k, cache_v, cache_idx_k, state_conv, state_delta, page_table, meta_tokens, w_in, conv_w, a_log, dt_bias, gdn_norm_g, w_branch_gdn, w_branch_attn, w_out, ln1_g, ln1_b, peer_wq, peer_subkeys, peer_u, peer_v, ln2_g, ln2_b):
    depth = w_in.shape[0]
    assert depth == 1, "single-layer step"
    batch, seq, d = x_prompt.shape
    db, t_new, _ = x_sample.shape
    n_meta = meta_tokens.shape[0]
    assert d == D and seq % GDN_CHUNK == 0 and n_meta % SUBLANE == 0 and n_meta <= GDN_CHUNK
    alpha = (2 * depth) ** 0.25
    qkv_w = 3 * GDN_HEADS * GDN_D
    kvw = ATT_KV_HEADS * ATT_DH

    tiles = _tiles(batch * seq, seq)
    w_r = _permute_w_in(w_in[0])
    xp = x_prompt.reshape(batch * seq, D)
    xs = x_sample.reshape(db * t_new, D)
    parts_p = _project(xp, w_r, tiles.rows)
    parts_s = _project(xs, w_r, min(tiles.rows, db * t_new))
    parts_m = _project(meta_tokens.astype(F32), w_r, n_meta)

    hp = jnp.zeros((2, LANE), F32).at[0, SM_A:SM_A + GDN_HEADS].set(a_log[0]).at[1, SM_A:SM_A + GDN_HEADS].set(dt_bias[0])
    cw = conv_w[0]
    ng = gdn_norm_g[0].reshape(1, GDN_D)
    zero_prev = jnp.zeros((1, SUBLANE, qkv_w), F32)
    zero_state = jnp.zeros((1, GDN_HEADS, GDN_D, GDN_D), F32)
    _, s_meta = _gdn(parts_m, zero_prev, zero_state, cw, hp, ng, batch=1, n_chunks=1, c_in=n_meta, n_valid=n_meta,
                     n_seq=1, shared_init=True)
    meta_prev = parts_m[n_meta - SUBLANE:, :qkv_w].reshape(1, SUBLANE, qkv_w)
    og_p, p_delta = _gdn(parts_p, meta_prev, s_meta, cw, hp, ng, batch=batch, n_chunks=seq // GDN_CHUNK,
                         c_in=GDN_CHUNK, n_valid=GDN_CHUNK, n_seq=1, shared_init=True)
    per = GDN_CHUNK // t_new
    assert per * t_new == GDN_CHUNK and t_new % SUBLANE == 0 and db % per == 0
    samp_prev = jnp.pad(state_conv.reshape(db, CONV_W - 1, qkv_w), ((0, 0), (SUBLANE - (CONV_W - 1), 0), (0, 0)))
    og_s, s_delta = _gdn(parts_s, samp_prev, state_delta.reshape(db, GDN_HEADS, GDN_D, GDN_D).astype(F32), cw, hp, ng,
                         batch=db // per, n_chunks=1, c_in=GDN_CHUNK, n_valid=GDN_CHUNK, n_seq=per, shared_init=False)

    oa_p = _attn_prompt(parts_p, parts_m, batch=batch, seq=seq, n_meta=n_meta, tq=tiles.attn_q, kt=tiles.attn_k)
    oa_s = _attn_sample(parts_s, cache_k, cache_v, cache_idx_k, page_table, t_new=t_new)

    wbg = w_branch_gdn[0].astype(BF16)
    wba = w_branch_attn[0].astype(BF16)
    wo = w_out[0].astype(BF16)
    g1 = ln1_g[0].reshape(1, D)
    b1 = ln1_b[0].reshape(1, D)
    h_p = _finish(xp, og_p, oa_p, parts_p, wbg, wba, wo, g1, b1, alpha=alpha, tm=tiles.rows)
    h_s = _finish(xs, og_s, oa_s, parts_s, wbg, wba, wo, g1, b1, alpha=alpha, tm=tiles.rows)

    wq = peer_wq[0].astype(BF16)
    sk = peer_subkeys[0].astype(BF16)
    u = peer_u[0].astype(BF16)
    v = peer_v[0].astype(BF16)
    g2 = ln2_g[0].reshape(1, D)
    b2 = ln2_b[0].reshape(1, D)
    y_p = _peer(h_p, wq, sk, u, v, g2, b2, alpha=alpha, tt=tiles.rows, eb=tiles.peer_experts)
    y_s = _peer(h_s, wq, sk, u, v, g2, b2, alpha=alpha, tt=min(tiles.rows, db * t_new), eb=tiles.peer_experts)

    y_prompt = y_p.reshape(batch, seq, D)
    y_sample = y_s.reshape(db, t_new, D)
    pp = parts_p.reshape(batch, seq, N_COLS)
    ps = parts_s.reshape(db, t_new, N_COLS)
    p_conv = pp[:, seq - (CONV_W - 1):, :qkv_w][None]
    s_conv = jnp.concatenate([state_conv[0].astype(F32), ps[:, :, :qkv_w]], axis=1)[:, -(CONV_W - 1):][None]

    def with_meta(col, width):
        m = jnp.broadcast_to(parts_m[None, :, col:col + width], (batch, n_meta, width))
        return jnp.concatenate([m, pp[:, :, col:col + width]], axis=1)

    p_k = with_meta(COL_AK, kvw).reshape(1, batch, seq + n_meta, ATT_KV_HEADS, ATT_DH)
    p_v = with_meta(COL_AV, kvw).reshape(1, batch, seq + n_meta, ATT_KV_HEADS, ATT_DH)
    p_idx_k = with_meta(COL_SM + SM_IK, IDX_DIM)[None]
    s_k = ps[:, :, COL_AK:COL_AK + kvw].reshape(1, db, t_new, ATT_KV_HEADS, ATT_DH)
    s_v = ps[:, :, COL_AV:COL_AV + kvw].reshape(1, db, t_new, ATT_KV_HEADS, ATT_DH)
    s_idx_k = ps[:, :, COL_SM + SM_IK:COL_SM + SM_IK + IDX_DIM][None]
    return (y_prompt, y_sample, p_conv, p_delta[None], p_k, p_v, p_idx_k, s_conv, s_delta[None].astype(state_delta.dtype),
            s_k, s_v, s_idx_k)
```

```python
import functools
from typing import NamedTuple

import jax
import jax.numpy as jnp
from jax import lax
from jax.experimental import pallas as pl
from jax.experimental.pallas import tpu as pltpu

F32 = jnp.float32
BF16 = jnp.bfloat16
I32 = jnp.int32

GDN_HEADS = 8
GDN_D = 128
CONV_W = 4
GDN_CHUNK = 64
ATT_HEADS = 8
ATT_KV_HEADS = 2
ATT_DH = 128
IDX_HEADS = 16
IDX_DIM = 64
TOPK_MAX = 256
PEER_HEADS = 8
PEER_NKEYS = 128
PEER_TOPK = 16
LN_EPS = 1e-5
RMS_EPS = 1e-6

LANE = 128
SUBLANE = 8
VMEM_LIMIT = 56 * 1024 * 1024

D = 1024
COL_QKV = 0
COL_Z = 3072
COL_AQ = 4096
COL_IQ = 5120
COL_GA = 6144
COL_GB = 7168
COL_AK = 8192
COL_AV = 8448
COL_SM = 8704
N_COLS = 8832
SM_IK = 0
SM_B = 64
SM_A = 72
SM_IW = 80

NEG_INF = float("-inf")
INT_MIN = -(2 ** 31)


def _nt(a, b):
    return lax.dot_general(a, b, (((1,), (1,)), ((), ())), preferred_element_type=F32)


def _tn(a, b):
    return lax.dot_general(a, b, (((0,), (0,)), ((), ())), preferred_element_type=F32)


def _mm(a, b):
    return jnp.dot(a, b, preferred_element_type=F32)


def _mm_f32(a, b):
    return jnp.dot(a, b, preferred_element_type=F32, precision=lax.Precision.HIGHEST)


def _cparams(sem):
    return pltpu.CompilerParams(dimension_semantics=sem, vmem_limit_bytes=VMEM_LIMIT)


def _proj_kernel(x_ref, w_ref, o_ref):
    o_ref[...] = _mm(x_ref[...].astype(BF16), w_ref[...])


def _project(x, w_bf, tm):
    n = x.shape[0]
    tn = N_COLS // 3
    return pl.pallas_call(
        _proj_kernel,
        grid=(3, n // tm),
        in_specs=[pl.BlockSpec((tm, D), lambda j, i: (i, 0)),
                  pl.BlockSpec((D, tn), lambda j, i: (0, j))],
        out_specs=pl.BlockSpec((tm, tn), lambda j, i: (i, j)),
        out_shape=jax.ShapeDtypeStruct((n, N_COLS), F32),
        compiler_params=_cparams(("arbitrary", "arbitrary")),
        name="in_projection",
    )(x, w_bf)


def _sibling_mask(ri, ci, lvl):
    return ((ri >> (lvl + 1)) == (ci >> (lvl + 1))) & (((ri >> lvl) & 1) == 1) & (((ci >> lvl) & 1) == 0)


_BATCH0 = ((0,), (0,))


def _bmm(a, b):
    return lax.dot_general(a, b, (((2,), (1,)), _BATCH0), preferred_element_type=F32)


def _bnt(a, b):
    return lax.dot_general(a, b, (((2,), (2,)), _BATCH0), preferred_element_type=F32)


def _btn(a, b):
    return lax.dot_general(a, b, (((1,), (1,)), _BATCH0), preferred_element_type=F32)


def _gdn_kernel(qkv_ref, z_ref, sm_ref, prev_ref, s0_ref, cw_ref, hp_ref, ng_ref,
                o_ref, sfin_ref, s_scr, ext_scr, *, c_in, n_valid, n_seq, shared_init):
    nh = GDN_HEADS
    hd = GDN_D
    n_par = qkv_ref.shape[0]
    stride = ext_scr.shape[1] // n_seq
    L = stride - SUBLANE
    C = L * n_seq
    ls = L.bit_length() - 1
    c = pl.program_id(1)

    @pl.when(c == 0)
    def _():
        for k in range(n_par):
            src = slice(0, n_seq) if shared_init else slice(k * n_seq, (k + 1) * n_seq)
            s_scr[k * n_seq:(k + 1) * n_seq] = s0_ref[src]
            for b in range(n_seq):
                ext_scr[k, b * stride:b * stride + SUBLANE, :] = prev_ref[src.start + b]

    cw = cw_ref[...]
    hp = hp_ref[...]
    row = lax.broadcasted_iota(I32, (C, 1), 0)
    ri = lax.broadcasted_iota(I32, (C, C), 0)
    ci = lax.broadcasted_iota(I32, (C, C), 1)
    same = (ri >> ls) == (ci >> ls)
    incl = same & (ri >= ci)
    strict = same & (ri > ci)
    eye = (ri == ci).astype(F32)

    q_l, k_l, kd_l, dec_l, lows_l, rhs_l, eg_l, z_l, g_l = [], [], [], [], [], [], [], [], []
    for kp in range(n_par):
        u = qkv_ref[kp]
        sm = sm_ref[kp]
        z = z_ref[kp]
        if c_in < C:
            u = jnp.concatenate([u, jnp.zeros((C - c_in, u.shape[1]), F32)], axis=0)
            sm = jnp.concatenate([sm, jnp.zeros((C - c_in, LANE), F32)], axis=0)
            z = jnp.concatenate([z, jnp.zeros((C - c_in, z.shape[1]), F32)], axis=0)
        pieces = []
        for b in range(n_seq):
            base = b * stride + SUBLANE
            ub_ = u[b * L:(b + 1) * L]
            ext_scr[kp, base:base + L, :] = ub_
            acc = ub_ * cw[CONV_W - 1:CONV_W, :]
            for i in range(CONV_W - 1):
                off = base - (CONV_W - 1) + i
                acc = acc + ext_scr[kp, off:off + L, :] * cw[i:i + 1, :]
            pieces.append(acc)
        conv = pieces[0] if n_seq == 1 else jnp.concatenate(pieces, axis=0)
        if n_seq == 1:
            tail = ext_scr[kp, c_in:c_in + SUBLANE, :]
            ext_scr[kp, 0:SUBLANE, :] = tail
        qkv = conv * jax.nn.sigmoid(conv)

        xs = sm + hp[1:2, :]
        softplus = jnp.maximum(xs, 0.0) + jnp.log1p(jnp.exp(-jnp.abs(xs)))
        la = -jnp.exp(hp[0:1, :]) * softplus
        beta = jax.nn.sigmoid(sm)
        if n_valid < C:
            valid = row < n_valid
            qkv = jnp.where(valid, qkv, 0.0)
            la = jnp.where(valid, la, 0.0)
            beta = jnp.where(valid, beta, 0.0)

        g = _mm_f32(incl.astype(F32), la)
        g_end = _mm_f32((ci == (ri | (L - 1))).astype(F32), g)
        gt = g.T
        g_l.append(g)
        for h in range(nh):
            col = SM_A + h
            gcol = g[:, col:col + 1]
            grow = gt[col:col + 1, :]
            bcol = beta[:, SM_B + h:SM_B + h + 1]
            q = qkv[:, h * hd:(h + 1) * hd]
            k = qkv[:, nh * hd + h * hd:nh * hd + (h + 1) * hd]
            v = qkv[:, 2 * nh * hd + h * hd:2 * nh * hd + (h + 1) * hd]
            q = q * lax.rsqrt(jnp.sum(q * q, axis=-1, keepdims=True) + RMS_EPS) * (hd ** -0.5)
            k = k * lax.rsqrt(jnp.sum(k * k, axis=-1, keepdims=True) + RMS_EPS)
            dec = jnp.exp(jnp.where(incl, gcol - grow, NEG_INF))
            eg = jnp.exp(gcol)
            q_l.append(q.astype(BF16))
            k_l.append(k.astype(BF16))
            kd_l.append((k * jnp.exp(g_end[:, col:col + 1] - gcol)).astype(BF16))
            dec_l.append(dec)
            lows_l.append(bcol * jnp.where(strict, dec, 0.0))
            rhs_l.append(jnp.concatenate([bcol * v, (bcol * eg) * k], axis=1))
            eg_l.append(eg)
            z_l.append(z[:, h * hd:(h + 1) * hd])
    qb = jnp.stack(q_l)
    kb = jnp.stack(k_l)
    kd = jnp.stack(kd_l)
    dec = jnp.stack(dec_l)
    eg = jnp.stack(eg_l)

    low = jnp.stack(lows_l) * _bnt(kb, kb)
    qk = _bnt(qb, kb)
    inv = eye[None] - jnp.where(_sibling_mask(ri, ci, 0)[None], low, 0.0)
    for lvl in range(1, ls):
        off = jnp.where(_sibling_mask(ri, ci, lvl)[None], low, 0.0)
        inv_b = inv.astype(BF16)
        inv = inv - _bmm(inv_b, _bmm(off.astype(BF16), inv_b).astype(BF16))
    uw = _bmm(inv.astype(BF16), jnp.stack(rhs_l).astype(BF16))
    w_b = uw[:, :, hd:].astype(BF16)

    def state(b):
        parts = [s_scr[kp * n_seq + b] for kp in range(n_par)]
        return parts[0] if n_par == 1 else jnp.concatenate(parts, axis=0)

    ws = None
    qs = None
    s_old = []
    for b in range(n_seq):
        S = state(b)
        s_old.append(S)
        Sb = S.astype(BF16)
        if n_seq == 1:
            wm, qm = w_b, qb
        else:
            mine = ((row >> ls) == b)[None]
            wm = jnp.where(mine, w_b, jnp.zeros_like(w_b))
            qm = jnp.where(mine, qb, jnp.zeros_like(qb))
        ws = _bmm(wm, Sb) if ws is None else ws + _bmm(wm, Sb)
        qs = _bmm(qm, Sb) if qs is None else qs + _bmm(qm, Sb)
    ub = (uw[:, :, :hd] - ws).astype(BF16)
    o = eg * qs + _bmm((dec * qk).astype(BF16), ub)
    for b in range(n_seq):
        last = b * L + L - 1
        glast = jnp.stack([g_l[kp][last:last + 1, SM_A + h:SM_A + h + 1] for kp in range(n_par) for h in range(nh)])
        kdm = kd if n_seq == 1 else jnp.where(((row >> ls) == b)[None], kd, jnp.zeros_like(kd))
        s_new = jnp.exp(glast) * s_old[b] + _btn(kdm, ub)
        for kp in range(n_par):
            s_scr[kp * n_seq + b] = s_new[kp * nh:(kp + 1) * nh]

    on = o * lax.rsqrt(jnp.mean(o * o, axis=-1, keepdims=True) + RMS_EPS) * ng_ref[...]
    zz = jnp.stack(z_l)
    res = on * (zz * jax.nn.sigmoid(zz))
    for kp in range(n_par):
        for h in range(nh):
            o_ref[kp, :, h * hd:(h + 1) * hd] = res[kp * nh + h, :c_in]

    @pl.when(c == pl.num_programs(1) - 1)
    def _():
        sfin_ref[...] = s_scr[...]


def _gdn(parts, prev, s0, cw, hp, ng, *, batch, n_chunks, c_in, n_valid, n_seq, n_par, shared_init):
    n = parts.shape[0]
    C = max(c_in, GDN_CHUNK)
    assert n_seq == 1 or (n_chunks == 1 and c_in == C)
    assert batch % n_par == 0
    L = C // n_seq
    n_init = n_seq if shared_init else n_par * n_seq
    init = (lambda b, c: (0, 0, 0)) if shared_init else (lambda b, c: (b, 0, 0))
    init4 = (lambda b, c: (0, 0, 0, 0)) if shared_init else (lambda b, c: (b, 0, 0, 0))
    qkv_w = 3 * GDN_HEADS * GDN_D
    v_w = GDN_HEADS * GDN_D
    rows = n_chunks * c_in
    parts4 = parts.reshape(batch // n_par, n_par, rows, parts.shape[1])
    o, s_fin = pl.pallas_call(
        functools.partial(_gdn_kernel, c_in=c_in, n_valid=n_valid, n_seq=n_seq, shared_init=shared_init),
        grid=(batch // n_par, n_chunks),
        in_specs=[
            pl.BlockSpec((None, n_par, c_in, qkv_w), lambda b, c: (b, 0, c, COL_QKV // qkv_w)),
            pl.BlockSpec((None, n_par, c_in, v_w), lambda b, c: (b, 0, c, COL_Z // v_w)),
            pl.BlockSpec((None, n_par, c_in, LANE), lambda b, c: (b, 0, c, COL_SM // LANE)),
            pl.BlockSpec((n_init, SUBLANE, qkv_w), init),
            pl.BlockSpec((n_init, GDN_HEADS, GDN_D, GDN_D), init4),
            pl.BlockSpec((CONV_W, qkv_w), lambda b, c: (0, 0)),
            pl.BlockSpec((2, LANE), lambda b, c: (0, 0)),
            pl.BlockSpec((1, GDN_D), lambda b, c: (0, 0)),
        ],
        out_specs=[
            pl.BlockSpec((None, n_par, c_in, v_w), lambda b, c: (b, 0, c, 0)),
            pl.BlockSpec((n_par * n_seq, GDN_HEADS, GDN_D, GDN_D), lambda b, c: (b, 0, 0, 0)),
        ],
        out_shape=[jax.ShapeDtypeStruct((batch // n_par, n_par, rows, v_w), F32),
                   jax.ShapeDtypeStruct((batch * n_seq, GDN_HEADS, GDN_D, GDN_D), F32)],
        scratch_shapes=[pltpu.VMEM((n_par * n_seq, GDN_HEADS, GDN_D, GDN_D), F32),
                        pltpu.VMEM((n_par, n_seq * (SUBLANE + L), qkv_w), F32)],
        compiler_params=_cparams(("arbitrary", "arbitrary")),
        name="gated_deltanet",
    )(parts4, parts4, parts4, prev, s0, cw, hp, ng)
    return o.reshape(n, v_w), s_fin


def _sort_key(x):
    b = pltpu.bitcast(x + 0.0, I32)
    return b ^ ((b >> 31) & 0x7FFFFFFF)


KEY_NEG_INF = -(2 ** 31) + 0x7FFFFF
SEARCH_BITS_PER_TRIP = 4


def _count_ge(key_tiles, cand):
    tot = None
    for kt in key_tiles:
        c = jnp.sum((kt >= cand).astype(F32), axis=1, keepdims=True)
        tot = c if tot is None else tot + c
    return tot


def _rows8(x):
    return x.reshape(x.shape[0] // SUBLANE, SUBLANE, x.shape[1])


def _sum_rows8(x, chains=4):
    rows = x.shape[0]
    if rows % (chains * SUBLANE):
        return jnp.sum(_rows8(x), axis=0)
    part = rows // chains
    sums = [jnp.sum(_rows8(x[k * part:(k + 1) * part]), axis=0) for k in range(chains)]
    while len(sums) > 1:
        sums = [a + b for a, b in zip(sums[0::2], sums[1::2])]
    return sums[0]


def _attn_prompt_kernel(q_ref, iq_ref, smq_ref, k_ref, v_ref, smk_ref, mk_ref, mv_ref, msm_ref,
                        o_ref, kbf, vtb, kibf, key_scr, bias_scr, s_scr, cut_scr, *, n_sel, n_meta, kt):
    tq = q_ref.shape[0]
    n_real = k_ref.shape[0]
    kvw = k_ref.shape[1]
    qb_i = pl.program_id(1)
    nt = (qb_i * tq + tq + kt - 1) // kt
    group = ATT_HEADS // ATT_KV_HEADS

    @pl.when(qb_i == 0)
    def _():
        zrows = LANE - n_meta
        kbf[0:n_real, :] = k_ref[...].astype(BF16)
        kbf[n_real:, :] = jnp.concatenate([mk_ref[...], jnp.zeros((zrows, kvw), F32)], axis=0).astype(BF16)
        vtb[:, 0:n_real] = v_ref[...].T.astype(BF16)
        vtb[:, n_real:] = jnp.concatenate([mv_ref[...], jnp.zeros((zrows, kvw), F32)], axis=0).T.astype(BF16)
        kibf[0:n_real, :] = smk_ref[...][:, SM_IK:SM_IK + IDX_DIM].astype(BF16)
        kibf[n_real:, :] = jnp.concatenate(
            [msm_ref[...][:, SM_IK:SM_IK + IDX_DIM], jnp.zeros((zrows, IDX_DIM), F32)], axis=0).astype(BF16)

    iq = iq_ref[...]
    iq_stack = jnp.concatenate([iq[:, h * IDX_DIM:(h + 1) * IDX_DIM] for h in range(IDX_HEADS)], axis=0).astype(BF16)
    w_t = smq_ref[...].T * ((IDX_DIM ** -0.5) * (IDX_HEADS ** -0.5))
    qpos = qb_i * tq + lax.broadcasted_iota(I32, (1, tq), 1)

    def index_scores(start, width):
        s_all = _nt(kibf[pl.ds(start, width), :], iq_stack)
        acc = jnp.zeros((width, tq), F32)
        for h in range(IDX_HEADS):
            acc = acc + jnp.maximum(s_all[:, h * tq:(h + 1) * tq], 0.0) * w_t[SM_IW + h:SM_IW + h + 1, :]
        return acc

    def real_tile(t, carry):
        start = pl.multiple_of(t * kt, kt)
        kpos = start + lax.broadcasted_iota(I32, (kt, 1), 0)
        key_scr[pl.ds(start, kt), :] = _sort_key(jnp.where(kpos <= qpos, index_scores(start, kt), NEG_INF))
        return carry

    lax.fori_loop(0, nt, real_tile, 0)
    mrow = lax.broadcasted_iota(I32, (LANE, 1), 0)
    key_scr[n_real:, :] = _sort_key(jnp.where(mrow < n_meta, index_scores(n_real, LANE), NEG_INF))

    def count_ge(cand):
        def body(t, acc):
            kk = key_scr[pl.ds(pl.multiple_of(t * kt, kt), kt), :]
            return acc + _sum_rows8((kk >= cand).astype(F32))

        acc = lax.fori_loop(0, nt, body, jnp.zeros((SUBLANE, tq), F32))
        acc = acc + jnp.sum(_rows8((key_scr[n_real:, :] >= cand).astype(F32)), axis=0)
        return jnp.sum(acc, axis=0, keepdims=True)

    few = qpos + 1 + n_meta <= n_sel

    def unsettled(state):
        it, _, cnt = state
        return (it < 32) & (jnp.max(jnp.where(few | (cnt == n_sel), 0.0, 1.0)) > 0.0)

    def refine(state):
        it, thr, cnt = state
        for step in range(SEARCH_BITS_PER_TRIP):
            cand = thr + lax.shift_left(jnp.int32(1), 31 - step - it)
            c = count_ge(cand)
            ok = c >= n_sel
            thr = jnp.where(ok, cand, thr)
            cnt = jnp.where(ok, c, cnt)
        return it + SEARCH_BITS_PER_TRIP, thr, cnt

    n_all = (nt * kt + LANE).astype(F32)
    _, thr, cnt = lax.while_loop(
        unsettled, refine, (jnp.int32(0), jnp.full((1, tq), INT_MIN, I32), jnp.full((1, tq), 1.0, F32) * n_all))

    tie = (cnt > n_sel) & (thr > KEY_NEG_INF) & jnp.logical_not(few)
    n_pos = n_real + n_meta
    cut_scr[...] = jnp.full((SUBLANE, tq), float(n_pos), F32)

    def pos_of(start, width, is_meta):
        r = lax.broadcasted_iota(I32, (width, 1), 0)
        return r if is_meta else n_meta + start + r

    @pl.when(jnp.max(tie.astype(F32)) > 0.0)
    def _tie_cut():
        need = n_sel - count_ge(thr + 1)
        nbits = max(1, (n_pos - 1).bit_length())

        def count_eq_le(cut):
            def body(t, acc):
                start = pl.multiple_of(t * kt, kt)
                kk = key_scr[pl.ds(start, kt), :]
                return acc + jnp.sum(_rows8(((kk == thr) & (pos_of(start, kt, False) <= cut)).astype(F32)), axis=0)

            acc = lax.fori_loop(0, nt, body, jnp.zeros((SUBLANE, tq), F32))
            mk = key_scr[n_real:, :]
            acc = acc + jnp.sum(_rows8(((mk == thr) & (pos_of(0, LANE, True) <= cut)).astype(F32)), axis=0)
            return jnp.sum(acc, axis=0, keepdims=True)

        def body(it, cut):
            cand = cut - lax.shift_left(jnp.int32(1), nbits - 1 - it)
            ok = (cand >= 0) & (count_eq_le(cand) >= need)
            return jnp.where(ok, cand, cut)

        cut = lax.fori_loop(0, nbits, body, jnp.full((1, tq), (1 << nbits) - 1, I32))
        cut = jnp.where(tie, cut, n_pos).astype(F32)
        cut_scr[...] = jnp.broadcast_to(cut, (SUBLANE, tq))

    cut = cut_scr[0:1, :].astype(I32)

    def bias_tile(start, width, is_meta):
        kk = key_scr[pl.ds(start, width), :]
        sel = (kk > thr) | ((kk == thr) & (pos_of(start, width, is_meta) <= cut))
        sel = sel & (kk > KEY_NEG_INF)
        bias_scr[pl.ds(start, width), :] = jnp.where(sel, 0.0, NEG_INF)

    def bias_body(t, carry):
        bias_tile(pl.multiple_of(t * kt, kt), kt, False)
        return carry

    lax.fori_loop(0, nt, bias_body, 0)
    bias_tile(n_real, LANE, True)

    scale = ATT_DH ** -0.5
    for g in range(ATT_KV_HEADS):
        q4 = jnp.concatenate([q_ref[:, (g * group + j) * ATT_DH:(g * group + j + 1) * ATT_DH] for j in range(group)],
                             axis=0).astype(BF16)

        def score_tile(start, width):
            s = _nt(kbf[pl.ds(start, width), g * ATT_DH:(g + 1) * ATT_DH], q4) * scale
            b = bias_scr[pl.ds(start, width), :]
            s = s + jnp.concatenate([b] * group, axis=1)
            s_scr[pl.ds(start, width), :] = s
            return jnp.max(_rows8(s), axis=0)

        def pass1(t, m):
            return jnp.maximum(m, score_tile(pl.multiple_of(t * kt, kt), kt))

        m = jnp.max(lax.fori_loop(0, nt, pass1, score_tile(n_real, LANE)), axis=0, keepdims=True)

        def prob_tile(start, width):
            p = jnp.exp(s_scr[pl.ds(start, width), :] - m)
            pv = _mm(vtb[g * ATT_DH:(g + 1) * ATT_DH, pl.ds(start, width)], p.astype(BF16))
            return _sum_rows8(p), pv

        def pass2(t, carry):
            l, acc = carry
            dl, dacc = prob_tile(pl.multiple_of(t * kt, kt), kt)
            return l + dl, acc + dacc

        l, acc = lax.fori_loop(0, nt, pass2, prob_tile(n_real, LANE))
        out_t = acc / jnp.sum(l, axis=0, keepdims=True)
        for j in range(group):
            hh = g * group + j
            o_ref[:, hh * ATT_DH:(hh + 1) * ATT_DH] = out_t[:, j * tq:(j + 1) * tq].T


def _attn_prompt(parts, mparts, *, batch, seq, n_meta, tq, kt):
    nqb = seq // tq
    n_sel = min(TOPK_MAX, (seq + n_meta) // 4)
    kvw = ATT_KV_HEADS * ATT_DH
    nk = seq + LANE
    return pl.pallas_call(
        functools.partial(_attn_prompt_kernel, n_sel=n_sel, n_meta=n_meta, kt=kt),
        grid=(batch, nqb),
        in_specs=[
            pl.BlockSpec((tq, D), lambda b, i: (b * nqb + i, COL_AQ // D)),
            pl.BlockSpec((tq, D), lambda b, i: (b * nqb + i, COL_IQ // D)),
            pl.BlockSpec((tq, LANE), lambda b, i: (b * nqb + i, COL_SM // LANE)),
            pl.BlockSpec((seq, kvw), lambda b, i: (b, COL_AK // kvw)),
            pl.BlockSpec((seq, kvw), lambda b, i: (b, COL_AV // kvw)),
            pl.BlockSpec((seq, LANE), lambda b, i: (b, COL_SM // LANE)),
            pl.BlockSpec((n_meta, kvw), lambda b, i: (0, COL_AK // kvw)),
            pl.BlockSpec((n_meta, kvw), lambda b, i: (0, COL_AV // kvw)),
            pl.BlockSpec((n_meta, LANE), lambda b, i: (0, COL_SM // LANE)),
        ],
        out_specs=pl.BlockSpec((tq, D), lambda b, i: (b * nqb + i, 0)),
        out_shape=jax.ShapeDtypeStruct((batch * seq, D), F32),
        scratch_shapes=[pltpu.VMEM((nk, kvw), BF16), pltpu.VMEM((kvw, nk), BF16), pltpu.VMEM((nk, IDX_DIM), BF16),
                        pltpu.VMEM((nk, tq), I32), pltpu.VMEM((nk, tq), F32),
                        pltpu.VMEM((nk, (ATT_HEADS // ATT_KV_HEADS) * tq), F32), pltpu.VMEM((SUBLANE, tq), F32)],
        compiler_params=_cparams(("arbitrary", "arbitrary")),
        name="sparse_attention_prompt",
    )(parts, parts, parts, parts, parts, parts, mparts, mparts, mparts)


def _attn_sample_kernel(pt_ref, *refs, n_pages, n_sel):
    kp = refs[0:n_pages]
    vp = refs[n_pages:2 * n_pages]
    ip = refs[2 * n_pages:3 * n_pages]
    q_ref, iq_ref, kn_ref, vn_ref, sm_ref, o_ref, kil, vil, kit, cut_scr = refs[3 * n_pages:]
    del pt_ref
    t_new = q_ref.shape[0]
    page = ip[0].shape[1]
    past = n_pages * page
    nkv = ATT_KV_HEADS
    for j in range(n_pages):
        kil[j * nkv * page:(j + 1) * nkv * page, :] = kp[j][...].astype(BF16)
        vil[j * nkv * page:(j + 1) * nkv * page, :] = vp[j][...].astype(BF16)
        kit[:, j * page:(j + 1) * page] = ip[j][...].astype(BF16)
    sm = sm_ref[...]
    zpad = jnp.zeros((LANE - t_new, nkv * ATT_DH), BF16)
    knew = jnp.concatenate([kn_ref[...].astype(BF16), zpad], axis=0)
    vnew = jnp.concatenate([vn_ref[...].astype(BF16), zpad], axis=0)
    ki_new = jnp.concatenate(
        [sm[:, SM_IK:SM_IK + IDX_DIM].astype(BF16), jnp.zeros((LANE - t_new, IDX_DIM), BF16)], axis=0)

    iq = iq_ref[...]
    qi = jnp.concatenate([iq[:, h * IDX_DIM:(h + 1) * IDX_DIM] for h in range(IDX_HEADS)], axis=0).astype(BF16)
    wcol = jnp.concatenate([sm[:, SM_IW + h:SM_IW + h + 1] for h in range(IDX_HEADS)], axis=0)
    wcol = wcol * ((IDX_DIM ** -0.5) * (IDX_HEADS ** -0.5))

    def idx_scores(qk):
        s = jnp.maximum(qk, 0.0) * wcol
        acc = s[0:t_new]
        for h in range(1, IDX_HEADS):
            acc = acc + s[h * t_new:(h + 1) * t_new]
        return acc

    i_past = idx_scores(_mm(qi, kit[...]))
    i_new = idx_scores(_nt(qi, ki_new))
    trow = lax.broadcasted_iota(I32, (t_new, LANE), 0)
    tlane = lax.broadcasted_iota(I32, (t_new, LANE), 1)
    key_past = _sort_key(i_past)
    key_new = _sort_key(jnp.where(tlane <= trow, i_new, NEG_INF))
    tiles = [key_past, key_new]

    def digit_body(it, thr):
        step = lax.shift_left(jnp.int32(1), 28 - 4 * it)
        digit = jnp.zeros((t_new, 1), I32)
        for j in range(1, 16):
            digit = digit + (_count_ge(tiles, thr + j * step) >= n_sel).astype(I32)
        return thr + digit * step

    thr = lax.fori_loop(0, 8, digit_body, jnp.full((t_new, 1), INT_MIN, I32))
    n_ge = _count_ge(tiles, thr)
    tie = (n_ge > n_sel) & (thr > KEY_NEG_INF)
    pos_past = lax.broadcasted_iota(I32, (1, past), 1)
    pos_new = past + lax.broadcasted_iota(I32, (1, LANE), 1)
    n_pos = past + t_new
    cut_scr[...] = jnp.full(cut_scr.shape, n_pos, I32)

    @pl.when(jnp.max(tie.astype(F32)) > 0.0)
    def _tie_cut():
        need = n_sel - _count_ge(tiles, thr + 1)
        nbits = max(1, (n_pos - 1).bit_length())

        def cut_body(it, cut):
            cand = cut - lax.shift_left(jnp.int32(1), nbits - 1 - it)
            cnt = (jnp.sum(((key_past == thr) & (pos_past <= cand)).astype(F32), axis=1, keepdims=True)
                   + jnp.sum(((key_new == thr) & (pos_new <= cand)).astype(F32), axis=1, keepdims=True))
            ok = (cand >= 0) & (cnt >= need)
            return jnp.where(ok, cand, cut)

        cut = lax.fori_loop(0, nbits, cut_body, jnp.full((t_new, 1), (1 << nbits) - 1, I32))
        cut_scr[...] = jnp.broadcast_to(jnp.where(tie, cut, n_pos), cut_scr.shape)

    cut = cut_scr[:, 0:1]

    def chosen(kk, pos):
        return ((kk > thr) | ((kk == thr) & (pos <= cut))) & (kk > KEY_NEG_INF)

    group = ATT_HEADS // ATT_KV_HEADS
    b_new = jnp.concatenate([jnp.where(chosen(key_new, pos_new), 0.0, NEG_INF)] * group, axis=0)
    sel01 = chosen(key_past, pos_past).astype(BF16)
    er = lax.broadcasted_iota(I32, (page, nkv * page), 0)
    ec = lax.broadcasted_iota(I32, (page, nkv * page), 1)
    expand = ((ec >= er * nkv) & (ec < (er + 1) * nkv)).astype(BF16)
    sel_il = jnp.concatenate([_mm(sel01[:, j * page:(j + 1) * page], expand) for j in range(n_pages)], axis=1)
    il_lane = lax.broadcasted_iota(I32, (1, nkv * past), 1)
    assert nkv & (nkv - 1) == 0
    il_head = il_lane & (nkv - 1)
    scale = ATT_DH ** -0.5
    q = q_ref[...]
    for g in range(ATT_KV_HEADS):
        qg = jnp.concatenate(
            [q[:, (g * group + j) * ATT_DH:(g * group + j + 1) * ATT_DH] for j in range(group)], axis=0).astype(BF16)
        b_g = jnp.where((sel_il > 0.5) & (il_head == g), 0.0, NEG_INF)
        s_p = _nt(qg, kil[...]) * scale + jnp.concatenate([b_g] * group, axis=0)
        s_n = _nt(qg, knew[:, g * ATT_DH:(g + 1) * ATT_DH]) * scale + b_new
        m = jnp.maximum(jnp.max(s_p, axis=1, keepdims=True), jnp.max(s_n, axis=1, keepdims=True))
        p_p = jnp.exp(s_p - m)
        p_n = jnp.exp(s_n - m)
        l = jnp.sum(p_p, axis=1, keepdims=True) + jnp.sum(p_n, axis=1, keepdims=True)
        acc = _mm(p_p.astype(BF16), vil[...]) + _mm(p_n.astype(BF16), vnew[:, g * ATT_DH:(g + 1) * ATT_DH])
        res = acc / l
        for j in range(group):
            hh = g * group + j
            o_ref[:, hh * ATT_DH:(hh + 1) * ATT_DH] = res[j * t_new:(j + 1) * t_new]


def _attn_sample(parts, cache_k, cache_v, cache_ik, page_table, *, t_new):
    db, n_pages = page_table.shape
    page = cache_k.shape[2]
    kvw = ATT_KV_HEADS * ATT_DH
    past = n_pages * page
    n_sel = min(TOPK_MAX, (past + t_new) // 4)

    n_pool = cache_k.shape[1]
    cache_k = cache_k.reshape(n_pool, page * ATT_KV_HEADS, ATT_DH)
    cache_v = cache_v.reshape(n_pool, page * ATT_KV_HEADS, ATT_DH)
    cache_ik = jnp.swapaxes(cache_ik, 2, 3).reshape(n_pool, IDX_DIM, page)

    def kv_page(j):
        return pl.BlockSpec((None, page * ATT_KV_HEADS, ATT_DH), lambda b, pt, j=j: (pt[b, j], 0, 0))

    def ik_page(j):
        return pl.BlockSpec((None, IDX_DIM, page), lambda b, pt, j=j: (pt[b, j], 0, 0))

    in_specs = ([kv_page(j) for j in range(n_pages)] + [kv_page(j) for j in range(n_pages)]
                + [ik_page(j) for j in range(n_pages)]
                + [pl.BlockSpec((t_new, D), lambda b, pt: (b, COL_AQ // D)),
                   pl.BlockSpec((t_new, D), lambda b, pt: (b, COL_IQ // D)),
                   pl.BlockSpec((t_new, kvw), lambda b, pt: (b, COL_AK // kvw)),
                   pl.BlockSpec((t_new, kvw), lambda b, pt: (b, COL_AV // kvw)),
                   pl.BlockSpec((t_new, LANE), lambda b, pt: (b, COL_SM // LANE))])
    grid_spec = pltpu.PrefetchScalarGridSpec(
        num_scalar_prefetch=1, grid=(db,), in_specs=in_specs,
        out_specs=pl.BlockSpec((t_new, D), lambda b, pt: (b, 0)),
        scratch_shapes=[pltpu.VMEM((past * ATT_KV_HEADS, ATT_DH), BF16), pltpu.VMEM((past * ATT_KV_HEADS, ATT_DH), BF16),
                        pltpu.VMEM((IDX_DIM, past), BF16), pltpu.VMEM((t_new, LANE), I32)])
    return pl.pallas_call(
        functools.partial(_attn_sample_kernel, n_pages=n_pages, n_sel=n_sel),
        grid_spec=grid_spec,
        out_shape=jax.ShapeDtypeStruct((db * t_new, D), F32),
        compiler_params=_cparams(("arbitrary",)),
        name="sparse_attention_sample",
    )(page_table, *([cache_k] * n_pages), *([cache_v] * n_pages), *([cache_ik] * n_pages),
      parts, parts, parts, parts, parts)


def _layer_norm(x, g, b):
    mu = jnp.mean(x, axis=-1, keepdims=True)
    xc = x - mu
    var = jnp.mean(xc * xc, axis=-1, keepdims=True)
    return xc * lax.rsqrt(var + LN_EPS) * g + b


def _finish_kernel(x_ref, og_ref, oa_ref, ga_ref, gb_ref, wbg_ref, wba_ref, wo_ref, g_ref, b_ref, h_ref, *, alpha):
    a = _mm(og_ref[...].astype(BF16), wbg_ref[...])
    b = _mm(oa_ref[...].astype(BF16), wba_ref[...])
    merged = jax.nn.sigmoid(ga_ref[...]) * a + jax.nn.sigmoid(gb_ref[...]) * b
    y = alpha * x_ref[...] + _mm(merged.astype(BF16), wo_ref[...])
    h_ref[...] = _layer_norm(y, g_ref[...], b_ref[...])


def _finish(x, o_gdn, o_att, parts, wbg, wba, wo, g, b, *, alpha, tm):
    n = x.shape[0]
    tm = min(tm, n)
    row = lambda i: (i, 0)
    full = lambda i: (0, 0)
    return pl.pallas_call(
        functools.partial(_finish_kernel, alpha=alpha),
        grid=(n // tm,),
        in_specs=[pl.BlockSpec((tm, D), row), pl.BlockSpec((tm, D), row), pl.BlockSpec((tm, D), row),
                  pl.BlockSpec((tm, D), lambda i: (i, COL_GA // D)), pl.BlockSpec((tm, D), lambda i: (i, COL_GB // D)),
                  pl.BlockSpec((D, D), full), pl.BlockSpec((D, D), full), pl.BlockSpec((D, D), full),
                  pl.BlockSpec((1, D), full), pl.BlockSpec((1, D), full)],
        out_specs=pl.BlockSpec((tm, D), row),
        out_shape=jax.ShapeDtypeStruct((n, D), F32),
        compiler_params=_cparams(("arbitrary",)),
        name="merge_layernorm",
    )(x, o_gdn, o_att, parts, parts, wbg, wba, wo, g, b)


_CAND = [(r0, r1) for r0 in range(PEER_TOPK) for r1 in range(PEER_TOPK) if (r0 + 1) * (r1 + 1) <= PEER_TOPK]
_CAND_OFF = [next(i for i, c in enumerate(_CAND) if c[0] == r0) for r0 in range(PEER_TOPK)]
_CAND_LEN = [sum(1 for c in _CAND if c[0] == r0) for r0 in range(PEER_TOPK)]
_CAND_ROWS = -(-len(_CAND) // SUBLANE) * SUBLANE


def _top_rows(s, n_top, break_ties):
    rows, cols = s.shape
    iota = lax.broadcasted_iota(I32, (rows, cols), 0).astype(F32)
    rank = jnp.full((rows, cols), float(n_top), F32)
    vals = []
    for r in range(n_top):
        m = jnp.max(s, axis=0, keepdims=True)
        hit = s == m
        if break_ties:
            hit = iota == jnp.min(jnp.where(hit, iota, float(rows)), axis=0, keepdims=True)
        vals.append(m)
        s = jnp.where(hit, NEG_INF, s)
        rank = jnp.where(hit, float(r), rank)
    n_ranked = jnp.sum((rank < float(n_top)).astype(F32), axis=0, keepdims=True)
    return jnp.concatenate(vals, axis=0), rank, n_ranked == float(n_top)


def _peer_kernel(h_ref, wq_ref, sk_ref, u0_ref, un_ref, vp_ref, vl_ref, g_ref, b_ref, y_ref,
                 hb_scr, q_scr, rank1_scr, bt_scr, nt_scr, at_scr, s1_scr, s_scr, p_scr, acc_scr, *, alpha, n_steps):
    tt = h_ref.shape[0]
    eb = un_ref.shape[0] // 2
    e = pl.program_id(1)
    nk = PEER_NKEYS
    kk = PEER_TOPK
    tchunks = tt // LANE

    @pl.when(e == 0)
    def _prep():
        hb = h_ref[...].astype(BF16)
        hb_scr[...] = hb
        q_scr[...] = _mm(hb, wq_ref[...]).astype(BF16)
        acc_scr[...] = jnp.zeros(acc_scr.shape, F32)

        def scores(hd, carry):
            q0 = q_scr[:, pl.ds(pl.multiple_of(hd * 2 * nk, nk), nk)]
            q1 = q_scr[:, pl.ds(pl.multiple_of(hd * 2 * nk + nk, nk), nk)]
            at_scr[hd] = _nt(sk_ref[hd, 0], q0)
            s1_scr[hd] = _nt(sk_ref[hd, 1], q1)
            return carry

        lax.fori_loop(0, PEER_HEADS, scores, 0)

        def select_chunk(hd, c0, s0, s1, break_ties):
            a, rank0, ok0 = _top_rows(s0, kk, break_ties)
            b, rank1, ok1 = _top_rows(s1, kk, break_ties)
            pad = [jnp.full((_CAND_ROWS - len(_CAND), LANE), NEG_INF, F32)]
            cand = jnp.concatenate([a[r0:r0 + 1] + b[r1:r1 + 1] for r0, r1 in _CAND] + pad, axis=0)
            top, crank, okc = _top_rows(cand, kk, break_ties)
            zsum = jnp.sum(jnp.exp(top - top[0:1]), axis=0, keepdims=True)
            chosen = (crank < float(kk)).astype(F32)
            crow = lax.broadcasted_iota(I32, (_CAND_ROWS, 1), 0)
            nsel = jnp.zeros((nk, LANE), F32)
            for r0 in range(kk):
                in_group = (crow >= _CAND_OFF[r0]) & (crow < _CAND_OFF[r0] + _CAND_LEN[r0])
                cnt = jnp.sum(jnp.where(in_group, chosen, 0.0), axis=0, keepdims=True)
                nsel = nsel + jnp.where(rank0 == float(r0), cnt, 0.0)
            rank1_scr[hd, :, pl.ds(c0, LANE)] = rank1.astype(BF16)
            nt_scr[hd, :, pl.ds(c0, LANE)] = nsel
            at_scr[hd, :, pl.ds(c0, LANE)] = jnp.exp(s0 - a[0:1])
            bt_scr[hd, :, pl.ds(c0, LANE)] = (jnp.exp(s1 - b[0:1]) / zsum).astype(BF16)
            return jnp.min((ok0 & ok1 & okc).astype(F32)) > 0.5

        pair = 2 if tchunks % 2 == 0 else 1

        def select(idx, carry):
            hd = idx // (tchunks // pair)
            first = (idx % (tchunks // pair)) * pair
            tie_free = None
            chunks = []
            for k in range(pair):
                c0 = pl.multiple_of((first + k) * LANE, LANE)
                s0 = at_scr[hd, :, pl.ds(c0, LANE)]
                s1 = s1_scr[hd, :, pl.ds(c0, LANE)]
                chunks.append((c0, s0, s1))
            for c0, s0, s1 in chunks:
                ok = select_chunk(hd, c0, s0, s1, False)
                tie_free = ok if tie_free is None else tie_free & ok

            @pl.when(jnp.logical_not(tie_free))
            def _():
                for c0, s0, s1 in chunks:
                    select_chunk(hd, c0, s0, s1, True)

            return carry

        lax.fori_loop(0, PEER_HEADS * tchunks // pair, select, 0)

        s_scr[1] = _nt(u0_ref[...], hb_scr[...])
        p_scr[1] = jnp.zeros(p_scr.shape[1:], BF16)

    groups = eb // nk

    assert 2 * groups == SUBLANE

    def weights(half, s_ref, p_ref):
        i0 = pl.multiple_of(e * SUBLANE, SUBLANE)
        zero = jnp.zeros((nk, LANE), BF16)

        for tc in range(tchunks):
            cols = slice(tc * LANE, (tc + 1) * LANE)
            gates = [zero] * groups
            for hd in range(PEER_HEADS):
                nblk = nt_scr[hd, pl.ds(i0, SUBLANE), cols]
                ablk = at_scr[hd, pl.ds(i0, SUBLANE), cols]
                rank1 = rank1_scr[hd, :, cols]
                bval = bt_scr[hd, :, cols]
                for ii in range(groups):
                    r = half * groups + ii
                    nrow = nblk[r:r + 1].astype(BF16)
                    arow = ablk[r:r + 1].astype(BF16)
                    gates[ii] = gates[ii] + jnp.where(rank1 < nrow, bval, zero) * arow
            for ii in range(groups):
                rows = slice(ii * nk, (ii + 1) * nk)
                s = s_ref[rows, cols]
                act = 0.5 * s * (1.0 + lax.erf(s * (2.0 ** -0.5)))
                p_ref[rows, cols] = gates[ii] * act.astype(BF16)

    def step(cur, nxt):
        s_scr[nxt] = _nt(un_ref[...], hb_scr[...])
        for half in range(2):
            rows = pl.ds(half * eb, eb)
            weights(half, s_scr.at[cur, rows], p_scr.at[nxt, rows])
        acc_scr[...] += _tn(p_scr[cur], vp_ref[...])

    @pl.when(e % 2 == 0)
    def _():
        step(1, 0)

    @pl.when(e % 2 == 1)
    def _():
        step(0, 1)

    last_slot = (n_steps - 1) % 2

    @pl.when(e == n_steps - 1)
    def _():
        y = alpha * h_ref[...] + (acc_scr[...] + _tn(p_scr[last_slot], vl_ref[...]))
        y_ref[...] = _layer_norm(y, g_ref[...], b_ref[...])


def _peer(h, wq, sk, u, v, g, b, *, alpha, tt, eb):
    n = h.shape[0]
    n_exp = u.shape[0]
    nk = PEER_NKEYS
    nb = n_exp // eb
    return pl.pallas_call(
        functools.partial(_peer_kernel, alpha=alpha, n_steps=nb),
        grid=(n // tt, nb),
        in_specs=[pl.BlockSpec((tt, D), lambda t, e: (t, 0)),
                  pl.BlockSpec(wq.shape, lambda t, e: (0, 0)),
                  pl.BlockSpec(sk.shape, lambda t, e: (0, 0, 0, 0)),
                  pl.BlockSpec((eb, D), lambda t, e: (0, 0)),
                  pl.BlockSpec((eb, D), lambda t, e: (jnp.minimum(e + 1, nb - 1), 0)),
                  pl.BlockSpec((eb, D), lambda t, e: (jnp.maximum(e - 1, 0), 0)),
                  pl.BlockSpec((eb, D), lambda t, e: (nb - 1, 0)),
                  pl.BlockSpec((1, D), lambda t, e: (0, 0)),
                  pl.BlockSpec((1, D), lambda t, e: (0, 0))],
        out_specs=pl.BlockSpec((tt, D), lambda t, e: (t, 0)),
        out_shape=jax.ShapeDtypeStruct((n, D), F32),
        scratch_shapes=[pltpu.VMEM((tt, D), BF16), pltpu.VMEM((tt, wq.shape[1]), BF16),
                        pltpu.VMEM((PEER_HEADS, nk, tt), BF16), pltpu.VMEM((PEER_HEADS, nk, tt), BF16),
                        pltpu.VMEM((PEER_HEADS, nk, tt), F32), pltpu.VMEM((PEER_HEADS, nk, tt), F32),
                        pltpu.VMEM((PEER_HEADS, nk, tt), F32),
                        pltpu.VMEM((2, eb, tt), F32), pltpu.VMEM((2, eb, tt), BF16), pltpu.VMEM((tt, D), F32)],
        compiler_params=_cparams(("arbitrary", "arbitrary")),
        name="peer_ffn",
    )(h, wq, sk, u, u, v, v, g, b)


class _Tiles(NamedTuple):
    rows: int
    attn_q: int
    attn_k: int
    peer_experts: int
    gdn_groups: int


def _tiles(batch, seq):
    rows = 512 if (batch * seq) % 512 == 0 else 256
    groups = 4 if batch % 4 == 0 else (2 if batch % 2 == 0 else 1)
    return _Tiles(rows=rows, attn_q=256 if seq % 256 == 0 else 128, attn_k=min(512, seq),
                  peer_experts=SUBLANE * PEER_NKEYS, gdn_groups=groups)


def _permute_w_in(w):
    gq = 3 * GDN_HEADS * GDN_D
    gv = GDN_HEADS * GDN_D
    aq = ATT_HEADS * ATT_DH
    akv = ATT_KV_HEADS * ATT_DH
    iqw = IDX_HEADS * IDX_DIM
    sizes = (gq, gv, GDN_HEADS, GDN_HEADS, aq, akv, akv, iqw, IDX_DIM, IDX_HEADS, D, D)
    offs = [0]
    for s in sizes:
        offs.append(offs[-1] + s)
    seg = [w[:, offs[i]:offs[i + 1]] for i in range(len(sizes))]
    qkv, z, b, a, q, k, v, iq, ik, iw, ga, gb = seg
    pad = jnp.zeros((w.shape[0], LANE - IDX_DIM - 2 * GDN_HEADS - IDX_HEADS), w.dtype)
    return jnp.concatenate([qkv, z, q, iq, ga, gb, k, v, ik, b, a, iw, pad], axis=1).astype(BF16)


def kernel(x_prompt, x_sample, cache_k, cache_v, cache_idx_k, state_conv, state_delta, page_table, meta_tokens, w_in, conv_w, a_log, dt_bias, gdn_norm_g, w_branch_gdn, w_branch_attn, w_out, ln1_g, ln1_b, peer_wq, peer_subkeys, peer_u, peer_v, ln2_g, ln2_b):
    depth = w_in.shape[0]
    assert depth == 1, "single-layer step"
    batch, seq, d = x_prompt.shape
    db, t_new, _ = x_sample.shape
    n_meta = meta_tokens.shape[0]
    assert d == D and seq % GDN_CHUNK == 0 and n_meta % SUBLANE == 0 and n_meta <= GDN_CHUNK
    alpha = (2 * depth) ** 0.25
    qkv_w = 3 * GDN_HEADS * GDN_D
    kvw = ATT_KV_HEADS * ATT_DH

    tiles = _tiles(batch, seq)
    w_r = _permute_w_in(w_in[0])
    xp = x_prompt.reshape(batch * seq, D)
    xs = x_sample.reshape(db * t_new, D)
    parts_p = _project(xp, w_r, tiles.rows)
    parts_s = _project(xs, w_r, min(tiles.rows, db * t_new))
    parts_m = _project(meta_tokens.astype(F32), w_r, n_meta)

    hp = jnp.zeros((2, LANE), F32).at[0, SM_A:SM_A + GDN_HEADS].set(a_log[0]).at[1, SM_A:SM_A + GDN_HEADS].set(dt_bias[0])
    cw = conv_w[0]
    ng = gdn_norm_g[0].reshape(1, GDN_D)
    zero_prev = jnp.zeros((1, SUBLANE, qkv_w), F32)
    zero_state = jnp.zeros((1, GDN_HEADS, GDN_D, GDN_D), F32)
    _, s_meta = _gdn(parts_m, zero_prev, zero_state, cw, hp, ng, batch=1, n_chunks=1, c_in=n_meta, n_valid=n_meta,
                     n_seq=1, n_par=1, shared_init=True)
    meta_prev = parts_m[n_meta - SUBLANE:, :qkv_w].reshape(1, SUBLANE, qkv_w)
    og_p, p_delta = _gdn(parts_p, meta_prev, s_meta, cw, hp, ng, batch=batch, n_chunks=seq // GDN_CHUNK,
                         c_in=GDN_CHUNK, n_valid=GDN_CHUNK, n_seq=1, n_par=tiles.gdn_groups, shared_init=True)
    per = GDN_CHUNK // t_new
    assert per * t_new == GDN_CHUNK and t_new % SUBLANE == 0 and db % per == 0
    samp_prev = jnp.pad(state_conv.reshape(db, CONV_W - 1, qkv_w), ((0, 0), (SUBLANE - (CONV_W - 1), 0), (0, 0)))
    og_s, s_delta = _gdn(parts_s, samp_prev, state_delta.reshape(db, GDN_HEADS, GDN_D, GDN_D).astype(F32), cw, hp, ng,
                         batch=db // per, n_chunks=1, c_in=GDN_CHUNK, n_valid=GDN_CHUNK, n_seq=per, n_par=1,
                         shared_init=False)

    oa_p = _attn_prompt(parts_p, parts_m, batch=batch, seq=seq, n_meta=n_meta, tq=tiles.attn_q, kt=tiles.attn_k)
    oa_s = _attn_sample(parts_s, cache_k, cache_v, cache_idx_k, page_table, t_new=t_new)

    wbg = w_branch_gdn[0].astype(BF16)
    wba = w_branch_attn[0].astype(BF16)
    wo = w_out[0].astype(BF16)
    g1 = ln1_g[0].reshape(1, D)
    b1 = ln1_b[0].reshape(1, D)
    h_p = _finish(xp, og_p, oa_p, parts_p, wbg, wba, wo, g1, b1, alpha=alpha, tm=tiles.rows)
    h_s = _finish(xs, og_s, oa_s, parts_s, wbg, wba, wo, g1, b1, alpha=alpha, tm=tiles.rows)

    wq = peer_wq[0].astype(BF16)
    sk = peer_subkeys[0].astype(BF16)
    u = peer_u[0].astype(BF16)
    v = peer_v[0].astype(BF16)
    g2 = ln2_g[0].reshape(1, D)
    b2 = ln2_b[0].reshape(1, D)
    y_p = _peer(h_p, wq, sk, u, v, g2, b2, alpha=alpha, tt=tiles.rows, eb=tiles.peer_experts)
    y_s = _peer(h_s, wq, sk, u, v, g2, b2, alpha=alpha, tt=min(tiles.rows, db * t_new), eb=tiles.peer_experts)

    y_prompt = y_p.reshape(batch, seq, D)
    y_sample = y_s.reshape(db, t_new, D)
    pp = parts_p.reshape(batch, seq, N_COLS)
    ps = parts_s.reshape(db, t_new, N_COLS)
    p_conv = pp[:, seq - (CONV_W - 1):, :qkv_w][None]
    s_conv = jnp.concatenate([state_conv[0].astype(F32), ps[:, :, :qkv_w]], axis=1)[:, -(CONV_W - 1):][None]

    def with_meta(col, width):
        m = jnp.broadcast_to(parts_m[None, :, col:col + width], (batch, n_meta, width))
        return jnp.concatenate([m, pp[:, :, col:col + width]], axis=1)

    p_k = with_meta(COL_AK, kvw).reshape(1, batch, seq + n_meta, ATT_KV_HEADS, ATT_DH)
    p_v = with_meta(COL_AV, kvw).reshape(1, batch, seq + n_meta, ATT_KV_HEADS, ATT_DH)
    p_idx_k = with_meta(COL_SM + SM_IK, IDX_DIM)[None]
    s_k = ps[:, :, COL_AK:COL_AK + kvw].reshape(1, db, t_new, ATT_KV_HEADS, ATT_DH)
    s_v = ps[:, :, COL_AV:COL_AV + kvw].reshape(1, db, t_new, ATT_KV_HEADS, ATT_DH)
    s_idx_k = ps[:, :, COL_SM + SM_IK:COL_SM + SM_IK + IDX_DIM][None]
    return (y_prompt, y_sample, p_conv, p_delta[None], p_k, p_v, p_idx_k, s_conv, s_delta[None].astype(state_delta.dtype),
            s_k, s_v, s_idx_k)
```

```python
import functools
from typing import NamedTuple

import jax
import jax.numpy as jnp
from jax import lax
from jax.experimental import pallas as pl
from jax.experimental.pallas import tpu as pltpu

F32 = jnp.float32
BF16 = jnp.bfloat16
I32 = jnp.int32

GDN_HEADS = 8
GDN_D = 128
CONV_W = 4
GDN_CHUNK = 64
ATT_HEADS = 8
ATT_KV_HEADS = 2
ATT_DH = 128
IDX_HEADS = 16
IDX_DIM = 64
TOPK_MAX = 256
PEER_HEADS = 8
PEER_NKEYS = 128
PEER_TOPK = 16
LN_EPS = 1e-5
RMS_EPS = 1e-6

LANE = 128
SUBLANE = 8
VMEM_LIMIT = 56 * 1024 * 1024

D = 1024
COL_QKV = 0
COL_Z = 3072
COL_AQ = 4096
COL_IQ = 5120
COL_GA = 6144
COL_GB = 7168
COL_AK = 8192
COL_AV = 8448
COL_SM = 8704
N_COLS = 8832
SM_IK = 0
SM_B = 64
SM_A = 72
SM_IW = 80

NEG_INF = float("-inf")
INT_MIN = -(2 ** 31)


def _nt(a, b):
    return lax.dot_general(a, b, (((1,), (1,)), ((), ())), preferred_element_type=F32)


def _tn(a, b):
    return lax.dot_general(a, b, (((0,), (0,)), ((), ())), preferred_element_type=F32)


def _mm(a, b):
    return jnp.dot(a, b, preferred_element_type=F32)


def _mm_f32(a, b):
    return jnp.dot(a, b, preferred_element_type=F32, precision=lax.Precision.HIGHEST)


def _cparams(sem):
    return pltpu.CompilerParams(dimension_semantics=sem, vmem_limit_bytes=VMEM_LIMIT)


def _proj_kernel(x_ref, w_ref, o_ref):
    o_ref[...] = _mm(x_ref[...].astype(BF16), w_ref[...])


def _project(x, w_bf, tm):
    n = x.shape[0]
    tn = N_COLS // 3
    return pl.pallas_call(
        _proj_kernel,
        grid=(3, n // tm),
        in_specs=[pl.BlockSpec((tm, D), lambda j, i: (i, 0)),
                  pl.BlockSpec((D, tn), lambda j, i: (0, j))],
        out_specs=pl.BlockSpec((tm, tn), lambda j, i: (i, j)),
        out_shape=jax.ShapeDtypeStruct((n, N_COLS), F32),
        compiler_params=_cparams(("arbitrary", "arbitrary")),
        name="in_projection",
    )(x, w_bf)


def _sibling_mask(ri, ci, lvl):
    return ((ri >> (lvl + 1)) == (ci >> (lvl + 1))) & (((ri >> lvl) & 1) == 1) & (((ci >> lvl) & 1) == 0)


_BATCH0 = ((0,), (0,))


def _bmm(a, b):
    return lax.dot_general(a, b, (((2,), (1,)), _BATCH0), preferred_element_type=F32)


def _bnt(a, b):
    return lax.dot_general(a, b, (((2,), (2,)), _BATCH0), preferred_element_type=F32)


def _btn(a, b):
    return lax.dot_general(a, b, (((1,), (1,)), _BATCH0), preferred_element_type=F32)


def _gdn_kernel(qkv_ref, z_ref, sm_ref, prev_ref, s0_ref, cw_ref, hp_ref, ng_ref,
                o_ref, sfin_ref, s_scr, ext_scr, *, c_in, n_valid, n_seq, shared_init):
    nh = GDN_HEADS
    hd = GDN_D
    n_par = qkv_ref.shape[0]
    stride = ext_scr.shape[1] // n_seq
    L = stride - SUBLANE
    C = L * n_seq
    ls = L.bit_length() - 1
    c = pl.program_id(1)

    @pl.when(c == 0)
    def _():
        for k in range(n_par):
            src = slice(0, n_seq) if shared_init else slice(k * n_seq, (k + 1) * n_seq)
            s_scr[k * n_seq:(k + 1) * n_seq] = s0_ref[src]
            for b in range(n_seq):
                ext_scr[k, b * stride:b * stride + SUBLANE, :] = prev_ref[src.start + b]

    cw = cw_ref[...]
    hp = hp_ref[...]
    row = lax.broadcasted_iota(I32, (C, 1), 0)
    ri = lax.broadcasted_iota(I32, (C, C), 0)
    ci = lax.broadcasted_iota(I32, (C, C), 1)
    same = (ri >> ls) == (ci >> ls)
    incl = same & (ri >= ci)
    strict = same & (ri > ci)
    eye = (ri == ci).astype(F32)

    q_l, k_l, kd_l, dec_l, lows_l, rhs_l, eg_l, z_l, g_l = [], [], [], [], [], [], [], [], []
    for kp in range(n_par):
        u = qkv_ref[kp]
        sm = sm_ref[kp]
        z = z_ref[kp]
        if c_in < C:
            u = jnp.concatenate([u, jnp.zeros((C - c_in, u.shape[1]), F32)], axis=0)
            sm = jnp.concatenate([sm, jnp.zeros((C - c_in, LANE), F32)], axis=0)
            z = jnp.concatenate([z, jnp.zeros((C - c_in, z.shape[1]), F32)], axis=0)
        pieces = []
        for b in range(n_seq):
            base = b * stride + SUBLANE
            ub_ = u[b * L:(b + 1) * L]
            ext_scr[kp, base:base + L, :] = ub_
            acc = ub_ * cw[CONV_W - 1:CONV_W, :]
            for i in range(CONV_W - 1):
                off = base - (CONV_W - 1) + i
                acc = acc + ext_scr[kp, off:off + L, :] * cw[i:i + 1, :]
            pieces.append(acc)
        conv = pieces[0] if n_seq == 1 else jnp.concatenate(pieces, axis=0)
        if n_seq == 1:
            tail = ext_scr[kp, c_in:c_in + SUBLANE, :]
            ext_scr[kp, 0:SUBLANE, :] = tail
        qkv = conv * jax.nn.sigmoid(conv)

        xs = sm + hp[1:2, :]
        softplus = jnp.maximum(xs, 0.0) + jnp.log1p(jnp.exp(-jnp.abs(xs)))
        la = -jnp.exp(hp[0:1, :]) * softplus
        beta = jax.nn.sigmoid(sm)
        if n_valid < C:
            valid = row < n_valid
            qkv = jnp.where(valid, qkv, 0.0)
            la = jnp.where(valid, la, 0.0)
            beta = jnp.where(valid, beta, 0.0)

        g = _mm_f32(incl.astype(F32), la)
        g_end = _mm_f32((ci == (ri | (L - 1))).astype(F32), g)
        gt = g.T
        g_l.append(g)
        for h in range(nh):
            col = SM_A + h
            gcol = g[:, col:col + 1]
            grow = gt[col:col + 1, :]
            bcol = beta[:, SM_B + h:SM_B + h + 1]
            q = qkv[:, h * hd:(h + 1) * hd]
            k = qkv[:, nh * hd + h * hd:nh * hd + (h + 1) * hd]
            v = qkv[:, 2 * nh * hd + h * hd:2 * nh * hd + (h + 1) * hd]
            q = q * lax.rsqrt(jnp.sum(q * q, axis=-1, keepdims=True) + RMS_EPS) * (hd ** -0.5)
            k = k * lax.rsqrt(jnp.sum(k * k, axis=-1, keepdims=True) + RMS_EPS)
            dec = jnp.exp(jnp.where(incl, gcol - grow, NEG_INF))
            eg = jnp.exp(gcol)
            q_l.append(q.astype(BF16))
            k_l.append(k.astype(BF16))
            kd_l.append((k * jnp.exp(g_end[:, col:col + 1] - gcol)).astype(BF16))
            dec_l.append(dec)
            lows_l.append(bcol * jnp.where(strict, dec, 0.0))
            rhs_l.append(jnp.concatenate([bcol * v, (bcol * eg) * k], axis=1))
            eg_l.append(eg)
            z_l.append(z[:, h * hd:(h + 1) * hd])
    qb = jnp.stack(q_l)
    kb = jnp.stack(k_l)
    kd = jnp.stack(kd_l)
    dec = jnp.stack(dec_l)
    eg = jnp.stack(eg_l)

    low = jnp.stack(lows_l) * _bnt(kb, kb)
    qk = _bnt(qb, kb)
    inv = eye[None] - jnp.where(_sibling_mask(ri, ci, 0)[None], low, 0.0)
    for lvl in range(1, ls):
        off = jnp.where(_sibling_mask(ri, ci, lvl)[None], low, 0.0)
        inv_b = inv.astype(BF16)
        inv = inv - _bmm(inv_b, _bmm(off.astype(BF16), inv_b).astype(BF16))
    uw = _bmm(inv.astype(BF16), jnp.stack(rhs_l).astype(BF16))
    w_b = uw[:, :, hd:].astype(BF16)

    def state(b):
        parts = [s_scr[kp * n_seq + b] for kp in range(n_par)]
        return parts[0] if n_par == 1 else jnp.concatenate(parts, axis=0)

    ws = None
    qs = None
    s_old = []
    for b in range(n_seq):
        S = state(b)
        s_old.append(S)
        Sb = S.astype(BF16)
        if n_seq == 1:
            wm, qm = w_b, qb
        else:
            mine = ((row >> ls) == b)[None]
            wm = jnp.where(mine, w_b, jnp.zeros_like(w_b))
            qm = jnp.where(mine, qb, jnp.zeros_like(qb))
        ws = _bmm(wm, Sb) if ws is None else ws + _bmm(wm, Sb)
        qs = _bmm(qm, Sb) if qs is None else qs + _bmm(qm, Sb)
    ub = (uw[:, :, :hd] - ws).astype(BF16)
    o = eg * qs + _bmm((dec * qk).astype(BF16), ub)
    for b in range(n_seq):
        last = b * L + L - 1
        glast = jnp.stack([g_l[kp][last:last + 1, SM_A + h:SM_A + h + 1] for kp in range(n_par) for h in range(nh)])
        kdm = kd if n_seq == 1 else jnp.where(((row >> ls) == b)[None], kd, jnp.zeros_like(kd))
        s_new = jnp.exp(glast) * s_old[b] + _btn(kdm, ub)
        for kp in range(n_par):
            s_scr[kp * n_seq + b] = s_new[kp * nh:(kp + 1) * nh]

    on = o * lax.rsqrt(jnp.mean(o * o, axis=-1, keepdims=True) + RMS_EPS) * ng_ref[...]
    zz = jnp.stack(z_l)
    res = on * (zz * jax.nn.sigmoid(zz))
    for kp in range(n_par):
        for h in range(nh):
            o_ref[kp, :, h * hd:(h + 1) * hd] = res[kp * nh + h, :c_in]

    @pl.when(c == pl.num_programs(1) - 1)
    def _():
        sfin_ref[...] = s_scr[...]


def _gdn(parts, prev, s0, cw, hp, ng, *, batch, n_chunks, c_in, n_valid, n_seq, n_par, shared_init):
    n = parts.shape[0]
    C = max(c_in, GDN_CHUNK)
    assert n_seq == 1 or (n_chunks == 1 and c_in == C)
    assert batch % n_par == 0
    L = C // n_seq
    n_init = n_seq if shared_init else n_par * n_seq
    init = (lambda b, c: (0, 0, 0)) if shared_init else (lambda b, c: (b, 0, 0))
    init4 = (lambda b, c: (0, 0, 0, 0)) if shared_init else (lambda b, c: (b, 0, 0, 0))
    qkv_w = 3 * GDN_HEADS * GDN_D
    v_w = GDN_HEADS * GDN_D
    rows = n_chunks * c_in
    parts4 = parts.reshape(batch // n_par, n_par, rows, parts.shape[1])
    o, s_fin = pl.pallas_call(
        functools.partial(_gdn_kernel, c_in=c_in, n_valid=n_valid, n_seq=n_seq, shared_init=shared_init),
        grid=(batch // n_par, n_chunks),
        in_specs=[
            pl.BlockSpec((None, n_par, c_in, qkv_w), lambda b, c: (b, 0, c, COL_QKV // qkv_w)),
            pl.BlockSpec((None, n_par, c_in, v_w), lambda b, c: (b, 0, c, COL_Z // v_w)),
            pl.BlockSpec((None, n_par, c_in, LANE), lambda b, c: (b, 0, c, COL_SM // LANE)),
            pl.BlockSpec((n_init, SUBLANE, qkv_w), init),
            pl.BlockSpec((n_init, GDN_HEADS, GDN_D, GDN_D), init4),
            pl.BlockSpec((CONV_W, qkv_w), lambda b, c: (0, 0)),
            pl.BlockSpec((2, LANE), lambda b, c: (0, 0)),
            pl.BlockSpec((1, GDN_D), lambda b, c: (0, 0)),
        ],
        out_specs=[
            pl.BlockSpec((None, n_par, c_in, v_w), lambda b, c: (b, 0, c, 0)),
            pl.BlockSpec((n_par * n_seq, GDN_HEADS, GDN_D, GDN_D), lambda b, c: (b, 0, 0, 0)),
        ],
        out_shape=[jax.ShapeDtypeStruct((batch // n_par, n_par, rows, v_w), F32),
                   jax.ShapeDtypeStruct((batch * n_seq, GDN_HEADS, GDN_D, GDN_D), F32)],
        scratch_shapes=[pltpu.VMEM((n_par * n_seq, GDN_HEADS, GDN_D, GDN_D), F32),
                        pltpu.VMEM((n_par, n_seq * (SUBLANE + L), qkv_w), F32)],
        compiler_params=_cparams(("arbitrary", "arbitrary")),
        name="gated_deltanet",
    )(parts4, parts4, parts4, prev, s0, cw, hp, ng)
    return o.reshape(n, v_w), s_fin


def _sort_key(x):
    b = pltpu.bitcast(x + 0.0, I32)
    return b ^ ((b >> 31) & 0x7FFFFFFF)


KEY_NEG_INF = -(2 ** 31) + 0x7FFFFF
SEARCH_BITS_PER_TRIP = 4


def _count_ge(key_tiles, cand):
    tot = None
    for kt in key_tiles:
        c = jnp.sum((kt >= cand).astype(F32), axis=1, keepdims=True)
        tot = c if tot is None else tot + c
    return tot


def _rows8(x):
    return x.reshape(x.shape[0] // SUBLANE, SUBLANE, x.shape[1])


def _sum_rows8(x, chains=4):
    rows = x.shape[0]
    if rows % (chains * SUBLANE):
        return jnp.sum(_rows8(x), axis=0)
    part = rows // chains
    sums = [jnp.sum(_rows8(x[k * part:(k + 1) * part]), axis=0) for k in range(chains)]
    while len(sums) > 1:
        sums = [a + b for a, b in zip(sums[0::2], sums[1::2])]
    return sums[0]


def _attn_prompt_kernel(q_ref, iq_ref, smq_ref, k_ref, v_ref, smk_ref, mk_ref, mv_ref, msm_ref,
                        o_ref, kbf, vtb, kibf, key_scr, bias_scr, s_scr, cut_scr, *, n_sel, n_meta, kt):
    tq = q_ref.shape[0]
    n_real = k_ref.shape[0]
    kvw = k_ref.shape[1]
    qb_i = pl.program_id(1)
    nt = (qb_i * tq + tq + kt - 1) // kt
    group = ATT_HEADS // ATT_KV_HEADS

    @pl.when(qb_i == 0)
    def _():
        zrows = LANE - n_meta
        kbf[0:n_real, :] = k_ref[...].astype(BF16)
        kbf[n_real:, :] = jnp.concatenate([mk_ref[...], jnp.zeros((zrows, kvw), F32)], axis=0).astype(BF16)
        vtb[:, 0:n_real] = v_ref[...].T.astype(BF16)
        vtb[:, n_real:] = jnp.concatenate([mv_ref[...], jnp.zeros((zrows, kvw), F32)], axis=0).T.astype(BF16)
        kibf[0:n_real, :] = smk_ref[...][:, SM_IK:SM_IK + IDX_DIM].astype(BF16)
        kibf[n_real:, :] = jnp.concatenate(
            [msm_ref[...][:, SM_IK:SM_IK + IDX_DIM], jnp.zeros((zrows, IDX_DIM), F32)], axis=0).astype(BF16)

    iq = iq_ref[...]
    iq_stack = jnp.concatenate([iq[:, h * IDX_DIM:(h + 1) * IDX_DIM] for h in range(IDX_HEADS)], axis=0).astype(BF16)
    w_t = smq_ref[...].T * ((IDX_DIM ** -0.5) * (IDX_HEADS ** -0.5))
    qpos = qb_i * tq + lax.broadcasted_iota(I32, (1, tq), 1)

    def index_scores(start, width):
        s_all = _nt(kibf[pl.ds(start, width), :], iq_stack)
        acc = jnp.zeros((width, tq), F32)
        for h in range(IDX_HEADS):
            acc = acc + jnp.maximum(s_all[:, h * tq:(h + 1) * tq], 0.0) * w_t[SM_IW + h:SM_IW + h + 1, :]
        return acc

    def real_tile(t, carry):
        start = pl.multiple_of(t * kt, kt)
        kpos = start + lax.broadcasted_iota(I32, (kt, 1), 0)
        key_scr[pl.ds(start, kt), :] = _sort_key(jnp.where(kpos <= qpos, index_scores(start, kt), NEG_INF))
        return carry

    lax.fori_loop(0, nt, real_tile, 0)
    mrow = lax.broadcasted_iota(I32, (LANE, 1), 0)
    key_scr[n_real:, :] = _sort_key(jnp.where(mrow < n_meta, index_scores(n_real, LANE), NEG_INF))

    def count_ge(cand):
        def body(t, acc):
            kk = key_scr[pl.ds(pl.multiple_of(t * kt, kt), kt), :]
            return acc + _sum_rows8((kk >= cand).astype(F32))

        acc = lax.fori_loop(0, nt, body, jnp.zeros((SUBLANE, tq), F32))
        acc = acc + jnp.sum(_rows8((key_scr[n_real:, :] >= cand).astype(F32)), axis=0)
        return jnp.sum(acc, axis=0, keepdims=True)

    few = qpos + 1 + n_meta <= n_sel

    def unsettled(state):
        it, _, cnt = state
        return (it < 32) & (jnp.max(jnp.where(few | (cnt == n_sel), 0.0, 1.0)) > 0.0)

    def refine(state):
        it, thr, cnt = state
        for step in range(SEARCH_BITS_PER_TRIP):
            cand = thr + lax.shift_left(jnp.int32(1), 31 - step - it)
            c = count_ge(cand)
            ok = c >= n_sel
            thr = jnp.where(ok, cand, thr)
            cnt = jnp.where(ok, c, cnt)
        return it + SEARCH_BITS_PER_TRIP, thr, cnt

    n_all = (nt * kt + LANE).astype(F32)
    _, thr, cnt = lax.while_loop(
        unsettled, refine, (jnp.int32(0), jnp.full((1, tq), INT_MIN, I32), jnp.full((1, tq), 1.0, F32) * n_all))

    tie = (cnt > n_sel) & (thr > KEY_NEG_INF) & jnp.logical_not(few)
    n_pos = n_real + n_meta
    cut_scr[...] = jnp.full((SUBLANE, tq), float(n_pos), F32)

    def pos_of(start, width, is_meta):
        r = lax.broadcasted_iota(I32, (width, 1), 0)
        return r if is_meta else n_meta + start + r

    @pl.when(jnp.max(tie.astype(F32)) > 0.0)
    def _tie_cut():
        need = n_sel - count_ge(thr + 1)
        nbits = max(1, (n_pos - 1).bit_length())

        def count_eq_le(cut):
            def body(t, acc):
                start = pl.multiple_of(t * kt, kt)
                kk = key_scr[pl.ds(start, kt), :]
                return acc + jnp.sum(_rows8(((kk == thr) & (pos_of(start, kt, False) <= cut)).astype(F32)), axis=0)

            acc = lax.fori_loop(0, nt, body, jnp.zeros((SUBLANE, tq), F32))
            mk = key_scr[n_real:, :]
            acc = acc + jnp.sum(_rows8(((mk == thr) & (pos_of(0, LANE, True) <= cut)).astype(F32)), axis=0)
            return jnp.sum(acc, axis=0, keepdims=True)

        def body(it, cut):
            cand = cut - lax.shift_left(jnp.int32(1), nbits - 1 - it)
            ok = (cand >= 0) & (count_eq_le(cand) >= need)
            return jnp.where(ok, cand, cut)

        cut = lax.fori_loop(0, nbits, body, jnp.full((1, tq), (1 << nbits) - 1, I32))
        cut = jnp.where(tie, cut, n_pos).astype(F32)
        cut_scr[...] = jnp.broadcast_to(cut, (SUBLANE, tq))

    cut = cut_scr[0:1, :].astype(I32)

    def bias_tile(start, width, is_meta):
        kk = key_scr[pl.ds(start, width), :]
        sel = (kk > thr) | ((kk == thr) & (pos_of(start, width, is_meta) <= cut))
        sel = sel & (kk > KEY_NEG_INF)
        bias_scr[pl.ds(start, width), :] = jnp.where(sel, 0.0, NEG_INF)

    def bias_body(t, carry):
        bias_tile(pl.multiple_of(t * kt, kt), kt, False)
        return carry

    lax.fori_loop(0, nt, bias_body, 0)
    bias_tile(n_real, LANE, True)

    scale = ATT_DH ** -0.5
    for g in range(ATT_KV_HEADS):
        q4 = jnp.concatenate([q_ref[:, (g * group + j) * ATT_DH:(g * group + j + 1) * ATT_DH] for j in range(group)],
                             axis=0).astype(BF16)

        def score_tile(start, width):
            s = _nt(kbf[pl.ds(start, width), g * ATT_DH:(g + 1) * ATT_DH], q4) * scale
            b = bias_scr[pl.ds(start, width), :]
            s = s + jnp.concatenate([b] * group, axis=1)
            s_scr[pl.ds(start, width), :] = s
            return jnp.max(_rows8(s), axis=0)

        def pass1(t, m):
            return jnp.maximum(m, score_tile(pl.multiple_of(t * kt, kt), kt))

        m = jnp.max(lax.fori_loop(0, nt, pass1, score_tile(n_real, LANE)), axis=0, keepdims=True)

        def prob_tile(start, width):
            p = jnp.exp(s_scr[pl.ds(start, width), :] - m)
            pv = _mm(vtb[g * ATT_DH:(g + 1) * ATT_DH, pl.ds(start, width)], p.astype(BF16))
            return _sum_rows8(p), pv

        def pass2(t, carry):
            l, acc = carry
            dl, dacc = prob_tile(pl.multiple_of(t * kt, kt), kt)
            return l + dl, acc + dacc

        l, acc = lax.fori_loop(0, nt, pass2, prob_tile(n_real, LANE))
        out_t = acc / jnp.sum(l, axis=0, keepdims=True)
        for j in range(group):
            hh = g * group + j
            o_ref[:, hh * ATT_DH:(hh + 1) * ATT_DH] = out_t[:, j * tq:(j + 1) * tq].T


def _attn_prompt(parts, mparts, *, batch, seq, n_meta, tq, kt):
    nqb = seq // tq
    n_sel = min(TOPK_MAX, (seq + n_meta) // 4)
    kvw = ATT_KV_HEADS * ATT_DH
    nk = seq + LANE
    return pl.pallas_call(
        functools.partial(_attn_prompt_kernel, n_sel=n_sel, n_meta=n_meta, kt=kt),
        grid=(batch, nqb),
        in_specs=[
            pl.BlockSpec((tq, D), lambda b, i: (b * nqb + i, COL_AQ // D)),
            pl.BlockSpec((tq, D), lambda b, i: (b * nqb + i, COL_IQ // D)),
            pl.BlockSpec((tq, LANE), lambda b, i: (b * nqb + i, COL_SM // LANE)),
            pl.BlockSpec((seq, kvw), lambda b, i: (b, COL_AK // kvw)),
            pl.BlockSpec((seq, kvw), lambda b, i: (b, COL_AV // kvw)),
            pl.BlockSpec((seq, LANE), lambda b, i: (b, COL_SM // LANE)),
            pl.BlockSpec((n_meta, kvw), lambda b, i: (0, COL_AK // kvw)),
            pl.BlockSpec((n_meta, kvw), lambda b, i: (0, COL_AV // kvw)),
            pl.BlockSpec((n_meta, LANE), lambda b, i: (0, COL_SM // LANE)),
        ],
        out_specs=pl.BlockSpec((tq, D), lambda b, i: (b * nqb + i, 0)),
        out_shape=jax.ShapeDtypeStruct((batch * seq, D), F32),
        scratch_shapes=[pltpu.VMEM((nk, kvw), BF16), pltpu.VMEM((kvw, nk), BF16), pltpu.VMEM((nk, IDX_DIM), BF16),
                        pltpu.VMEM((nk, tq), I32), pltpu.VMEM((nk, tq), F32),
                        pltpu.VMEM((nk, (ATT_HEADS // ATT_KV_HEADS) * tq), F32), pltpu.VMEM((SUBLANE, tq), F32)],
        compiler_params=_cparams(("arbitrary", "arbitrary")),
        name="sparse_attention_prompt",
    )(parts, parts, parts, parts, parts, parts, mparts, mparts, mparts)


def _attn_sample_kernel(pt_ref, *refs, n_pages, n_sel):
    kp = refs[0:n_pages]
    vp = refs[n_pages:2 * n_pages]
    ip = refs[2 * n_pages:3 * n_pages]
    q_ref, iq_ref, kn_ref, vn_ref, sm_ref, o_ref, kil, vil, kit, cut_scr = refs[3 * n_pages:]
    del pt_ref
    t_new = q_ref.shape[0]
    page = ip[0].shape[1]
    past = n_pages * page
    nkv = ATT_KV_HEADS
    for j in range(n_pages):
        kil[j * nkv * page:(j + 1) * nkv * page, :] = kp[j][...].astype(BF16)
        vil[j * nkv * page:(j + 1) * nkv * page, :] = vp[j][...].astype(BF16)
        kit[:, j * page:(j + 1) * page] = ip[j][...].astype(BF16)
    sm = sm_ref[...]
    zpad = jnp.zeros((LANE - t_new, nkv * ATT_DH), BF16)
    knew = jnp.concatenate([kn_ref[...].astype(BF16), zpad], axis=0)
    vnew = jnp.concatenate([vn_ref[...].astype(BF16), zpad], axis=0)
    ki_new = jnp.concatenate(
        [sm[:, SM_IK:SM_IK + IDX_DIM].astype(BF16), jnp.zeros((LANE - t_new, IDX_DIM), BF16)], axis=0)

    iq = iq_ref[...]
    qi = jnp.concatenate([iq[:, h * IDX_DIM:(h + 1) * IDX_DIM] for h in range(IDX_HEADS)], axis=0).astype(BF16)
    wcol = jnp.concatenate([sm[:, SM_IW + h:SM_IW + h + 1] for h in range(IDX_HEADS)], axis=0)
    wcol = wcol * ((IDX_DIM ** -0.5) * (IDX_HEADS ** -0.5))

    def idx_scores(qk):
        s = jnp.maximum(qk, 0.0) * wcol
        acc = s[0:t_new]
        for h in range(1, IDX_HEADS):
            acc = acc + s[h * t_new:(h + 1) * t_new]
        return acc

    i_past = idx_scores(_mm(qi, kit[...]))
    i_new = idx_scores(_nt(qi, ki_new))
    trow = lax.broadcasted_iota(I32, (t_new, LANE), 0)
    tlane = lax.broadcasted_iota(I32, (t_new, LANE), 1)
    key_past = _sort_key(i_past)
    key_new = _sort_key(jnp.where(tlane <= trow, i_new, NEG_INF))
    tiles = [key_past, key_new]

    def digit_body(it, thr):
        step = lax.shift_left(jnp.int32(1), 28 - 4 * it)
        digit = jnp.zeros((t_new, 1), I32)
        for j in range(1, 16):
            digit = digit + (_count_ge(tiles, thr + j * step) >= n_sel).astype(I32)
        return thr + digit * step

    thr = lax.fori_loop(0, 8, digit_body, jnp.full((t_new, 1), INT_MIN, I32))
    n_ge = _count_ge(tiles, thr)
    tie = (n_ge > n_sel) & (thr > KEY_NEG_INF)
    pos_past = lax.broadcasted_iota(I32, (1, past), 1)
    pos_new = past + lax.broadcasted_iota(I32, (1, LANE), 1)
    n_pos = past + t_new
    cut_scr[...] = jnp.full(cut_scr.shape, n_pos, I32)

    @pl.when(jnp.max(tie.astype(F32)) > 0.0)
    def _tie_cut():
        need = n_sel - _count_ge(tiles, thr + 1)
        nbits = max(1, (n_pos - 1).bit_length())

        def cut_body(it, cut):
            cand = cut - lax.shift_left(jnp.int32(1), nbits - 1 - it)
            cnt = (jnp.sum(((key_past == thr) & (pos_past <= cand)).astype(F32), axis=1, keepdims=True)
                   + jnp.sum(((key_new == thr) & (pos_new <= cand)).astype(F32), axis=1, keepdims=True))
            ok = (cand >= 0) & (cnt >= need)
            return jnp.where(ok, cand, cut)

        cut = lax.fori_loop(0, nbits, cut_body, jnp.full((t_new, 1), (1 << nbits) - 1, I32))
        cut_scr[...] = jnp.broadcast_to(jnp.where(tie, cut, n_pos), cut_scr.shape)

    cut = cut_scr[:, 0:1]

    def chosen(kk, pos):
        return ((kk > thr) | ((kk == thr) & (pos <= cut))) & (kk > KEY_NEG_INF)

    group = ATT_HEADS // ATT_KV_HEADS
    b_new = jnp.concatenate([jnp.where(chosen(key_new, pos_new), 0.0, NEG_INF)] * group, axis=0)
    sel01 = chosen(key_past, pos_past).astype(BF16)
    er = lax.broadcasted_iota(I32, (page, nkv * page), 0)
    ec = lax.broadcasted_iota(I32, (page, nkv * page), 1)
    expand = ((ec >= er * nkv) & (ec < (er + 1) * nkv)).astype(BF16)
    sel_il = jnp.concatenate([_mm(sel01[:, j * page:(j + 1) * page], expand) for j in range(n_pages)], axis=1)
    il_lane = lax.broadcasted_iota(I32, (1, nkv * past), 1)
    assert nkv & (nkv - 1) == 0
    il_head = il_lane & (nkv - 1)
    scale = ATT_DH ** -0.5
    q = q_ref[...]
    for g in range(ATT_KV_HEADS):
        qg = jnp.concatenate(
            [q[:, (g * group + j) * ATT_DH:(g * group + j + 1) * ATT_DH] for j in range(group)], axis=0).astype(BF16)
        b_g = jnp.where((sel_il > 0.5) & (il_head == g), 0.0, NEG_INF)
        s_p = _nt(qg, kil[...]) * scale + jnp.concatenate([b_g] * group, axis=0)
        s_n = _nt(qg, knew[:, g * ATT_DH:(g + 1) * ATT_DH]) * scale + b_new
        m = jnp.maximum(jnp.max(s_p, axis=1, keepdims=True), jnp.max(s_n, axis=1, keepdims=True))
        p_p = jnp.exp(s_p - m)
        p_n = jnp.exp(s_n - m)
        l = jnp.sum(p_p, axis=1, keepdims=True) + jnp.sum(p_n, axis=1, keepdims=True)
        acc = _mm(p_p.astype(BF16), vil[...]) + _mm(p_n.astype(BF16), vnew[:, g * ATT_DH:(g + 1) * ATT_DH])
        res = acc / l
        for j in range(group):
            hh = g * group + j
            o_ref[:, hh * ATT_DH:(hh + 1) * ATT_DH] = res[j * t_new:(j + 1) * t_new]


def _attn_sample(parts, cache_k, cache_v, cache_ik, page_table, *, t_new):
    db, n_pages = page_table.shape
    page = cache_k.shape[2]
    kvw = ATT_KV_HEADS * ATT_DH
    past = n_pages * page
    n_sel = min(TOPK_MAX, (past + t_new) // 4)

    n_pool = cache_k.shape[1]
    cache_k = cache_k.reshape(n_pool, page * ATT_KV_HEADS, ATT_DH)
    cache_v = cache_v.reshape(n_pool, page * ATT_KV_HEADS, ATT_DH)
    cache_ik = jnp.swapaxes(cache_ik, 2, 3).reshape(n_pool, IDX_DIM, page)

    def kv_page(j):
        return pl.BlockSpec((None, page * ATT_KV_HEADS, ATT_DH), lambda b, pt, j=j: (pt[b, j], 0, 0))

    def ik_page(j):
        return pl.BlockSpec((None, IDX_DIM, page), lambda b, pt, j=j: (pt[b, j], 0, 0))

    in_specs = ([kv_page(j) for j in range(n_pages)] + [kv_page(j) for j in range(n_pages)]
                + [ik_page(j) for j in range(n_pages)]
                + [pl.BlockSpec((t_new, D), lambda b, pt: (b, COL_AQ // D)),
                   pl.BlockSpec((t_new, D), lambda b, pt: (b, COL_IQ // D)),
                   pl.BlockSpec((t_new, kvw), lambda b, pt: (b, COL_AK // kvw)),
                   pl.BlockSpec((t_new, kvw), lambda b, pt: (b, COL_AV // kvw)),
                   pl.BlockSpec((t_new, LANE), lambda b, pt: (b, COL_SM // LANE))])
    grid_spec = pltpu.PrefetchScalarGridSpec(
        num_scalar_prefetch=1, grid=(db,), in_specs=in_specs,
        out_specs=pl.BlockSpec((t_new, D), lambda b, pt: (b, 0)),
        scratch_shapes=[pltpu.VMEM((past * ATT_KV_HEADS, ATT_DH), BF16), pltpu.VMEM((past * ATT_KV_HEADS, ATT_DH), BF16),
                        pltpu.VMEM((IDX_DIM, past), BF16), pltpu.VMEM((t_new, LANE), I32)])
    return pl.pallas_call(
        functools.partial(_attn_sample_kernel, n_pages=n_pages, n_sel=n_sel),
        grid_spec=grid_spec,
        out_shape=jax.ShapeDtypeStruct((db * t_new, D), F32),
        compiler_params=_cparams(("arbitrary",)),
        name="sparse_attention_sample",
    )(page_table, *([cache_k] * n_pages), *([cache_v] * n_pages), *([cache_ik] * n_pages),
      parts, parts, parts, parts, parts)


def _layer_norm(x, g, b):
    mu = jnp.mean(x, axis=-1, keepdims=True)
    xc = x - mu
    var = jnp.mean(xc * xc, axis=-1, keepdims=True)
    return xc * lax.rsqrt(var + LN_EPS) * g + b


def _finish_kernel(x_ref, og_ref, oa_ref, ga_ref, gb_ref, wbg_ref, wba_ref, wo_ref, g_ref, b_ref, h_ref, *, alpha):
    a = _mm(og_ref[...].astype(BF16), wbg_ref[...])
    b = _mm(oa_ref[...].astype(BF16), wba_ref[...])
    merged = jax.nn.sigmoid(ga_ref[...]) * a + jax.nn.sigmoid(gb_ref[...]) * b
    y = alpha * x_ref[...] + _mm(merged.astype(BF16), wo_ref[...])
    h_ref[...] = _layer_norm(y, g_ref[...], b_ref[...])


def _finish(x, o_gdn, o_att, parts, wbg, wba, wo, g, b, *, alpha, tm):
    n = x.shape[0]
    tm = min(tm, n)
    row = lambda i: (i, 0)
    full = lambda i: (0, 0)
    return pl.pallas_call(
        functools.partial(_finish_kernel, alpha=alpha),
        grid=(n // tm,),
        in_specs=[pl.BlockSpec((tm, D), row), pl.BlockSpec((tm, D), row), pl.BlockSpec((tm, D), row),
                  pl.BlockSpec((tm, D), lambda i: (i, COL_GA // D)), pl.BlockSpec((tm, D), lambda i: (i, COL_GB // D)),
                  pl.BlockSpec((D, D), full), pl.BlockSpec((D, D), full), pl.BlockSpec((D, D), full),
                  pl.BlockSpec((1, D), full), pl.BlockSpec((1, D), full)],
        out_specs=pl.BlockSpec((tm, D), row),
        out_shape=jax.ShapeDtypeStruct((n, D), F32),
        compiler_params=_cparams(("arbitrary",)),
        name="merge_layernorm",
    )(x, o_gdn, o_att, parts, parts, wbg, wba, wo, g, b)


_CAND = [(r0, r1) for r0 in range(PEER_TOPK) for r1 in range(PEER_TOPK) if (r0 + 1) * (r1 + 1) <= PEER_TOPK]
_CAND_OFF = [next(i for i, c in enumerate(_CAND) if c[0] == r0) for r0 in range(PEER_TOPK)]
_CAND_LEN = [sum(1 for c in _CAND if c[0] == r0) for r0 in range(PEER_TOPK)]
_CAND_ROWS = -(-len(_CAND) // SUBLANE) * SUBLANE


def _top_rows(s, n_top, break_ties):
    rows, cols = s.shape
    iota = lax.broadcasted_iota(I32, (rows, cols), 0).astype(F32)
    rank = jnp.full((rows, cols), float(n_top), F32)
    vals = []
    for r in range(n_top):
        m = jnp.max(s, axis=0, keepdims=True)
        hit = s == m
        if break_ties:
            hit = iota == jnp.min(jnp.where(hit, iota, float(rows)), axis=0, keepdims=True)
        vals.append(m)
        s = jnp.where(hit, NEG_INF, s)
        rank = jnp.where(hit, float(r), rank)
    n_ranked = jnp.sum((rank < float(n_top)).astype(F32), axis=0, keepdims=True)
    return jnp.concatenate(vals, axis=0), rank, n_ranked == float(n_top)


def _peer_kernel(h_ref, wq_ref, sk_ref, u0_ref, un_ref, vp_ref, vl_ref, g_ref, b_ref, y_ref,
                 hb_scr, q_scr, rank1_scr, bt_scr, nt_scr, at_scr, s1_scr, s_scr, p_scr, acc_scr, *, alpha, n_steps):
    tt = h_ref.shape[0]
    eb = un_ref.shape[0] // 2
    e = pl.program_id(1)
    nk = PEER_NKEYS
    kk = PEER_TOPK
    tchunks = tt // LANE

    @pl.when(e == 0)
    def _prep():
        hb = h_ref[...].astype(BF16)
        hb_scr[...] = hb
        q_scr[...] = _mm(hb, wq_ref[...]).astype(BF16)
        acc_scr[...] = jnp.zeros(acc_scr.shape, F32)

        def scores(hd, carry):
            q0 = q_scr[:, pl.ds(pl.multiple_of(hd * 2 * nk, nk), nk)]
            q1 = q_scr[:, pl.ds(pl.multiple_of(hd * 2 * nk + nk, nk), nk)]
            at_scr[hd] = _nt(sk_ref[hd, 0], q0)
            s1_scr[hd] = _nt(sk_ref[hd, 1], q1)
            return carry

        lax.fori_loop(0, PEER_HEADS, scores, 0)

        def select_chunk(hd, c0, s0, s1, break_ties):
            a, rank0, ok0 = _top_rows(s0, kk, break_ties)
            b, rank1, ok1 = _top_rows(s1, kk, break_ties)
            pad = [jnp.full((_CAND_ROWS - len(_CAND), LANE), NEG_INF, F32)]
            cand = jnp.concatenate([a[r0:r0 + 1] + b[r1:r1 + 1] for r0, r1 in _CAND] + pad, axis=0)
            top, crank, okc = _top_rows(cand, kk, break_ties)
            zsum = jnp.sum(jnp.exp(top - top[0:1]), axis=0, keepdims=True)
            chosen = (crank < float(kk)).astype(F32)
            crow = lax.broadcasted_iota(I32, (_CAND_ROWS, 1), 0)
            nsel = jnp.zeros((nk, LANE), F32)
            for r0 in range(kk):
                in_group = (crow >= _CAND_OFF[r0]) & (crow < _CAND_OFF[r0] + _CAND_LEN[r0])
                cnt = jnp.sum(jnp.where(in_group, chosen, 0.0), axis=0, keepdims=True)
                nsel = nsel + jnp.where(rank0 == float(r0), cnt, 0.0)
            rank1_scr[hd, :, pl.ds(c0, LANE)] = rank1.astype(BF16)
            nt_scr[hd, :, pl.ds(c0, LANE)] = nsel
            at_scr[hd, :, pl.ds(c0, LANE)] = jnp.exp(s0 - a[0:1])
            bt_scr[hd, :, pl.ds(c0, LANE)] = (jnp.exp(s1 - b[0:1]) / zsum).astype(BF16)
            return jnp.min((ok0 & ok1 & okc).astype(F32)) > 0.5

        pair = 4 if tchunks % 4 == 0 else (2 if tchunks % 2 == 0 else 1)

        def select(idx, carry):
            hd = idx // (tchunks // pair)
            first = (idx % (tchunks // pair)) * pair
            tie_free = None
            chunks = []
            for k in range(pair):
                c0 = pl.multiple_of((first + k) * LANE, LANE)
                s0 = at_scr[hd, :, pl.ds(c0, LANE)]
                s1 = s1_scr[hd, :, pl.ds(c0, LANE)]
                chunks.append((c0, s0, s1))
            for c0, s0, s1 in chunks:
                ok = select_chunk(hd, c0, s0, s1, False)
                tie_free = ok if tie_free is None else tie_free & ok

            @pl.when(jnp.logical_not(tie_free))
            def _():
                for c0, s0, s1 in chunks:
                    select_chunk(hd, c0, s0, s1, True)

            return carry

        lax.fori_loop(0, PEER_HEADS * tchunks // pair, select, 0)

        s_scr[1] = _nt(u0_ref[...], hb_scr[...])
        p_scr[1] = jnp.zeros(p_scr.shape[1:], BF16)

    groups = eb // nk

    assert 2 * groups == SUBLANE

    def weights(half, s_ref, p_ref):
        i0 = pl.multiple_of(e * SUBLANE, SUBLANE)
        zero = jnp.zeros((nk, LANE), BF16)

        for tc in range(tchunks):
            cols = slice(tc * LANE, (tc + 1) * LANE)
            gates = [zero] * groups
            for hd in range(PEER_HEADS):
                nblk = nt_scr[hd, pl.ds(i0, SUBLANE), cols]
                ablk = at_scr[hd, pl.ds(i0, SUBLANE), cols]
                rank1 = rank1_scr[hd, :, cols]
                bval = bt_scr[hd, :, cols]
                for ii in range(groups):
                    r = half * groups + ii
                    nrow = nblk[r:r + 1].astype(BF16)
                    arow = ablk[r:r + 1].astype(BF16)
                    gates[ii] = gates[ii] + jnp.where(rank1 < nrow, bval, zero) * arow
            for ii in range(groups):
                rows = slice(ii * nk, (ii + 1) * nk)
                s = s_ref[rows, cols]
                act = 0.5 * s * (1.0 + lax.erf(s * (2.0 ** -0.5)))
                p_ref[rows, cols] = gates[ii] * act.astype(BF16)

    def step(cur, nxt):
        s_scr[nxt] = _nt(un_ref[...], hb_scr[...])
        for half in range(2):
            rows = pl.ds(half * eb, eb)
            weights(half, s_scr.at[cur, rows], p_scr.at[nxt, rows])
        acc_scr[...] += _tn(p_scr[cur], vp_ref[...])

    @pl.when(e % 2 == 0)
    def _():
        step(1, 0)

    @pl.when(e % 2 == 1)
    def _():
        step(0, 1)

    last_slot = (n_steps - 1) % 2

    @pl.when(e == n_steps - 1)
    def _():
        y = alpha * h_ref[...] + (acc_scr[...] + _tn(p_scr[last_slot], vl_ref[...]))
        y_ref[...] = _layer_norm(y, g_ref[...], b_ref[...])


def _peer(h, wq, sk, u, v, g, b, *, alpha, tt, eb):
    n = h.shape[0]
    n_exp = u.shape[0]
    nk = PEER_NKEYS
    nb = n_exp // eb
    return pl.pallas_call(
        functools.partial(_peer_kernel, alpha=alpha, n_steps=nb),
        grid=(n // tt, nb),
        in_specs=[pl.BlockSpec((tt, D), lambda t, e: (t, 0)),
                  pl.BlockSpec(wq.shape, lambda t, e: (0, 0)),
                  pl.BlockSpec(sk.shape, lambda t, e: (0, 0, 0, 0)),
                  pl.BlockSpec((eb, D), lambda t, e: (0, 0)),
                  pl.BlockSpec((eb, D), lambda t, e: (jnp.minimum(e + 1, nb - 1), 0)),
                  pl.BlockSpec((eb, D), lambda t, e: (jnp.maximum(e - 1, 0), 0)),
                  pl.BlockSpec((eb, D), lambda t, e: (nb - 1, 0)),
                  pl.BlockSpec((1, D), lambda t, e: (0, 0)),
                  pl.BlockSpec((1, D), lambda t, e: (0, 0))],
        out_specs=pl.BlockSpec((tt, D), lambda t, e: (t, 0)),
        out_shape=jax.ShapeDtypeStruct((n, D), F32),
        scratch_shapes=[pltpu.VMEM((tt, D), BF16), pltpu.VMEM((tt, wq.shape[1]), BF16),
                        pltpu.VMEM((PEER_HEADS, nk, tt), BF16), pltpu.VMEM((PEER_HEADS, nk, tt), BF16),
                        pltpu.VMEM((PEER_HEADS, nk, tt), F32), pltpu.VMEM((PEER_HEADS, nk, tt), F32),
                        pltpu.VMEM((PEER_HEADS, nk, tt), F32),
                        pltpu.VMEM((2, eb, tt), F32), pltpu.VMEM((2, eb, tt), BF16), pltpu.VMEM((tt, D), F32)],
        compiler_params=_cparams(("arbitrary", "arbitrary")),
        name="peer_ffn",
    )(h, wq, sk, u, u, v, v, g, b)


class _Tiles(NamedTuple):
    proj_rows: int
    rows: int
    attn_q: int
    attn_k: int
    peer_experts: int
    gdn_groups: int


def _tiles(batch, seq):
    rows = 512 if (batch * seq) % 512 == 0 else 256
    groups = 4 if batch % 4 == 0 else (2 if batch % 2 == 0 else 1)
    proj_rows = 1024 if (batch * seq) % 1024 == 0 else rows
    return _Tiles(proj_rows=proj_rows, rows=rows, attn_q=256 if seq % 256 == 0 else 128, attn_k=min(512, seq),
                  peer_experts=SUBLANE * PEER_NKEYS, gdn_groups=groups)


def _permute_w_in(w):
    gq = 3 * GDN_HEADS * GDN_D
    gv = GDN_HEADS * GDN_D
    aq = ATT_HEADS * ATT_DH
    akv = ATT_KV_HEADS * ATT_DH
    iqw = IDX_HEADS * IDX_DIM
    sizes = (gq, gv, GDN_HEADS, GDN_HEADS, aq, akv, akv, iqw, IDX_DIM, IDX_HEADS, D, D)
    offs = [0]
    for s in sizes:
        offs.append(offs[-1] + s)
    seg = [w[:, offs[i]:offs[i + 1]] for i in range(len(sizes))]
    qkv, z, b, a, q, k, v, iq, ik, iw, ga, gb = seg
    pad = jnp.zeros((w.shape[0], LANE - IDX_DIM - 2 * GDN_HEADS - IDX_HEADS), w.dtype)
    return jnp.concatenate([qkv, z, q, iq, ga, gb, k, v, ik, b, a, iw, pad], axis=1).astype(BF16)


def kernel(x_prompt, x_sample, cache_k, cache_v, cache_idx_k, state_conv, state_delta, page_table, meta_tokens, w_in, conv_w, a_log, dt_bias, gdn_norm_g, w_branch_gdn, w_branch_attn, w_out, ln1_g, ln1_b, peer_wq, peer_subkeys, peer_u, peer_v, ln2_g, ln2_b):
    depth = w_in.shape[0]
    assert depth == 1, "single-layer step"
    batch, seq, d = x_prompt.shape
    db, t_new, _ = x_sample.shape
    n_meta = meta_tokens.shape[0]
    assert d == D and seq % GDN_CHUNK == 0 and n_meta % SUBLANE == 0 and n_meta <= GDN_CHUNK
    alpha = (2 * depth) ** 0.25
    qkv_w = 3 * GDN_HEADS * GDN_D
    kvw = ATT_KV_HEADS * ATT_DH

    tiles = _tiles(batch, seq)
    w_r = _permute_w_in(w_in[0])
    xp = x_prompt.reshape(batch * seq, D)
    xs = x_sample.reshape(db * t_new, D)
    parts_p = _project(xp, w_r, tiles.proj_rows)
    parts_s = _project(xs, w_r, min(tiles.rows, db * t_new))
    parts_m = _project(meta_tokens.astype(F32), w_r, n_meta)

    hp = jnp.zeros((2, LANE), F32).at[0, SM_A:SM_A + GDN_HEADS].set(a_log[0]).at[1, SM_A:SM_A + GDN_HEADS].set(dt_bias[0])
    cw = conv_w[0]
    ng = gdn_norm_g[0].reshape(1, GDN_D)
    zero_prev = jnp.zeros((1, SUBLANE, qkv_w), F32)
    zero_state = jnp.zeros((1, GDN_HEADS, GDN_D, GDN_D), F32)
    _, s_meta = _gdn(parts_m, zero_prev, zero_state, cw, hp, ng, batch=1, n_chunks=1, c_in=n_meta, n_valid=n_meta,
                     n_seq=1, n_par=1, shared_init=True)
    meta_prev = parts_m[n_meta - SUBLANE:, :qkv_w].reshape(1, SUBLANE, qkv_w)
    og_p, p_delta = _gdn(parts_p, meta_prev, s_meta, cw, hp, ng, batch=batch, n_chunks=seq // GDN_CHUNK,
                         c_in=GDN_CHUNK, n_valid=GDN_CHUNK, n_seq=1, n_par=tiles.gdn_groups, shared_init=True)
    per = GDN_CHUNK // t_new
    assert per * t_new == GDN_CHUNK and t_new % SUBLANE == 0 and db % per == 0
    samp_prev = jnp.pad(state_conv.reshape(db, CONV_W - 1, qkv_w), ((0, 0), (SUBLANE - (CONV_W - 1), 0), (0, 0)))
    og_s, s_delta = _gdn(parts_s, samp_prev, state_delta.reshape(db, GDN_HEADS, GDN_D, GDN_D).astype(F32), cw, hp, ng,
                         batch=db // per, n_chunks=1, c_in=GDN_CHUNK, n_valid=GDN_CHUNK, n_seq=per, n_par=1,
                         shared_init=False)

    oa_p = _attn_prompt(parts_p, parts_m, batch=batch, seq=seq, n_meta=n_meta, tq=tiles.attn_q, kt=tiles.attn_k)
    oa_s = _attn_sample(parts_s, cache_k, cache_v, cache_idx_k, page_table, t_new=t_new)

    wbg = w_branch_gdn[0].astype(BF16)
    wba = w_branch_attn[0].astype(BF16)
    wo = w_out[0].astype(BF16)
    g1 = ln1_g[0].reshape(1, D)
    b1 = ln1_b[0].reshape(1, D)
    h_p = _finish(xp, og_p, oa_p, parts_p, wbg, wba, wo, g1, b1, alpha=alpha, tm=tiles.rows)
    h_s = _finish(xs, og_s, oa_s, parts_s, wbg, wba, wo, g1, b1, alpha=alpha, tm=tiles.rows)

    wq = peer_wq[0].astype(BF16)
    sk = peer_subkeys[0].astype(BF16)
    u = peer_u[0].astype(BF16)
    v = peer_v[0].astype(BF16)
    g2 = ln2_g[0].reshape(1, D)
    b2 = ln2_b[0].reshape(1, D)
    y_p = _peer(h_p, wq, sk, u, v, g2, b2, alpha=alpha, tt=tiles.rows, eb=tiles.peer_experts)
    y_s = _peer(h_s, wq, sk, u, v, g2, b2, alpha=alpha, tt=min(tiles.rows, db * t_new), eb=tiles.peer_experts)

    y_prompt = y_p.reshape(batch, seq, D)
    y_sample = y_s.reshape(db, t_new, D)
    pp = parts_p.reshape(batch, seq, N_COLS)
    ps = parts_s.reshape(db, t_new, N_COLS)
    p_conv = pp[:, seq - (CONV_W - 1):, :qkv_w][None]
    s_conv = jnp.concatenate([state_conv[0].astype(F32), ps[:, :, :qkv_w]], axis=1)[:, -(CONV_W - 1):][None]

    def with_meta(col, width):
        m = jnp.broadcast_to(parts_m[None, :, col:col + width], (batch, n_meta, width))
        return jnp.concatenate([m, pp[:, :, col:col + width]], axis=1)

    p_k = with_meta(COL_AK, kvw).reshape(1, batch, seq + n_meta, ATT_KV_HEADS, ATT_DH)
    p_v = with_meta(COL_AV, kvw).reshape(1, batch, seq + n_meta, ATT_KV_HEADS, ATT_DH)
    p_idx_k = with_meta(COL_SM + SM_IK, IDX_DIM)[None]
    s_k = ps[:, :, COL_AK:COL_AK + kvw].reshape(1, db, t_new, ATT_KV_HEADS, ATT_DH)
    s_v = ps[:, :, COL_AV:COL_AV + kvw].reshape(1, db, t_new, ATT_KV_HEADS, ATT_DH)
    s_idx_k = ps[:, :, COL_SM + SM_IK:COL_SM + SM_IK + IDX_DIM][None]
    return (y_prompt, y_sample, p_conv, p_delta[None], p_k, p_v, p_idx_k, s_conv, s_delta[None].astype(state_delta.dtype),
            s_k, s_v, s_idx_k)
```

```python
import functools
from typing import NamedTuple

import jax
import jax.numpy as jnp
from jax import lax
from jax.experimental import pallas as pl
from jax.experimental.pallas import tpu as pltpu

F32 = jnp.float32
BF16 = jnp.bfloat16
I32 = jnp.int32

GDN_HEADS = 8
GDN_D = 128
CONV_W = 4
GDN_CHUNK = 64
ATT_HEADS = 8
ATT_KV_HEADS = 2
ATT_DH = 128
IDX_HEADS = 16
IDX_DIM = 64
TOPK_MAX = 256
PEER_HEADS = 8
PEER_NKEYS = 128
PEER_TOPK = 16
LN_EPS = 1e-5
RMS_EPS = 1e-6

LANE = 128
SUBLANE = 8
VMEM_LIMIT = 56 * 1024 * 1024

D = 1024
COL_QKV = 0
COL_Z = 3072
COL_AQ = 4096
COL_IQ = 5120
COL_GA = 6144
COL_GB = 7168
COL_AK = 8192
COL_AV = 8448
COL_SM = 8704
N_COLS = 8832
SM_IK = 0
SM_B = 64
SM_A = 72
SM_IW = 80

NEG_INF = float("-inf")
INT_MIN = -(2 ** 31)


def _nt(a, b):
    return lax.dot_general(a, b, (((1,), (1,)), ((), ())), preferred_element_type=F32)


def _tn(a, b):
    return lax.dot_general(a, b, (((0,), (0,)), ((), ())), preferred_element_type=F32)


def _mm(a, b):
    return jnp.dot(a, b, preferred_element_type=F32)


def _mm_f32(a, b):
    return jnp.dot(a, b, preferred_element_type=F32, precision=lax.Precision.HIGHEST)


def _cparams(sem):
    return pltpu.CompilerParams(dimension_semantics=sem, vmem_limit_bytes=VMEM_LIMIT)


def _proj_kernel(x_ref, w_ref, o_ref):
    o_ref[...] = _mm(x_ref[...].astype(BF16), w_ref[...])


def _project(x, w_bf, tm):
    n = x.shape[0]
    tn = N_COLS // 3
    return pl.pallas_call(
        _proj_kernel,
        grid=(3, n // tm),
        in_specs=[pl.BlockSpec((tm, D), lambda j, i: (i, 0)),
                  pl.BlockSpec((D, tn), lambda j, i: (0, j))],
        out_specs=pl.BlockSpec((tm, tn), lambda j, i: (i, j)),
        out_shape=jax.ShapeDtypeStruct((n, N_COLS), F32),
        compiler_params=_cparams(("arbitrary", "arbitrary")),
        name="in_projection",
    )(x, w_bf)


def _sibling_mask(ri, ci, lvl):
    return ((ri >> (lvl + 1)) == (ci >> (lvl + 1))) & (((ri >> lvl) & 1) == 1) & (((ci >> lvl) & 1) == 0)


_BATCH0 = ((0,), (0,))


def _bmm(a, b):
    return lax.dot_general(a, b, (((2,), (1,)), _BATCH0), preferred_element_type=F32)


def _bnt(a, b):
    return lax.dot_general(a, b, (((2,), (2,)), _BATCH0), preferred_element_type=F32)


def _btn(a, b):
    return lax.dot_general(a, b, (((1,), (1,)), _BATCH0), preferred_element_type=F32)


def _gdn_kernel(qkv_ref, z_ref, sm_ref, prev_ref, s0_ref, cw_ref, hp_ref, ng_ref,
                o_ref, sfin_ref, s_scr, ext_scr, *, c_in, n_valid, n_seq, shared_init):
    nh = GDN_HEADS
    hd = GDN_D
    n_par = qkv_ref.shape[0]
    stride = ext_scr.shape[1] // n_seq
    L = stride - SUBLANE
    C = L * n_seq
    ls = L.bit_length() - 1
    c = pl.program_id(1)

    @pl.when(c == 0)
    def _():
        for k in range(n_par):
            src = slice(0, n_seq) if shared_init else slice(k * n_seq, (k + 1) * n_seq)
            s_scr[k * n_seq:(k + 1) * n_seq] = s0_ref[src]
            for b in range(n_seq):
                ext_scr[k, b * stride:b * stride + SUBLANE, :] = prev_ref[src.start + b]

    cw = cw_ref[...]
    hp = hp_ref[...]
    row = lax.broadcasted_iota(I32, (C, 1), 0)
    ri = lax.broadcasted_iota(I32, (C, C), 0)
    ci = lax.broadcasted_iota(I32, (C, C), 1)
    same = (ri >> ls) == (ci >> ls)
    incl = same & (ri >= ci)
    strict = same & (ri > ci)
    eye = (ri == ci).astype(F32)

    q_l, k_l, kd_l, dec_l, lows_l, rhs_l, eg_l, z_l, g_l = [], [], [], [], [], [], [], [], []
    for kp in range(n_par):
        u = qkv_ref[kp]
        sm = sm_ref[kp]
        z = z_ref[kp]
        if c_in < C:
            u = jnp.concatenate([u, jnp.zeros((C - c_in, u.shape[1]), F32)], axis=0)
            sm = jnp.concatenate([sm, jnp.zeros((C - c_in, LANE), F32)], axis=0)
            z = jnp.concatenate([z, jnp.zeros((C - c_in, z.shape[1]), F32)], axis=0)
        pieces = []
        for b in range(n_seq):
            base = b * stride + SUBLANE
            ub_ = u[b * L:(b + 1) * L]
            ext_scr[kp, base:base + L, :] = ub_
            acc = ub_ * cw[CONV_W - 1:CONV_W, :]
            for i in range(CONV_W - 1):
                off = base - (CONV_W - 1) + i
                acc = acc + ext_scr[kp, off:off + L, :] * cw[i:i + 1, :]
            pieces.append(acc)
        conv = pieces[0] if n_seq == 1 else jnp.concatenate(pieces, axis=0)
        if n_seq == 1:
            tail = ext_scr[kp, c_in:c_in + SUBLANE, :]
            ext_scr[kp, 0:SUBLANE, :] = tail
        qkv = conv * jax.nn.sigmoid(conv)

        xs = sm + hp[1:2, :]
        softplus = jnp.maximum(xs, 0.0) + jnp.log1p(jnp.exp(-jnp.abs(xs)))
        la = -jnp.exp(hp[0:1, :]) * softplus
        beta = jax.nn.sigmoid(sm)
        if n_valid < C:
            valid = row < n_valid
            qkv = jnp.where(valid, qkv, 0.0)
            la = jnp.where(valid, la, 0.0)
            beta = jnp.where(valid, beta, 0.0)

        g = _mm_f32(incl.astype(F32), la)
        g_end = _mm_f32((ci == (ri | (L - 1))).astype(F32), g)
        gt = g.T
        g_l.append(g)
        for h in range(nh):
            col = SM_A + h
            gcol = g[:, col:col + 1]
            grow = gt[col:col + 1, :]
            bcol = beta[:, SM_B + h:SM_B + h + 1]
            q = qkv[:, h * hd:(h + 1) * hd]
            k = qkv[:, nh * hd + h * hd:nh * hd + (h + 1) * hd]
            v = qkv[:, 2 * nh * hd + h * hd:2 * nh * hd + (h + 1) * hd]
            q = q * lax.rsqrt(jnp.sum(q * q, axis=-1, keepdims=True) + RMS_EPS) * (hd ** -0.5)
            k = k * lax.rsqrt(jnp.sum(k * k, axis=-1, keepdims=True) + RMS_EPS)
            dec = jnp.exp(jnp.where(incl, gcol - grow, NEG_INF))
            eg = jnp.exp(gcol)
            q_l.append(q.astype(BF16))
            k_l.append(k.astype(BF16))
            kd_l.append((k * jnp.exp(g_end[:, col:col + 1] - gcol)).astype(BF16))
            dec_l.append(dec)
            lows_l.append(bcol * jnp.where(strict, dec, 0.0))
            rhs_l.append(jnp.concatenate([bcol * v, (bcol * eg) * k], axis=1))
            eg_l.append(eg)
            z_l.append(z[:, h * hd:(h + 1) * hd])
    qb = jnp.stack(q_l)
    kb = jnp.stack(k_l)
    kd = jnp.stack(kd_l)
    dec = jnp.stack(dec_l)
    eg = jnp.stack(eg_l)

    low = jnp.stack(lows_l) * _bnt(kb, kb)
    qk = _bnt(qb, kb)
    inv = eye[None] - jnp.where(_sibling_mask(ri, ci, 0)[None], low, 0.0)
    for lvl in range(1, ls):
        off = jnp.where(_sibling_mask(ri, ci, lvl)[None], low, 0.0)
        inv_b = inv.astype(BF16)
        inv = inv - _bmm(inv_b, _bmm(off.astype(BF16), inv_b).astype(BF16))
    uw = _bmm(inv.astype(BF16), jnp.stack(rhs_l).astype(BF16))
    w_b = uw[:, :, hd:].astype(BF16)

    def state(b):
        parts = [s_scr[kp * n_seq + b] for kp in range(n_par)]
        return parts[0] if n_par == 1 else jnp.concatenate(parts, axis=0)

    ws = None
    qs = None
    s_old = []
    for b in range(n_seq):
        S = state(b)
        s_old.append(S)
        Sb = S.astype(BF16)
        if n_seq == 1:
            wm, qm = w_b, qb
        else:
            mine = ((row >> ls) == b)[None]
            wm = jnp.where(mine, w_b, jnp.zeros_like(w_b))
            qm = jnp.where(mine, qb, jnp.zeros_like(qb))
        ws = _bmm(wm, Sb) if ws is None else ws + _bmm(wm, Sb)
        qs = _bmm(qm, Sb) if qs is None else qs + _bmm(qm, Sb)
    ub = (uw[:, :, :hd] - ws).astype(BF16)
    o = eg * qs + _bmm((dec * qk).astype(BF16), ub)
    for b in range(n_seq):
        last = b * L + L - 1
        glast = jnp.stack([g_l[kp][last:last + 1, SM_A + h:SM_A + h + 1] for kp in range(n_par) for h in range(nh)])
        kdm = kd if n_seq == 1 else jnp.where(((row >> ls) == b)[None], kd, jnp.zeros_like(kd))
        s_new = jnp.exp(glast) * s_old[b] + _btn(kdm, ub)
        for kp in range(n_par):
            s_scr[kp * n_seq + b] = s_new[kp * nh:(kp + 1) * nh]

    on = o * lax.rsqrt(jnp.mean(o * o, axis=-1, keepdims=True) + RMS_EPS) * ng_ref[...]
    zz = jnp.stack(z_l)
    res = on * (zz * jax.nn.sigmoid(zz))
    for kp in range(n_par):
        for h in range(nh):
            o_ref[kp, :, h * hd:(h + 1) * hd] = res[kp * nh + h, :c_in]

    @pl.when(c == pl.num_programs(1) - 1)
    def _():
        sfin_ref[...] = s_scr[...]


def _gdn(parts, prev, s0, cw, hp, ng, *, batch, n_chunks, c_in, n_valid, n_seq, n_par, shared_init):
    n = parts.shape[0]
    C = max(c_in, GDN_CHUNK)
    assert n_seq == 1 or (n_chunks == 1 and c_in == C)
    assert batch % n_par == 0
    L = C // n_seq
    n_init = n_seq if shared_init else n_par * n_seq
    init = (lambda b, c: (0, 0, 0)) if shared_init else (lambda b, c: (b, 0, 0))
    init4 = (lambda b, c: (0, 0, 0, 0)) if shared_init else (lambda b, c: (b, 0, 0, 0))
    qkv_w = 3 * GDN_HEADS * GDN_D
    v_w = GDN_HEADS * GDN_D
    rows = n_chunks * c_in
    parts4 = parts.reshape(batch // n_par, n_par, rows, parts.shape[1])
    o, s_fin = pl.pallas_call(
        functools.partial(_gdn_kernel, c_in=c_in, n_valid=n_valid, n_seq=n_seq, shared_init=shared_init),
        grid=(batch // n_par, n_chunks),
        in_specs=[
            pl.BlockSpec((None, n_par, c_in, qkv_w), lambda b, c: (b, 0, c, COL_QKV // qkv_w)),
            pl.BlockSpec((None, n_par, c_in, v_w), lambda b, c: (b, 0, c, COL_Z // v_w)),
            pl.BlockSpec((None, n_par, c_in, LANE), lambda b, c: (b, 0, c, COL_SM // LANE)),
            pl.BlockSpec((n_init, SUBLANE, qkv_w), init),
            pl.BlockSpec((n_init, GDN_HEADS, GDN_D, GDN_D), init4),
            pl.BlockSpec((CONV_W, qkv_w), lambda b, c: (0, 0)),
            pl.BlockSpec((2, LANE), lambda b, c: (0, 0)),
            pl.BlockSpec((1, GDN_D), lambda b, c: (0, 0)),
        ],
        out_specs=[
            pl.BlockSpec((None, n_par, c_in, v_w), lambda b, c: (b, 0, c, 0)),
            pl.BlockSpec((n_par * n_seq, GDN_HEADS, GDN_D, GDN_D), lambda b, c: (b, 0, 0, 0)),
        ],
        out_shape=[jax.ShapeDtypeStruct((batch // n_par, n_par, rows, v_w), F32),
                   jax.ShapeDtypeStruct((batch * n_seq, GDN_HEADS, GDN_D, GDN_D), F32)],
        scratch_shapes=[pltpu.VMEM((n_par * n_seq, GDN_HEADS, GDN_D, GDN_D), F32),
                        pltpu.VMEM((n_par, n_seq * (SUBLANE + L), qkv_w), F32)],
        compiler_params=_cparams(("arbitrary", "arbitrary")),
        name="gated_deltanet",
    )(parts4, parts4, parts4, prev, s0, cw, hp, ng)
    return o.reshape(n, v_w), s_fin


def _sort_key(x):
    b = pltpu.bitcast(x + 0.0, I32)
    return b ^ ((b >> 31) & 0x7FFFFFFF)


KEY_NEG_INF = -(2 ** 31) + 0x7FFFFF
SEARCH_BITS_PER_TRIP = 4


def _count_ge(key_tiles, cand):
    tot = None
    for kt in key_tiles:
        c = jnp.sum((kt >= cand).astype(F32), axis=1, keepdims=True)
        tot = c if tot is None else tot + c
    return tot


def _rows8(x):
    return x.reshape(x.shape[0] // SUBLANE, SUBLANE, x.shape[1])


def _sum_rows8(x, chains=4):
    rows = x.shape[0]
    if rows % (chains * SUBLANE):
        return jnp.sum(_rows8(x), axis=0)
    part = rows // chains
    sums = [jnp.sum(_rows8(x[k * part:(k + 1) * part]), axis=0) for k in range(chains)]
    while len(sums) > 1:
        sums = [a + b for a, b in zip(sums[0::2], sums[1::2])]
    return sums[0]


def _attn_prompt_kernel(q_ref, iq_ref, smq_ref, k_ref, v_ref, smk_ref, mk_ref, mv_ref, msm_ref,
                        o_ref, kbf, vtb, kibf, key_scr, bias_scr, s_scr, cut_scr, *, n_sel, n_meta, kt):
    tq = q_ref.shape[0]
    n_real = k_ref.shape[0]
    kvw = k_ref.shape[1]
    qb_i = pl.program_id(1)
    nt = (qb_i * tq + tq + kt - 1) // kt
    group = ATT_HEADS // ATT_KV_HEADS

    @pl.when(qb_i == 0)
    def _():
        zrows = LANE - n_meta
        kbf[0:n_real, :] = k_ref[...].astype(BF16)
        kbf[n_real:, :] = jnp.concatenate([mk_ref[...], jnp.zeros((zrows, kvw), F32)], axis=0).astype(BF16)
        vtb[:, 0:n_real] = v_ref[...].T.astype(BF16)
        vtb[:, n_real:] = jnp.concatenate([mv_ref[...], jnp.zeros((zrows, kvw), F32)], axis=0).T.astype(BF16)
        kibf[0:n_real, :] = smk_ref[...][:, SM_IK:SM_IK + IDX_DIM].astype(BF16)
        kibf[n_real:, :] = jnp.concatenate(
            [msm_ref[...][:, SM_IK:SM_IK + IDX_DIM], jnp.zeros((zrows, IDX_DIM), F32)], axis=0).astype(BF16)

    iq = iq_ref[...]
    iq_stack = jnp.concatenate([iq[:, h * IDX_DIM:(h + 1) * IDX_DIM] for h in range(IDX_HEADS)], axis=0).astype(BF16)
    w_t = smq_ref[...].T * ((IDX_DIM ** -0.5) * (IDX_HEADS ** -0.5))
    qpos = qb_i * tq + lax.broadcasted_iota(I32, (1, tq), 1)

    def index_scores(start, width):
        s_all = _nt(kibf[pl.ds(start, width), :], iq_stack)
        acc = jnp.zeros((width, tq), F32)
        for h in range(IDX_HEADS):
            acc = acc + jnp.maximum(s_all[:, h * tq:(h + 1) * tq], 0.0) * w_t[SM_IW + h:SM_IW + h + 1, :]
        return acc

    def real_tile(t, carry):
        start = pl.multiple_of(t * kt, kt)
        kpos = start + lax.broadcasted_iota(I32, (kt, 1), 0)
        key_scr[pl.ds(start, kt), :] = _sort_key(jnp.where(kpos <= qpos, index_scores(start, kt), NEG_INF))
        return carry

    lax.fori_loop(0, nt, real_tile, 0)
    mrow = lax.broadcasted_iota(I32, (LANE, 1), 0)
    key_scr[n_real:, :] = _sort_key(jnp.where(mrow < n_meta, index_scores(n_real, LANE), NEG_INF))

    def count_ge(cand):
        def body(t, acc):
            kk = key_scr[pl.ds(pl.multiple_of(t * kt, kt), kt), :]
            return acc + _sum_rows8((kk >= cand).astype(F32))

        acc = lax.fori_loop(0, nt, body, jnp.zeros((SUBLANE, tq), F32))
        acc = acc + jnp.sum(_rows8((key_scr[n_real:, :] >= cand).astype(F32)), axis=0)
        return jnp.sum(acc, axis=0, keepdims=True)

    few = qpos + 1 + n_meta <= n_sel

    def unsettled(state):
        it, _, cnt = state
        return (it < 32) & (jnp.max(jnp.where(few | (cnt == n_sel), 0.0, 1.0)) > 0.0)

    def refine(state):
        it, thr, cnt = state
        for step in range(SEARCH_BITS_PER_TRIP):
            cand = thr + lax.shift_left(jnp.int32(1), 31 - step - it)
            c = count_ge(cand)
            ok = c >= n_sel
            thr = jnp.where(ok, cand, thr)
            cnt = jnp.where(ok, c, cnt)
        return it + SEARCH_BITS_PER_TRIP, thr, cnt

    n_all = (nt * kt + LANE).astype(F32)
    _, thr, cnt = lax.while_loop(
        unsettled, refine, (jnp.int32(0), jnp.full((1, tq), INT_MIN, I32), jnp.full((1, tq), 1.0, F32) * n_all))

    tie = (cnt > n_sel) & (thr > KEY_NEG_INF) & jnp.logical_not(few)
    n_pos = n_real + n_meta
    cut_scr[...] = jnp.full((SUBLANE, tq), float(n_pos), F32)

    def pos_of(start, width, is_meta):
        r = lax.broadcasted_iota(I32, (width, 1), 0)
        return r if is_meta else n_meta + start + r

    @pl.when(jnp.max(tie.astype(F32)) > 0.0)
    def _tie_cut():
        need = n_sel - count_ge(thr + 1)
        nbits = max(1, (n_pos - 1).bit_length())

        def count_eq_le(cut):
            def body(t, acc):
                start = pl.multiple_of(t * kt, kt)
                kk = key_scr[pl.ds(start, kt), :]
                return acc + jnp.sum(_rows8(((kk == thr) & (pos_of(start, kt, False) <= cut)).astype(F32)), axis=0)

            acc = lax.fori_loop(0, nt, body, jnp.zeros((SUBLANE, tq), F32))
            mk = key_scr[n_real:, :]
            acc = acc + jnp.sum(_rows8(((mk == thr) & (pos_of(0, LANE, True) <= cut)).astype(F32)), axis=0)
            return jnp.sum(acc, axis=0, keepdims=True)

        def body(it, cut):
            cand = cut - lax.shift_left(jnp.int32(1), nbits - 1 - it)
            ok = (cand >= 0) & (count_eq_le(cand) >= need)
            return jnp.where(ok, cand, cut)

        cut = lax.fori_loop(0, nbits, body, jnp.full((1, tq), (1 << nbits) - 1, I32))
        cut = jnp.where(tie, cut, n_pos).astype(F32)
        cut_scr[...] = jnp.broadcast_to(cut, (SUBLANE, tq))

    cut = cut_scr[0:1, :].astype(I32)

    def bias_tile(start, width, is_meta):
        kk = key_scr[pl.ds(start, width), :]
        sel = (kk > thr) | ((kk == thr) & (pos_of(start, width, is_meta) <= cut))
        sel = sel & (kk > KEY_NEG_INF)
        bias_scr[pl.ds(start, width), :] = jnp.where(sel, 0.0, NEG_INF)

    def bias_body(t, carry):
        bias_tile(pl.multiple_of(t * kt, kt), kt, False)
        return carry

    lax.fori_loop(0, nt, bias_body, 0)
    bias_tile(n_real, LANE, True)

    scale = ATT_DH ** -0.5
    for g in range(ATT_KV_HEADS):
        q4 = jnp.concatenate([q_ref[:, (g * group + j) * ATT_DH:(g * group + j + 1) * ATT_DH] for j in range(group)],
                             axis=0).astype(BF16)

        def score_tile(start, width):
            s = _nt(kbf[pl.ds(start, width), g * ATT_DH:(g + 1) * ATT_DH], q4) * scale
            b = bias_scr[pl.ds(start, width), :]
            s = s + jnp.concatenate([b] * group, axis=1)
            s_scr[pl.ds(start, width), :] = s
            return jnp.max(_rows8(s), axis=0)

        def pass1(t, m):
            return jnp.maximum(m, score_tile(pl.multiple_of(t * kt, kt), kt))

        m = jnp.max(lax.fori_loop(0, nt, pass1, score_tile(n_real, LANE)), axis=0, keepdims=True)

        def prob_tile(start, width):
            p = jnp.exp(s_scr[pl.ds(start, width), :] - m)
            pv = _mm(vtb[g * ATT_DH:(g + 1) * ATT_DH, pl.ds(start, width)], p.astype(BF16))
            return _sum_rows8(p), pv

        def pass2(t, carry):
            l, acc = carry
            dl, dacc = prob_tile(pl.multiple_of(t * kt, kt), kt)
            return l + dl, acc + dacc

        l, acc = lax.fori_loop(0, nt, pass2, prob_tile(n_real, LANE))
        out_t = acc / jnp.sum(l, axis=0, keepdims=True)
        for j in range(group):
            hh = g * group + j
            o_ref[:, hh * ATT_DH:(hh + 1) * ATT_DH] = out_t[:, j * tq:(j + 1) * tq].T


def _attn_prompt(parts, mparts, *, batch, seq, n_meta, tq, kt):
    nqb = seq // tq
    n_sel = min(TOPK_MAX, (seq + n_meta) // 4)
    kvw = ATT_KV_HEADS * ATT_DH
    nk = seq + LANE
    return pl.pallas_call(
        functools.partial(_attn_prompt_kernel, n_sel=n_sel, n_meta=n_meta, kt=kt),
        grid=(batch, nqb),
        in_specs=[
            pl.BlockSpec((tq, D), lambda b, i: (b * nqb + i, COL_AQ // D)),
            pl.BlockSpec((tq, D), lambda b, i: (b * nqb + i, COL_IQ // D)),
            pl.BlockSpec((tq, LANE), lambda b, i: (b * nqb + i, COL_SM // LANE)),
            pl.BlockSpec((seq, kvw), lambda b, i: (b, COL_AK // kvw)),
            pl.BlockSpec((seq, kvw), lambda b, i: (b, COL_AV // kvw)),
            pl.BlockSpec((seq, LANE), lambda b, i: (b, COL_SM // LANE)),
            pl.BlockSpec((n_meta, kvw), lambda b, i: (0, COL_AK // kvw)),
            pl.BlockSpec((n_meta, kvw), lambda b, i: (0, COL_AV // kvw)),
            pl.BlockSpec((n_meta, LANE), lambda b, i: (0, COL_SM // LANE)),
        ],
        out_specs=pl.BlockSpec((tq, D), lambda b, i: (b * nqb + i, 0)),
        out_shape=jax.ShapeDtypeStruct((batch * seq, D), F32),
        scratch_shapes=[pltpu.VMEM((nk, kvw), BF16), pltpu.VMEM((kvw, nk), BF16), pltpu.VMEM((nk, IDX_DIM), BF16),
                        pltpu.VMEM((nk, tq), I32), pltpu.VMEM((nk, tq), F32),
                        pltpu.VMEM((nk, (ATT_HEADS // ATT_KV_HEADS) * tq), F32), pltpu.VMEM((SUBLANE, tq), F32)],
        compiler_params=_cparams(("arbitrary", "arbitrary")),
        name="sparse_attention_prompt",
    )(parts, parts, parts, parts, parts, parts, mparts, mparts, mparts)


def _attn_sample_kernel(pt_ref, *refs, n_pages, n_sel, n_el):
    pages = refs[0:3 * n_pages * n_el]
    q_ref, iq_ref, kn_ref, vn_ref, sm_ref, o_ref, kil, vil, kit, cut_scr = refs[3 * n_pages * n_el:]
    del pt_ref
    t_new = q_ref.shape[0] // n_el
    page = pages[2 * n_pages].shape[1]
    past = n_pages * page
    nkv = ATT_KV_HEADS
    group = ATT_HEADS // ATT_KV_HEADS
    zpad = jnp.zeros((LANE - t_new, nkv * ATT_DH), BF16)
    trow = lax.broadcasted_iota(I32, (t_new, LANE), 0)
    tlane = lax.broadcasted_iota(I32, (t_new, LANE), 1)

    knew, vnew, key_past_l, key_new_l = [], [], [], []
    for el in range(n_el):
        kp = pages[3 * n_pages * el:3 * n_pages * el + n_pages]
        vp = pages[3 * n_pages * el + n_pages:3 * n_pages * el + 2 * n_pages]
        ip = pages[3 * n_pages * el + 2 * n_pages:3 * n_pages * (el + 1)]
        rows = slice(el * t_new, (el + 1) * t_new)
        for j in range(n_pages):
            kil[el, j * nkv * page:(j + 1) * nkv * page, :] = kp[j][...].astype(BF16)
            vil[el, j * nkv * page:(j + 1) * nkv * page, :] = vp[j][...].astype(BF16)
            kit[el, :, j * page:(j + 1) * page] = ip[j][...].astype(BF16)
        sm = sm_ref[rows, :]
        knew.append(jnp.concatenate([kn_ref[rows, :].astype(BF16), zpad], axis=0))
        vnew.append(jnp.concatenate([vn_ref[rows, :].astype(BF16), zpad], axis=0))
        ki_new = jnp.concatenate(
            [sm[:, SM_IK:SM_IK + IDX_DIM].astype(BF16), jnp.zeros((LANE - t_new, IDX_DIM), BF16)], axis=0)

        iq = iq_ref[rows, :]
        qi = jnp.concatenate([iq[:, h * IDX_DIM:(h + 1) * IDX_DIM] for h in range(IDX_HEADS)], axis=0).astype(BF16)
        wcol = jnp.concatenate([sm[:, SM_IW + h:SM_IW + h + 1] for h in range(IDX_HEADS)], axis=0)
        wcol = wcol * ((IDX_DIM ** -0.5) * (IDX_HEADS ** -0.5))

        def idx_scores(qk):
            s = jnp.maximum(qk, 0.0) * wcol
            acc = s[0:t_new]
            for h in range(1, IDX_HEADS):
                acc = acc + s[h * t_new:(h + 1) * t_new]
            return acc

        key_past_l.append(_sort_key(idx_scores(_mm(qi, kit[el]))))
        key_new_l.append(_sort_key(jnp.where(tlane <= trow, idx_scores(_nt(qi, ki_new)), NEG_INF)))

    key_past = key_past_l[0] if n_el == 1 else jnp.concatenate(key_past_l, axis=0)
    key_new = key_new_l[0] if n_el == 1 else jnp.concatenate(key_new_l, axis=0)
    tiles = [key_past, key_new]
    tr = n_el * t_new

    def digit_body(it, thr):
        step = lax.shift_left(jnp.int32(1), 28 - 4 * it)
        digit = jnp.zeros((tr, 1), I32)
        for j in range(1, 16):
            digit = digit + (_count_ge(tiles, thr + j * step) >= n_sel).astype(I32)
        return thr + digit * step

    thr = lax.fori_loop(0, 8, digit_body, jnp.full((tr, 1), INT_MIN, I32))
    n_ge = _count_ge(tiles, thr)
    tie = (n_ge > n_sel) & (thr > KEY_NEG_INF)
    pos_past = lax.broadcasted_iota(I32, (1, past), 1)
    pos_new = past + lax.broadcasted_iota(I32, (1, LANE), 1)
    n_pos = past + t_new
    cut_scr[...] = jnp.full(cut_scr.shape, n_pos, I32)

    @pl.when(jnp.max(tie.astype(F32)) > 0.0)
    def _tie_cut():
        need = n_sel - _count_ge(tiles, thr + 1)
        nbits = max(1, (n_pos - 1).bit_length())

        def cut_body(it, cut):
            cand = cut - lax.shift_left(jnp.int32(1), nbits - 1 - it)
            cnt = (jnp.sum(((key_past == thr) & (pos_past <= cand)).astype(F32), axis=1, keepdims=True)
                   + jnp.sum(((key_new == thr) & (pos_new <= cand)).astype(F32), axis=1, keepdims=True))
            ok = (cand >= 0) & (cnt >= need)
            return jnp.where(ok, cand, cut)

        cut = lax.fori_loop(0, nbits, cut_body, jnp.full((tr, 1), (1 << nbits) - 1, I32))
        cut_scr[...] = jnp.broadcast_to(jnp.where(tie, cut, n_pos), cut_scr.shape)

    cut = cut_scr[:, 0:1]

    def chosen(kk, pos):
        return ((kk > thr) | ((kk == thr) & (pos <= cut))) & (kk > KEY_NEG_INF)

    bias_new = jnp.where(chosen(key_new, pos_new), 0.0, NEG_INF)
    sel01 = chosen(key_past, pos_past).astype(BF16)
    er = lax.broadcasted_iota(I32, (page, nkv * page), 0)
    ec = lax.broadcasted_iota(I32, (page, nkv * page), 1)
    expand = ((ec >= er * nkv) & (ec < (er + 1) * nkv)).astype(BF16)
    sel_il = jnp.concatenate([_mm(sel01[:, j * page:(j + 1) * page], expand) for j in range(n_pages)], axis=1)
    il_lane = lax.broadcasted_iota(I32, (1, nkv * past), 1)
    assert nkv & (nkv - 1) == 0
    il_head = il_lane & (nkv - 1)
    scale = ATT_DH ** -0.5
    for el in range(n_el):
        rows = slice(el * t_new, (el + 1) * t_new)
        q = q_ref[rows, :]
        b_new = jnp.concatenate([bias_new[rows]] * group, axis=0)
        for g in range(ATT_KV_HEADS):
            qg = jnp.concatenate(
                [q[:, (g * group + j) * ATT_DH:(g * group + j + 1) * ATT_DH] for j in range(group)], axis=0).astype(BF16)
            b_g = jnp.where((sel_il[rows] > 0.5) & (il_head == g), 0.0, NEG_INF)
            s_p = _nt(qg, kil[el]) * scale + jnp.concatenate([b_g] * group, axis=0)
            s_n = _nt(qg, knew[el][:, g * ATT_DH:(g + 1) * ATT_DH]) * scale + b_new
            m = jnp.maximum(jnp.max(s_p, axis=1, keepdims=True), jnp.max(s_n, axis=1, keepdims=True))
            p_p = jnp.exp(s_p - m)
            p_n = jnp.exp(s_n - m)
            l = jnp.sum(p_p, axis=1, keepdims=True) + jnp.sum(p_n, axis=1, keepdims=True)
            acc = _mm(p_p.astype(BF16), vil[el]) + _mm(p_n.astype(BF16), vnew[el][:, g * ATT_DH:(g + 1) * ATT_DH])
            res = acc / l
            for j in range(group):
                hh = g * group + j
                o_ref[rows, hh * ATT_DH:(hh + 1) * ATT_DH] = res[j * t_new:(j + 1) * t_new]


def _attn_sample(parts, cache_k, cache_v, cache_ik, page_table, *, t_new, n_el):
    db, n_pages = page_table.shape
    assert db % n_el == 0
    page = cache_k.shape[2]
    kvw = ATT_KV_HEADS * ATT_DH
    past = n_pages * page
    n_sel = min(TOPK_MAX, (past + t_new) // 4)
    tr = n_el * t_new

    n_pool = cache_k.shape[1]
    cache_k = cache_k.reshape(n_pool, page * ATT_KV_HEADS, ATT_DH)
    cache_v = cache_v.reshape(n_pool, page * ATT_KV_HEADS, ATT_DH)
    cache_ik = jnp.swapaxes(cache_ik, 2, 3).reshape(n_pool, IDX_DIM, page)

    def kv_page(el, j):
        return pl.BlockSpec((None, page * ATT_KV_HEADS, ATT_DH), lambda b, pt, el=el, j=j: (pt[b * n_el + el, j], 0, 0))

    def ik_page(el, j):
        return pl.BlockSpec((None, IDX_DIM, page), lambda b, pt, el=el, j=j: (pt[b * n_el + el, j], 0, 0))

    in_specs, operands = [], []
    for el in range(n_el):
        in_specs += [kv_page(el, j) for j in range(n_pages)] + [kv_page(el, j) for j in range(n_pages)]
        in_specs += [ik_page(el, j) for j in range(n_pages)]
        operands += [cache_k] * n_pages + [cache_v] * n_pages + [cache_ik] * n_pages
    in_specs += [pl.BlockSpec((tr, D), lambda b, pt: (b, COL_AQ // D)),
                 pl.BlockSpec((tr, D), lambda b, pt: (b, COL_IQ // D)),
                 pl.BlockSpec((tr, kvw), lambda b, pt: (b, COL_AK // kvw)),
                 pl.BlockSpec((tr, kvw), lambda b, pt: (b, COL_AV // kvw)),
                 pl.BlockSpec((tr, LANE), lambda b, pt: (b, COL_SM // LANE))]
    grid_spec = pltpu.PrefetchScalarGridSpec(
        num_scalar_prefetch=1, grid=(db // n_el,), in_specs=in_specs,
        out_specs=pl.BlockSpec((tr, D), lambda b, pt: (b, 0)),
        scratch_shapes=[pltpu.VMEM((n_el, past * ATT_KV_HEADS, ATT_DH), BF16),
                        pltpu.VMEM((n_el, past * ATT_KV_HEADS, ATT_DH), BF16),
                        pltpu.VMEM((n_el, IDX_DIM, past), BF16), pltpu.VMEM((tr, LANE), I32)])
    return pl.pallas_call(
        functools.partial(_attn_sample_kernel, n_pages=n_pages, n_sel=n_sel, n_el=n_el),
        grid_spec=grid_spec,
        out_shape=jax.ShapeDtypeStruct((db * t_new, D), F32),
        compiler_params=_cparams(("arbitrary",)),
        name="sparse_attention_sample",
    )(page_table, *operands, parts, parts, parts, parts, parts)


def _layer_norm(x, g, b):
    mu = jnp.mean(x, axis=-1, keepdims=True)
    xc = x - mu
    var = jnp.mean(xc * xc, axis=-1, keepdims=True)
    return xc * lax.rsqrt(var + LN_EPS) * g + b


def _finish_kernel(x_ref, og_ref, oa_ref, ga_ref, gb_ref, wbg_ref, wba_ref, wo_ref, g_ref, b_ref, h_ref, *, alpha):
    a = _mm(og_ref[...].astype(BF16), wbg_ref[...])
    b = _mm(oa_ref[...].astype(BF16), wba_ref[...])
    merged = jax.nn.sigmoid(ga_ref[...]) * a + jax.nn.sigmoid(gb_ref[...]) * b
    y = alpha * x_ref[...] + _mm(merged.astype(BF16), wo_ref[...])
    h_ref[...] = _layer_norm(y, g_ref[...], b_ref[...])


def _finish(x, o_gdn, o_att, parts, wbg, wba, wo, g, b, *, alpha, tm):
    n = x.shape[0]
    tm = min(tm, n)
    row = lambda i: (i, 0)
    full = lambda i: (0, 0)
    return pl.pallas_call(
        functools.partial(_finish_kernel, alpha=alpha),
        grid=(n // tm,),
        in_specs=[pl.BlockSpec((tm, D), row), pl.BlockSpec((tm, D), row), pl.BlockSpec((tm, D), row),
                  pl.BlockSpec((tm, D), lambda i: (i, COL_GA // D)), pl.BlockSpec((tm, D), lambda i: (i, COL_GB // D)),
                  pl.BlockSpec((D, D), full), pl.BlockSpec((D, D), full), pl.BlockSpec((D, D), full),
                  pl.BlockSpec((1, D), full), pl.BlockSpec((1, D), full)],
        out_specs=pl.BlockSpec((tm, D), row),
        out_shape=jax.ShapeDtypeStruct((n, D), F32),
        compiler_params=_cparams(("arbitrary",)),
        name="merge_layernorm",
    )(x, o_gdn, o_att, parts, parts, wbg, wba, wo, g, b)


_CAND = [(r0, r1) for r0 in range(PEER_TOPK) for r1 in range(PEER_TOPK) if (r0 + 1) * (r1 + 1) <= PEER_TOPK]
_CAND_OFF = [next(i for i, c in enumerate(_CAND) if c[0] == r0) for r0 in range(PEER_TOPK)]
_CAND_LEN = [sum(1 for c in _CAND if c[0] == r0) for r0 in range(PEER_TOPK)]
_CAND_ROWS = -(-len(_CAND) // SUBLANE) * SUBLANE


def _top_rows(s, n_top, break_ties):
    rows, cols = s.shape
    iota = lax.broadcasted_iota(I32, (rows, cols), 0).astype(F32)
    rank = jnp.full((rows, cols), float(n_top), F32)
    vals = []
    for r in range(n_top):
        m = jnp.max(s, axis=0, keepdims=True)
        hit = s == m
        if break_ties:
            hit = iota == jnp.min(jnp.where(hit, iota, float(rows)), axis=0, keepdims=True)
        vals.append(m)
        s = jnp.where(hit, NEG_INF, s)
        rank = jnp.where(hit, float(r), rank)
    n_ranked = jnp.sum((rank < float(n_top)).astype(F32), axis=0, keepdims=True)
    return jnp.concatenate(vals, axis=0), rank, n_ranked == float(n_top)


def _peer_kernel(h_ref, wq_ref, sk_ref, u0_ref, un_ref, vp_ref, vl_ref, g_ref, b_ref, y_ref,
                 hb_scr, q_scr, rank1_scr, bt_scr, nt_scr, at_scr, s1_scr, s_scr, p_scr, acc_scr, *, alpha, n_steps):
    tt = h_ref.shape[0]
    eb = un_ref.shape[0] // 2
    e = pl.program_id(1)
    nk = PEER_NKEYS
    kk = PEER_TOPK
    tchunks = tt // LANE

    @pl.when(e == 0)
    def _prep():
        hb = h_ref[...].astype(BF16)
        hb_scr[...] = hb
        q_scr[...] = _mm(hb, wq_ref[...]).astype(BF16)
        acc_scr[...] = jnp.zeros(acc_scr.shape, F32)

        def scores(hd, carry):
            q0 = q_scr[:, pl.ds(pl.multiple_of(hd * 2 * nk, nk), nk)]
            q1 = q_scr[:, pl.ds(pl.multiple_of(hd * 2 * nk + nk, nk), nk)]
            at_scr[hd] = _nt(sk_ref[hd, 0], q0)
            s1_scr[hd] = _nt(sk_ref[hd, 1], q1)
            return carry

        lax.fori_loop(0, PEER_HEADS, scores, 0)

        def select_chunk(hd, c0, s0, s1, break_ties):
            a, rank0, ok0 = _top_rows(s0, kk, break_ties)
            b, rank1, ok1 = _top_rows(s1, kk, break_ties)
            pad = [jnp.full((_CAND_ROWS - len(_CAND), LANE), NEG_INF, F32)]
            cand = jnp.concatenate([a[r0:r0 + 1] + b[r1:r1 + 1] for r0, r1 in _CAND] + pad, axis=0)
            top, crank, okc = _top_rows(cand, kk, break_ties)
            zsum = jnp.sum(jnp.exp(top - top[0:1]), axis=0, keepdims=True)
            chosen = (crank < float(kk)).astype(F32)
            crow = lax.broadcasted_iota(I32, (_CAND_ROWS, 1), 0)
            nsel = jnp.zeros((nk, LANE), F32)
            for r0 in range(kk):
                in_group = (crow >= _CAND_OFF[r0]) & (crow < _CAND_OFF[r0] + _CAND_LEN[r0])
                cnt = jnp.sum(jnp.where(in_group, chosen, 0.0), axis=0, keepdims=True)
                nsel = nsel + jnp.where(rank0 == float(r0), cnt, 0.0)
            rank1_scr[hd, :, pl.ds(c0, LANE)] = rank1.astype(BF16)
            nt_scr[hd, :, pl.ds(c0, LANE)] = nsel
            at_scr[hd, :, pl.ds(c0, LANE)] = jnp.exp(s0 - a[0:1])
            bt_scr[hd, :, pl.ds(c0, LANE)] = (jnp.exp(s1 - b[0:1]) / zsum).astype(BF16)
            return jnp.min((ok0 & ok1 & okc).astype(F32)) > 0.5

        pair = 4 if tchunks % 4 == 0 else (2 if tchunks % 2 == 0 else 1)

        def select(idx, carry):
            hd = idx // (tchunks // pair)
            first = (idx % (tchunks // pair)) * pair
            tie_free = None
            chunks = []
            for k in range(pair):
                c0 = pl.multiple_of((first + k) * LANE, LANE)
                s0 = at_scr[hd, :, pl.ds(c0, LANE)]
                s1 = s1_scr[hd, :, pl.ds(c0, LANE)]
                chunks.append((c0, s0, s1))
            for c0, s0, s1 in chunks:
                ok = select_chunk(hd, c0, s0, s1, False)
                tie_free = ok if tie_free is None else tie_free & ok

            @pl.when(jnp.logical_not(tie_free))
            def _():
                for c0, s0, s1 in chunks:
                    select_chunk(hd, c0, s0, s1, True)

            return carry

        lax.fori_loop(0, PEER_HEADS * tchunks // pair, select, 0)

        s_scr[1] = _nt(u0_ref[...], hb_scr[...])
        p_scr[1] = jnp.zeros(p_scr.shape[1:], BF16)

    groups = eb // nk

    assert 2 * groups == SUBLANE

    def weights(half, s_ref, p_ref):
        i0 = pl.multiple_of(e * SUBLANE, SUBLANE)
        zero = jnp.zeros((nk, LANE), BF16)

        for tc in range(tchunks):
            cols = slice(tc * LANE, (tc + 1) * LANE)
            gates = [zero] * groups
            for hd in range(PEER_HEADS):
                nblk = nt_scr[hd, pl.ds(i0, SUBLANE), cols]
                ablk = at_scr[hd, pl.ds(i0, SUBLANE), cols]
                rank1 = rank1_scr[hd, :, cols]
                bval = bt_scr[hd, :, cols]
                for ii in range(groups):
                    r = half * groups + ii
                    nrow = nblk[r:r + 1].astype(BF16)
                    arow = ablk[r:r + 1].astype(BF16)
                    gates[ii] = gates[ii] + jnp.where(rank1 < nrow, bval, zero) * arow
            for ii in range(groups):
                rows = slice(ii * nk, (ii + 1) * nk)
                s = s_ref[rows, cols]
                act = 0.5 * s * (1.0 + lax.erf(s * (2.0 ** -0.5)))
                p_ref[rows, cols] = gates[ii] * act.astype(BF16)

    def step(cur, nxt):
        s_scr[nxt] = _nt(un_ref[...], hb_scr[...])
        for half in range(2):
            rows = pl.ds(half * eb, eb)
            weights(half, s_scr.at[cur, rows], p_scr.at[nxt, rows])
        acc_scr[...] += _tn(p_scr[cur], vp_ref[...])

    @pl.when(e % 2 == 0)
    def _():
        step(1, 0)

    @pl.when(e % 2 == 1)
    def _():
        step(0, 1)

    last_slot = (n_steps - 1) % 2

    @pl.when(e == n_steps - 1)
    def _():
        y = alpha * h_ref[...] + (acc_scr[...] + _tn(p_scr[last_slot], vl_ref[...]))
        y_ref[...] = _layer_norm(y, g_ref[...], b_ref[...])


def _peer(h, wq, sk, u, v, g, b, *, alpha, tt, eb):
    n = h.shape[0]
    n_exp = u.shape[0]
    nk = PEER_NKEYS
    nb = n_exp // eb
    return pl.pallas_call(
        functools.partial(_peer_kernel, alpha=alpha, n_steps=nb),
        grid=(n // tt, nb),
        in_specs=[pl.BlockSpec((tt, D), lambda t, e: (t, 0)),
                  pl.BlockSpec(wq.shape, lambda t, e: (0, 0)),
                  pl.BlockSpec(sk.shape, lambda t, e: (0, 0, 0, 0)),
                  pl.BlockSpec((eb, D), lambda t, e: (0, 0)),
                  pl.BlockSpec((eb, D), lambda t, e: (jnp.minimum(e + 1, nb - 1), 0)),
                  pl.BlockSpec((eb, D), lambda t, e: (jnp.maximum(e - 1, 0), 0)),
                  pl.BlockSpec((eb, D), lambda t, e: (nb - 1, 0)),
                  pl.BlockSpec((1, D), lambda t, e: (0, 0)),
                  pl.BlockSpec((1, D), lambda t, e: (0, 0))],
        out_specs=pl.BlockSpec((tt, D), lambda t, e: (t, 0)),
        out_shape=jax.ShapeDtypeStruct((n, D), F32),
        scratch_shapes=[pltpu.VMEM((tt, D), BF16), pltpu.VMEM((tt, wq.shape[1]), BF16),
                        pltpu.VMEM((PEER_HEADS, nk, tt), BF16), pltpu.VMEM((PEER_HEADS, nk, tt), BF16),
                        pltpu.VMEM((PEER_HEADS, nk, tt), F32), pltpu.VMEM((PEER_HEADS, nk, tt), F32),
                        pltpu.VMEM((PEER_HEADS, nk, tt), F32),
                        pltpu.VMEM((2, eb, tt), F32), pltpu.VMEM((2, eb, tt), BF16), pltpu.VMEM((tt, D), F32)],
        compiler_params=_cparams(("arbitrary", "arbitrary")),
        name="peer_ffn",
    )(h, wq, sk, u, u, v, v, g, b)


class _Tiles(NamedTuple):
    proj_rows: int
    rows: int
    attn_q: int
    attn_k: int
    peer_experts: int
    gdn_groups: int


def _tiles(batch, seq):
    rows = 512 if (batch * seq) % 512 == 0 else 256
    groups = 4 if batch % 4 == 0 else (2 if batch % 2 == 0 else 1)
    proj_rows = 1024 if (batch * seq) % 1024 == 0 else rows
    return _Tiles(proj_rows=proj_rows, rows=rows, attn_q=256 if seq % 256 == 0 else 128, attn_k=min(512, seq),
                  peer_experts=SUBLANE * PEER_NKEYS, gdn_groups=groups)


def _permute_w_in(w):
    gq = 3 * GDN_HEADS * GDN_D
    gv = GDN_HEADS * GDN_D
    aq = ATT_HEADS * ATT_DH
    akv = ATT_KV_HEADS * ATT_DH
    iqw = IDX_HEADS * IDX_DIM
    sizes = (gq, gv, GDN_HEADS, GDN_HEADS, aq, akv, akv, iqw, IDX_DIM, IDX_HEADS, D, D)
    offs = [0]
    for s in sizes:
        offs.append(offs[-1] + s)
    seg = [w[:, offs[i]:offs[i + 1]] for i in range(len(sizes))]
    qkv, z, b, a, q, k, v, iq, ik, iw, ga, gb = seg
    pad = jnp.zeros((w.shape[0], LANE - IDX_DIM - 2 * GDN_HEADS - IDX_HEADS), w.dtype)
    return jnp.concatenate([qkv, z, q, iq, ga, gb, k, v, ik, b, a, iw, pad], axis=1).astype(BF16)


def kernel(x_prompt, x_sample, cache_k, cache_v, cache_idx_k, state_conv, state_delta, page_table, meta_tokens, w_in, conv_w, a_log, dt_bias, gdn_norm_g, w_branch_gdn, w_branch_attn, w_out, ln1_g, ln1_b, peer_wq, peer_subkeys, peer_u, peer_v, ln2_g, ln2_b):
    depth = w_in.shape[0]
    assert depth == 1, "single-layer step"
    batch, seq, d = x_prompt.shape
    db, t_new, _ = x_sample.shape
    n_meta = meta_tokens.shape[0]
    assert d == D and seq % GDN_CHUNK == 0 and n_meta % SUBLANE == 0 and n_meta <= GDN_CHUNK
    alpha = (2 * depth) ** 0.25
    qkv_w = 3 * GDN_HEADS * GDN_D
    kvw = ATT_KV_HEADS * ATT_DH

    tiles = _tiles(batch, seq)
    w_r = _permute_w_in(w_in[0])
    xp = x_prompt.reshape(batch * seq, D)
    xs = x_sample.reshape(db * t_new, D)
    parts_p = _project(xp, w_r, tiles.proj_rows)
    parts_s = _project(xs, w_r, min(tiles.rows, db * t_new))
    parts_m = _project(meta_tokens.astype(F32), w_r, n_meta)

    hp = jnp.zeros((2, LANE), F32).at[0, SM_A:SM_A + GDN_HEADS].set(a_log[0]).at[1, SM_A:SM_A + GDN_HEADS].set(dt_bias[0])
    cw = conv_w[0]
    ng = gdn_norm_g[0].reshape(1, GDN_D)
    zero_prev = jnp.zeros((1, SUBLANE, qkv_w), F32)
    zero_state = jnp.zeros((1, GDN_HEADS, GDN_D, GDN_D), F32)
    _, s_meta = _gdn(parts_m, zero_prev, zero_state, cw, hp, ng, batch=1, n_chunks=1, c_in=n_meta, n_valid=n_meta,
                     n_seq=1, n_par=1, shared_init=True)
    meta_prev = parts_m[n_meta - SUBLANE:, :qkv_w].reshape(1, SUBLANE, qkv_w)
    og_p, p_delta = _gdn(parts_p, meta_prev, s_meta, cw, hp, ng, batch=batch, n_chunks=seq // GDN_CHUNK,
                         c_in=GDN_CHUNK, n_valid=GDN_CHUNK, n_seq=1, n_par=tiles.gdn_groups, shared_init=True)
    per = GDN_CHUNK // t_new
    assert per * t_new == GDN_CHUNK and t_new % SUBLANE == 0 and db % per == 0
    samp_prev = jnp.pad(state_conv.reshape(db, CONV_W - 1, qkv_w), ((0, 0), (SUBLANE - (CONV_W - 1), 0), (0, 0)))
    og_s, s_delta = _gdn(parts_s, samp_prev, state_delta.reshape(db, GDN_HEADS, GDN_D, GDN_D).astype(F32), cw, hp, ng,
                         batch=db // per, n_chunks=1, c_in=GDN_CHUNK, n_valid=GDN_CHUNK, n_seq=per, n_par=1,
                         shared_init=False)

    oa_p = _attn_prompt(parts_p, parts_m, batch=batch, seq=seq, n_meta=n_meta, tq=tiles.attn_q, kt=tiles.attn_k)
    n_el = 4 if db % 4 == 0 else (2 if db % 2 == 0 else 1)
    oa_s = _attn_sample(parts_s, cache_k, cache_v, cache_idx_k, page_table, t_new=t_new, n_el=n_el)

    wbg = w_branch_gdn[0].astype(BF16)
    wba = w_branch_attn[0].astype(BF16)
    wo = w_out[0].astype(BF16)
    g1 = ln1_g[0].reshape(1, D)
    b1 = ln1_b[0].reshape(1, D)
    h_p = _finish(xp, og_p, oa_p, parts_p, wbg, wba, wo, g1, b1, alpha=alpha, tm=tiles.rows)
    h_s = _finish(xs, og_s, oa_s, parts_s, wbg, wba, wo, g1, b1, alpha=alpha, tm=tiles.rows)

    wq = peer_wq[0].astype(BF16)
    sk = peer_subkeys[0].astype(BF16)
    u = peer_u[0].astype(BF16)
    v = peer_v[0].astype(BF16)
    g2 = ln2_g[0].reshape(1, D)
    b2 = ln2_b[0].reshape(1, D)
    y_p = _peer(h_p, wq, sk, u, v, g2, b2, alpha=alpha, tt=tiles.rows, eb=tiles.peer_experts)
    y_s = _peer(h_s, wq, sk, u, v, g2, b2, alpha=alpha, tt=min(tiles.rows, db * t_new), eb=tiles.peer_experts)

    y_prompt = y_p.reshape(batch, seq, D)
    y_sample = y_s.reshape(db, t_new, D)
    pp = parts_p.reshape(batch, seq, N_COLS)
    ps = parts_s.reshape(db, t_new, N_COLS)
    p_conv = pp[:, seq - (CONV_W - 1):, :qkv_w][None]
    s_conv = jnp.concatenate([state_conv[0].astype(F32), ps[:, :, :qkv_w]], axis=1)[:, -(CONV_W - 1):][None]

    def with_meta(col, width):
        m = jnp.broadcast_to(parts_m[None, :, col:col + width], (batch, n_meta, width))
        return jnp.concatenate([m, pp[:, :, col:col + width]], axis=1)

    p_k = with_meta(COL_AK, kvw).reshape(1, batch, seq + n_meta, ATT_KV_HEADS, ATT_DH)
    p_v = with_meta(COL_AV, kvw).reshape(1, batch, seq + n_meta, ATT_KV_HEADS, ATT_DH)
    p_idx_k = with_meta(COL_SM + SM_IK, IDX_DIM)[None]
    s_k = ps[:, :, COL_AK:COL_AK + kvw].reshape(1, db, t_new, ATT_KV_HEADS, ATT_DH)
    s_v = ps[:, :, COL_AV:COL_AV + kvw].reshape(1, db, t_new, ATT_KV_HEADS, ATT_DH)
    s_idx_k = ps[:, :, COL_SM + SM_IK:COL_SM + SM_IK + IDX_DIM][None]
    return (y_prompt, y_sample, p_conv, p_delta[None], p_k, p_v, p_idx_k, s_conv, s_delta[None].astype(state_delta.dtype),
            s_k, s_v, s_idx_k)
```

```python
import functools
from typing import NamedTuple

import jax
import jax.numpy as jnp
from jax import lax
from jax.experimental import pallas as pl
from jax.experimental.pallas import tpu as pltpu

F32 = jnp.float32
BF16 = jnp.bfloat16
I32 = jnp.int32

GDN_HEADS = 8
GDN_D = 128
CONV_W = 4
GDN_CHUNK = 64
ATT_HEADS = 8
ATT_KV_HEADS = 2
ATT_DH = 128
IDX_HEADS = 16
IDX_DIM = 64
TOPK_MAX = 256
PEER_HEADS = 8
PEER_NKEYS = 128
PEER_TOPK = 16
LN_EPS = 1e-5
RMS_EPS = 1e-6

LANE = 128
SUBLANE = 8
VMEM_LIMIT = 56 * 1024 * 1024

D = 1024
COL_QKV = 0
COL_Z = 3072
COL_AQ = 4096
COL_IQ = 5120
COL_GA = 6144
COL_GB = 7168
COL_AK = 8192
COL_AV = 8448
COL_SM = 8704
N_COLS = 8832
SM_IK = 0
SM_B = 64
SM_A = 72
SM_IW = 80

NEG_INF = float("-inf")
INT_MIN = -(2 ** 31)


def _nt(a, b):
    return lax.dot_general(a, b, (((1,), (1,)), ((), ())), preferred_element_type=F32)


def _tn(a, b):
    return lax.dot_general(a, b, (((0,), (0,)), ((), ())), preferred_element_type=F32)


def _mm(a, b):
    return jnp.dot(a, b, preferred_element_type=F32)


def _mm_f32(a, b):
    return jnp.dot(a, b, preferred_element_type=F32, precision=lax.Precision.HIGHEST)


def _cparams(sem):
    return pltpu.CompilerParams(dimension_semantics=sem, vmem_limit_bytes=VMEM_LIMIT)


def _proj_kernel(x_ref, w_ref, o_ref):
    o_ref[...] = _mm(x_ref[...].astype(BF16), w_ref[...])


def _project(x, w_bf, tm):
    n = x.shape[0]
    tn = N_COLS // 3
    return pl.pallas_call(
        _proj_kernel,
        grid=(3, n // tm),
        in_specs=[pl.BlockSpec((tm, D), lambda j, i: (i, 0)),
                  pl.BlockSpec((D, tn), lambda j, i: (0, j))],
        out_specs=pl.BlockSpec((tm, tn), lambda j, i: (i, j)),
        out_shape=jax.ShapeDtypeStruct((n, N_COLS), F32),
        compiler_params=_cparams(("arbitrary", "arbitrary")),
        name="in_projection",
    )(x, w_bf)


def _sibling_mask(ri, ci, lvl):
    return ((ri >> (lvl + 1)) == (ci >> (lvl + 1))) & (((ri >> lvl) & 1) == 1) & (((ci >> lvl) & 1) == 0)


_BATCH0 = ((0,), (0,))


def _bmm(a, b):
    return lax.dot_general(a, b, (((2,), (1,)), _BATCH0), preferred_element_type=F32)


def _bnt(a, b):
    return lax.dot_general(a, b, (((2,), (2,)), _BATCH0), preferred_element_type=F32)


def _btn(a, b):
    return lax.dot_general(a, b, (((1,), (1,)), _BATCH0), preferred_element_type=F32)


def _gdn_kernel(qkv_ref, z_ref, sm_ref, prev_ref, s0_ref, cw_ref, hp_ref, ng_ref,
                o_ref, sfin_ref, s_scr, ext_scr, *, c_in, n_valid, n_seq, shared_init):
    nh = GDN_HEADS
    hd = GDN_D
    n_par = qkv_ref.shape[0]
    stride = ext_scr.shape[1] // n_seq
    L = stride - SUBLANE
    C = L * n_seq
    ls = L.bit_length() - 1
    c = pl.program_id(1)

    @pl.when(c == 0)
    def _():
        for k in range(n_par):
            src = slice(0, n_seq) if shared_init else slice(k * n_seq, (k + 1) * n_seq)
            s_scr[k * n_seq:(k + 1) * n_seq] = s0_ref[src]
            for b in range(n_seq):
                ext_scr[k, b * stride:b * stride + SUBLANE, :] = prev_ref[src.start + b]

    cw = cw_ref[...]
    hp = hp_ref[...]
    row = lax.broadcasted_iota(I32, (C, 1), 0)
    ri = lax.broadcasted_iota(I32, (C, C), 0)
    ci = lax.broadcasted_iota(I32, (C, C), 1)
    same = (ri >> ls) == (ci >> ls)
    incl = same & (ri >= ci)
    strict = same & (ri > ci)
    eye = (ri == ci).astype(F32)

    q_l, k_l, kd_l, dec_l, lows_l, rhs_l, eg_l, z_l, g_l = [], [], [], [], [], [], [], [], []
    for kp in range(n_par):
        u = qkv_ref[kp]
        sm = sm_ref[kp]
        z = z_ref[kp]
        if c_in < C:
            u = jnp.concatenate([u, jnp.zeros((C - c_in, u.shape[1]), F32)], axis=0)
            sm = jnp.concatenate([sm, jnp.zeros((C - c_in, LANE), F32)], axis=0)
            z = jnp.concatenate([z, jnp.zeros((C - c_in, z.shape[1]), F32)], axis=0)
        pieces = []
        for b in range(n_seq):
            base = b * stride + SUBLANE
            ub_ = u[b * L:(b + 1) * L]
            ext_scr[kp, base:base + L, :] = ub_
            acc = ub_ * cw[CONV_W - 1:CONV_W, :]
            for i in range(CONV_W - 1):
                off = base - (CONV_W - 1) + i
                acc = acc + ext_scr[kp, off:off + L, :] * cw[i:i + 1, :]
            pieces.append(acc)
        conv = pieces[0] if n_seq == 1 else jnp.concatenate(pieces, axis=0)
        if n_seq == 1:
            tail = ext_scr[kp, c_in:c_in + SUBLANE, :]
            ext_scr[kp, 0:SUBLANE, :] = tail
        qkv = conv * jax.nn.sigmoid(conv)

        xs = sm + hp[1:2, :]
        softplus = jnp.maximum(xs, 0.0) + jnp.log1p(jnp.exp(-jnp.abs(xs)))
        la = -jnp.exp(hp[0:1, :]) * softplus
        beta = jax.nn.sigmoid(sm)
        if n_valid < C:
            valid = row < n_valid
            qkv = jnp.where(valid, qkv, 0.0)
            la = jnp.where(valid, la, 0.0)
            beta = jnp.where(valid, beta, 0.0)

        g = _mm_f32(incl.astype(F32), la)
        g_end = _mm_f32((ci == (ri | (L - 1))).astype(F32), g)
        gt = g.T
        g_l.append(g)
        for h in range(nh):
            col = SM_A + h
            gcol = g[:, col:col + 1]
            grow = gt[col:col + 1, :]
            bcol = beta[:, SM_B + h:SM_B + h + 1]
            q = qkv[:, h * hd:(h + 1) * hd]
            k = qkv[:, nh * hd + h * hd:nh * hd + (h + 1) * hd]
            v = qkv[:, 2 * nh * hd + h * hd:2 * nh * hd + (h + 1) * hd]
            q = q * lax.rsqrt(jnp.sum(q * q, axis=-1, keepdims=True) + RMS_EPS) * (hd ** -0.5)
            k = k * lax.rsqrt(jnp.sum(k * k, axis=-1, keepdims=True) + RMS_EPS)
            dec = jnp.exp(jnp.where(incl, gcol - grow, NEG_INF))
            eg = jnp.exp(gcol)
            q_l.append(q.astype(BF16))
            k_l.append(k.astype(BF16))
            kd_l.append((k * jnp.exp(g_end[:, col:col + 1] - gcol)).astype(BF16))
            dec_l.append(dec)
            lows_l.append(bcol * jnp.where(strict, dec, 0.0))
            rhs_l.append(jnp.concatenate([bcol * v, (bcol * eg) * k], axis=1))
            eg_l.append(eg)
            z_l.append(z[:, h * hd:(h + 1) * hd])
    qb = jnp.stack(q_l)
    kb = jnp.stack(k_l)
    kd = jnp.stack(kd_l)
    dec = jnp.stack(dec_l)
    eg = jnp.stack(eg_l)

    low = jnp.stack(lows_l) * _bnt(kb, kb)
    qk = _bnt(qb, kb)
    inv = eye[None] - jnp.where(_sibling_mask(ri, ci, 0)[None], low, 0.0)
    for lvl in range(1, ls):
        off = jnp.where(_sibling_mask(ri, ci, lvl)[None], low, 0.0)
        inv_b = inv.astype(BF16)
        inv = inv - _bmm(inv_b, _bmm(off.astype(BF16), inv_b).astype(BF16))
    uw = _bmm(inv.astype(BF16), jnp.stack(rhs_l).astype(BF16))
    w_b = uw[:, :, hd:].astype(BF16)

    def state(b):
        parts = [s_scr[kp * n_seq + b] for kp in range(n_par)]
        return parts[0] if n_par == 1 else jnp.concatenate(parts, axis=0)

    ws = None
    qs = None
    s_old = []
    for b in range(n_seq):
        S = state(b)
        s_old.append(S)
        Sb = S.astype(BF16)
        if n_seq == 1:
            wm, qm = w_b, qb
        else:
            mine = ((row >> ls) == b)[None]
            wm = jnp.where(mine, w_b, jnp.zeros_like(w_b))
            qm = jnp.where(mine, qb, jnp.zeros_like(qb))
        ws = _bmm(wm, Sb) if ws is None else ws + _bmm(wm, Sb)
        qs = _bmm(qm, Sb) if qs is None else qs + _bmm(qm, Sb)
    ub = (uw[:, :, :hd] - ws).astype(BF16)
    o = eg * qs + _bmm((dec * qk).astype(BF16), ub)
    for b in range(n_seq):
        last = b * L + L - 1
        glast = jnp.stack([g_l[kp][last:last + 1, SM_A + h:SM_A + h + 1] for kp in range(n_par) for h in range(nh)])
        kdm = kd if n_seq == 1 else jnp.where(((row >> ls) == b)[None], kd, jnp.zeros_like(kd))
        s_new = jnp.exp(glast) * s_old[b] + _btn(kdm, ub)
        for kp in range(n_par):
            s_scr[kp * n_seq + b] = s_new[kp * nh:(kp + 1) * nh]

    on = o * lax.rsqrt(jnp.mean(o * o, axis=-1, keepdims=True) + RMS_EPS) * ng_ref[...]
    zz = jnp.stack(z_l)
    res = on * (zz * jax.nn.sigmoid(zz))
    for kp in range(n_par):
        for h in range(nh):
            o_ref[kp, :, h * hd:(h + 1) * hd] = res[kp * nh + h, :c_in]

    @pl.when(c == pl.num_programs(1) - 1)
    def _():
        sfin_ref[...] = s_scr[...]


def _gdn(parts, prev, s0, cw, hp, ng, *, batch, n_chunks, c_in, n_valid, n_seq, n_par, shared_init):
    n = parts.shape[0]
    C = max(c_in, GDN_CHUNK)
    assert n_seq == 1 or (n_chunks == 1 and c_in == C)
    assert batch % n_par == 0
    L = C // n_seq
    n_init = n_seq if shared_init else n_par * n_seq
    init = (lambda b, c: (0, 0, 0)) if shared_init else (lambda b, c: (b, 0, 0))
    init4 = (lambda b, c: (0, 0, 0, 0)) if shared_init else (lambda b, c: (b, 0, 0, 0))
    qkv_w = 3 * GDN_HEADS * GDN_D
    v_w = GDN_HEADS * GDN_D
    rows = n_chunks * c_in
    parts4 = parts.reshape(batch // n_par, n_par, rows, parts.shape[1])
    o, s_fin = pl.pallas_call(
        functools.partial(_gdn_kernel, c_in=c_in, n_valid=n_valid, n_seq=n_seq, shared_init=shared_init),
        grid=(batch // n_par, n_chunks),
        in_specs=[
            pl.BlockSpec((None, n_par, c_in, qkv_w), lambda b, c: (b, 0, c, COL_QKV // qkv_w)),
            pl.BlockSpec((None, n_par, c_in, v_w), lambda b, c: (b, 0, c, COL_Z // v_w)),
            pl.BlockSpec((None, n_par, c_in, LANE), lambda b, c: (b, 0, c, COL_SM // LANE)),
            pl.BlockSpec((n_init, SUBLANE, qkv_w), init),
            pl.BlockSpec((n_init, GDN_HEADS, GDN_D, GDN_D), init4),
            pl.BlockSpec((CONV_W, qkv_w), lambda b, c: (0, 0)),
            pl.BlockSpec((2, LANE), lambda b, c: (0, 0)),
            pl.BlockSpec((1, GDN_D), lambda b, c: (0, 0)),
        ],
        out_specs=[
            pl.BlockSpec((None, n_par, c_in, v_w), lambda b, c: (b, 0, c, 0)),
            pl.BlockSpec((n_par * n_seq, GDN_HEADS, GDN_D, GDN_D), lambda b, c: (b, 0, 0, 0)),
        ],
        out_shape=[jax.ShapeDtypeStruct((batch // n_par, n_par, rows, v_w), F32),
                   jax.ShapeDtypeStruct((batch * n_seq, GDN_HEADS, GDN_D, GDN_D), F32)],
        scratch_shapes=[pltpu.VMEM((n_par * n_seq, GDN_HEADS, GDN_D, GDN_D), F32),
                        pltpu.VMEM((n_par, n_seq * (SUBLANE + L), qkv_w), F32)],
        compiler_params=_cparams(("arbitrary", "arbitrary")),
        name="gated_deltanet",
    )(parts4, parts4, parts4, prev, s0, cw, hp, ng)
    return o.reshape(n, v_w), s_fin


def _sort_key(x):
    b = pltpu.bitcast(x + 0.0, I32)
    return b ^ ((b >> 31) & 0x7FFFFFFF)


KEY_NEG_INF = -(2 ** 31) + 0x7FFFFF
SEARCH_BITS_PER_TRIP = 4


def _count_ge(key_tiles, cand):
    tot = None
    for kt in key_tiles:
        c = jnp.sum((kt >= cand).astype(F32), axis=1, keepdims=True)
        tot = c if tot is None else tot + c
    return tot


def _rows8(x):
    return x.reshape(x.shape[0] // SUBLANE, SUBLANE, x.shape[1])


def _sum_rows8(x, chains=4):
    rows = x.shape[0]
    if rows % (chains * SUBLANE):
        return jnp.sum(_rows8(x), axis=0)
    part = rows // chains
    sums = [jnp.sum(_rows8(x[k * part:(k + 1) * part]), axis=0) for k in range(chains)]
    while len(sums) > 1:
        sums = [a + b for a, b in zip(sums[0::2], sums[1::2])]
    return sums[0]


def _attn_prompt_kernel(q_ref, iq_ref, smq_ref, k_ref, v_ref, smk_ref, mk_ref, mv_ref, msm_ref,
                        o_ref, kbf, vtb, kibf, key_scr, bias_scr, s_scr, cut_scr, *, n_sel, n_meta, kt):
    tq = q_ref.shape[0]
    n_real = k_ref.shape[0]
    kvw = k_ref.shape[1]
    qb_i = pl.program_id(1)
    nt = (qb_i * tq + tq + kt - 1) // kt
    group = ATT_HEADS // ATT_KV_HEADS

    @pl.when(qb_i == 0)
    def _():
        zrows = LANE - n_meta
        kbf[0:n_real, :] = k_ref[...].astype(BF16)
        kbf[n_real:, :] = jnp.concatenate([mk_ref[...], jnp.zeros((zrows, kvw), F32)], axis=0).astype(BF16)
        vtb[:, 0:n_real] = v_ref[...].T.astype(BF16)
        vtb[:, n_real:] = jnp.concatenate([mv_ref[...], jnp.zeros((zrows, kvw), F32)], axis=0).T.astype(BF16)
        kibf[0:n_real, :] = smk_ref[...][:, SM_IK:SM_IK + IDX_DIM].astype(BF16)
        kibf[n_real:, :] = jnp.concatenate(
            [msm_ref[...][:, SM_IK:SM_IK + IDX_DIM], jnp.zeros((zrows, IDX_DIM), F32)], axis=0).astype(BF16)

    iq = iq_ref[...]
    iq_stack = jnp.concatenate([iq[:, h * IDX_DIM:(h + 1) * IDX_DIM] for h in range(IDX_HEADS)], axis=0).astype(BF16)
    w_t = smq_ref[...].T * ((IDX_DIM ** -0.5) * (IDX_HEADS ** -0.5))
    qpos = qb_i * tq + lax.broadcasted_iota(I32, (1, tq), 1)

    def index_scores(start, width):
        s_all = _nt(kibf[pl.ds(start, width), :], iq_stack)
        acc = jnp.zeros((width, tq), F32)
        for h in range(IDX_HEADS):
            acc = acc + jnp.maximum(s_all[:, h * tq:(h + 1) * tq], 0.0) * w_t[SM_IW + h:SM_IW + h + 1, :]
        return acc

    def real_tile(t, carry):
        start = pl.multiple_of(t * kt, kt)
        kpos = start + lax.broadcasted_iota(I32, (kt, 1), 0)
        key_scr[pl.ds(start, kt), :] = _sort_key(jnp.where(kpos <= qpos, index_scores(start, kt), NEG_INF))
        return carry

    lax.fori_loop(0, nt, real_tile, 0)
    mrow = lax.broadcasted_iota(I32, (LANE, 1), 0)
    key_scr[n_real:, :] = _sort_key(jnp.where(mrow < n_meta, index_scores(n_real, LANE), NEG_INF))

    def count_ge(cand):
        def body(t, acc):
            kk = key_scr[pl.ds(pl.multiple_of(t * kt, kt), kt), :]
            return acc + _sum_rows8((kk >= cand).astype(F32))

        acc = lax.fori_loop(0, nt, body, jnp.zeros((SUBLANE, tq), F32))
        acc = acc + jnp.sum(_rows8((key_scr[n_real:, :] >= cand).astype(F32)), axis=0)
        return jnp.sum(acc, axis=0, keepdims=True)

    few = qpos + 1 + n_meta <= n_sel

    def unsettled(state):
        it, _, cnt = state
        return (it < 32) & (jnp.max(jnp.where(few | (cnt == n_sel), 0.0, 1.0)) > 0.0)

    def refine(state):
        it, thr, cnt = state
        for step in range(SEARCH_BITS_PER_TRIP):
            cand = thr + lax.shift_left(jnp.int32(1), 31 - step - it)
            c = count_ge(cand)
            ok = c >= n_sel
            thr = jnp.where(ok, cand, thr)
            cnt = jnp.where(ok, c, cnt)
        return it + SEARCH_BITS_PER_TRIP, thr, cnt

    n_all = (nt * kt + LANE).astype(F32)
    _, thr, cnt = lax.while_loop(
        unsettled, refine, (jnp.int32(0), jnp.full((1, tq), INT_MIN, I32), jnp.full((1, tq), 1.0, F32) * n_all))

    tie = (cnt > n_sel) & (thr > KEY_NEG_INF) & jnp.logical_not(few)
    n_pos = n_real + n_meta
    cut_scr[...] = jnp.full((SUBLANE, tq), float(n_pos), F32)

    def pos_of(start, width, is_meta):
        r = lax.broadcasted_iota(I32, (width, 1), 0)
        return r if is_meta else n_meta + start + r

    @pl.when(jnp.max(tie.astype(F32)) > 0.0)
    def _tie_cut():
        need = n_sel - count_ge(thr + 1)
        nbits = max(1, (n_pos - 1).bit_length())

        def count_eq_le(cut):
            def body(t, acc):
                start = pl.multiple_of(t * kt, kt)
                kk = key_scr[pl.ds(start, kt), :]
                return acc + jnp.sum(_rows8(((kk == thr) & (pos_of(start, kt, False) <= cut)).astype(F32)), axis=0)

            acc = lax.fori_loop(0, nt, body, jnp.zeros((SUBLANE, tq), F32))
            mk = key_scr[n_real:, :]
            acc = acc + jnp.sum(_rows8(((mk == thr) & (pos_of(0, LANE, True) <= cut)).astype(F32)), axis=0)
            return jnp.sum(acc, axis=0, keepdims=True)

        def body(it, cut):
            cand = cut - lax.shift_left(jnp.int32(1), nbits - 1 - it)
            ok = (cand >= 0) & (count_eq_le(cand) >= need)
            return jnp.where(ok, cand, cut)

        cut = lax.fori_loop(0, nbits, body, jnp.full((1, tq), (1 << nbits) - 1, I32))
        cut = jnp.where(tie, cut, n_pos).astype(F32)
        cut_scr[...] = jnp.broadcast_to(cut, (SUBLANE, tq))

    cut = cut_scr[0:1, :].astype(I32)

    def bias_tile(start, width, is_meta):
        kk = key_scr[pl.ds(start, width), :]
        sel = (kk > thr) | ((kk == thr) & (pos_of(start, width, is_meta) <= cut))
        sel = sel & (kk > KEY_NEG_INF)
        bias_scr[pl.ds(start, width), :] = jnp.where(sel, 0.0, NEG_INF)

    def bias_body(t, carry):
        bias_tile(pl.multiple_of(t * kt, kt), kt, False)
        return carry

    lax.fori_loop(0, nt, bias_body, 0)
    bias_tile(n_real, LANE, True)

    scale = ATT_DH ** -0.5
    ng = ATT_KV_HEADS
    gw = group * tq
    q4 = [jnp.concatenate([q_ref[:, (g * group + j) * ATT_DH:(g * group + j + 1) * ATT_DH] for j in range(group)],
                          axis=0).astype(BF16) for g in range(ng)]

    def score_tile(start, width):
        b = bias_scr[pl.ds(start, width), :]
        bb = jnp.concatenate([b] * group, axis=1)
        ms = []
        for g in range(ng):
            s = _nt(kbf[pl.ds(start, width), g * ATT_DH:(g + 1) * ATT_DH], q4[g]) * scale + bb
            s_scr[pl.ds(start, width), g * gw:(g + 1) * gw] = s
            ms.append(jnp.max(_rows8(s), axis=0))
        return tuple(ms)

    def pass1(t, m):
        new = score_tile(pl.multiple_of(t * kt, kt), kt)
        return tuple(jnp.maximum(a, c) for a, c in zip(m, new))

    m8 = lax.fori_loop(0, nt, pass1, score_tile(n_real, LANE))
    m = [jnp.max(x, axis=0, keepdims=True) for x in m8]

    def prob_tile(start, width):
        out = []
        for g in range(ng):
            p = jnp.exp(s_scr[pl.ds(start, width), g * gw:(g + 1) * gw] - m[g])
            pv = _mm(vtb[g * ATT_DH:(g + 1) * ATT_DH, pl.ds(start, width)], p.astype(BF16))
            out += [_sum_rows8(p), pv]
        return tuple(out)

    def pass2(t, carry):
        new = prob_tile(pl.multiple_of(t * kt, kt), kt)
        return tuple(a + c for a, c in zip(carry, new))

    res = lax.fori_loop(0, nt, pass2, prob_tile(n_real, LANE))
    for g in range(ng):
        l, acc = res[2 * g], res[2 * g + 1]
        out_t = acc / jnp.sum(l, axis=0, keepdims=True)
        for j in range(group):
            hh = g * group + j
            o_ref[:, hh * ATT_DH:(hh + 1) * ATT_DH] = out_t[:, j * tq:(j + 1) * tq].T


def _attn_prompt(parts, mparts, *, batch, seq, n_meta, tq, kt):
    nqb = seq // tq
    n_sel = min(TOPK_MAX, (seq + n_meta) // 4)
    kvw = ATT_KV_HEADS * ATT_DH
    nk = seq + LANE
    return pl.pallas_call(
        functools.partial(_attn_prompt_kernel, n_sel=n_sel, n_meta=n_meta, kt=kt),
        grid=(batch, nqb),
        in_specs=[
            pl.BlockSpec((tq, D), lambda b, i: (b * nqb + i, COL_AQ // D)),
            pl.BlockSpec((tq, D), lambda b, i: (b * nqb + i, COL_IQ // D)),
            pl.BlockSpec((tq, LANE), lambda b, i: (b * nqb + i, COL_SM // LANE)),
            pl.BlockSpec((seq, kvw), lambda b, i: (b, COL_AK // kvw)),
            pl.BlockSpec((seq, kvw), lambda b, i: (b, COL_AV // kvw)),
            pl.BlockSpec((seq, LANE), lambda b, i: (b, COL_SM // LANE)),
            pl.BlockSpec((n_meta, kvw), lambda b, i: (0, COL_AK // kvw)),
            pl.BlockSpec((n_meta, kvw), lambda b, i: (0, COL_AV // kvw)),
            pl.BlockSpec((n_meta, LANE), lambda b, i: (0, COL_SM // LANE)),
        ],
        out_specs=pl.BlockSpec((tq, D), lambda b, i: (b * nqb + i, 0)),
        out_shape=jax.ShapeDtypeStruct((batch * seq, D), F32),
        scratch_shapes=[pltpu.VMEM((nk, kvw), BF16), pltpu.VMEM((kvw, nk), BF16), pltpu.VMEM((nk, IDX_DIM), BF16),
                        pltpu.VMEM((nk, tq), I32), pltpu.VMEM((nk, tq), F32),
                        pltpu.VMEM((nk, ATT_HEADS * tq), F32), pltpu.VMEM((SUBLANE, tq), F32)],
        compiler_params=_cparams(("arbitrary", "arbitrary")),
        name="sparse_attention_prompt",
    )(parts, parts, parts, parts, parts, parts, mparts, mparts, mparts)


def _attn_sample_kernel(pt_ref, *refs, n_pages, n_sel, n_el):
    pages = refs[0:3 * n_pages * n_el]
    q_ref, iq_ref, kn_ref, vn_ref, sm_ref, o_ref, kil, vil, kit, cut_scr = refs[3 * n_pages * n_el:]
    del pt_ref
    t_new = q_ref.shape[0] // n_el
    page = pages[2 * n_pages].shape[1]
    past = n_pages * page
    nkv = ATT_KV_HEADS
    group = ATT_HEADS // ATT_KV_HEADS
    zpad = jnp.zeros((LANE - t_new, nkv * ATT_DH), BF16)
    trow = lax.broadcasted_iota(I32, (t_new, LANE), 0)
    tlane = lax.broadcasted_iota(I32, (t_new, LANE), 1)

    knew, vnew, key_past_l, key_new_l = [], [], [], []
    for el in range(n_el):
        kp = pages[3 * n_pages * el:3 * n_pages * el + n_pages]
        vp = pages[3 * n_pages * el + n_pages:3 * n_pages * el + 2 * n_pages]
        ip = pages[3 * n_pages * el + 2 * n_pages:3 * n_pages * (el + 1)]
        rows = slice(el * t_new, (el + 1) * t_new)
        for j in range(n_pages):
            kil[el, j * nkv * page:(j + 1) * nkv * page, :] = kp[j][...].astype(BF16)
            vil[el, j * nkv * page:(j + 1) * nkv * page, :] = vp[j][...].astype(BF16)
            kit[el, :, j * page:(j + 1) * page] = ip[j][...].astype(BF16)
        sm = sm_ref[rows, :]
        knew.append(jnp.concatenate([kn_ref[rows, :].astype(BF16), zpad], axis=0))
        vnew.append(jnp.concatenate([vn_ref[rows, :].astype(BF16), zpad], axis=0))
        ki_new = jnp.concatenate(
            [sm[:, SM_IK:SM_IK + IDX_DIM].astype(BF16), jnp.zeros((LANE - t_new, IDX_DIM), BF16)], axis=0)

        iq = iq_ref[rows, :]
        qi = jnp.concatenate([iq[:, h * IDX_DIM:(h + 1) * IDX_DIM] for h in range(IDX_HEADS)], axis=0).astype(BF16)
        wcol = jnp.concatenate([sm[:, SM_IW + h:SM_IW + h + 1] for h in range(IDX_HEADS)], axis=0)
        wcol = wcol * ((IDX_DIM ** -0.5) * (IDX_HEADS ** -0.5))

        def idx_scores(qk):
            s = jnp.maximum(qk, 0.0) * wcol
            acc = s[0:t_new]
            for h in range(1, IDX_HEADS):
                acc = acc + s[h * t_new:(h + 1) * t_new]
            return acc

        key_past_l.append(_sort_key(idx_scores(_mm(qi, kit[el]))))
        key_new_l.append(_sort_key(jnp.where(tlane <= trow, idx_scores(_nt(qi, ki_new)), NEG_INF)))

    key_past = key_past_l[0] if n_el == 1 else jnp.concatenate(key_past_l, axis=0)
    key_new = key_new_l[0] if n_el == 1 else jnp.concatenate(key_new_l, axis=0)
    tiles = [key_past, key_new]
    tr = n_el * t_new

    def digit_body(it, thr):
        step = lax.shift_left(jnp.int32(1), 28 - 4 * it)
        digit = jnp.zeros((tr, 1), I32)
        for j in range(1, 16):
            digit = digit + (_count_ge(tiles, thr + j * step) >= n_sel).astype(I32)
        return thr + digit * step

    thr = lax.fori_loop(0, 8, digit_body, jnp.full((tr, 1), INT_MIN, I32))
    n_ge = _count_ge(tiles, thr)
    tie = (n_ge > n_sel) & (thr > KEY_NEG_INF)
    pos_past = lax.broadcasted_iota(I32, (1, past), 1)
    pos_new = past + lax.broadcasted_iota(I32, (1, LANE), 1)
    n_pos = past + t_new
    cut_scr[...] = jnp.full(cut_scr.shape, n_pos, I32)

    @pl.when(jnp.max(tie.astype(F32)) > 0.0)
    def _tie_cut():
        need = n_sel - _count_ge(tiles, thr + 1)
        nbits = max(1, (n_pos - 1).bit_length())

        def cut_body(it, cut):
            cand = cut - lax.shift_left(jnp.int32(1), nbits - 1 - it)
            cnt = (jnp.sum(((key_past == thr) & (pos_past <= cand)).astype(F32), axis=1, keepdims=True)
                   + jnp.sum(((key_new == thr) & (pos_new <= cand)).astype(F32), axis=1, keepdims=True))
            ok = (cand >= 0) & (cnt >= need)
            return jnp.where(ok, cand, cut)

        cut = lax.fori_loop(0, nbits, cut_body, jnp.full((tr, 1), (1 << nbits) - 1, I32))
        cut_scr[...] = jnp.broadcast_to(jnp.where(tie, cut, n_pos), cut_scr.shape)

    cut = cut_scr[:, 0:1]

    def chosen(kk, pos):
        return ((kk > thr) | ((kk == thr) & (pos <= cut))) & (kk > KEY_NEG_INF)

    bias_new = jnp.where(chosen(key_new, pos_new), 0.0, NEG_INF)
    sel01 = chosen(key_past, pos_past).astype(BF16)
    er = lax.broadcasted_iota(I32, (page, nkv * page), 0)
    ec = lax.broadcasted_iota(I32, (page, nkv * page), 1)
    expand = ((ec >= er * nkv) & (ec < (er + 1) * nkv)).astype(BF16)
    sel_il = jnp.concatenate([_mm(sel01[:, j * page:(j + 1) * page], expand) for j in range(n_pages)], axis=1)
    il_lane = lax.broadcasted_iota(I32, (1, nkv * past), 1)
    assert nkv & (nkv - 1) == 0
    il_head = il_lane & (nkv - 1)
    scale = ATT_DH ** -0.5
    for el in range(n_el):
        rows = slice(el * t_new, (el + 1) * t_new)
        q = q_ref[rows, :]
        b_new = jnp.concatenate([bias_new[rows]] * group, axis=0)
        for g in range(ATT_KV_HEADS):
            qg = jnp.concatenate(
                [q[:, (g * group + j) * ATT_DH:(g * group + j + 1) * ATT_DH] for j in range(group)], axis=0).astype(BF16)
            b_g = jnp.where((sel_il[rows] > 0.5) & (il_head == g), 0.0, NEG_INF)
            s_p = _nt(qg, kil[el]) * scale + jnp.concatenate([b_g] * group, axis=0)
            s_n = _nt(qg, knew[el][:, g * ATT_DH:(g + 1) * ATT_DH]) * scale + b_new
            m = jnp.maximum(jnp.max(s_p, axis=1, keepdims=True), jnp.max(s_n, axis=1, keepdims=True))
            p_p = jnp.exp(s_p - m)
            p_n = jnp.exp(s_n - m)
            l = jnp.sum(p_p, axis=1, keepdims=True) + jnp.sum(p_n, axis=1, keepdims=True)
            acc = _mm(p_p.astype(BF16), vil[el]) + _mm(p_n.astype(BF16), vnew[el][:, g * ATT_DH:(g + 1) * ATT_DH])
            res = acc / l
            for j in range(group):
                hh = g * group + j
                o_ref[rows, hh * ATT_DH:(hh + 1) * ATT_DH] = res[j * t_new:(j + 1) * t_new]


def _attn_sample(parts, cache_k, cache_v, cache_ik, page_table, *, t_new, n_el):
    db, n_pages = page_table.shape
    assert db % n_el == 0
    page = cache_k.shape[2]
    kvw = ATT_KV_HEADS * ATT_DH
    past = n_pages * page
    n_sel = min(TOPK_MAX, (past + t_new) // 4)
    tr = n_el * t_new

    n_pool = cache_k.shape[1]
    cache_k = cache_k.reshape(n_pool, page * ATT_KV_HEADS, ATT_DH)
    cache_v = cache_v.reshape(n_pool, page * ATT_KV_HEADS, ATT_DH)
    cache_ik = jnp.swapaxes(cache_ik, 2, 3).reshape(n_pool, IDX_DIM, page)

    def kv_page(el, j):
        return pl.BlockSpec((None, page * ATT_KV_HEADS, ATT_DH), lambda b, pt, el=el, j=j: (pt[b * n_el + el, j], 0, 0))

    def ik_page(el, j):
        return pl.BlockSpec((None, IDX_DIM, page), lambda b, pt, el=el, j=j: (pt[b * n_el + el, j], 0, 0))

    in_specs, operands = [], []
    for el in range(n_el):
        in_specs += [kv_page(el, j) for j in range(n_pages)] + [kv_page(el, j) for j in range(n_pages)]
        in_specs += [ik_page(el, j) for j in range(n_pages)]
        operands += [cache_k] * n_pages + [cache_v] * n_pages + [cache_ik] * n_pages
    in_specs += [pl.BlockSpec((tr, D), lambda b, pt: (b, COL_AQ // D)),
                 pl.BlockSpec((tr, D), lambda b, pt: (b, COL_IQ // D)),
                 pl.BlockSpec((tr, kvw), lambda b, pt: (b, COL_AK // kvw)),
                 pl.BlockSpec((tr, kvw), lambda b, pt: (b, COL_AV // kvw)),
                 pl.BlockSpec((tr, LANE), lambda b, pt: (b, COL_SM // LANE))]
    grid_spec = pltpu.PrefetchScalarGridSpec(
        num_scalar_prefetch=1, grid=(db // n_el,), in_specs=in_specs,
        out_specs=pl.BlockSpec((tr, D), lambda b, pt: (b, 0)),
        scratch_shapes=[pltpu.VMEM((n_el, past * ATT_KV_HEADS, ATT_DH), BF16),
                        pltpu.VMEM((n_el, past * ATT_KV_HEADS, ATT_DH), BF16),
                        pltpu.VMEM((n_el, IDX_DIM, past), BF16), pltpu.VMEM((tr, LANE), I32)])
    return pl.pallas_call(
        functools.partial(_attn_sample_kernel, n_pages=n_pages, n_sel=n_sel, n_el=n_el),
        grid_spec=grid_spec,
        out_shape=jax.ShapeDtypeStruct((db * t_new, D), F32),
        compiler_params=_cparams(("arbitrary",)),
        name="sparse_attention_sample",
    )(page_table, *operands, parts, parts, parts, parts, parts)


def _layer_norm(x, g, b):
    mu = jnp.mean(x, axis=-1, keepdims=True)
    xc = x - mu
    var = jnp.mean(xc * xc, axis=-1, keepdims=True)
    return xc * lax.rsqrt(var + LN_EPS) * g + b


def _finish_kernel(x_ref, og_ref, oa_ref, ga_ref, gb_ref, wbg_ref, wba_ref, wo_ref, g_ref, b_ref, h_ref, *, alpha):
    a = _mm(og_ref[...].astype(BF16), wbg_ref[...])
    b = _mm(oa_ref[...].astype(BF16), wba_ref[...])
    merged = jax.nn.sigmoid(ga_ref[...]) * a + jax.nn.sigmoid(gb_ref[...]) * b
    y = alpha * x_ref[...] + _mm(merged.astype(BF16), wo_ref[...])
    h_ref[...] = _layer_norm(y, g_ref[...], b_ref[...])


def _finish(x, o_gdn, o_att, parts, wbg, wba, wo, g, b, *, alpha, tm):
    n = x.shape[0]
    tm = min(tm, n)
    row = lambda i: (i, 0)
    full = lambda i: (0, 0)
    return pl.pallas_call(
        functools.partial(_finish_kernel, alpha=alpha),
        grid=(n // tm,),
        in_specs=[pl.BlockSpec((tm, D), row), pl.BlockSpec((tm, D), row), pl.BlockSpec((tm, D), row),
                  pl.BlockSpec((tm, D), lambda i: (i, COL_GA // D)), pl.BlockSpec((tm, D), lambda i: (i, COL_GB // D)),
                  pl.BlockSpec((D, D), full), pl.BlockSpec((D, D), full), pl.BlockSpec((D, D), full),
                  pl.BlockSpec((1, D), full), pl.BlockSpec((1, D), full)],
        out_specs=pl.BlockSpec((tm, D), row),
        out_shape=jax.ShapeDtypeStruct((n, D), F32),
        compiler_params=_cparams(("arbitrary",)),
        name="merge_layernorm",
    )(x, o_gdn, o_att, parts, parts, wbg, wba, wo, g, b)


_CAND = [(r0, r1) for r0 in range(PEER_TOPK) for r1 in range(PEER_TOPK) if (r0 + 1) * (r1 + 1) <= PEER_TOPK]
_CAND_OFF = [next(i for i, c in enumerate(_CAND) if c[0] == r0) for r0 in range(PEER_TOPK)]
_CAND_LEN = [sum(1 for c in _CAND if c[0] == r0) for r0 in range(PEER_TOPK)]
_CAND_ROWS = -(-len(_CAND) // SUBLANE) * SUBLANE


def _top_rows(s, n_top, break_ties):
    rows, cols = s.shape
    iota = lax.broadcasted_iota(I32, (rows, cols), 0).astype(F32)
    rank = jnp.full((rows, cols), float(n_top), F32)
    vals = []
    for r in range(n_top):
        m = jnp.max(s, axis=0, keepdims=True)
        hit = s == m
        if break_ties:
            hit = iota == jnp.min(jnp.where(hit, iota, float(rows)), axis=0, keepdims=True)
        vals.append(m)
        s = jnp.where(hit, NEG_INF, s)
        rank = jnp.where(hit, float(r), rank)
    n_ranked = jnp.sum((rank < float(n_top)).astype(F32), axis=0, keepdims=True)
    return jnp.concatenate(vals, axis=0), rank, n_ranked == float(n_top)


def _peer_kernel(h_ref, wq_ref, sk_ref, u0_ref, un_ref, vp_ref, vl_ref, g_ref, b_ref, y_ref,
                 hb_scr, q_scr, rank1_scr, bt_scr, nt_scr, at_scr, s1_scr, s_scr, p_scr, acc_scr, *, alpha, n_steps):
    tt = h_ref.shape[0]
    eb = un_ref.shape[0] // 2
    e = pl.program_id(1)
    nk = PEER_NKEYS
    kk = PEER_TOPK
    tchunks = tt // LANE

    @pl.when(e == 0)
    def _prep():
        hb = h_ref[...].astype(BF16)
        hb_scr[...] = hb
        q_scr[...] = _mm(hb, wq_ref[...]).astype(BF16)
        acc_scr[...] = jnp.zeros(acc_scr.shape, F32)

        def scores(hd, carry):
            q0 = q_scr[:, pl.ds(pl.multiple_of(hd * 2 * nk, nk), nk)]
            q1 = q_scr[:, pl.ds(pl.multiple_of(hd * 2 * nk + nk, nk), nk)]
            at_scr[hd] = _nt(sk_ref[hd, 0], q0)
            s1_scr[hd] = _nt(sk_ref[hd, 1], q1)
            return carry

        lax.fori_loop(0, PEER_HEADS, scores, 0)

        def select_chunk(hd, c0, s0, s1, break_ties):
            a, rank0, ok0 = _top_rows(s0, kk, break_ties)
            b, rank1, ok1 = _top_rows(s1, kk, break_ties)
            pad = [jnp.full((_CAND_ROWS - len(_CAND), LANE), NEG_INF, F32)]
            cand = jnp.concatenate([a[r0:r0 + 1] + b[r1:r1 + 1] for r0, r1 in _CAND] + pad, axis=0)
            top, crank, okc = _top_rows(cand, kk, break_ties)
            zsum = jnp.sum(jnp.exp(top - top[0:1]), axis=0, keepdims=True)
            chosen = (crank < float(kk)).astype(F32)
            crow = lax.broadcasted_iota(I32, (_CAND_ROWS, 1), 0)
            nsel = jnp.zeros((nk, LANE), F32)
            for r0 in range(kk):
                in_group = (crow >= _CAND_OFF[r0]) & (crow < _CAND_OFF[r0] + _CAND_LEN[r0])
                cnt = jnp.sum(jnp.where(in_group, chosen, 0.0), axis=0, keepdims=True)
                nsel = nsel + jnp.where(rank0 == float(r0), cnt, 0.0)
            rank1_scr[hd, :, pl.ds(c0, LANE)] = rank1.astype(BF16)
            nt_scr[hd, :, pl.ds(c0, LANE)] = nsel
            at_scr[hd, :, pl.ds(c0, LANE)] = jnp.exp(s0 - a[0:1])
            bt_scr[hd, :, pl.ds(c0, LANE)] = (jnp.exp(s1 - b[0:1]) / zsum).astype(BF16)
            return jnp.min((ok0 & ok1 & okc).astype(F32)) > 0.5

        pair = 4 if tchunks % 4 == 0 else (2 if tchunks % 2 == 0 else 1)

        def select(idx, carry):
            hd = idx // (tchunks // pair)
            first = (idx % (tchunks // pair)) * pair
            tie_free = None
            chunks = []
            for k in range(pair):
                c0 = pl.multiple_of((first + k) * LANE, LANE)
                s0 = at_scr[hd, :, pl.ds(c0, LANE)]
                s1 = s1_scr[hd, :, pl.ds(c0, LANE)]
                chunks.append((c0, s0, s1))
            for c0, s0, s1 in chunks:
                ok = select_chunk(hd, c0, s0, s1, False)
                tie_free = ok if tie_free is None else tie_free & ok

            @pl.when(jnp.logical_not(tie_free))
            def _():
                for c0, s0, s1 in chunks:
                    select_chunk(hd, c0, s0, s1, True)

            return carry

        lax.fori_loop(0, PEER_HEADS * tchunks // pair, select, 0)

        s_scr[1] = _nt(u0_ref[...], hb_scr[...])
        p_scr[1] = jnp.zeros(p_scr.shape[1:], BF16)

    groups = eb // nk

    assert 2 * groups == SUBLANE

    def weights(half, s_ref, p_ref):
        i0 = pl.multiple_of(e * SUBLANE, SUBLANE)
        zero = jnp.zeros((nk, LANE), BF16)

        for tc in range(tchunks):
            cols = slice(tc * LANE, (tc + 1) * LANE)
            gates = [zero] * groups
            for hd in range(PEER_HEADS):
                nblk = nt_scr[hd, pl.ds(i0, SUBLANE), cols]
                ablk = at_scr[hd, pl.ds(i0, SUBLANE), cols]
                rank1 = rank1_scr[hd, :, cols]
                bval = bt_scr[hd, :, cols]
                for ii in range(groups):
                    r = half * groups + ii
                    nrow = nblk[r:r + 1].astype(BF16)
                    arow = ablk[r:r + 1].astype(BF16)
                    gates[ii] = gates[ii] + jnp.where(rank1 < nrow, bval, zero) * arow
            for ii in range(groups):
                rows = slice(ii * nk, (ii + 1) * nk)
                s = s_ref[rows, cols]
                act = 0.5 * s * (1.0 + lax.erf(s * (2.0 ** -0.5)))
                p_ref[rows, cols] = gates[ii] * act.astype(BF16)

    def step(cur, nxt):
        s_scr[nxt] = _nt(un_ref[...], hb_scr[...])
        for half in range(2):
            rows = pl.ds(half * eb, eb)
            weights(half, s_scr.at[cur, rows], p_scr.at[nxt, rows])
        acc_scr[...] += _tn(p_scr[cur], vp_ref[...])

    @pl.when(e % 2 == 0)
    def _():
        step(1, 0)

    @pl.when(e % 2 == 1)
    def _():
        step(0, 1)

    last_slot = (n_steps - 1) % 2

    @pl.when(e == n_steps - 1)
    def _():
        y = alpha * h_ref[...] + (acc_scr[...] + _tn(p_scr[last_slot], vl_ref[...]))
        y_ref[...] = _layer_norm(y, g_ref[...], b_ref[...])


def _peer(h, wq, sk, u, v, g, b, *, alpha, tt, eb):
    n = h.shape[0]
    n_exp = u.shape[0]
    nk = PEER_NKEYS
    nb = n_exp // eb
    return pl.pallas_call(
        functools.partial(_peer_kernel, alpha=alpha, n_steps=nb),
        grid=(n // tt, nb),
        in_specs=[pl.BlockSpec((tt, D), lambda t, e: (t, 0)),
                  pl.BlockSpec(wq.shape, lambda t, e: (0, 0)),
                  pl.BlockSpec(sk.shape, lambda t, e: (0, 0, 0, 0)),
                  pl.BlockSpec((eb, D), lambda t, e: (0, 0)),
                  pl.BlockSpec((eb, D), lambda t, e: (jnp.minimum(e + 1, nb - 1), 0)),
                  pl.BlockSpec((eb, D), lambda t, e: (jnp.maximum(e - 1, 0), 0)),
                  pl.BlockSpec((eb, D), lambda t, e: (nb - 1, 0)),
                  pl.BlockSpec((1, D), lambda t, e: (0, 0)),
                  pl.BlockSpec((1, D), lambda t, e: (0, 0))],
        out_specs=pl.BlockSpec((tt, D), lambda t, e: (t, 0)),
        out_shape=jax.ShapeDtypeStruct((n, D), F32),
        scratch_shapes=[pltpu.VMEM((tt, D), BF16), pltpu.VMEM((tt, wq.shape[1]), BF16),
                        pltpu.VMEM((PEER_HEADS, nk, tt), BF16), pltpu.VMEM((PEER_HEADS, nk, tt), BF16),
                        pltpu.VMEM((PEER_HEADS, nk, tt), F32), pltpu.VMEM((PEER_HEADS, nk, tt), F32),
                        pltpu.VMEM((PEER_HEADS, nk, tt), F32),
                        pltpu.VMEM((2, eb, tt), F32), pltpu.VMEM((2, eb, tt), BF16), pltpu.VMEM((tt, D), F32)],
        compiler_params=_cparams(("arbitrary", "arbitrary")),
        name="peer_ffn",
    )(h, wq, sk, u, u, v, v, g, b)


class _Tiles(NamedTuple):
    proj_rows: int
    rows: int
    attn_q: int
    attn_k: int
    peer_experts: int
    gdn_groups: int


def _tiles(batch, seq):
    rows = 512 if (batch * seq) % 512 == 0 else 256
    groups = 4 if batch % 4 == 0 else (2 if batch % 2 == 0 else 1)
    proj_rows = 1024 if (batch * seq) % 1024 == 0 else rows
    return _Tiles(proj_rows=proj_rows, rows=rows, attn_q=256 if seq % 256 == 0 else 128, attn_k=min(512, seq),
                  peer_experts=SUBLANE * PEER_NKEYS, gdn_groups=groups)


def _permute_w_in(w):
    gq = 3 * GDN_HEADS * GDN_D
    gv = GDN_HEADS * GDN_D
    aq = ATT_HEADS * ATT_DH
    akv = ATT_KV_HEADS * ATT_DH
    iqw = IDX_HEADS * IDX_DIM
    sizes = (gq, gv, GDN_HEADS, GDN_HEADS, aq, akv, akv, iqw, IDX_DIM, IDX_HEADS, D, D)
    offs = [0]
    for s in sizes:
        offs.append(offs[-1] + s)
    seg = [w[:, offs[i]:offs[i + 1]] for i in range(len(sizes))]
    qkv, z, b, a, q, k, v, iq, ik, iw, ga, gb = seg
    pad = jnp.zeros((w.shape[0], LANE - IDX_DIM - 2 * GDN_HEADS - IDX_HEADS), w.dtype)
    return jnp.concatenate([qkv, z, q, iq, ga, gb, k, v, ik, b, a, iw, pad], axis=1).astype(BF16)


def kernel(x_prompt, x_sample, cache_k, cache_v, cache_idx_k, state_conv, state_delta, page_table, meta_tokens, w_in, conv_w, a_log, dt_bias, gdn_norm_g, w_branch_gdn, w_branch_attn, w_out, ln1_g, ln1_b, peer_wq, peer_subkeys, peer_u, peer_v, ln2_g, ln2_b):
    depth = w_in.shape[0]
    assert depth == 1, "single-layer step"
    batch, seq, d = x_prompt.shape
    db, t_new, _ = x_sample.shape
    n_meta = meta_tokens.shape[0]
    assert d == D and seq % GDN_CHUNK == 0 and n_meta % SUBLANE == 0 and n_meta <= GDN_CHUNK
    alpha = (2 * depth) ** 0.25
    qkv_w = 3 * GDN_HEADS * GDN_D
    kvw = ATT_KV_HEADS * ATT_DH

    tiles = _tiles(batch, seq)
    w_r = _permute_w_in(w_in[0])
    xp = x_prompt.reshape(batch * seq, D)
    xs = x_sample.reshape(db * t_new, D)
    parts_p = _project(xp, w_r, tiles.proj_rows)
    parts_s = _project(xs, w_r, min(tiles.rows, db * t_new))
    parts_m = _project(meta_tokens.astype(F32), w_r, n_meta)

    hp = jnp.zeros((2, LANE), F32).at[0, SM_A:SM_A + GDN_HEADS].set(a_log[0]).at[1, SM_A:SM_A + GDN_HEADS].set(dt_bias[0])
    cw = conv_w[0]
    ng = gdn_norm_g[0].reshape(1, GDN_D)
    zero_prev = jnp.zeros((1, SUBLANE, qkv_w), F32)
    zero_state = jnp.zeros((1, GDN_HEADS, GDN_D, GDN_D), F32)
    _, s_meta = _gdn(parts_m, zero_prev, zero_state, cw, hp, ng, batch=1, n_chunks=1, c_in=n_meta, n_valid=n_meta,
                     n_seq=1, n_par=1, shared_init=True)
    meta_prev = parts_m[n_meta - SUBLANE:, :qkv_w].reshape(1, SUBLANE, qkv_w)
    og_p, p_delta = _gdn(parts_p, meta_prev, s_meta, cw, hp, ng, batch=batch, n_chunks=seq // GDN_CHUNK,
                         c_in=GDN_CHUNK, n_valid=GDN_CHUNK, n_seq=1, n_par=tiles.gdn_groups, shared_init=True)
    per = GDN_CHUNK // t_new
    assert per * t_new == GDN_CHUNK and t_new % SUBLANE == 0 and db % per == 0
    samp_prev = jnp.pad(state_conv.reshape(db, CONV_W - 1, qkv_w), ((0, 0), (SUBLANE - (CONV_W - 1), 0), (0, 0)))
    og_s, s_delta = _gdn(parts_s, samp_prev, state_delta.reshape(db, GDN_HEADS, GDN_D, GDN_D).astype(F32), cw, hp, ng,
                         batch=db // per, n_chunks=1, c_in=GDN_CHUNK, n_valid=GDN_CHUNK, n_seq=per, n_par=1,
                         shared_init=False)

    oa_p = _attn_prompt(parts_p, parts_m, batch=batch, seq=seq, n_meta=n_meta, tq=tiles.attn_q, kt=tiles.attn_k)
    n_el = 4 if db % 4 == 0 else (2 if db % 2 == 0 else 1)
    oa_s = _attn_sample(parts_s, cache_k, cache_v, cache_idx_k, page_table, t_new=t_new, n_el=n_el)

    wbg = w_branch_gdn[0].astype(BF16)
    wba = w_branch_attn[0].astype(BF16)
    wo = w_out[0].astype(BF16)
    g1 = ln1_g[0].reshape(1, D)
    b1 = ln1_b[0].reshape(1, D)
    h_p = _finish(xp, og_p, oa_p, parts_p, wbg, wba, wo, g1, b1, alpha=alpha, tm=tiles.rows)
    h_s = _finish(xs, og_s, oa_s, parts_s, wbg, wba, wo, g1, b1, alpha=alpha, tm=tiles.rows)

    wq = peer_wq[0].astype(BF16)
    sk = peer_subkeys[0].astype(BF16)
    u = peer_u[0].astype(BF16)
    v = peer_v[0].astype(BF16)
    g2 = ln2_g[0].reshape(1, D)
    b2 = ln2_b[0].reshape(1, D)
    y_p = _peer(h_p, wq, sk, u, v, g2, b2, alpha=alpha, tt=tiles.rows, eb=tiles.peer_experts)
    y_s = _peer(h_s, wq, sk, u, v, g2, b2, alpha=alpha, tt=min(tiles.rows, db * t_new), eb=tiles.peer_experts)

    y_prompt = y_p.reshape(batch, seq, D)
    y_sample = y_s.reshape(db, t_new, D)
    pp = parts_p.reshape(batch, seq, N_COLS)
    ps = parts_s.reshape(db, t_new, N_COLS)
    p_conv = pp[:, seq - (CONV_W - 1):, :qkv_w][None]
    s_conv = jnp.concatenate([state_conv[0].astype(F32), ps[:, :, :qkv_w]], axis=1)[:, -(CONV_W - 1):][None]

    def with_meta(col, width):
        m = jnp.broadcast_to(parts_m[None, :, col:col + width], (batch, n_meta, width))
        return jnp.concatenate([m, pp[:, :, col:col + width]], axis=1)

    p_k = with_meta(COL_AK, kvw).reshape(1, batch, seq + n_meta, ATT_KV_HEADS, ATT_DH)
    p_v = with_meta(COL_AV, kvw).reshape(1, batch, seq + n_meta, ATT_KV_HEADS, ATT_DH)
    p_idx_k = with_meta(COL_SM + SM_IK, IDX_DIM)[None]
    s_k = ps[:, :, COL_AK:COL_AK + kvw].reshape(1, db, t_new, ATT_KV_HEADS, ATT_DH)
    s_v = ps[:, :, COL_AV:COL_AV + kvw].reshape(1, db, t_new, ATT_KV_HEADS, ATT_DH)
    s_idx_k = ps[:, :, COL_SM + SM_IK:COL_SM + SM_IK + IDX_DIM][None]
    return (y_prompt, y_sample, p_conv, p_delta[None], p_k, p_v, p_idx_k, s_conv, s_delta[None].astype(state_delta.dtype),
            s_k, s_v, s_idx_k)
```

```python
import functools
from typing import NamedTuple

import jax
import jax.numpy as jnp
from jax import lax
from jax.experimental import pallas as pl
from jax.experimental.pallas import tpu as pltpu

F32 = jnp.float32
BF16 = jnp.bfloat16
I32 = jnp.int32

GDN_HEADS = 8
GDN_D = 128
CONV_W = 4
GDN_CHUNK = 64
ATT_HEADS = 8
ATT_KV_HEADS = 2
ATT_DH = 128
IDX_HEADS = 16
IDX_DIM = 64
TOPK_MAX = 256
PEER_HEADS = 8
PEER_NKEYS = 128
PEER_TOPK = 16
LN_EPS = 1e-5
RMS_EPS = 1e-6

LANE = 128
SUBLANE = 8
VMEM_LIMIT = 56 * 1024 * 1024

D = 1024
COL_QKV = 0
COL_Z = 3072
COL_AQ = 4096
COL_IQ = 5120
COL_GA = 6144
COL_GB = 7168
COL_AK = 8192
COL_AV = 8448
COL_SM = 8704
N_COLS = 8832
SM_IK = 0
SM_B = 64
SM_A = 72
SM_IW = 80

NEG_INF = float("-inf")
INT_MIN = -(2 ** 31)


def _nt(a, b):
    return lax.dot_general(a, b, (((1,), (1,)), ((), ())), preferred_element_type=F32)


def _tn(a, b):
    return lax.dot_general(a, b, (((0,), (0,)), ((), ())), preferred_element_type=F32)


def _mm(a, b):
    return jnp.dot(a, b, preferred_element_type=F32)


def _mm_f32(a, b):
    return jnp.dot(a, b, preferred_element_type=F32, precision=lax.Precision.HIGHEST)


def _cparams(sem):
    return pltpu.CompilerParams(dimension_semantics=sem, vmem_limit_bytes=VMEM_LIMIT)


def _proj_kernel(x_ref, w_ref, o_ref):
    o_ref[...] = _mm(x_ref[...].astype(BF16), w_ref[...])


def _project(x, w_bf, tm):
    n = x.shape[0]
    tn = N_COLS // 3
    return pl.pallas_call(
        _proj_kernel,
        grid=(3, n // tm),
        in_specs=[pl.BlockSpec((tm, D), lambda j, i: (i, 0)),
                  pl.BlockSpec((D, tn), lambda j, i: (0, j))],
        out_specs=pl.BlockSpec((tm, tn), lambda j, i: (i, j)),
        out_shape=jax.ShapeDtypeStruct((n, N_COLS), F32),
        compiler_params=_cparams(("arbitrary", "arbitrary")),
        name="in_projection",
    )(x, w_bf)


def _sibling_mask(ri, ci, lvl):
    return ((ri >> (lvl + 1)) == (ci >> (lvl + 1))) & (((ri >> lvl) & 1) == 1) & (((ci >> lvl) & 1) == 0)


_BATCH0 = ((0,), (0,))


def _bmm(a, b):
    return lax.dot_general(a, b, (((2,), (1,)), _BATCH0), preferred_element_type=F32)


def _bnt(a, b):
    return lax.dot_general(a, b, (((2,), (2,)), _BATCH0), preferred_element_type=F32)


def _btn(a, b):
    return lax.dot_general(a, b, (((1,), (1,)), _BATCH0), preferred_element_type=F32)


def _gdn_kernel(qkv_ref, z_ref, sm_ref, prev_ref, s0_ref, cw_ref, hp_ref, ng_ref,
                o_ref, sfin_ref, s_scr, ext_scr, *, c_in, n_valid, n_seq, shared_init):
    nh = GDN_HEADS
    hd = GDN_D
    n_par = qkv_ref.shape[0]
    stride = ext_scr.shape[1] // n_seq
    L = stride - SUBLANE
    C = L * n_seq
    ls = L.bit_length() - 1
    c = pl.program_id(1)

    @pl.when(c == 0)
    def _():
        for k in range(n_par):
            src = slice(0, n_seq) if shared_init else slice(k * n_seq, (k + 1) * n_seq)
            s_scr[k * n_seq:(k + 1) * n_seq] = s0_ref[src]
            for b in range(n_seq):
                ext_scr[k, b * stride:b * stride + SUBLANE, :] = prev_ref[src.start + b]

    cw = cw_ref[...]
    hp = hp_ref[...]
    row = lax.broadcasted_iota(I32, (C, 1), 0)
    ri = lax.broadcasted_iota(I32, (C, C), 0)
    ci = lax.broadcasted_iota(I32, (C, C), 1)
    same = (ri >> ls) == (ci >> ls)
    incl = same & (ri >= ci)
    strict = same & (ri > ci)
    eye = (ri == ci).astype(F32)

    q_l, k_l, kd_l, dec_l, lows_l, rhs_l, eg_l, z_l, g_l = [], [], [], [], [], [], [], [], []
    for kp in range(n_par):
        u = qkv_ref[kp]
        sm = sm_ref[kp]
        z = z_ref[kp]
        if c_in < C:
            u = jnp.concatenate([u, jnp.zeros((C - c_in, u.shape[1]), F32)], axis=0)
            sm = jnp.concatenate([sm, jnp.zeros((C - c_in, LANE), F32)], axis=0)
            z = jnp.concatenate([z, jnp.zeros((C - c_in, z.shape[1]), F32)], axis=0)
        pieces = []
        for b in range(n_seq):
            base = b * stride + SUBLANE
            ub_ = u[b * L:(b + 1) * L]
            ext_scr[kp, base:base + L, :] = ub_
            acc = ub_ * cw[CONV_W - 1:CONV_W, :]
            for i in range(CONV_W - 1):
                off = base - (CONV_W - 1) + i
                acc = acc + ext_scr[kp, off:off + L, :] * cw[i:i + 1, :]
            pieces.append(acc)
        conv = pieces[0] if n_seq == 1 else jnp.concatenate(pieces, axis=0)
        if n_seq == 1:
            tail = ext_scr[kp, c_in:c_in + SUBLANE, :]
            ext_scr[kp, 0:SUBLANE, :] = tail
        qkv = conv * jax.nn.sigmoid(conv)

        xs = sm + hp[1:2, :]
        softplus = jnp.maximum(xs, 0.0) + jnp.log1p(jnp.exp(-jnp.abs(xs)))
        la = -jnp.exp(hp[0:1, :]) * softplus
        beta = jax.nn.sigmoid(sm)
        if n_valid < C:
            valid = row < n_valid
            qkv = jnp.where(valid, qkv, 0.0)
            la = jnp.where(valid, la, 0.0)
            beta = jnp.where(valid, beta, 0.0)

        g = _mm_f32(incl.astype(F32), la)
        g_end = _mm_f32((ci == (ri | (L - 1))).astype(F32), g)
        gt = g.T
        g_l.append(g)
        for h in range(nh):
            col = SM_A + h
            gcol = g[:, col:col + 1]
            grow = gt[col:col + 1, :]
            bcol = beta[:, SM_B + h:SM_B + h + 1]
            q = qkv[:, h * hd:(h + 1) * hd]
            k = qkv[:, nh * hd + h * hd:nh * hd + (h + 1) * hd]
            v = qkv[:, 2 * nh * hd + h * hd:2 * nh * hd + (h + 1) * hd]
            q = q * lax.rsqrt(jnp.sum(q * q, axis=-1, keepdims=True) + RMS_EPS) * (hd ** -0.5)
            k = k * lax.rsqrt(jnp.sum(k * k, axis=-1, keepdims=True) + RMS_EPS)
            dec = jnp.exp(jnp.where(incl, gcol - grow, NEG_INF))
            eg = jnp.exp(gcol)
            q_l.append(q.astype(BF16))
            k_l.append(k.astype(BF16))
            kd_l.append((k * jnp.exp(g_end[:, col:col + 1] - gcol)).astype(BF16))
            dec_l.append(dec)
            lows_l.append(bcol * jnp.where(strict, dec, 0.0))
            rhs_l.append(jnp.concatenate([bcol * v, (bcol * eg) * k], axis=1))
            eg_l.append(eg)
            z_l.append(z[:, h * hd:(h + 1) * hd])
    qb = jnp.stack(q_l)
    kb = jnp.stack(k_l)
    kd = jnp.stack(kd_l)
    dec = jnp.stack(dec_l)
    eg = jnp.stack(eg_l)

    low = jnp.stack(lows_l) * _bnt(kb, kb)
    qk = _bnt(qb, kb)
    inv = eye[None] - jnp.where(_sibling_mask(ri, ci, 0)[None], low, 0.0)
    for lvl in range(1, ls):
        off = jnp.where(_sibling_mask(ri, ci, lvl)[None], low, 0.0)
        inv_b = inv.astype(BF16)
        inv = inv - _bmm(inv_b, _bmm(off.astype(BF16), inv_b).astype(BF16))
    uw = _bmm(inv.astype(BF16), jnp.stack(rhs_l).astype(BF16))
    w_b = uw[:, :, hd:].astype(BF16)

    def state(b):
        parts = [s_scr[kp * n_seq + b] for kp in range(n_par)]
        return parts[0] if n_par == 1 else jnp.concatenate(parts, axis=0)

    ws = None
    qs = None
    s_old = []
    for b in range(n_seq):
        S = state(b)
        s_old.append(S)
        Sb = S.astype(BF16)
        if n_seq == 1:
            wm, qm = w_b, qb
        else:
            mine = ((row >> ls) == b)[None]
            wm = jnp.where(mine, w_b, jnp.zeros_like(w_b))
            qm = jnp.where(mine, qb, jnp.zeros_like(qb))
        ws = _bmm(wm, Sb) if ws is None else ws + _bmm(wm, Sb)
        qs = _bmm(qm, Sb) if qs is None else qs + _bmm(qm, Sb)
    ub = (uw[:, :, :hd] - ws).astype(BF16)
    o = eg * qs + _bmm((dec * qk).astype(BF16), ub)
    for b in range(n_seq):
        last = b * L + L - 1
        glast = jnp.stack([g_l[kp][last:last + 1, SM_A + h:SM_A + h + 1] for kp in range(n_par) for h in range(nh)])
        kdm = kd if n_seq == 1 else jnp.where(((row >> ls) == b)[None], kd, jnp.zeros_like(kd))
        s_new = jnp.exp(glast) * s_old[b] + _btn(kdm, ub)
        for kp in range(n_par):
            s_scr[kp * n_seq + b] = s_new[kp * nh:(kp + 1) * nh]

    on = o * lax.rsqrt(jnp.mean(o * o, axis=-1, keepdims=True) + RMS_EPS) * ng_ref[...]
    zz = jnp.stack(z_l)
    res = on * (zz * jax.nn.sigmoid(zz))
    for kp in range(n_par):
        for h in range(nh):
            o_ref[kp, :, h * hd:(h + 1) * hd] = res[kp * nh + h, :c_in]

    @pl.when(c == pl.num_programs(1) - 1)
    def _():
        sfin_ref[...] = s_scr[...]


def _gdn(parts, prev, s0, cw, hp, ng, *, batch, n_chunks, c_in, n_valid, n_seq, n_par, shared_init):
    n = parts.shape[0]
    C = max(c_in, GDN_CHUNK)
    assert n_seq == 1 or (n_chunks == 1 and c_in == C)
    assert batch % n_par == 0
    L = C // n_seq
    n_init = n_seq if shared_init else n_par * n_seq
    init = (lambda b, c: (0, 0, 0)) if shared_init else (lambda b, c: (b, 0, 0))
    init4 = (lambda b, c: (0, 0, 0, 0)) if shared_init else (lambda b, c: (b, 0, 0, 0))
    qkv_w = 3 * GDN_HEADS * GDN_D
    v_w = GDN_HEADS * GDN_D
    rows = n_chunks * c_in
    parts4 = parts.reshape(batch // n_par, n_par, rows, parts.shape[1])
    o, s_fin = pl.pallas_call(
        functools.partial(_gdn_kernel, c_in=c_in, n_valid=n_valid, n_seq=n_seq, shared_init=shared_init),
        grid=(batch // n_par, n_chunks),
        in_specs=[
            pl.BlockSpec((None, n_par, c_in, qkv_w), lambda b, c: (b, 0, c, COL_QKV // qkv_w)),
            pl.BlockSpec((None, n_par, c_in, v_w), lambda b, c: (b, 0, c, COL_Z // v_w)),
            pl.BlockSpec((None, n_par, c_in, LANE), lambda b, c: (b, 0, c, COL_SM // LANE)),
            pl.BlockSpec((n_init, SUBLANE, qkv_w), init),
            pl.BlockSpec((n_init, GDN_HEADS, GDN_D, GDN_D), init4),
            pl.BlockSpec((CONV_W, qkv_w), lambda b, c: (0, 0)),
            pl.BlockSpec((2, LANE), lambda b, c: (0, 0)),
            pl.BlockSpec((1, GDN_D), lambda b, c: (0, 0)),
        ],
        out_specs=[
            pl.BlockSpec((None, n_par, c_in, v_w), lambda b, c: (b, 0, c, 0)),
            pl.BlockSpec((n_par * n_seq, GDN_HEADS, GDN_D, GDN_D), lambda b, c: (b, 0, 0, 0)),
        ],
        out_shape=[jax.ShapeDtypeStruct((batch // n_par, n_par, rows, v_w), F32),
                   jax.ShapeDtypeStruct((batch * n_seq, GDN_HEADS, GDN_D, GDN_D), F32)],
        scratch_shapes=[pltpu.VMEM((n_par * n_seq, GDN_HEADS, GDN_D, GDN_D), F32),
                        pltpu.VMEM((n_par, n_seq * (SUBLANE + L), qkv_w), F32)],
        compiler_params=_cparams(("arbitrary", "arbitrary")),
        name="gated_deltanet",
    )(parts4, parts4, parts4, prev, s0, cw, hp, ng)
    return o.reshape(n, v_w), s_fin


def _sort_key(x):
    b = pltpu.bitcast(x + 0.0, I32)
    return b ^ ((b >> 31) & 0x7FFFFFFF)


KEY_NEG_INF = -(2 ** 31) + 0x7FFFFF
SEARCH_BITS_PER_TRIP = 4


def _count_ge(key_tiles, cand):
    tot = None
    for kt in key_tiles:
        c = jnp.sum((kt >= cand).astype(F32), axis=1, keepdims=True)
        tot = c if tot is None else tot + c
    return tot


def _rows8(x):
    return x.reshape(x.shape[0] // SUBLANE, SUBLANE, x.shape[1])


def _sum_rows8(x, chains=4):
    rows = x.shape[0]
    if rows % (chains * SUBLANE):
        return jnp.sum(_rows8(x), axis=0)
    part = rows // chains
    sums = [jnp.sum(_rows8(x[k * part:(k + 1) * part]), axis=0) for k in range(chains)]
    while len(sums) > 1:
        sums = [a + b for a, b in zip(sums[0::2], sums[1::2])]
    return sums[0]


def _attn_prompt_kernel(q_ref, iq_ref, smq_ref, k_ref, v_ref, smk_ref, mk_ref, mv_ref, msm_ref,
                        o_ref, kbf, vtb, kibf, key_scr, bias_scr, s_scr, cut_scr, *, n_sel, n_meta, kt):
    tq = q_ref.shape[0]
    n_real = k_ref.shape[0]
    kvw = k_ref.shape[1]
    qb_i = pl.program_id(1)
    nt = (qb_i * tq + tq + kt - 1) // kt
    group = ATT_HEADS // ATT_KV_HEADS

    @pl.when(qb_i == 0)
    def _():
        zrows = LANE - n_meta
        kbf[0:n_real, :] = k_ref[...].astype(BF16)
        kbf[n_real:, :] = jnp.concatenate([mk_ref[...], jnp.zeros((zrows, kvw), F32)], axis=0).astype(BF16)
        vtb[:, 0:n_real] = v_ref[...].T.astype(BF16)
        vtb[:, n_real:] = jnp.concatenate([mv_ref[...], jnp.zeros((zrows, kvw), F32)], axis=0).T.astype(BF16)
        kibf[0:n_real, :] = smk_ref[...][:, SM_IK:SM_IK + IDX_DIM].astype(BF16)
        kibf[n_real:, :] = jnp.concatenate(
            [msm_ref[...][:, SM_IK:SM_IK + IDX_DIM], jnp.zeros((zrows, IDX_DIM), F32)], axis=0).astype(BF16)

    iq = iq_ref[...]
    iq_stack = jnp.concatenate([iq[:, h * IDX_DIM:(h + 1) * IDX_DIM] for h in range(IDX_HEADS)], axis=0).astype(BF16)
    w_t = smq_ref[...].T * ((IDX_DIM ** -0.5) * (IDX_HEADS ** -0.5))
    qpos = qb_i * tq + lax.broadcasted_iota(I32, (1, tq), 1)

    def index_scores(start, width):
        s_all = _nt(kibf[pl.ds(start, width), :], iq_stack)
        acc = jnp.zeros((width, tq), F32)
        for h in range(IDX_HEADS):
            acc = acc + jnp.maximum(s_all[:, h * tq:(h + 1) * tq], 0.0) * w_t[SM_IW + h:SM_IW + h + 1, :]
        return acc

    def real_tile(t, carry):
        start = pl.multiple_of(t * kt, kt)
        kpos = start + lax.broadcasted_iota(I32, (kt, 1), 0)
        key_scr[pl.ds(start, kt), :] = _sort_key(jnp.where(kpos <= qpos, index_scores(start, kt), NEG_INF))
        return carry

    lax.fori_loop(0, nt, real_tile, 0)
    mrow = lax.broadcasted_iota(I32, (LANE, 1), 0)
    key_scr[n_real:, :] = _sort_key(jnp.where(mrow < n_meta, index_scores(n_real, LANE), NEG_INF))

    def count_ge(cand):
        def body(t, acc):
            kk = key_scr[pl.ds(pl.multiple_of(t * kt, kt), kt), :]
            return acc + _sum_rows8((kk >= cand).astype(F32))

        acc = lax.fori_loop(0, nt, body, jnp.zeros((SUBLANE, tq), F32))
        acc = acc + jnp.sum(_rows8((key_scr[n_real:, :] >= cand).astype(F32)), axis=0)
        return jnp.sum(acc, axis=0, keepdims=True)

    few = qpos + 1 + n_meta <= n_sel

    def unsettled(state):
        it, _, cnt = state
        return (it < 32) & (jnp.max(jnp.where(few | (cnt == n_sel), 0.0, 1.0)) > 0.0)

    def refine(state):
        it, thr, cnt = state
        for step in range(SEARCH_BITS_PER_TRIP):
            cand = thr + lax.shift_left(jnp.int32(1), 31 - step - it)
            c = count_ge(cand)
            ok = c >= n_sel
            thr = jnp.where(ok, cand, thr)
            cnt = jnp.where(ok, c, cnt)
        return it + SEARCH_BITS_PER_TRIP, thr, cnt

    n_all = (nt * kt + LANE).astype(F32)
    _, thr, cnt = lax.while_loop(
        unsettled, refine, (jnp.int32(0), jnp.full((1, tq), INT_MIN, I32), jnp.full((1, tq), 1.0, F32) * n_all))

    tie = (cnt > n_sel) & (thr > KEY_NEG_INF) & jnp.logical_not(few)
    n_pos = n_real + n_meta
    cut_scr[...] = jnp.full((SUBLANE, tq), float(n_pos), F32)

    def pos_of(start, width, is_meta):
        r = lax.broadcasted_iota(I32, (width, 1), 0)
        return r if is_meta else n_meta + start + r

    @pl.when(jnp.max(tie.astype(F32)) > 0.0)
    def _tie_cut():
        need = n_sel - count_ge(thr + 1)
        nbits = max(1, (n_pos - 1).bit_length())

        def count_eq_le(cut):
            def body(t, acc):
                start = pl.multiple_of(t * kt, kt)
                kk = key_scr[pl.ds(start, kt), :]
                return acc + jnp.sum(_rows8(((kk == thr) & (pos_of(start, kt, False) <= cut)).astype(F32)), axis=0)

            acc = lax.fori_loop(0, nt, body, jnp.zeros((SUBLANE, tq), F32))
            mk = key_scr[n_real:, :]
            acc = acc + jnp.sum(_rows8(((mk == thr) & (pos_of(0, LANE, True) <= cut)).astype(F32)), axis=0)
            return jnp.sum(acc, axis=0, keepdims=True)

        def body(it, cut):
            cand = cut - lax.shift_left(jnp.int32(1), nbits - 1 - it)
            ok = (cand >= 0) & (count_eq_le(cand) >= need)
            return jnp.where(ok, cand, cut)

        cut = lax.fori_loop(0, nbits, body, jnp.full((1, tq), (1 << nbits) - 1, I32))
        cut = jnp.where(tie, cut, n_pos).astype(F32)
        cut_scr[...] = jnp.broadcast_to(cut, (SUBLANE, tq))

    cut = cut_scr[0:1, :].astype(I32)

    def bias_tile(start, width, is_meta):
        kk = key_scr[pl.ds(start, width), :]
        sel = (kk > thr) | ((kk == thr) & (pos_of(start, width, is_meta) <= cut))
        sel = sel & (kk > KEY_NEG_INF)
        bias_scr[pl.ds(start, width), :] = jnp.where(sel, 0.0, NEG_INF)

    def bias_body(t, carry):
        bias_tile(pl.multiple_of(t * kt, kt), kt, False)
        return carry

    lax.fori_loop(0, nt, bias_body, 0)
    bias_tile(n_real, LANE, True)

    scale = ATT_DH ** -0.5
    ng = ATT_KV_HEADS
    gw = group * tq
    q4 = [jnp.concatenate([q_ref[:, (g * group + j) * ATT_DH:(g * group + j + 1) * ATT_DH] for j in range(group)],
                          axis=0).astype(BF16) for g in range(ng)]

    def score_tile(start, width):
        b = bias_scr[pl.ds(start, width), :]
        bb = jnp.concatenate([b] * group, axis=1)
        ms = []
        for g in range(ng):
            s = _nt(kbf[pl.ds(start, width), g * ATT_DH:(g + 1) * ATT_DH], q4[g]) * scale + bb
            s_scr[pl.ds(start, width), g * gw:(g + 1) * gw] = s
            ms.append(jnp.max(_rows8(s), axis=0))
        return tuple(ms)

    def pass1(t, m):
        new = score_tile(pl.multiple_of(t * kt, kt), kt)
        return tuple(jnp.maximum(a, c) for a, c in zip(m, new))

    m8 = lax.fori_loop(0, nt, pass1, score_tile(n_real, LANE))
    m = [jnp.max(x, axis=0, keepdims=True) for x in m8]

    def prob_tile(start, width):
        out = []
        for g in range(ng):
            p = jnp.exp(s_scr[pl.ds(start, width), g * gw:(g + 1) * gw] - m[g])
            pv = _mm(vtb[g * ATT_DH:(g + 1) * ATT_DH, pl.ds(start, width)], p.astype(BF16))
            out += [_sum_rows8(p), pv]
        return tuple(out)

    def pass2(t, carry):
        new = prob_tile(pl.multiple_of(t * kt, kt), kt)
        return tuple(a + c for a, c in zip(carry, new))

    res = lax.fori_loop(0, nt, pass2, prob_tile(n_real, LANE))
    for g in range(ng):
        l, acc = res[2 * g], res[2 * g + 1]
        out_t = acc / jnp.sum(l, axis=0, keepdims=True)
        for j in range(group):
            hh = g * group + j
            o_ref[:, hh * ATT_DH:(hh + 1) * ATT_DH] = out_t[:, j * tq:(j + 1) * tq].T


def _attn_prompt(parts, mparts, *, batch, seq, n_meta, tq, kt):
    nqb = seq // tq
    n_sel = min(TOPK_MAX, (seq + n_meta) // 4)
    kvw = ATT_KV_HEADS * ATT_DH
    nk = seq + LANE
    return pl.pallas_call(
        functools.partial(_attn_prompt_kernel, n_sel=n_sel, n_meta=n_meta, kt=kt),
        grid=(batch, nqb),
        in_specs=[
            pl.BlockSpec((tq, D), lambda b, i: (b * nqb + i, COL_AQ // D)),
            pl.BlockSpec((tq, D), lambda b, i: (b * nqb + i, COL_IQ // D)),
            pl.BlockSpec((tq, LANE), lambda b, i: (b * nqb + i, COL_SM // LANE)),
            pl.BlockSpec((seq, kvw), lambda b, i: (b, COL_AK // kvw)),
            pl.BlockSpec((seq, kvw), lambda b, i: (b, COL_AV // kvw)),
            pl.BlockSpec((seq, LANE), lambda b, i: (b, COL_SM // LANE)),
            pl.BlockSpec((n_meta, kvw), lambda b, i: (0, COL_AK // kvw)),
            pl.BlockSpec((n_meta, kvw), lambda b, i: (0, COL_AV // kvw)),
            pl.BlockSpec((n_meta, LANE), lambda b, i: (0, COL_SM // LANE)),
        ],
        out_specs=pl.BlockSpec((tq, D), lambda b, i: (b * nqb + i, 0)),
        out_shape=jax.ShapeDtypeStruct((batch * seq, D), F32),
        scratch_shapes=[pltpu.VMEM((nk, kvw), BF16), pltpu.VMEM((kvw, nk), BF16), pltpu.VMEM((nk, IDX_DIM), BF16),
                        pltpu.VMEM((nk, tq), I32), pltpu.VMEM((nk, tq), F32),
                        pltpu.VMEM((nk, ATT_HEADS * tq), F32), pltpu.VMEM((SUBLANE, tq), F32)],
        compiler_params=_cparams(("arbitrary", "arbitrary")),
        name="sparse_attention_prompt",
    )(parts, parts, parts, parts, parts, parts, mparts, mparts, mparts)


def _attn_sample_kernel(pt_ref, *refs, n_pages, n_sel, n_el):
    pages = refs[0:3 * n_pages * n_el]
    q_ref, iq_ref, kn_ref, vn_ref, sm_ref, o_ref, kil, vil, kit, cut_scr = refs[3 * n_pages * n_el:]
    del pt_ref
    t_new = q_ref.shape[0] // n_el
    page = pages[2 * n_pages].shape[1]
    past = n_pages * page
    nkv = ATT_KV_HEADS
    group = ATT_HEADS // ATT_KV_HEADS
    zpad = jnp.zeros((LANE - t_new, nkv * ATT_DH), BF16)
    trow = lax.broadcasted_iota(I32, (t_new, LANE), 0)
    tlane = lax.broadcasted_iota(I32, (t_new, LANE), 1)

    knew, vnew, key_past_l, key_new_l = [], [], [], []
    for el in range(n_el):
        kp = pages[3 * n_pages * el:3 * n_pages * el + n_pages]
        vp = pages[3 * n_pages * el + n_pages:3 * n_pages * el + 2 * n_pages]
        ip = pages[3 * n_pages * el + 2 * n_pages:3 * n_pages * (el + 1)]
        rows = slice(el * t_new, (el + 1) * t_new)
        for j in range(n_pages):
            kil[el, j * nkv * page:(j + 1) * nkv * page, :] = kp[j][...].astype(BF16)
            vil[el, j * nkv * page:(j + 1) * nkv * page, :] = vp[j][...].astype(BF16)
            kit[el, :, j * page:(j + 1) * page] = ip[j][...].astype(BF16)
        sm = sm_ref[rows, :]
        knew.append(jnp.concatenate([kn_ref[rows, :].astype(BF16), zpad], axis=0))
        vnew.append(jnp.concatenate([vn_ref[rows, :].astype(BF16), zpad], axis=0))
        ki_new = jnp.concatenate(
            [sm[:, SM_IK:SM_IK + IDX_DIM].astype(BF16), jnp.zeros((LANE - t_new, IDX_DIM), BF16)], axis=0)

        iq = iq_ref[rows, :]
        qi = jnp.concatenate([iq[:, h * IDX_DIM:(h + 1) * IDX_DIM] for h in range(IDX_HEADS)], axis=0).astype(BF16)
        wcol = jnp.concatenate([sm[:, SM_IW + h:SM_IW + h + 1] for h in range(IDX_HEADS)], axis=0)
        wcol = wcol * ((IDX_DIM ** -0.5) * (IDX_HEADS ** -0.5))

        def idx_scores(qk):
            s = jnp.maximum(qk, 0.0) * wcol
            acc = s[0:t_new]
            for h in range(1, IDX_HEADS):
                acc = acc + s[h * t_new:(h + 1) * t_new]
            return acc

        key_past_l.append(_sort_key(idx_scores(_mm(qi, kit[el]))))
        key_new_l.append(_sort_key(jnp.where(tlane <= trow, idx_scores(_nt(qi, ki_new)), NEG_INF)))

    key_past = key_past_l[0] if n_el == 1 else jnp.concatenate(key_past_l, axis=0)
    key_new = key_new_l[0] if n_el == 1 else jnp.concatenate(key_new_l, axis=0)
    tiles = [key_past, key_new]
    tr = n_el * t_new

    def digit_body(it, thr):
        step = lax.shift_left(jnp.int32(1), 28 - 4 * it)
        digit = jnp.zeros((tr, 1), I32)
        for j in range(1, 16):
            digit = digit + (_count_ge(tiles, thr + j * step) >= n_sel).astype(I32)
        return thr + digit * step

    thr = lax.fori_loop(0, 8, digit_body, jnp.full((tr, 1), INT_MIN, I32))
    n_ge = _count_ge(tiles, thr)
    tie = (n_ge > n_sel) & (thr > KEY_NEG_INF)
    pos_past = lax.broadcasted_iota(I32, (1, past), 1)
    pos_new = past + lax.broadcasted_iota(I32, (1, LANE), 1)
    n_pos = past + t_new
    cut_scr[...] = jnp.full(cut_scr.shape, n_pos, I32)

    @pl.when(jnp.max(tie.astype(F32)) > 0.0)
    def _tie_cut():
        need = n_sel - _count_ge(tiles, thr + 1)
        nbits = max(1, (n_pos - 1).bit_length())

        def cut_body(it, cut):
            cand = cut - lax.shift_left(jnp.int32(1), nbits - 1 - it)
            cnt = (jnp.sum(((key_past == thr) & (pos_past <= cand)).astype(F32), axis=1, keepdims=True)
                   + jnp.sum(((key_new == thr) & (pos_new <= cand)).astype(F32), axis=1, keepdims=True))
            ok = (cand >= 0) & (cnt >= need)
            return jnp.where(ok, cand, cut)

        cut = lax.fori_loop(0, nbits, cut_body, jnp.full((tr, 1), (1 << nbits) - 1, I32))
        cut_scr[...] = jnp.broadcast_to(jnp.where(tie, cut, n_pos), cut_scr.shape)

    cut = cut_scr[:, 0:1]

    def chosen(kk, pos):
        return ((kk > thr) | ((kk == thr) & (pos <= cut))) & (kk > KEY_NEG_INF)

    bias_new = jnp.where(chosen(key_new, pos_new), 0.0, NEG_INF)
    sel01 = chosen(key_past, pos_past).astype(BF16)
    er = lax.broadcasted_iota(I32, (page, nkv * page), 0)
    ec = lax.broadcasted_iota(I32, (page, nkv * page), 1)
    expand = ((ec >= er * nkv) & (ec < (er + 1) * nkv)).astype(BF16)
    sel_il = jnp.concatenate([_mm(sel01[:, j * page:(j + 1) * page], expand) for j in range(n_pages)], axis=1)
    il_lane = lax.broadcasted_iota(I32, (1, nkv * past), 1)
    assert nkv & (nkv - 1) == 0
    il_head = il_lane & (nkv - 1)
    scale = ATT_DH ** -0.5
    for el in range(n_el):
        rows = slice(el * t_new, (el + 1) * t_new)
        q = q_ref[rows, :]
        b_new = jnp.concatenate([bias_new[rows]] * group, axis=0)
        for g in range(ATT_KV_HEADS):
            qg = jnp.concatenate(
                [q[:, (g * group + j) * ATT_DH:(g * group + j + 1) * ATT_DH] for j in range(group)], axis=0).astype(BF16)
            b_g = jnp.where((sel_il[rows] > 0.5) & (il_head == g), 0.0, NEG_INF)
            s_p = _nt(qg, kil[el]) * scale + jnp.concatenate([b_g] * group, axis=0)
            s_n = _nt(qg, knew[el][:, g * ATT_DH:(g + 1) * ATT_DH]) * scale + b_new
            m = jnp.maximum(jnp.max(s_p, axis=1, keepdims=True), jnp.max(s_n, axis=1, keepdims=True))
            p_p = jnp.exp(s_p - m)
            p_n = jnp.exp(s_n - m)
            l = jnp.sum(p_p, axis=1, keepdims=True) + jnp.sum(p_n, axis=1, keepdims=True)
            acc = _mm(p_p.astype(BF16), vil[el]) + _mm(p_n.astype(BF16), vnew[el][:, g * ATT_DH:(g + 1) * ATT_DH])
            res = acc / l
            for j in range(group):
                hh = g * group + j
                o_ref[rows, hh * ATT_DH:(hh + 1) * ATT_DH] = res[j * t_new:(j + 1) * t_new]


def _attn_sample(parts, cache_k, cache_v, cache_ik, page_table, *, t_new, n_el):
    db, n_pages = page_table.shape
    assert db % n_el == 0
    page = cache_k.shape[2]
    kvw = ATT_KV_HEADS * ATT_DH
    past = n_pages * page
    n_sel = min(TOPK_MAX, (past + t_new) // 4)
    tr = n_el * t_new

    n_pool = cache_k.shape[1]
    cache_k = cache_k.reshape(n_pool, page * ATT_KV_HEADS, ATT_DH)
    cache_v = cache_v.reshape(n_pool, page * ATT_KV_HEADS, ATT_DH)
    cache_ik = jnp.swapaxes(cache_ik, 2, 3).reshape(n_pool, IDX_DIM, page)

    def kv_page(el, j):
        return pl.BlockSpec((None, page * ATT_KV_HEADS, ATT_DH), lambda b, pt, el=el, j=j: (pt[b * n_el + el, j], 0, 0))

    def ik_page(el, j):
        return pl.BlockSpec((None, IDX_DIM, page), lambda b, pt, el=el, j=j: (pt[b * n_el + el, j], 0, 0))

    in_specs, operands = [], []
    for el in range(n_el):
        in_specs += [kv_page(el, j) for j in range(n_pages)] + [kv_page(el, j) for j in range(n_pages)]
        in_specs += [ik_page(el, j) for j in range(n_pages)]
        operands += [cache_k] * n_pages + [cache_v] * n_pages + [cache_ik] * n_pages
    in_specs += [pl.BlockSpec((tr, D), lambda b, pt: (b, COL_AQ // D)),
                 pl.BlockSpec((tr, D), lambda b, pt: (b, COL_IQ // D)),
                 pl.BlockSpec((tr, kvw), lambda b, pt: (b, COL_AK // kvw)),
                 pl.BlockSpec((tr, kvw), lambda b, pt: (b, COL_AV // kvw)),
                 pl.BlockSpec((tr, LANE), lambda b, pt: (b, COL_SM // LANE))]
    grid_spec = pltpu.PrefetchScalarGridSpec(
        num_scalar_prefetch=1, grid=(db // n_el,), in_specs=in_specs,
        out_specs=pl.BlockSpec((tr, D), lambda b, pt: (b, 0)),
        scratch_shapes=[pltpu.VMEM((n_el, past * ATT_KV_HEADS, ATT_DH), BF16),
                        pltpu.VMEM((n_el, past * ATT_KV_HEADS, ATT_DH), BF16),
                        pltpu.VMEM((n_el, IDX_DIM, past), BF16), pltpu.VMEM((tr, LANE), I32)])
    return pl.pallas_call(
        functools.partial(_attn_sample_kernel, n_pages=n_pages, n_sel=n_sel, n_el=n_el),
        grid_spec=grid_spec,
        out_shape=jax.ShapeDtypeStruct((db * t_new, D), F32),
        compiler_params=_cparams(("arbitrary",)),
        name="sparse_attention_sample",
    )(page_table, *operands, parts, parts, parts, parts, parts)


def _layer_norm(x, g, b):
    mu = jnp.mean(x, axis=-1, keepdims=True)
    xc = x - mu
    var = jnp.mean(xc * xc, axis=-1, keepdims=True)
    return xc * lax.rsqrt(var + LN_EPS) * g + b


def _finish_kernel(x_ref, og_ref, oa_ref, ga_ref, gb_ref, wbg_ref, wba_ref, wo_ref, g_ref, b_ref, h_ref, *, alpha):
    a = _mm(og_ref[...].astype(BF16), wbg_ref[...])
    b = _mm(oa_ref[...].astype(BF16), wba_ref[...])
    merged = jax.nn.sigmoid(ga_ref[...]) * a + jax.nn.sigmoid(gb_ref[...]) * b
    y = alpha * x_ref[...] + _mm(merged.astype(BF16), wo_ref[...])
    h_ref[...] = _layer_norm(y, g_ref[...], b_ref[...])


def _finish(x, o_gdn, o_att, parts, wbg, wba, wo, g, b, *, alpha, tm):
    n = x.shape[0]
    tm = min(tm, n)
    row = lambda i: (i, 0)
    full = lambda i: (0, 0)
    return pl.pallas_call(
        functools.partial(_finish_kernel, alpha=alpha),
        grid=(n // tm,),
        in_specs=[pl.BlockSpec((tm, D), row), pl.BlockSpec((tm, D), row), pl.BlockSpec((tm, D), row),
                  pl.BlockSpec((tm, D), lambda i: (i, COL_GA // D)), pl.BlockSpec((tm, D), lambda i: (i, COL_GB // D)),
                  pl.BlockSpec((D, D), full), pl.BlockSpec((D, D), full), pl.BlockSpec((D, D), full),
                  pl.BlockSpec((1, D), full), pl.BlockSpec((1, D), full)],
        out_specs=pl.BlockSpec((tm, D), row),
        out_shape=jax.ShapeDtypeStruct((n, D), F32),
        compiler_params=_cparams(("arbitrary",)),
        name="merge_layernorm",
    )(x, o_gdn, o_att, parts, parts, wbg, wba, wo, g, b)


_CAND = [(r0, r1) for r0 in range(PEER_TOPK) for r1 in range(PEER_TOPK) if (r0 + 1) * (r1 + 1) <= PEER_TOPK]
_CAND_OFF = [next(i for i, c in enumerate(_CAND) if c[0] == r0) for r0 in range(PEER_TOPK)]
_CAND_LEN = [sum(1 for c in _CAND if c[0] == r0) for r0 in range(PEER_TOPK)]
_CAND_ROWS = -(-len(_CAND) // SUBLANE) * SUBLANE


def _top_rows(s, n_top, break_ties):
    rows, cols = s.shape
    iota = lax.broadcasted_iota(I32, (rows, cols), 0).astype(F32)
    rank = jnp.full((rows, cols), float(n_top), F32)
    vals = []
    for r in range(n_top):
        m = jnp.max(s, axis=0, keepdims=True)
        hit = s == m
        if break_ties:
            hit = iota == jnp.min(jnp.where(hit, iota, float(rows)), axis=0, keepdims=True)
        vals.append(m)
        s = jnp.where(hit, NEG_INF, s)
        rank = jnp.where(hit, float(r), rank)
    n_ranked = jnp.sum((rank < float(n_top)).astype(F32), axis=0, keepdims=True)
    return jnp.concatenate(vals, axis=0), rank, n_ranked == float(n_top)


def _peer_kernel(h_ref, wq_ref, sk_ref, u0_ref, un_ref, vp_ref, vl_ref, g_ref, b_ref, y_ref,
                 hb_scr, q_scr, rank1_scr, bt_scr, nt_scr, at_scr, s1_scr, s_scr, p_scr, acc_scr, *, alpha, n_steps):
    tt = h_ref.shape[0]
    eb = un_ref.shape[0] // 2
    e = pl.program_id(1)
    nk = PEER_NKEYS
    kk = PEER_TOPK
    tchunks = tt // LANE

    @pl.when(e == 0)
    def _prep():
        hb = h_ref[...].astype(BF16)
        hb_scr[...] = hb
        q_scr[...] = _mm(hb, wq_ref[...]).astype(BF16)
        acc_scr[...] = jnp.zeros(acc_scr.shape, F32)

        def scores(hd, carry):
            q0 = q_scr[:, pl.ds(pl.multiple_of(hd * 2 * nk, nk), nk)]
            q1 = q_scr[:, pl.ds(pl.multiple_of(hd * 2 * nk + nk, nk), nk)]
            at_scr[hd] = _nt(sk_ref[hd, 0], q0)
            s1_scr[hd] = _nt(sk_ref[hd, 1], q1)
            return carry

        lax.fori_loop(0, PEER_HEADS, scores, 0, unroll=True)

        def select_chunk(hd, c0, s0, s1, break_ties):
            a, rank0, ok0 = _top_rows(s0, kk, break_ties)
            b, rank1, ok1 = _top_rows(s1, kk, break_ties)
            pad = [jnp.full((_CAND_ROWS - len(_CAND), LANE), NEG_INF, F32)]
            cand = jnp.concatenate([a[r0:r0 + 1] + b[r1:r1 + 1] for r0, r1 in _CAND] + pad, axis=0)
            top, crank, okc = _top_rows(cand, kk, break_ties)
            zsum = jnp.sum(jnp.exp(top - top[0:1]), axis=0, keepdims=True)
            chosen = (crank < float(kk)).astype(F32)
            crow = lax.broadcasted_iota(I32, (_CAND_ROWS, 1), 0)
            nsel = jnp.zeros((nk, LANE), F32)
            for r0 in range(kk):
                in_group = (crow >= _CAND_OFF[r0]) & (crow < _CAND_OFF[r0] + _CAND_LEN[r0])
                cnt = jnp.sum(jnp.where(in_group, chosen, 0.0), axis=0, keepdims=True)
                nsel = nsel + jnp.where(rank0 == float(r0), cnt, 0.0)
            rank1_scr[hd, :, pl.ds(c0, LANE)] = rank1.astype(BF16)
            nt_scr[hd, :, pl.ds(c0, LANE)] = nsel
            at_scr[hd, :, pl.ds(c0, LANE)] = jnp.exp(s0 - a[0:1])
            bt_scr[hd, :, pl.ds(c0, LANE)] = (jnp.exp(s1 - b[0:1]) / zsum).astype(BF16)
            return jnp.min((ok0 & ok1 & okc).astype(F32)) > 0.5

        pair = 4 if tchunks % 4 == 0 else (2 if tchunks % 2 == 0 else 1)

        def select(idx, carry):
            hd = idx // (tchunks // pair)
            first = (idx % (tchunks // pair)) * pair
            tie_free = None
            chunks = []
            for k in range(pair):
                c0 = pl.multiple_of((first + k) * LANE, LANE)
                s0 = at_scr[hd, :, pl.ds(c0, LANE)]
                s1 = s1_scr[hd, :, pl.ds(c0, LANE)]
                chunks.append((c0, s0, s1))
            for c0, s0, s1 in chunks:
                ok = select_chunk(hd, c0, s0, s1, False)
                tie_free = ok if tie_free is None else tie_free & ok

            @pl.when(jnp.logical_not(tie_free))
            def _():
                for c0, s0, s1 in chunks:
                    select_chunk(hd, c0, s0, s1, True)

            return carry

        lax.fori_loop(0, PEER_HEADS * tchunks // pair, select, 0)

        s_scr[1] = _nt(u0_ref[...], hb_scr[...])
        p_scr[1] = jnp.zeros(p_scr.shape[1:], BF16)

    groups = eb // nk

    assert 2 * groups == SUBLANE

    def weights(half, s_ref, p_ref):
        i0 = pl.multiple_of(e * SUBLANE, SUBLANE)
        zero = jnp.zeros((nk, LANE), BF16)

        for tc in range(tchunks):
            cols = slice(tc * LANE, (tc + 1) * LANE)
            gates = [zero] * groups
            for hd in range(PEER_HEADS):
                nblk = nt_scr[hd, pl.ds(i0, SUBLANE), cols]
                ablk = at_scr[hd, pl.ds(i0, SUBLANE), cols]
                rank1 = rank1_scr[hd, :, cols]
                bval = bt_scr[hd, :, cols]
                for ii in range(groups):
                    r = half * groups + ii
                    nrow = nblk[r:r + 1].astype(BF16)
                    arow = ablk[r:r + 1].astype(BF16)
                    gates[ii] = gates[ii] + jnp.where(rank1 < nrow, bval, zero) * arow
            for ii in range(groups):
                rows = slice(ii * nk, (ii + 1) * nk)
                s = s_ref[rows, cols]
                act = 0.5 * s * (1.0 + lax.erf(s * (2.0 ** -0.5)))
                p_ref[rows, cols] = gates[ii] * act.astype(BF16)

    def step(cur, nxt):
        s_scr[nxt] = _nt(un_ref[...], hb_scr[...])
        for half in range(2):
            rows = pl.ds(half * eb, eb)
            weights(half, s_scr.at[cur, rows], p_scr.at[nxt, rows])
        acc_scr[...] += _tn(p_scr[cur], vp_ref[...])

    @pl.when(e % 2 == 0)
    def _():
        step(1, 0)

    @pl.when(e % 2 == 1)
    def _():
        step(0, 1)

    last_slot = (n_steps - 1) % 2

    @pl.when(e == n_steps - 1)
    def _():
        y = alpha * h_ref[...] + (acc_scr[...] + _tn(p_scr[last_slot], vl_ref[...]))
        y_ref[...] = _layer_norm(y, g_ref[...], b_ref[...])


def _peer(h, wq, sk, u, v, g, b, *, alpha, tt, eb):
    n = h.shape[0]
    n_exp = u.shape[0]
    nk = PEER_NKEYS
    nb = n_exp // eb
    return pl.pallas_call(
        functools.partial(_peer_kernel, alpha=alpha, n_steps=nb),
        grid=(n // tt, nb),
        in_specs=[pl.BlockSpec((tt, D), lambda t, e: (t, 0)),
                  pl.BlockSpec(wq.shape, lambda t, e: (0, 0)),
                  pl.BlockSpec(sk.shape, lambda t, e: (0, 0, 0, 0)),
                  pl.BlockSpec((eb, D), lambda t, e: (0, 0)),
                  pl.BlockSpec((eb, D), lambda t, e: (jnp.minimum(e + 1, nb - 1), 0)),
                  pl.BlockSpec((eb, D), lambda t, e: (jnp.maximum(e - 1, 0), 0)),
                  pl.BlockSpec((eb, D), lambda t, e: (nb - 1, 0)),
                  pl.BlockSpec((1, D), lambda t, e: (0, 0)),
                  pl.BlockSpec((1, D), lambda t, e: (0, 0))],
        out_specs=pl.BlockSpec((tt, D), lambda t, e: (t, 0)),
        out_shape=jax.ShapeDtypeStruct((n, D), F32),
        scratch_shapes=[pltpu.VMEM((tt, D), BF16), pltpu.VMEM((tt, wq.shape[1]), BF16),
                        pltpu.VMEM((PEER_HEADS, nk, tt), BF16), pltpu.VMEM((PEER_HEADS, nk, tt), BF16),
                        pltpu.VMEM((PEER_HEADS, nk, tt), F32), pltpu.VMEM((PEER_HEADS, nk, tt), F32),
                        pltpu.VMEM((PEER_HEADS, nk, tt), F32),
                        pltpu.VMEM((2, eb, tt), F32), pltpu.VMEM((2, eb, tt), BF16), pltpu.VMEM((tt, D), F32)],
        compiler_params=_cparams(("arbitrary", "arbitrary")),
        name="peer_ffn",
    )(h, wq, sk, u, u, v, v, g, b)


class _Tiles(NamedTuple):
    proj_rows: int
    rows: int
    attn_q: int
    attn_k: int
    peer_experts: int
    gdn_groups: int


def _tiles(batch, seq):
    rows = 512 if (batch * seq) % 512 == 0 else 256
    groups = 4 if batch % 4 == 0 else (2 if batch % 2 == 0 else 1)
    proj_rows = 1024 if (batch * seq) % 1024 == 0 else rows
    return _Tiles(proj_rows=proj_rows, rows=rows, attn_q=256 if seq % 256 == 0 else 128, attn_k=min(512, seq),
                  peer_experts=SUBLANE * PEER_NKEYS, gdn_groups=groups)


def _permute_w_in(w):
    gq = 3 * GDN_HEADS * GDN_D
    gv = GDN_HEADS * GDN_D
    aq = ATT_HEADS * ATT_DH
    akv = ATT_KV_HEADS * ATT_DH
    iqw = IDX_HEADS * IDX_DIM
    sizes = (gq, gv, GDN_HEADS, GDN_HEADS, aq, akv, akv, iqw, IDX_DIM, IDX_HEADS, D, D)
    offs = [0]
    for s in sizes:
        offs.append(offs[-1] + s)
    seg = [w[:, offs[i]:offs[i + 1]] for i in range(len(sizes))]
    qkv, z, b, a, q, k, v, iq, ik, iw, ga, gb = seg
    pad = jnp.zeros((w.shape[0], LANE - IDX_DIM - 2 * GDN_HEADS - IDX_HEADS), w.dtype)
    return jnp.concatenate([qkv, z, q, iq, ga, gb, k, v, ik, b, a, iw, pad], axis=1).astype(BF16)


def kernel(x_prompt, x_sample, cache_k, cache_v, cache_idx_k, state_conv, state_delta, page_table, meta_tokens, w_in, conv_w, a_log, dt_bias, gdn_norm_g, w_branch_gdn, w_branch_attn, w_out, ln1_g, ln1_b, peer_wq, peer_subkeys, peer_u, peer_v, ln2_g, ln2_b):
    depth = w_in.shape[0]
    assert depth == 1, "single-layer step"
    batch, seq, d = x_prompt.shape
    db, t_new, _ = x_sample.shape
    n_meta = meta_tokens.shape[0]
    assert d == D and seq % GDN_CHUNK == 0 and n_meta % SUBLANE == 0 and n_meta <= GDN_CHUNK
    alpha = (2 * depth) ** 0.25
    qkv_w = 3 * GDN_HEADS * GDN_D
    kvw = ATT_KV_HEADS * ATT_DH

    tiles = _tiles(batch, seq)
    w_r = _permute_w_in(w_in[0])
    xp = x_prompt.reshape(batch * seq, D)
    xs = x_sample.reshape(db * t_new, D)
    parts_p = _project(xp, w_r, tiles.proj_rows)
    parts_s = _project(xs, w_r, min(tiles.rows, db * t_new))
    parts_m = _project(meta_tokens.astype(F32), w_r, n_meta)

    hp = jnp.zeros((2, LANE), F32).at[0, SM_A:SM_A + GDN_HEADS].set(a_log[0]).at[1, SM_A:SM_A + GDN_HEADS].set(dt_bias[0])
    cw = conv_w[0]
    ng = gdn_norm_g[0].reshape(1, GDN_D)
    zero_prev = jnp.zeros((1, SUBLANE, qkv_w), F32)
    zero_state = jnp.zeros((1, GDN_HEADS, GDN_D, GDN_D), F32)
    _, s_meta = _gdn(parts_m, zero_prev, zero_state, cw, hp, ng, batch=1, n_chunks=1, c_in=n_meta, n_valid=n_meta,
                     n_seq=1, n_par=1, shared_init=True)
    meta_prev = parts_m[n_meta - SUBLANE:, :qkv_w].reshape(1, SUBLANE, qkv_w)
    og_p, p_delta = _gdn(parts_p, meta_prev, s_meta, cw, hp, ng, batch=batch, n_chunks=seq // GDN_CHUNK,
                         c_in=GDN_CHUNK, n_valid=GDN_CHUNK, n_seq=1, n_par=tiles.gdn_groups, shared_init=True)
    per = GDN_CHUNK // t_new
    assert per * t_new == GDN_CHUNK and t_new % SUBLANE == 0 and db % per == 0
    samp_prev = jnp.pad(state_conv.reshape(db, CONV_W - 1, qkv_w), ((0, 0), (SUBLANE - (CONV_W - 1), 0), (0, 0)))
    og_s, s_delta = _gdn(parts_s, samp_prev, state_delta.reshape(db, GDN_HEADS, GDN_D, GDN_D).astype(F32), cw, hp, ng,
                         batch=db // per, n_chunks=1, c_in=GDN_CHUNK, n_valid=GDN_CHUNK, n_seq=per, n_par=1,
                         shared_init=False)

    oa_p = _attn_prompt(parts_p, parts_m, batch=batch, seq=seq, n_meta=n_meta, tq=tiles.attn_q, kt=tiles.attn_k)
    n_el = 4 if db % 4 == 0 else (2 if db % 2 == 0 else 1)
    oa_s = _attn_sample(parts_s, cache_k, cache_v, cache_idx_k, page_table, t_new=t_new, n_el=n_el)

    wbg = w_branch_gdn[0].astype(BF16)
    wba = w_branch_attn[0].astype(BF16)
    wo = w_out[0].astype(BF16)
    g1 = ln1_g[0].reshape(1, D)
    b1 = ln1_b[0].reshape(1, D)
    h_p = _finish(xp, og_p, oa_p, parts_p, wbg, wba, wo, g1, b1, alpha=alpha, tm=tiles.rows)
    h_s = _finish(xs, og_s, oa_s, parts_s, wbg, wba, wo, g1, b1, alpha=alpha, tm=tiles.rows)

    wq = peer_wq[0].astype(BF16)
    sk = peer_subkeys[0].astype(BF16)
    u = peer_u[0].astype(BF16)
    v = peer_v[0].astype(BF16)
    g2 = ln2_g[0].reshape(1, D)
    b2 = ln2_b[0].reshape(1, D)
    y_p = _peer(h_p, wq, sk, u, v, g2, b2, alpha=alpha, tt=tiles.rows, eb=tiles.peer_experts)
    y_s = _peer(h_s, wq, sk, u, v, g2, b2, alpha=alpha, tt=min(tiles.rows, db * t_new), eb=tiles.peer_experts)

    y_prompt = y_p.reshape(batch, seq, D)
    y_sample = y_s.reshape(db, t_new, D)
    pp = parts_p.reshape(batch, seq, N_COLS)
    ps = parts_s.reshape(db, t_new, N_COLS)
    p_conv = pp[:, seq - (CONV_W - 1):, :qkv_w][None]
    s_conv = jnp.concatenate([state_conv[0].astype(F32), ps[:, :, :qkv_w]], axis=1)[:, -(CONV_W - 1):][None]

    def with_meta(col, width):
        m = jnp.broadcast_to(parts_m[None, :, col:col + width], (batch, n_meta, width))
        return jnp.concatenate([m, pp[:, :, col:col + width]], axis=1)

    p_k = with_meta(COL_AK, kvw).reshape(1, batch, seq + n_meta, ATT_KV_HEADS, ATT_DH)
    p_v = with_meta(COL_AV, kvw).reshape(1, batch, seq + n_meta, ATT_KV_HEADS, ATT_DH)
    p_idx_k = with_meta(COL_SM + SM_IK, IDX_DIM)[None]
    s_k = ps[:, :, COL_AK:COL_AK + kvw].reshape(1, db, t_new, ATT_KV_HEADS, ATT_DH)
    s_v = ps[:, :, COL_AV:COL_AV + kvw].reshape(1, db, t_new, ATT_KV_HEADS, ATT_DH)
    s_idx_k = ps[:, :, COL_SM + SM_IK:COL_SM + SM_IK + IDX_DIM][None]
    return (y_prompt, y_sample, p_conv, p_delta[None], p_k, p_v, p_idx_k, s_conv, s_delta[None].astype(state_delta.dtype),
            s_k, s_v, s_idx_k)
```

```python
import functools
from typing import NamedTuple

import jax
import jax.numpy as jnp
from jax import lax
from jax.experimental import pallas as pl
from jax.experimental.pallas import tpu as pltpu

F32 = jnp.float32
BF16 = jnp.bfloat16
I32 = jnp.int32

GDN_HEADS = 8
GDN_D = 128
CONV_W = 4
GDN_CHUNK = 64
ATT_HEADS = 8
ATT_KV_HEADS = 2
ATT_DH = 128
IDX_HEADS = 16
IDX_DIM = 64
TOPK_MAX = 256
PEER_HEADS = 8
PEER_NKEYS = 128
PEER_TOPK = 16
LN_EPS = 1e-5
RMS_EPS = 1e-6

LANE = 128
SUBLANE = 8
VMEM_LIMIT = 56 * 1024 * 1024

D = 1024
COL_QKV = 0
COL_Z = 3072
COL_AQ = 4096
COL_IQ = 5120
COL_GA = 6144
COL_GB = 7168
COL_AK = 8192
COL_AV = 8448
COL_SM = 8704
N_COLS = 8832
SM_IK = 0
SM_B = 64
SM_A = 72
SM_IW = 80

NEG_INF = float("-inf")
INT_MIN = -(2 ** 31)


def _nt(a, b):
    return lax.dot_general(a, b, (((1,), (1,)), ((), ())), preferred_element_type=F32)


def _tn(a, b):
    return lax.dot_general(a, b, (((0,), (0,)), ((), ())), preferred_element_type=F32)


def _mm(a, b):
    return jnp.dot(a, b, preferred_element_type=F32)


def _mm_f32(a, b):
    return jnp.dot(a, b, preferred_element_type=F32, precision=lax.Precision.HIGHEST)


def _cparams(sem):
    return pltpu.CompilerParams(dimension_semantics=sem, vmem_limit_bytes=VMEM_LIMIT)


def _proj_kernel(x_ref, w_ref, o_ref):
    o_ref[...] = _mm(x_ref[...].astype(BF16), w_ref[...])


def _project(x, w_bf, tm):
    n = x.shape[0]
    tn = N_COLS // 3
    return pl.pallas_call(
        _proj_kernel,
        grid=(3, n // tm),
        in_specs=[pl.BlockSpec((tm, D), lambda j, i: (i, 0)),
                  pl.BlockSpec((D, tn), lambda j, i: (0, j))],
        out_specs=pl.BlockSpec((tm, tn), lambda j, i: (i, j)),
        out_shape=jax.ShapeDtypeStruct((n, N_COLS), F32),
        compiler_params=_cparams(("arbitrary", "arbitrary")),
        name="in_projection",
    )(x, w_bf)


def _sibling_mask(ri, ci, lvl):
    return ((ri >> (lvl + 1)) == (ci >> (lvl + 1))) & (((ri >> lvl) & 1) == 1) & (((ci >> lvl) & 1) == 0)


_BATCH0 = ((0,), (0,))


def _bmm(a, b):
    return lax.dot_general(a, b, (((2,), (1,)), _BATCH0), preferred_element_type=F32)


def _bnt(a, b):
    return lax.dot_general(a, b, (((2,), (2,)), _BATCH0), preferred_element_type=F32)


def _btn(a, b):
    return lax.dot_general(a, b, (((1,), (1,)), _BATCH0), preferred_element_type=F32)


def _gdn_kernel(qkv_ref, z_ref, sm_ref, prev_ref, s0_ref, cw_ref, hp_ref, ng_ref,
                o_ref, sfin_ref, s_scr, ext_scr, *, c_in, n_valid, n_seq, shared_init):
    nh = GDN_HEADS
    hd = GDN_D
    n_par = qkv_ref.shape[0]
    stride = ext_scr.shape[1] // n_seq
    L = stride - SUBLANE
    C = L * n_seq
    ls = L.bit_length() - 1
    c = pl.program_id(1)

    @pl.when(c == 0)
    def _():
        for k in range(n_par):
            src = slice(0, n_seq) if shared_init else slice(k * n_seq, (k + 1) * n_seq)
            s_scr[k * n_seq:(k + 1) * n_seq] = s0_ref[src]
            for b in range(n_seq):
                ext_scr[k, b * stride:b * stride + SUBLANE, :] = prev_ref[src.start + b]

    cw = cw_ref[...]
    hp = hp_ref[...]
    row = lax.broadcasted_iota(I32, (C, 1), 0)
    ri = lax.broadcasted_iota(I32, (C, C), 0)
    ci = lax.broadcasted_iota(I32, (C, C), 1)
    same = (ri >> ls) == (ci >> ls)
    incl = same & (ri >= ci)
    strict = same & (ri > ci)
    eye = (ri == ci).astype(F32)

    q_l, k_l, kd_l, dec_l, lows_l, rhs_l, eg_l, z_l, g_l = [], [], [], [], [], [], [], [], []
    for kp in range(n_par):
        u = qkv_ref[kp]
        sm = sm_ref[kp]
        z = z_ref[kp]
        if c_in < C:
            u = jnp.concatenate([u, jnp.zeros((C - c_in, u.shape[1]), F32)], axis=0)
            sm = jnp.concatenate([sm, jnp.zeros((C - c_in, LANE), F32)], axis=0)
            z = jnp.concatenate([z, jnp.zeros((C - c_in, z.shape[1]), F32)], axis=0)
        pieces = []
        for b in range(n_seq):
            base = b * stride + SUBLANE
            ub_ = u[b * L:(b + 1) * L]
            ext_scr[kp, base:base + L, :] = ub_
            acc = ub_ * cw[CONV_W - 1:CONV_W, :]
            win = ext_scr[kp, base - SUBLANE:base + L, :]
            for i in range(CONV_W - 1):
                acc = acc + pltpu.roll(win, CONV_W - 1 - i, axis=0)[SUBLANE:, :] * cw[i:i + 1, :]
            pieces.append(acc)
        conv = pieces[0] if n_seq == 1 else jnp.concatenate(pieces, axis=0)
        if n_seq == 1:
            tail = ext_scr[kp, c_in:c_in + SUBLANE, :]
            ext_scr[kp, 0:SUBLANE, :] = tail
        qkv = conv * jax.nn.sigmoid(conv)

        xs = sm + hp[1:2, :]
        softplus = jnp.maximum(xs, 0.0) + jnp.log1p(jnp.exp(-jnp.abs(xs)))
        la = -jnp.exp(hp[0:1, :]) * softplus
        beta = jax.nn.sigmoid(sm)
        if n_valid < C:
            valid = row < n_valid
            qkv = jnp.where(valid, qkv, 0.0)
            la = jnp.where(valid, la, 0.0)
            beta = jnp.where(valid, beta, 0.0)

        g = _mm_f32(incl.astype(F32), la)
        g_end = _mm_f32((ci == (ri | (L - 1))).astype(F32), g)
        gt = g.T
        g_l.append(g)
        for h in range(nh):
            col = SM_A + h
            gcol = g[:, col:col + 1]
            grow = gt[col:col + 1, :]
            bcol = beta[:, SM_B + h:SM_B + h + 1]
            q = qkv[:, h * hd:(h + 1) * hd]
            k = qkv[:, nh * hd + h * hd:nh * hd + (h + 1) * hd]
            v = qkv[:, 2 * nh * hd + h * hd:2 * nh * hd + (h + 1) * hd]
            q = q * lax.rsqrt(jnp.sum(q * q, axis=-1, keepdims=True) + RMS_EPS) * (hd ** -0.5)
            k = k * lax.rsqrt(jnp.sum(k * k, axis=-1, keepdims=True) + RMS_EPS)
            dec = jnp.exp(jnp.where(incl, gcol - grow, NEG_INF))
            eg = jnp.exp(gcol)
            q_l.append(q.astype(BF16))
            k_l.append(k.astype(BF16))
            kd_l.append((k * jnp.exp(g_end[:, col:col + 1] - gcol)).astype(BF16))
            dec_l.append(dec)
            lows_l.append(bcol * jnp.where(strict, dec, 0.0))
            rhs_l.append(jnp.concatenate([bcol * v, (bcol * eg) * k], axis=1))
            eg_l.append(eg)
            z_l.append(z[:, h * hd:(h + 1) * hd])
    qb = jnp.stack(q_l)
    kb = jnp.stack(k_l)
    kd = jnp.stack(kd_l)
    dec = jnp.stack(dec_l)
    eg = jnp.stack(eg_l)

    low = jnp.stack(lows_l) * _bnt(kb, kb)
    qk = _bnt(qb, kb)
    inv = eye[None] - jnp.where(_sibling_mask(ri, ci, 0)[None], low, 0.0)
    for lvl in range(1, ls):
        off = jnp.where(_sibling_mask(ri, ci, lvl)[None], low, 0.0)
        inv_b = inv.astype(BF16)
        inv = inv - _bmm(inv_b, _bmm(off.astype(BF16), inv_b).astype(BF16))
    uw = _bmm(inv.astype(BF16), jnp.stack(rhs_l).astype(BF16))
    w_b = uw[:, :, hd:].astype(BF16)

    def state(b):
        parts = [s_scr[kp * n_seq + b] for kp in range(n_par)]
        return parts[0] if n_par == 1 else jnp.concatenate(parts, axis=0)

    ws = None
    qs = None
    s_old = []
    for b in range(n_seq):
        S = state(b)
        s_old.append(S)
        Sb = S.astype(BF16)
        if n_seq == 1:
            wm, qm = w_b, qb
        else:
            mine = ((row >> ls) == b)[None]
            wm = jnp.where(mine, w_b, jnp.zeros_like(w_b))
            qm = jnp.where(mine, qb, jnp.zeros_like(qb))
        ws = _bmm(wm, Sb) if ws is None else ws + _bmm(wm, Sb)
        qs = _bmm(qm, Sb) if qs is None else qs + _bmm(qm, Sb)
    ub = (uw[:, :, :hd] - ws).astype(BF16)
    o = eg * qs + _bmm((dec * qk).astype(BF16), ub)
    for b in range(n_seq):
        last = b * L + L - 1
        glast = jnp.stack([g_l[kp][last:last + 1, SM_A + h:SM_A + h + 1] for kp in range(n_par) for h in range(nh)])
        kdm = kd if n_seq == 1 else jnp.where(((row >> ls) == b)[None], kd, jnp.zeros_like(kd))
        s_new = jnp.exp(glast) * s_old[b] + _btn(kdm, ub)
        for kp in range(n_par):
            s_scr[kp * n_seq + b] = s_new[kp * nh:(kp + 1) * nh]

    on = o * lax.rsqrt(jnp.mean(o * o, axis=-1, keepdims=True) + RMS_EPS) * ng_ref[...]
    zz = jnp.stack(z_l)
    res = on * (zz * jax.nn.sigmoid(zz))
    for kp in range(n_par):
        for h in range(nh):
            o_ref[kp, :, h * hd:(h + 1) * hd] = res[kp * nh + h, :c_in]

    @pl.when(c == pl.num_programs(1) - 1)
    def _():
        sfin_ref[...] = s_scr[...]


def _gdn(parts, prev, s0, cw, hp, ng, *, batch, n_chunks, c_in, n_valid, n_seq, n_par, shared_init):
    n = parts.shape[0]
    C = max(c_in, GDN_CHUNK)
    assert n_seq == 1 or (n_chunks == 1 and c_in == C)
    assert batch % n_par == 0
    L = C // n_seq
    n_init = n_seq if shared_init else n_par * n_seq
    init = (lambda b, c: (0, 0, 0)) if shared_init else (lambda b, c: (b, 0, 0))
    init4 = (lambda b, c: (0, 0, 0, 0)) if shared_init else (lambda b, c: (b, 0, 0, 0))
    qkv_w = 3 * GDN_HEADS * GDN_D
    v_w = GDN_HEADS * GDN_D
    rows = n_chunks * c_in
    parts4 = parts.reshape(batch // n_par, n_par, rows, parts.shape[1])
    o, s_fin = pl.pallas_call(
        functools.partial(_gdn_kernel, c_in=c_in, n_valid=n_valid, n_seq=n_seq, shared_init=shared_init),
        grid=(batch // n_par, n_chunks),
        in_specs=[
            pl.BlockSpec((None, n_par, c_in, qkv_w), lambda b, c: (b, 0, c, COL_QKV // qkv_w)),
            pl.BlockSpec((None, n_par, c_in, v_w), lambda b, c: (b, 0, c, COL_Z // v_w)),
            pl.BlockSpec((None, n_par, c_in, LANE), lambda b, c: (b, 0, c, COL_SM // LANE)),
            pl.BlockSpec((n_init, SUBLANE, qkv_w), init),
            pl.BlockSpec((n_init, GDN_HEADS, GDN_D, GDN_D), init4),
            pl.BlockSpec((CONV_W, qkv_w), lambda b, c: (0, 0)),
            pl.BlockSpec((2, LANE), lambda b, c: (0, 0)),
            pl.BlockSpec((1, GDN_D), lambda b, c: (0, 0)),
        ],
        out_specs=[
            pl.BlockSpec((None, n_par, c_in, v_w), lambda b, c: (b, 0, c, 0)),
            pl.BlockSpec((n_par * n_seq, GDN_HEADS, GDN_D, GDN_D), lambda b, c: (b, 0, 0, 0)),
        ],
        out_shape=[jax.ShapeDtypeStruct((batch // n_par, n_par, rows, v_w), F32),
                   jax.ShapeDtypeStruct((batch * n_seq, GDN_HEADS, GDN_D, GDN_D), F32)],
        scratch_shapes=[pltpu.VMEM((n_par * n_seq, GDN_HEADS, GDN_D, GDN_D), F32),
                        pltpu.VMEM((n_par, n_seq * (SUBLANE + L), qkv_w), F32)],
        compiler_params=_cparams(("arbitrary", "arbitrary")),
        name="gated_deltanet",
    )(parts4, parts4, parts4, prev, s0, cw, hp, ng)
    return o.reshape(n, v_w), s_fin


def _sort_key(x):
    b = pltpu.bitcast(x + 0.0, I32)
    return b ^ ((b >> 31) & 0x7FFFFFFF)


KEY_NEG_INF = -(2 ** 31) + 0x7FFFFF
SEARCH_BITS_PER_TRIP = 4


def _count_ge(key_tiles, cand):
    tot = None
    for kt in key_tiles:
        c = jnp.sum((kt >= cand).astype(F32), axis=1, keepdims=True)
        tot = c if tot is None else tot + c
    return tot


def _rows8(x):
    return x.reshape(x.shape[0] // SUBLANE, SUBLANE, x.shape[1])


def _sum_rows8(x, chains=4):
    rows = x.shape[0]
    if rows % (chains * SUBLANE):
        return jnp.sum(_rows8(x), axis=0)
    part = rows // chains
    sums = [jnp.sum(_rows8(x[k * part:(k + 1) * part]), axis=0) for k in range(chains)]
    while len(sums) > 1:
        sums = [a + b for a, b in zip(sums[0::2], sums[1::2])]
    return sums[0]


def _attn_prompt_kernel(q_ref, iq_ref, smq_ref, k_ref, v_ref, smk_ref, mk_ref, mv_ref, msm_ref,
                        o_ref, kbf, vtb, kibf, key_scr, bias_scr, s_scr, cut_scr, *, n_sel, n_meta, kt):
    tq = q_ref.shape[0]
    n_real = k_ref.shape[0]
    kvw = k_ref.shape[1]
    qb_i = pl.program_id(1)
    nt = (qb_i * tq + tq + kt - 1) // kt
    group = ATT_HEADS // ATT_KV_HEADS

    @pl.when(qb_i == 0)
    def _():
        zrows = LANE - n_meta
        kbf[0:n_real, :] = k_ref[...].astype(BF16)
        kbf[n_real:, :] = jnp.concatenate([mk_ref[...], jnp.zeros((zrows, kvw), F32)], axis=0).astype(BF16)
        vtb[:, 0:n_real] = v_ref[...].T.astype(BF16)
        vtb[:, n_real:] = jnp.concatenate([mv_ref[...], jnp.zeros((zrows, kvw), F32)], axis=0).T.astype(BF16)
        kibf[0:n_real, :] = smk_ref[...][:, SM_IK:SM_IK + IDX_DIM].astype(BF16)
        kibf[n_real:, :] = jnp.concatenate(
            [msm_ref[...][:, SM_IK:SM_IK + IDX_DIM], jnp.zeros((zrows, IDX_DIM), F32)], axis=0).astype(BF16)

    iq = iq_ref[...]
    iq_stack = jnp.concatenate([iq[:, h * IDX_DIM:(h + 1) * IDX_DIM] for h in range(IDX_HEADS)], axis=0).astype(BF16)
    w_t = smq_ref[...].T * ((IDX_DIM ** -0.5) * (IDX_HEADS ** -0.5))
    qpos = qb_i * tq + lax.broadcasted_iota(I32, (1, tq), 1)

    def index_scores(start, width):
        s_all = _nt(kibf[pl.ds(start, width), :], iq_stack)
        acc = jnp.zeros((width, tq), F32)
        for h in range(IDX_HEADS):
            acc = acc + jnp.maximum(s_all[:, h * tq:(h + 1) * tq], 0.0) * w_t[SM_IW + h:SM_IW + h + 1, :]
        return acc

    def real_tile(t, carry):
        start = pl.multiple_of(t * kt, kt)
        kpos = start + lax.broadcasted_iota(I32, (kt, 1), 0)
        key_scr[pl.ds(start, kt), :] = _sort_key(jnp.where(kpos <= qpos, index_scores(start, kt), NEG_INF))
        return carry

    lax.fori_loop(0, nt, real_tile, 0)
    mrow = lax.broadcasted_iota(I32, (LANE, 1), 0)
    key_scr[n_real:, :] = _sort_key(jnp.where(mrow < n_meta, index_scores(n_real, LANE), NEG_INF))

    def count_ge(cand):
        def body(t, acc):
            kk = key_scr[pl.ds(pl.multiple_of(t * kt, kt), kt), :]
            return acc + _sum_rows8((kk >= cand).astype(F32))

        acc = lax.fori_loop(0, nt, body, jnp.zeros((SUBLANE, tq), F32))
        acc = acc + jnp.sum(_rows8((key_scr[n_real:, :] >= cand).astype(F32)), axis=0)
        return jnp.sum(acc, axis=0, keepdims=True)

    few = qpos + 1 + n_meta <= n_sel

    def unsettled(state):
        it, _, cnt = state
        return (it < 32) & (jnp.max(jnp.where(few | (cnt == n_sel), 0.0, 1.0)) > 0.0)

    def refine(state):
        it, thr, cnt = state
        for step in range(SEARCH_BITS_PER_TRIP):
            cand = thr + lax.shift_left(jnp.int32(1), 31 - step - it)
            c = count_ge(cand)
            ok = c >= n_sel
            thr = jnp.where(ok, cand, thr)
            cnt = jnp.where(ok, c, cnt)
        return it + SEARCH_BITS_PER_TRIP, thr, cnt

    n_all = (nt * kt + LANE).astype(F32)
    _, thr, cnt = lax.while_loop(
        unsettled, refine, (jnp.int32(0), jnp.full((1, tq), INT_MIN, I32), jnp.full((1, tq), 1.0, F32) * n_all))

    tie = (cnt > n_sel) & (thr > KEY_NEG_INF) & jnp.logical_not(few)
    n_pos = n_real + n_meta
    cut_scr[...] = jnp.full((SUBLANE, tq), float(n_pos), F32)

    def pos_of(start, width, is_meta):
        r = lax.broadcasted_iota(I32, (width, 1), 0)
        return r if is_meta else n_meta + start + r

    @pl.when(jnp.max(tie.astype(F32)) > 0.0)
    def _tie_cut():
        need = n_sel - count_ge(thr + 1)
        nbits = max(1, (n_pos - 1).bit_length())

        def count_eq_le(cut):
            def body(t, acc):
                start = pl.multiple_of(t * kt, kt)
                kk = key_scr[pl.ds(start, kt), :]
                return acc + jnp.sum(_rows8(((kk == thr) & (pos_of(start, kt, False) <= cut)).astype(F32)), axis=0)

            acc = lax.fori_loop(0, nt, body, jnp.zeros((SUBLANE, tq), F32))
            mk = key_scr[n_real:, :]
            acc = acc + jnp.sum(_rows8(((mk == thr) & (pos_of(0, LANE, True) <= cut)).astype(F32)), axis=0)
            return jnp.sum(acc, axis=0, keepdims=True)

        def body(it, cut):
            cand = cut - lax.shift_left(jnp.int32(1), nbits - 1 - it)
            ok = (cand >= 0) & (count_eq_le(cand) >= need)
            return jnp.where(ok, cand, cut)

        cut = lax.fori_loop(0, nbits, body, jnp.full((1, tq), (1 << nbits) - 1, I32))
        cut = jnp.where(tie, cut, n_pos).astype(F32)
        cut_scr[...] = jnp.broadcast_to(cut, (SUBLANE, tq))

    cut = cut_scr[0:1, :].astype(I32)

    def bias_tile(start, width, is_meta):
        kk = key_scr[pl.ds(start, width), :]
        sel = (kk > thr) | ((kk == thr) & (pos_of(start, width, is_meta) <= cut))
        sel = sel & (kk > KEY_NEG_INF)
        bias_scr[pl.ds(start, width), :] = jnp.where(sel, 0.0, NEG_INF)

    def bias_body(t, carry):
        bias_tile(pl.multiple_of(t * kt, kt), kt, False)
        return carry

    lax.fori_loop(0, nt, bias_body, 0)
    bias_tile(n_real, LANE, True)

    scale = ATT_DH ** -0.5
    ng = ATT_KV_HEADS
    gw = group * tq
    q4 = [jnp.concatenate([q_ref[:, (g * group + j) * ATT_DH:(g * group + j + 1) * ATT_DH] for j in range(group)],
                          axis=0).astype(BF16) for g in range(ng)]

    def score_tile(start, width):
        b = bias_scr[pl.ds(start, width), :]
        bb = jnp.concatenate([b] * group, axis=1)
        ms = []
        for g in range(ng):
            s = _nt(kbf[pl.ds(start, width), g * ATT_DH:(g + 1) * ATT_DH], q4[g]) * scale + bb
            s_scr[pl.ds(start, width), g * gw:(g + 1) * gw] = s
            ms.append(jnp.max(_rows8(s), axis=0))
        return tuple(ms)

    def pass1(t, m):
        new = score_tile(pl.multiple_of(t * kt, kt), kt)
        return tuple(jnp.maximum(a, c) for a, c in zip(m, new))

    m8 = lax.fori_loop(0, nt, pass1, score_tile(n_real, LANE))
    m = [jnp.max(x, axis=0, keepdims=True) for x in m8]

    def prob_tile(start, width):
        out = []
        for g in range(ng):
            p = jnp.exp(s_scr[pl.ds(start, width), g * gw:(g + 1) * gw] - m[g])
            pv = _mm(vtb[g * ATT_DH:(g + 1) * ATT_DH, pl.ds(start, width)], p.astype(BF16))
            out += [_sum_rows8(p), pv]
        return tuple(out)

    def pass2(t, carry):
        new = prob_tile(pl.multiple_of(t * kt, kt), kt)
        return tuple(a + c for a, c in zip(carry, new))

    res = lax.fori_loop(0, nt, pass2, prob_tile(n_real, LANE))
    for g in range(ng):
        l, acc = res[2 * g], res[2 * g + 1]
        out_t = acc / jnp.sum(l, axis=0, keepdims=True)
        for j in range(group):
            hh = g * group + j
            o_ref[:, hh * ATT_DH:(hh + 1) * ATT_DH] = out_t[:, j * tq:(j + 1) * tq].T


def _attn_prompt(parts, mparts, *, batch, seq, n_meta, tq, kt):
    nqb = seq // tq
    n_sel = min(TOPK_MAX, (seq + n_meta) // 4)
    kvw = ATT_KV_HEADS * ATT_DH
    nk = seq + LANE
    return pl.pallas_call(
        functools.partial(_attn_prompt_kernel, n_sel=n_sel, n_meta=n_meta, kt=kt),
        grid=(batch, nqb),
        in_specs=[
            pl.BlockSpec((tq, D), lambda b, i: (b * nqb + i, COL_AQ // D)),
            pl.BlockSpec((tq, D), lambda b, i: (b * nqb + i, COL_IQ // D)),
            pl.BlockSpec((tq, LANE), lambda b, i: (b * nqb + i, COL_SM // LANE)),
            pl.BlockSpec((seq, kvw), lambda b, i: (b, COL_AK // kvw)),
            pl.BlockSpec((seq, kvw), lambda b, i: (b, COL_AV // kvw)),
            pl.BlockSpec((seq, LANE), lambda b, i: (b, COL_SM // LANE)),
            pl.BlockSpec((n_meta, kvw), lambda b, i: (0, COL_AK // kvw)),
            pl.BlockSpec((n_meta, kvw), lambda b, i: (0, COL_AV // kvw)),
            pl.BlockSpec((n_meta, LANE), lambda b, i: (0, COL_SM // LANE)),
        ],
        out_specs=pl.BlockSpec((tq, D), lambda b, i: (b * nqb + i, 0)),
        out_shape=jax.ShapeDtypeStruct((batch * seq, D), F32),
        scratch_shapes=[pltpu.VMEM((nk, kvw), BF16), pltpu.VMEM((kvw, nk), BF16), pltpu.VMEM((nk, IDX_DIM), BF16),
                        pltpu.VMEM((nk, tq), I32), pltpu.VMEM((nk, tq), F32),
                        pltpu.VMEM((nk, ATT_HEADS * tq), F32), pltpu.VMEM((SUBLANE, tq), F32)],
        compiler_params=_cparams(("arbitrary", "arbitrary")),
        name="sparse_attention_prompt",
    )(parts, parts, parts, parts, parts, parts, mparts, mparts, mparts)


def _attn_sample_kernel(pt_ref, *refs, n_pages, n_sel, n_el):
    pages = refs[0:3 * n_pages * n_el]
    q_ref, iq_ref, kn_ref, vn_ref, sm_ref, o_ref, kil, vil, kit, cut_scr = refs[3 * n_pages * n_el:]
    del pt_ref
    t_new = q_ref.shape[0] // n_el
    page = pages[2 * n_pages].shape[1]
    past = n_pages * page
    nkv = ATT_KV_HEADS
    group = ATT_HEADS // ATT_KV_HEADS
    zpad = jnp.zeros((LANE - t_new, nkv * ATT_DH), BF16)
    trow = lax.broadcasted_iota(I32, (t_new, LANE), 0)
    tlane = lax.broadcasted_iota(I32, (t_new, LANE), 1)

    knew, vnew, key_past_l, key_new_l = [], [], [], []
    for el in range(n_el):
        kp = pages[3 * n_pages * el:3 * n_pages * el + n_pages]
        vp = pages[3 * n_pages * el + n_pages:3 * n_pages * el + 2 * n_pages]
        ip = pages[3 * n_pages * el + 2 * n_pages:3 * n_pages * (el + 1)]
        rows = slice(el * t_new, (el + 1) * t_new)
        for j in range(n_pages):
            kil[el, j * nkv * page:(j + 1) * nkv * page, :] = kp[j][...].astype(BF16)
            vil[el, j * nkv * page:(j + 1) * nkv * page, :] = vp[j][...].astype(BF16)
            kit[el, :, j * page:(j + 1) * page] = ip[j][...].astype(BF16)
        sm = sm_ref[rows, :]
        knew.append(jnp.concatenate([kn_ref[rows, :].astype(BF16), zpad], axis=0))
        vnew.append(jnp.concatenate([vn_ref[rows, :].astype(BF16), zpad], axis=0))
        ki_new = jnp.concatenate(
            [sm[:, SM_IK:SM_IK + IDX_DIM].astype(BF16), jnp.zeros((LANE - t_new, IDX_DIM), BF16)], axis=0)

        iq = iq_ref[rows, :]
        qi = jnp.concatenate([iq[:, h * IDX_DIM:(h + 1) * IDX_DIM] for h in range(IDX_HEADS)], axis=0).astype(BF16)
        wcol = jnp.concatenate([sm[:, SM_IW + h:SM_IW + h + 1] for h in range(IDX_HEADS)], axis=0)
        wcol = wcol * ((IDX_DIM ** -0.5) * (IDX_HEADS ** -0.5))

        def idx_scores(qk):
            s = jnp.maximum(qk, 0.0) * wcol
            acc = s[0:t_new]
            for h in range(1, IDX_HEADS):
                acc = acc + s[h * t_new:(h + 1) * t_new]
            return acc

        key_past_l.append(_sort_key(idx_scores(_mm(qi, kit[el]))))
        key_new_l.append(_sort_key(jnp.where(tlane <= trow, idx_scores(_nt(qi, ki_new)), NEG_INF)))

    key_past = key_past_l[0] if n_el == 1 else jnp.concatenate(key_past_l, axis=0)
    key_new = key_new_l[0] if n_el == 1 else jnp.concatenate(key_new_l, axis=0)
    tiles = [key_past, key_new]
    tr = n_el * t_new

    def digit_body(it, thr):
        step = lax.shift_left(jnp.int32(1), 28 - 4 * it)
        digit = jnp.zeros((tr, 1), I32)
        for j in range(1, 16):
            digit = digit + (_count_ge(tiles, thr + j * step) >= n_sel).astype(I32)
        return thr + digit * step

    thr = lax.fori_loop(0, 8, digit_body, jnp.full((tr, 1), INT_MIN, I32))
    n_ge = _count_ge(tiles, thr)
    tie = (n_ge > n_sel) & (thr > KEY_NEG_INF)
    pos_past = lax.broadcasted_iota(I32, (1, past), 1)
    pos_new = past + lax.broadcasted_iota(I32, (1, LANE), 1)
    n_pos = past + t_new
    cut_scr[...] = jnp.full(cut_scr.shape, n_pos, I32)

    @pl.when(jnp.max(tie.astype(F32)) > 0.0)
    def _tie_cut():
        need = n_sel - _count_ge(tiles, thr + 1)
        nbits = max(1, (n_pos - 1).bit_length())

        def cut_body(it, cut):
            cand = cut - lax.shift_left(jnp.int32(1), nbits - 1 - it)
            cnt = (jnp.sum(((key_past == thr) & (pos_past <= cand)).astype(F32), axis=1, keepdims=True)
                   + jnp.sum(((key_new == thr) & (pos_new <= cand)).astype(F32), axis=1, keepdims=True))
            ok = (cand >= 0) & (cnt >= need)
            return jnp.where(ok, cand, cut)

        cut = lax.fori_loop(0, nbits, cut_body, jnp.full((tr, 1), (1 << nbits) - 1, I32))
        cut_scr[...] = jnp.broadcast_to(jnp.where(tie, cut, n_pos), cut_scr.shape)

    cut = cut_scr[:, 0:1]

    def chosen(kk, pos):
        return ((kk > thr) | ((kk == thr) & (pos <= cut))) & (kk > KEY_NEG_INF)

    bias_new = jnp.where(chosen(key_new, pos_new), 0.0, NEG_INF)
    sel01 = chosen(key_past, pos_past).astype(BF16)
    er = lax.broadcasted_iota(I32, (page, nkv * page), 0)
    ec = lax.broadcasted_iota(I32, (page, nkv * page), 1)
    expand = ((ec >= er * nkv) & (ec < (er + 1) * nkv)).astype(BF16)
    sel_il = jnp.concatenate([_mm(sel01[:, j * page:(j + 1) * page], expand) for j in range(n_pages)], axis=1)
    il_lane = lax.broadcasted_iota(I32, (1, nkv * past), 1)
    assert nkv & (nkv - 1) == 0
    il_head = il_lane & (nkv - 1)
    scale = ATT_DH ** -0.5
    for el in range(n_el):
        rows = slice(el * t_new, (el + 1) * t_new)
        q = q_ref[rows, :]
        b_new = jnp.concatenate([bias_new[rows]] * group, axis=0)
        for g in range(ATT_KV_HEADS):
            qg = jnp.concatenate(
                [q[:, (g * group + j) * ATT_DH:(g * group + j + 1) * ATT_DH] for j in range(group)], axis=0).astype(BF16)
            b_g = jnp.where((sel_il[rows] > 0.5) & (il_head == g), 0.0, NEG_INF)
            s_p = _nt(qg, kil[el]) * scale + jnp.concatenate([b_g] * group, axis=0)
            s_n = _nt(qg, knew[el][:, g * ATT_DH:(g + 1) * ATT_DH]) * scale + b_new
            m = jnp.maximum(jnp.max(s_p, axis=1, keepdims=True), jnp.max(s_n, axis=1, keepdims=True))
            p_p = jnp.exp(s_p - m)
            p_n = jnp.exp(s_n - m)
            l = jnp.sum(p_p, axis=1, keepdims=True) + jnp.sum(p_n, axis=1, keepdims=True)
            acc = _mm(p_p.astype(BF16), vil[el]) + _mm(p_n.astype(BF16), vnew[el][:, g * ATT_DH:(g + 1) * ATT_DH])
            res = acc / l
            for j in range(group):
                hh = g * group + j
                o_ref[rows, hh * ATT_DH:(hh + 1) * ATT_DH] = res[j * t_new:(j + 1) * t_new]


def _attn_sample(parts, cache_k, cache_v, cache_ik, page_table, *, t_new, n_el):
    db, n_pages = page_table.shape
    assert db % n_el == 0
    page = cache_k.shape[2]
    kvw = ATT_KV_HEADS * ATT_DH
    past = n_pages * page
    n_sel = min(TOPK_MAX, (past + t_new) // 4)
    tr = n_el * t_new

    n_pool = cache_k.shape[1]
    cache_k = cache_k.reshape(n_pool, page * ATT_KV_HEADS, ATT_DH)
    cache_v = cache_v.reshape(n_pool, page * ATT_KV_HEADS, ATT_DH)
    cache_ik = jnp.swapaxes(cache_ik, 2, 3).reshape(n_pool, IDX_DIM, page)

    def kv_page(el, j):
        return pl.BlockSpec((None, page * ATT_KV_HEADS, ATT_DH), lambda b, pt, el=el, j=j: (pt[b * n_el + el, j], 0, 0))

    def ik_page(el, j):
        return pl.BlockSpec((None, IDX_DIM, page), lambda b, pt, el=el, j=j: (pt[b * n_el + el, j], 0, 0))

    in_specs, operands = [], []
    for el in range(n_el):
        in_specs += [kv_page(el, j) for j in range(n_pages)] + [kv_page(el, j) for j in range(n_pages)]
        in_specs += [ik_page(el, j) for j in range(n_pages)]
        operands += [cache_k] * n_pages + [cache_v] * n_pages + [cache_ik] * n_pages
    in_specs += [pl.BlockSpec((tr, D), lambda b, pt: (b, COL_AQ // D)),
                 pl.BlockSpec((tr, D), lambda b, pt: (b, COL_IQ // D)),
                 pl.BlockSpec((tr, kvw), lambda b, pt: (b, COL_AK // kvw)),
                 pl.BlockSpec((tr, kvw), lambda b, pt: (b, COL_AV // kvw)),
                 pl.BlockSpec((tr, LANE), lambda b, pt: (b, COL_SM // LANE))]
    grid_spec = pltpu.PrefetchScalarGridSpec(
        num_scalar_prefetch=1, grid=(db // n_el,), in_specs=in_specs,
        out_specs=pl.BlockSpec((tr, D), lambda b, pt: (b, 0)),
        scratch_shapes=[pltpu.VMEM((n_el, past * ATT_KV_HEADS, ATT_DH), BF16),
                        pltpu.VMEM((n_el, past * ATT_KV_HEADS, ATT_DH), BF16),
                        pltpu.VMEM((n_el, IDX_DIM, past), BF16), pltpu.VMEM((tr, LANE), I32)])
    return pl.pallas_call(
        functools.partial(_attn_sample_kernel, n_pages=n_pages, n_sel=n_sel, n_el=n_el),
        grid_spec=grid_spec,
        out_shape=jax.ShapeDtypeStruct((db * t_new, D), F32),
        compiler_params=_cparams(("arbitrary",)),
        name="sparse_attention_sample",
    )(page_table, *operands, parts, parts, parts, parts, parts)


def _layer_norm(x, g, b):
    mu = jnp.mean(x, axis=-1, keepdims=True)
    xc = x - mu
    var = jnp.mean(xc * xc, axis=-1, keepdims=True)
    return xc * lax.rsqrt(var + LN_EPS) * g + b


def _finish_kernel(x_ref, og_ref, oa_ref, ga_ref, gb_ref, wbg_ref, wba_ref, wo_ref, g_ref, b_ref, h_ref, *, alpha):
    a = _mm(og_ref[...].astype(BF16), wbg_ref[...])
    b = _mm(oa_ref[...].astype(BF16), wba_ref[...])
    merged = jax.nn.sigmoid(ga_ref[...]) * a + jax.nn.sigmoid(gb_ref[...]) * b
    y = alpha * x_ref[...] + _mm(merged.astype(BF16), wo_ref[...])
    h_ref[...] = _layer_norm(y, g_ref[...], b_ref[...])


def _finish(x, o_gdn, o_att, parts, wbg, wba, wo, g, b, *, alpha, tm):
    n = x.shape[0]
    tm = min(tm, n)
    row = lambda i: (i, 0)
    full = lambda i: (0, 0)
    return pl.pallas_call(
        functools.partial(_finish_kernel, alpha=alpha),
        grid=(n // tm,),
        in_specs=[pl.BlockSpec((tm, D), row), pl.BlockSpec((tm, D), row), pl.BlockSpec((tm, D), row),
                  pl.BlockSpec((tm, D), lambda i: (i, COL_GA // D)), pl.BlockSpec((tm, D), lambda i: (i, COL_GB // D)),
                  pl.BlockSpec((D, D), full), pl.BlockSpec((D, D), full), pl.BlockSpec((D, D), full),
                  pl.BlockSpec((1, D), full), pl.BlockSpec((1, D), full)],
        out_specs=pl.BlockSpec((tm, D), row),
        out_shape=jax.ShapeDtypeStruct((n, D), F32),
        compiler_params=_cparams(("arbitrary",)),
        name="merge_layernorm",
    )(x, o_gdn, o_att, parts, parts, wbg, wba, wo, g, b)


_CAND = [(r0, r1) for r0 in range(PEER_TOPK) for r1 in range(PEER_TOPK) if (r0 + 1) * (r1 + 1) <= PEER_TOPK]
_CAND_OFF = [next(i for i, c in enumerate(_CAND) if c[0] == r0) for r0 in range(PEER_TOPK)]
_CAND_LEN = [sum(1 for c in _CAND if c[0] == r0) for r0 in range(PEER_TOPK)]
_CAND_ROWS = -(-len(_CAND) // SUBLANE) * SUBLANE


def _top_rows(s, n_top, break_ties):
    rows, cols = s.shape
    iota = lax.broadcasted_iota(I32, (rows, cols), 0).astype(F32)
    rank = jnp.full((rows, cols), float(n_top), F32)
    vals = []
    for r in range(n_top):
        m = jnp.max(s, axis=0, keepdims=True)
        hit = s == m
        if break_ties:
            hit = iota == jnp.min(jnp.where(hit, iota, float(rows)), axis=0, keepdims=True)
        vals.append(m)
        s = jnp.where(hit, NEG_INF, s)
        rank = jnp.where(hit, float(r), rank)
    n_ranked = jnp.sum((rank < float(n_top)).astype(F32), axis=0, keepdims=True)
    return jnp.concatenate(vals, axis=0), rank, n_ranked == float(n_top)


def _peer_kernel(h_ref, wq_ref, sk_ref, u0_ref, un_ref, vp_ref, vl_ref, g_ref, b_ref, y_ref,
                 hb_scr, q_scr, rank1_scr, bt_scr, nt_scr, at_scr, s1_scr, s_scr, p_scr, acc_scr, *, alpha, n_steps):
    tt = h_ref.shape[0]
    eb = un_ref.shape[0] // 2
    e = pl.program_id(1)
    nk = PEER_NKEYS
    kk = PEER_TOPK
    tchunks = tt // LANE

    @pl.when(e == 0)
    def _prep():
        hb = h_ref[...].astype(BF16)
        hb_scr[...] = hb
        q_scr[...] = _mm(hb, wq_ref[...]).astype(BF16)
        acc_scr[...] = jnp.zeros(acc_scr.shape, F32)

        def scores(hd, carry):
            q0 = q_scr[:, pl.ds(pl.multiple_of(hd * 2 * nk, nk), nk)]
            q1 = q_scr[:, pl.ds(pl.multiple_of(hd * 2 * nk + nk, nk), nk)]
            at_scr[hd] = _nt(sk_ref[hd, 0], q0)
            s1_scr[hd] = _nt(sk_ref[hd, 1], q1)
            return carry

        lax.fori_loop(0, PEER_HEADS, scores, 0, unroll=True)

        def select_chunk(hd, c0, s0, s1, break_ties):
            a, rank0, ok0 = _top_rows(s0, kk, break_ties)
            b, rank1, ok1 = _top_rows(s1, kk, break_ties)
            pad = [jnp.full((_CAND_ROWS - len(_CAND), LANE), NEG_INF, F32)]
            cand = jnp.concatenate([a[r0:r0 + 1] + b[r1:r1 + 1] for r0, r1 in _CAND] + pad, axis=0)
            top, crank, okc = _top_rows(cand, kk, break_ties)
            zsum = jnp.sum(jnp.exp(top - top[0:1]), axis=0, keepdims=True)
            chosen = (crank < float(kk)).astype(F32)
            crow = lax.broadcasted_iota(I32, (_CAND_ROWS, 1), 0)
            nsel = jnp.zeros((nk, LANE), F32)
            for r0 in range(kk):
                in_group = (crow >= _CAND_OFF[r0]) & (crow < _CAND_OFF[r0] + _CAND_LEN[r0])
                cnt = jnp.sum(jnp.where(in_group, chosen, 0.0), axis=0, keepdims=True)
                nsel = nsel + jnp.where(rank0 == float(r0), cnt, 0.0)
            rank1_scr[hd, :, pl.ds(c0, LANE)] = rank1.astype(BF16)
            nt_scr[hd, :, pl.ds(c0, LANE)] = nsel
            at_scr[hd, :, pl.ds(c0, LANE)] = jnp.exp(s0 - a[0:1])
            bt_scr[hd, :, pl.ds(c0, LANE)] = (jnp.exp(s1 - b[0:1]) / zsum).astype(BF16)
            return jnp.min((ok0 & ok1 & okc).astype(F32)) > 0.5

        pair = 4 if tchunks % 4 == 0 else (2 if tchunks % 2 == 0 else 1)

        def select(idx, carry):
            hd = idx // (tchunks // pair)
            first = (idx % (tchunks // pair)) * pair
            tie_free = None
            chunks = []
            for k in range(pair):
                c0 = pl.multiple_of((first + k) * LANE, LANE)
                s0 = at_scr[hd, :, pl.ds(c0, LANE)]
                s1 = s1_scr[hd, :, pl.ds(c0, LANE)]
                chunks.append((c0, s0, s1))
            for c0, s0, s1 in chunks:
                ok = select_chunk(hd, c0, s0, s1, False)
                tie_free = ok if tie_free is None else tie_free & ok

            @pl.when(jnp.logical_not(tie_free))
            def _():
                for c0, s0, s1 in chunks:
                    select_chunk(hd, c0, s0, s1, True)

            return carry

        lax.fori_loop(0, PEER_HEADS * tchunks // pair, select, 0)

        s_scr[1] = _nt(u0_ref[...], hb_scr[...])
        p_scr[1] = jnp.zeros(p_scr.shape[1:], BF16)

    groups = eb // nk

    assert 2 * groups == SUBLANE

    def weights(half, s_ref, p_ref):
        i0 = pl.multiple_of(e * SUBLANE, SUBLANE)
        zero = jnp.zeros((nk, LANE), BF16)

        for tc in range(tchunks):
            cols = slice(tc * LANE, (tc + 1) * LANE)
            gates = [zero] * groups
            for hd in range(PEER_HEADS):
                nblk = nt_scr[hd, pl.ds(i0, SUBLANE), cols]
                ablk = at_scr[hd, pl.ds(i0, SUBLANE), cols]
                rank1 = rank1_scr[hd, :, cols]
                bval = bt_scr[hd, :, cols]
                for ii in range(groups):
                    r = half * groups + ii
                    nrow = nblk[r:r + 1].astype(BF16)
                    arow = ablk[r:r + 1].astype(BF16)
                    gates[ii] = gates[ii] + jnp.where(rank1 < nrow, bval, zero) * arow
            for ii in range(groups):
                rows = slice(ii * nk, (ii + 1) * nk)
                s = s_ref[rows, cols]
                act = 0.5 * s * (1.0 + lax.erf(s * (2.0 ** -0.5)))
                p_ref[rows, cols] = gates[ii] * act.astype(BF16)

    def step(cur, nxt):
        s_scr[nxt] = _nt(un_ref[...], hb_scr[...])
        for half in range(2):
            rows = pl.ds(half * eb, eb)
            weights(half, s_scr.at[cur, rows], p_scr.at[nxt, rows])
        acc_scr[...] += _tn(p_scr[cur], vp_ref[...])

    @pl.when(e % 2 == 0)
    def _():
        step(1, 0)

    @pl.when(e % 2 == 1)
    def _():
        step(0, 1)

    last_slot = (n_steps - 1) % 2

    @pl.when(e == n_steps - 1)
    def _():
        y = alpha * h_ref[...] + (acc_scr[...] + _tn(p_scr[last_slot], vl_ref[...]))
        y_ref[...] = _layer_norm(y, g_ref[...], b_ref[...])


def _peer(h, wq, sk, u, v, g, b, *, alpha, tt, eb):
    n = h.shape[0]
    n_exp = u.shape[0]
    nk = PEER_NKEYS
    nb = n_exp // eb
    return pl.pallas_call(
        functools.partial(_peer_kernel, alpha=alpha, n_steps=nb),
        grid=(n // tt, nb),
        in_specs=[pl.BlockSpec((tt, D), lambda t, e: (t, 0)),
                  pl.BlockSpec(wq.shape, lambda t, e: (0, 0)),
                  pl.BlockSpec(sk.shape, lambda t, e: (0, 0, 0, 0)),
                  pl.BlockSpec((eb, D), lambda t, e: (0, 0)),
                  pl.BlockSpec((eb, D), lambda t, e: (jnp.minimum(e + 1, nb - 1), 0)),
                  pl.BlockSpec((eb, D), lambda t, e: (jnp.maximum(e - 1, 0), 0)),
                  pl.BlockSpec((eb, D), lambda t, e: (nb - 1, 0)),
                  pl.BlockSpec((1, D), lambda t, e: (0, 0)),
                  pl.BlockSpec((1, D), lambda t, e: (0, 0))],
        out_specs=pl.BlockSpec((tt, D), lambda t, e: (t, 0)),
        out_shape=jax.ShapeDtypeStruct((n, D), F32),
        scratch_shapes=[pltpu.VMEM((tt, D), BF16), pltpu.VMEM((tt, wq.shape[1]), BF16),
                        pltpu.VMEM((PEER_HEADS, nk, tt), BF16), pltpu.VMEM((PEER_HEADS, nk, tt), BF16),
                        pltpu.VMEM((PEER_HEADS, nk, tt), F32), pltpu.VMEM((PEER_HEADS, nk, tt), F32),
                        pltpu.VMEM((PEER_HEADS, nk, tt), F32),
                        pltpu.VMEM((2, eb, tt), F32), pltpu.VMEM((2, eb, tt), BF16), pltpu.VMEM((tt, D), F32)],
        compiler_params=_cparams(("arbitrary", "arbitrary")),
        name="peer_ffn",
    )(h, wq, sk, u, u, v, v, g, b)


class _Tiles(NamedTuple):
    proj_rows: int
    rows: int
    attn_q: int
    attn_k: int
    peer_experts: int
    gdn_groups: int


def _tiles(batch, seq):
    rows = 512 if (batch * seq) % 512 == 0 else 256
    groups = 4 if batch % 4 == 0 else (2 if batch % 2 == 0 else 1)
    proj_rows = 1024 if (batch * seq) % 1024 == 0 else rows
    return _Tiles(proj_rows=proj_rows, rows=rows, attn_q=256 if seq % 256 == 0 else 128, attn_k=min(512, seq),
                  peer_experts=SUBLANE * PEER_NKEYS, gdn_groups=groups)


def _permute_w_in(w):
    gq = 3 * GDN_HEADS * GDN_D
    gv = GDN_HEADS * GDN_D
    aq = ATT_HEADS * ATT_DH
    akv = ATT_KV_HEADS * ATT_DH
    iqw = IDX_HEADS * IDX_DIM
    sizes = (gq, gv, GDN_HEADS, GDN_HEADS, aq, akv, akv, iqw, IDX_DIM, IDX_HEADS, D, D)
    offs = [0]
    for s in sizes:
        offs.append(offs[-1] + s)
    seg = [w[:, offs[i]:offs[i + 1]] for i in range(len(sizes))]
    qkv, z, b, a, q, k, v, iq, ik, iw, ga, gb = seg
    pad = jnp.zeros((w.shape[0], LANE - IDX_DIM - 2 * GDN_HEADS - IDX_HEADS), w.dtype)
    return jnp.concatenate([qkv, z, q, iq, ga, gb, k, v, ik, b, a, iw, pad], axis=1).astype(BF16)


def kernel(x_prompt, x_sample, cache_k, cache_v, cache_idx_k, state_conv, state_delta, page_table, meta_tokens, w_in, conv_w, a_log, dt_bias, gdn_norm_g, w_branch_gdn, w_branch_attn, w_out, ln1_g, ln1_b, peer_wq, peer_subkeys, peer_u, peer_v, ln2_g, ln2_b):
    depth = w_in.shape[0]
    assert depth == 1, "single-layer step"
    batch, seq, d = x_prompt.shape
    db, t_new, _ = x_sample.shape
    n_meta = meta_tokens.shape[0]
    assert d == D and seq % GDN_CHUNK == 0 and n_meta % SUBLANE == 0 and n_meta <= GDN_CHUNK
    alpha = (2 * depth) ** 0.25
    qkv_w = 3 * GDN_HEADS * GDN_D
    kvw = ATT_KV_HEADS * ATT_DH

    tiles = _tiles(batch, seq)
    w_r = _permute_w_in(w_in[0])
    xp = x_prompt.reshape(batch * seq, D)
    xs = x_sample.reshape(db * t_new, D)
    parts_p = _project(xp, w_r, tiles.proj_rows)
    parts_s = _project(xs, w_r, min(tiles.rows, db * t_new))
    parts_m = _project(meta_tokens.astype(F32), w_r, n_meta)

    hp = jnp.zeros((2, LANE), F32).at[0, SM_A:SM_A + GDN_HEADS].set(a_log[0]).at[1, SM_A:SM_A + GDN_HEADS].set(dt_bias[0])
    cw = conv_w[0]
    ng = gdn_norm_g[0].reshape(1, GDN_D)
    zero_prev = jnp.zeros((1, SUBLANE, qkv_w), F32)
    zero_state = jnp.zeros((1, GDN_HEADS, GDN_D, GDN_D), F32)
    _, s_meta = _gdn(parts_m, zero_prev, zero_state, cw, hp, ng, batch=1, n_chunks=1, c_in=n_meta, n_valid=n_meta,
                     n_seq=1, n_par=1, shared_init=True)
    meta_prev = parts_m[n_meta - SUBLANE:, :qkv_w].reshape(1, SUBLANE, qkv_w)
    og_p, p_delta = _gdn(parts_p, meta_prev, s_meta, cw, hp, ng, batch=batch, n_chunks=seq // GDN_CHUNK,
                         c_in=GDN_CHUNK, n_valid=GDN_CHUNK, n_seq=1, n_par=tiles.gdn_groups, shared_init=True)
    per = GDN_CHUNK // t_new
    assert per * t_new == GDN_CHUNK and t_new % SUBLANE == 0 and db % per == 0
    samp_prev = jnp.pad(state_conv.reshape(db, CONV_W - 1, qkv_w), ((0, 0), (SUBLANE - (CONV_W - 1), 0), (0, 0)))
    og_s, s_delta = _gdn(parts_s, samp_prev, state_delta.reshape(db, GDN_HEADS, GDN_D, GDN_D).astype(F32), cw, hp, ng,
                         batch=db // per, n_chunks=1, c_in=GDN_CHUNK, n_valid=GDN_CHUNK, n_seq=per, n_par=1,
                         shared_init=False)

    oa_p = _attn_prompt(parts_p, parts_m, batch=batch, seq=seq, n_meta=n_meta, tq=tiles.attn_q, kt=tiles.attn_k)
    n_el = 4 if db % 4 == 0 else (2 if db % 2 == 0 else 1)
    oa_s = _attn_sample(parts_s, cache_k, cache_v, cache_idx_k, page_table, t_new=t_new, n_el=n_el)

    wbg = w_branch_gdn[0].astype(BF16)
    wba = w_branch_attn[0].astype(BF16)
    wo = w_out[0].astype(BF16)
    g1 = ln1_g[0].reshape(1, D)
    b1 = ln1_b[0].reshape(1, D)
    h_p = _finish(xp, og_p, oa_p, parts_p, wbg, wba, wo, g1, b1, alpha=alpha, tm=tiles.rows)
    h_s = _finish(xs, og_s, oa_s, parts_s, wbg, wba, wo, g1, b1, alpha=alpha, tm=tiles.rows)

    wq = peer_wq[0].astype(BF16)
    sk = peer_subkeys[0].astype(BF16)
    u = peer_u[0].astype(BF16)
    v = peer_v[0].astype(BF16)
    g2 = ln2_g[0].reshape(1, D)
    b2 = ln2_b[0].reshape(1, D)
    y_p = _peer(h_p, wq, sk, u, v, g2, b2, alpha=alpha, tt=tiles.rows, eb=tiles.peer_experts)
    y_s = _peer(h_s, wq, sk, u, v, g2, b2, alpha=alpha, tt=min(tiles.rows, db * t_new), eb=tiles.peer_experts)

    y_prompt = y_p.reshape(batch, seq, D)
    y_sample = y_s.reshape(db, t_new, D)
    pp = parts_p.reshape(batch, seq, N_COLS)
    ps = parts_s.reshape(db, t_new, N_COLS)
    p_conv = pp[:, seq - (CONV_W - 1):, :qkv_w][None]
    s_conv = jnp.concatenate([state_conv[0].astype(F32), ps[:, :, :qkv_w]], axis=1)[:, -(CONV_W - 1):][None]

    def with_meta(col, width):
        m = jnp.broadcast_to(parts_m[None, :, col:col + width], (batch, n_meta, width))
        return jnp.concatenate([m, pp[:, :, col:col + width]], axis=1)

    p_k = with_meta(COL_AK, kvw).reshape(1, batch, seq + n_meta, ATT_KV_HEADS, ATT_DH)
    p_v = with_meta(COL_AV, kvw).reshape(1, batch, seq + n_meta, ATT_KV_HEADS, ATT_DH)
    p_idx_k = with_meta(COL_SM + SM_IK, IDX_DIM)[None]
    s_k = ps[:, :, COL_AK:COL_AK + kvw].reshape(1, db, t_new, ATT_KV_HEADS, ATT_DH)
    s_v = ps[:, :, COL_AV:COL_AV + kvw].reshape(1, db, t_new, ATT_KV_HEADS, ATT_DH)
    s_idx_k = ps[:, :, COL_SM + SM_IK:COL_SM + SM_IK + IDX_DIM][None]
    return (y_prompt, y_sample, p_conv, p_delta[None], p_k, p_v, p_idx_k, s_conv, s_delta[None].astype(state_delta.dtype),
            s_k, s_v, s_idx_k)
```
